```python
import jax, jax.numpy as jnp
from jax import lax
import numpy as np

D_MODEL = 2048
BATCH = 8
SEQ = 8192
DEPTH = 1

MEM_LEN = 256
HGRN_WIDTH = D_MODEL // 2
HGRN_HEADS = 8
HGRN_KDIM = HGRN_WIDTH // HGRN_HEADS
HGRN_VDIM = HGRN_WIDTH // HGRN_HEADS
CONV_CH = D_MODEL - HGRN_WIDTH
CONV_GROUPS = 8
SHORT_CONV_K = 3
IN_COLS = 4 * HGRN_WIDTH + 3 * CONV_CH
CHUNK = 64
MEM_HEADS = 4
MEM_HEAD_DIM = D_MODEL // MEM_HEADS
D_FF = 5632
FFN_CONV_K = 3
EPS = 1e-6

kernel_name = "hgrn2_shortconv_hybrid_block"


def rmsnorm(x, w):
    xf = x.astype(jnp.float32)
    y = xf * lax.rsqrt(jnp.mean(xf * xf, axis=-1, keepdims=True) + EPS)
    return (y * w.astype(jnp.float32)).astype(x.dtype)


def causal_dwconv(x, w):
    k = w.shape[0]
    s = x.shape[1]
    xp = jnp.pad(x, ((0, 0), (k - 1, 0), (0, 0)))
    y = xp[:, 0:s] * w[0]
    for j in range(1, k):
        y = y + xp[:, j:j + s] * w[j]
    return y


def hgrn2_chunked(q, k, v, logf):
    bb, s, h, kd = q.shape
    vd = v.shape[-1]
    n = s // CHUNK

    def to_chunks(t):
        return t.reshape(bb, n, CHUNK, h, t.shape[-1]).transpose(1, 0, 3, 2, 4)

    qc, kc, vc, gc = to_chunks(q), to_chunks(k), to_chunks(v), to_chunks(logf)
    causal = jnp.tril(jnp.ones((CHUNK, CHUNK), dtype=bool))

    def step(state, inp):
        q_, k_, v_, g_ = inp
        b = jnp.cumsum(g_, axis=2)
        o_inter = jnp.einsum('bhtk,bhkv->bhtv', q_ * jnp.exp(b), state)
        diff = b[:, :, :, None, :] - b[:, :, None, :, :]
        decay = jnp.exp(jnp.where(causal[:, :, None], diff, -jnp.inf))
        scores = jnp.einsum('bhtk,bhtsk,bhsk->bhts', q_, decay, k_)
        o = o_inter + jnp.einsum('bhts,bhsv->bhtv', scores, v_)
        b_last = b[:, :, -1:, :]
        new_state = (jnp.exp(b_last[:, :, 0, :])[..., None] * state
                     + jnp.einsum('bhsk,bhsv->bhkv', k_ * jnp.exp(b_last - b), v_))
        return new_state, o

    s0 = jnp.zeros((bb, h, kd, vd), jnp.float32)
    _, o = lax.scan(step, s0, (qc, kc, vc, gc))
    return o.transpose(1, 0, 3, 2, 4).reshape(bb, s, h, vd)


def hybrid_mixer(h, w_in, lb, hgrn_norm_w, sconv_w, w_out):
    bb, s, _ = h.shape
    proj = h @ w_in
    W, C = HGRN_WIDTH, CONV_CH
    splits = [W, 2 * W, 3 * W, 4 * W, 4 * W + C, 4 * W + 2 * C]
    q, f_pre, i_in, g, cb, cc, ch = jnp.split(proj, splits, axis=-1)

    f = lb + (1.0 - lb) * jax.nn.sigmoid(f_pre.astype(jnp.float32))
    logf = jnp.log(f)
    k = 1.0 - f
    qf = jax.nn.silu(q.astype(jnp.float32))
    heads = lambda t, d: t.reshape(bb, s, HGRN_HEADS, d)
    o = hgrn2_chunked(heads(qf, HGRN_KDIM), heads(k, HGRN_KDIM),
                      heads(i_in.astype(jnp.float32), HGRN_VDIM), heads(logf, HGRN_KDIM))
    o = rmsnorm(o, hgrn_norm_w).reshape(bb, s, W).astype(h.dtype)
    o = o * jax.nn.silu(g)

    y = cb * causal_dwconv(cc * ch, sconv_w)

    return jnp.concatenate([o, y], axis=-1) @ w_out


def memory_cross_attention(h, mem_n, wq, wk, wv, wo):
    bb, s, _ = h.shape
    m = mem_n.shape[1]
    q = (h @ wq).reshape(bb, s, MEM_HEADS, MEM_HEAD_DIM)
    k = (mem_n @ wk).reshape(bb, m, MEM_HEADS, MEM_HEAD_DIM)
    v = (mem_n @ wv).reshape(bb, m, MEM_HEADS, MEM_HEAD_DIM)
    sc = jnp.einsum('bqhd,bmhd->bhqm', q, k).astype(jnp.float32) * (MEM_HEAD_DIM ** -0.5)
    p = jax.nn.softmax(sc, axis=-1).astype(v.dtype)
    o = jnp.einsum('bhqm,bmhd->bqhd', p, v).reshape(bb, s, D_MODEL)
    return o @ wo


def conv_ffn(h, w_gate, w_up, conv_w, conv_b, w_down):
    a = causal_dwconv(h @ w_gate, conv_w) + conv_b
    return (jax.nn.silu(a) * (h @ w_up)) @ w_down


def _fwd_setup_inputs(seed: int = 0) -> dict:
    key = jax.random.key(seed)
    ks = jax.random.split(key, 24)
    f32 = jnp.float32
    nrm = lambda k, shape, scale: jax.random.normal(k, shape, f32) * scale
    gain = lambda k, shape: 1.0 + 0.02 * jax.random.normal(k, shape, f32)
    L = DEPTH
    return {
        "x": nrm(ks[0], (BATCH, SEQ, D_MODEL), 1.0),
        "mem": nrm(ks[1], (BATCH, MEM_LEN, D_MODEL), 1.0),
        "hgrn_lb": nrm(ks[2], (DEPTH + 1, HGRN_WIDTH), 0.1),
        "norm1_w": gain(ks[3], (L, D_MODEL)),
        "w_in": nrm(ks[4], (L, D_MODEL, IN_COLS), D_MODEL ** -0.5),
        "hgrn_norm_w": gain(ks[5], (L, HGRN_VDIM)),
        "sconv_w": nrm(ks[6], (L, SHORT_CONV_K, CONV_CH), SHORT_CONV_K ** -0.5),
        "w_out": nrm(ks[7], (L, D_MODEL, D_MODEL), D_MODEL ** -0.5),
        "norm2_w": gain(ks[8], (L, D_MODEL)),
        "mem_norm_w": gain(ks[9], (L, D_MODEL)),
        "wq": nrm(ks[10], (L, D_MODEL, D_MODEL), D_MODEL ** -0.5),
        "wk": nrm(ks[11], (L, D_MODEL, D_MODEL), D_MODEL ** -0.5),
        "wv": nrm(ks[12], (L, D_MODEL, D_MODEL), D_MODEL ** -0.5),
        "wo": nrm(ks[13], (L, D_MODEL, D_MODEL), D_MODEL ** -0.5),
        "norm3_w": gain(ks[14], (L, D_MODEL)),
        "w_gate": nrm(ks[15], (L, D_MODEL, D_FF), D_MODEL ** -0.5),
        "w_up": nrm(ks[16], (L, D_MODEL, D_FF), D_MODEL ** -0.5),
        "ffn_conv_w": nrm(ks[17], (L, FFN_CONV_K, D_FF), FFN_CONV_K ** -0.5),
        "ffn_conv_b": nrm(ks[18], (L, D_FF), 0.02),
        "w_down": nrm(ks[19], (L, D_FF, D_MODEL), D_FF ** -0.5),
        "final_norm_w": gain(ks[20], (D_MODEL,)),
    }


def _fwd_reference(x, mem, hgrn_lb, norm1_w, w_in, hgrn_norm_w, sconv_w, w_out,
              norm2_w, mem_norm_w, wq, wk, wv, wo, norm3_w, w_gate, w_up,
              ffn_conv_w, ffn_conv_b, w_down, final_norm_w):
    lb_table = jnp.cumsum(jax.nn.softmax(hgrn_lb.astype(jnp.float32), axis=0), axis=0)
    for l in range(DEPTH):
        h = rmsnorm(x, norm1_w[l])
        x = x + hybrid_mixer(h, w_in[l], lb_table[l], hgrn_norm_w[l], sconv_w[l], w_out[l])
        h = rmsnorm(x, norm2_w[l])
        mem_n = rmsnorm(mem, mem_norm_w[l])
        x = x + memory_cross_attention(h, mem_n, wq[l], wk[l], wv[l], wo[l])
        h = rmsnorm(x, norm3_w[l])
        x = x + conv_ffn(h, w_gate[l], w_up[l], ffn_conv_w[l], ffn_conv_b[l], w_down[l])
    return rmsnorm(x, final_norm_w)


import jax as _jax
import jax.numpy as _jnp

TWIN_FORMAT = 'train_step'
FWD_PARAMS = ['x', 'mem', 'hgrn_lb', 'norm1_w', 'w_in', 'hgrn_norm_w', 'sconv_w', 'w_out', 'norm2_w', 'mem_norm_w', 'wq', 'wk', 'wv', 'wo', 'norm3_w', 'w_gate', 'w_up', 'ffn_conv_w', 'ffn_conv_b', 'w_down', 'final_norm_w']
TWIN_WEIGHTS = ['hgrn_lb', 'norm1_w', 'w_in', 'hgrn_norm_w', 'sconv_w', 'w_out', 'norm2_w', 'mem_norm_w', 'wq', 'wk', 'wv', 'wo', 'norm3_w', 'w_gate', 'w_up', 'ffn_conv_w', 'ffn_conv_b', 'w_down', 'final_norm_w']
TWIN_DIFF_INPUT = 'x'
TWIN_INPUTS = ['x', 'mem', 'hgrn_lb', 'norm1_w', 'w_in', 'hgrn_norm_w', 'sconv_w', 'w_out', 'norm2_w', 'mem_norm_w', 'wq', 'wk', 'wv', 'wo', 'norm3_w', 'w_gate', 'w_up', 'ffn_conv_w', 'ffn_conv_b', 'w_down', 'final_norm_w', 'loss_target', 'm_hgrn_lb', 'm_norm1_w', 'm_w_in', 'm_hgrn_norm_w', 'm_sconv_w', 'm_w_out', 'm_norm2_w', 'm_mem_norm_w', 'm_wq', 'm_wk', 'm_wv', 'm_wo', 'm_norm3_w', 'm_w_gate', 'm_w_up', 'm_ffn_conv_w', 'm_ffn_conv_b', 'm_w_down', 'm_final_norm_w', 'v_hgrn_lb', 'v_norm1_w', 'v_w_in', 'v_hgrn_norm_w', 'v_sconv_w', 'v_w_out', 'v_norm2_w', 'v_mem_norm_w', 'v_wq', 'v_wk', 'v_wv', 'v_wo', 'v_norm3_w', 'v_w_gate', 'v_w_up', 'v_ffn_conv_w', 'v_ffn_conv_b', 'v_w_down', 'v_final_norm_w']
TWIN_OUTPUTS = ['loss', 'grad_x', 'grad_hgrn_lb', 'grad_norm1_w', 'grad_w_in', 'grad_hgrn_norm_w', 'grad_sconv_w', 'grad_w_out', 'grad_norm2_w', 'grad_mem_norm_w', 'grad_wq', 'grad_wk', 'grad_wv', 'grad_wo', 'grad_norm3_w', 'grad_w_gate', 'grad_w_up', 'grad_ffn_conv_w', 'grad_ffn_conv_b', 'grad_w_down', 'grad_final_norm_w', 'delta_hgrn_lb', 'delta_norm1_w', 'delta_w_in', 'delta_hgrn_norm_w', 'delta_sconv_w', 'delta_w_out', 'delta_norm2_w', 'delta_mem_norm_w', 'delta_wq', 'delta_wk', 'delta_wv', 'delta_wo', 'delta_norm3_w', 'delta_w_gate', 'delta_w_up', 'delta_ffn_conv_w', 'delta_ffn_conv_b', 'delta_w_down', 'delta_final_norm_w', 'new_m_hgrn_lb', 'new_m_norm1_w', 'new_m_w_in', 'new_m_hgrn_norm_w', 'new_m_sconv_w', 'new_m_w_out', 'new_m_norm2_w', 'new_m_mem_norm_w', 'new_m_wq', 'new_m_wk', 'new_m_wv', 'new_m_wo', 'new_m_norm3_w', 'new_m_w_gate', 'new_m_w_up', 'new_m_ffn_conv_w', 'new_m_ffn_conv_b', 'new_m_w_down', 'new_m_final_norm_w', 'new_v_hgrn_lb', 'new_v_norm1_w', 'new_v_w_in', 'new_v_hgrn_norm_w', 'new_v_sconv_w', 'new_v_w_out', 'new_v_norm2_w', 'new_v_mem_norm_w', 'new_v_wq', 'new_v_wk', 'new_v_wv', 'new_v_wo', 'new_v_norm3_w', 'new_v_w_gate', 'new_v_w_up', 'new_v_ffn_conv_w', 'new_v_ffn_conv_b', 'new_v_w_down', 'new_v_final_norm_w']
TWIN_LEAF_KINDS = {'loss': 'loss', 'grad_x': 'grad_x', 'grad_hgrn_lb': 'grad_w', 'grad_norm1_w': 'grad_w', 'grad_w_in': 'grad_w', 'grad_hgrn_norm_w': 'grad_w', 'grad_sconv_w': 'grad_w', 'grad_w_out': 'grad_w', 'grad_norm2_w': 'grad_w', 'grad_mem_norm_w': 'grad_w', 'grad_wq': 'grad_w', 'grad_wk': 'grad_w', 'grad_wv': 'grad_w', 'grad_wo': 'grad_w', 'grad_norm3_w': 'grad_w', 'grad_w_gate': 'grad_w', 'grad_w_up': 'grad_w', 'grad_ffn_conv_w': 'grad_w', 'grad_ffn_conv_b': 'grad_w', 'grad_w_down': 'grad_w', 'grad_final_norm_w': 'grad_w', 'delta_hgrn_lb': 'delta_w', 'delta_norm1_w': 'delta_w', 'delta_w_in': 'delta_w', 'delta_hgrn_norm_w': 'delta_w', 'delta_sconv_w': 'delta_w', 'delta_w_out': 'delta_w', 'delta_norm2_w': 'delta_w', 'delta_mem_norm_w': 'delta_w', 'delta_wq': 'delta_w', 'delta_wk': 'delta_w', 'delta_wv': 'delta_w', 'delta_wo': 'delta_w', 'delta_norm3_w': 'delta_w', 'delta_w_gate': 'delta_w', 'delta_w_up': 'delta_w', 'delta_ffn_conv_w': 'delta_w', 'delta_ffn_conv_b': 'delta_w', 'delta_w_down': 'delta_w', 'delta_final_norm_w': 'delta_w', 'new_m_hgrn_lb': 'new_m', 'new_m_norm1_w': 'new_m', 'new_m_w_in': 'new_m', 'new_m_hgrn_norm_w': 'new_m', 'new_m_sconv_w': 'new_m', 'new_m_w_out': 'new_m', 'new_m_norm2_w': 'new_m', 'new_m_mem_norm_w': 'new_m', 'new_m_wq': 'new_m', 'new_m_wk': 'new_m', 'new_m_wv': 'new_m', 'new_m_wo': 'new_m', 'new_m_norm3_w': 'new_m', 'new_m_w_gate': 'new_m', 'new_m_w_up': 'new_m', 'new_m_ffn_conv_w': 'new_m', 'new_m_ffn_conv_b': 'new_m', 'new_m_w_down': 'new_m', 'new_m_final_norm_w': 'new_m', 'new_v_hgrn_lb': 'new_v', 'new_v_norm1_w': 'new_v', 'new_v_w_in': 'new_v', 'new_v_hgrn_norm_w': 'new_v', 'new_v_sconv_w': 'new_v', 'new_v_w_out': 'new_v', 'new_v_norm2_w': 'new_v', 'new_v_mem_norm_w': 'new_v', 'new_v_wq': 'new_v', 'new_v_wk': 'new_v', 'new_v_wv': 'new_v', 'new_v_wo': 'new_v', 'new_v_norm3_w': 'new_v', 'new_v_w_gate': 'new_v', 'new_v_w_up': 'new_v', 'new_v_ffn_conv_w': 'new_v', 'new_v_ffn_conv_b': 'new_v', 'new_v_w_down': 'new_v', 'new_v_final_norm_w': 'new_v'}


def _forward(args):
    return _fwd_reference(*[args[k] for k in FWD_PARAMS])


def _output_shape():
    def fwd():
        inp = _fwd_setup_inputs(0)
        return _fwd_reference(*[inp[k] for k in FWD_PARAMS])
    out = _jax.eval_shape(fwd)
    return out.shape, out.dtype

N_MICROBATCH = 1
ADAM_LR = 0.001
ADAM_B1 = 0.9
ADAM_B2 = 0.999
ADAM_EPS = 1e-08
ADAM_WD = 0.01
ADAM_STEP = 10
PER_EXAMPLE_BATCH_AXIS = {'x': 0, 'mem': 0, 'loss_target': 0}
SHARED_INPUTS = []
_WEIGHT_DTYPES = {'hgrn_lb': _jnp.float32, 'norm1_w': _jnp.float32, 'w_in': _jnp.float32, 'hgrn_norm_w': _jnp.float32, 'sconv_w': _jnp.float32, 'w_out': _jnp.float32, 'norm2_w': _jnp.float32, 'mem_norm_w': _jnp.float32, 'wq': _jnp.float32, 'wk': _jnp.float32, 'wv': _jnp.float32, 'wo': _jnp.float32, 'norm3_w': _jnp.float32, 'w_gate': _jnp.float32, 'w_up': _jnp.float32, 'ffn_conv_w': _jnp.float32, 'ffn_conv_b': _jnp.float32, 'w_down': _jnp.float32, 'final_norm_w': _jnp.float32}
MOMENT_SCALE = {'hgrn_lb': 5.969280e-03, 'norm1_w': 1.468258e-01, 'w_in': 7.708765e-02, 'hgrn_norm_w': 2.021445e-01, 'sconv_w': 1.093773e-01, 'w_out': 8.694901e-02, 'norm2_w': 1.085461e-02, 'mem_norm_w': 1.637216e-02, 'wq': 1.078485e-02, 'wk': 1.078590e-02, 'wv': 1.101811e-02, 'wo': 1.105272e-02, 'norm3_w': 7.736487e-02, 'w_gate': 3.313459e-02, 'w_up': 3.217536e-02, 'ffn_conv_w': 3.316979e-02, 'ffn_conv_b': 3.147014e-02, 'w_down': 5.338289e-02, 'final_norm_w': 3.198119e+01}


def _to_microbatches(a, axis):
    t = _jnp.moveaxis(a, axis, 0)
    t = t.reshape((N_MICROBATCH, t.shape[0] // N_MICROBATCH) + t.shape[1:])
    return _jnp.moveaxis(t, 1, axis + 1)


def setup_inputs(seed: int = 0) -> dict:
    inp = _fwd_setup_inputs(seed)
    key = _jax.random.fold_in(_jax.random.key(seed), 7919)
    shape, _ = _output_shape()
    out = dict(inp)
    out["loss_target"] = _jax.random.normal(_jax.random.fold_in(key, 0), shape, _jnp.float32)
    for i, name in enumerate(TWIN_WEIGHTS):
        w = inp[name].astype(_jnp.float32)
        if MOMENT_SCALE is None:
            s = _jnp.sqrt(_jnp.mean(_jnp.square(w)) + 1e-30)
        else:
            s = MOMENT_SCALE[name]
        km, kv = _jax.random.split(_jax.random.fold_in(key, i + 1))
        out[name] = w
        out["m_" + name] = s * _jax.random.normal(km, w.shape, _jnp.float32)
        out["v_" + name] = (s * s) * _jax.random.uniform(kv, w.shape, _jnp.float32, 0.5, 1.5)
    if N_MICROBATCH > 1:
        for name, axis in PER_EXAMPLE_BATCH_AXIS.items():
            out[name] = _to_microbatches(out[name], axis)
    return {'x': out['x'], 'mem': out['mem'], 'hgrn_lb': out['hgrn_lb'], 'norm1_w': out['norm1_w'], 'w_in': out['w_in'], 'hgrn_norm_w': out['hgrn_norm_w'], 'sconv_w': out['sconv_w'], 'w_out': out['w_out'], 'norm2_w': out['norm2_w'], 'mem_norm_w': out['mem_norm_w'], 'wq': out['wq'], 'wk': out['wk'], 'wv': out['wv'], 'wo': out['wo'], 'norm3_w': out['norm3_w'], 'w_gate': out['w_gate'], 'w_up': out['w_up'], 'ffn_conv_w': out['ffn_conv_w'], 'ffn_conv_b': out['ffn_conv_b'], 'w_down': out['w_down'], 'final_norm_w': out['final_norm_w'], 'loss_target': out['loss_target'], 'm_hgrn_lb': out['m_hgrn_lb'], 'm_norm1_w': out['m_norm1_w'], 'm_w_in': out['m_w_in'], 'm_hgrn_norm_w': out['m_hgrn_norm_w'], 'm_sconv_w': out['m_sconv_w'], 'm_w_out': out['m_w_out'], 'm_norm2_w': out['m_norm2_w'], 'm_mem_norm_w': out['m_mem_norm_w'], 'm_wq': out['m_wq'], 'm_wk': out['m_wk'], 'm_wv': out['m_wv'], 'm_wo': out['m_wo'], 'm_norm3_w': out['m_norm3_w'], 'm_w_gate': out['m_w_gate'], 'm_w_up': out['m_w_up'], 'm_ffn_conv_w': out['m_ffn_conv_w'], 'm_ffn_conv_b': out['m_ffn_conv_b'], 'm_w_down': out['m_w_down'], 'm_final_norm_w': out['m_final_norm_w'], 'v_hgrn_lb': out['v_hgrn_lb'], 'v_norm1_w': out['v_norm1_w'], 'v_w_in': out['v_w_in'], 'v_hgrn_norm_w': out['v_hgrn_norm_w'], 'v_sconv_w': out['v_sconv_w'], 'v_w_out': out['v_w_out'], 'v_norm2_w': out['v_norm2_w'], 'v_mem_norm_w': out['v_mem_norm_w'], 'v_wq': out['v_wq'], 'v_wk': out['v_wk'], 'v_wv': out['v_wv'], 'v_wo': out['v_wo'], 'v_norm3_w': out['v_norm3_w'], 'v_w_gate': out['v_w_gate'], 'v_w_up': out['v_w_up'], 'v_ffn_conv_w': out['v_ffn_conv_w'], 'v_ffn_conv_b': out['v_ffn_conv_b'], 'v_w_down': out['v_w_down'], 'v_final_norm_w': out['v_final_norm_w']}


def _loss(weights, diff, rest, loss_target):
    with _jax.named_scope("forward"):
        args = {**rest, TWIN_DIFF_INPUT: diff, **{k: w.astype(_WEIGHT_DTYPES[k]) for k, w in weights.items()}}
        y = _forward(args)
    with _jax.named_scope("loss_head"):
        err = _jnp.square(y.astype(_jnp.float32) - loss_target)
        return 0.5 * _jnp.sum(_jnp.mean(err, axis=-1)) if err.ndim else 0.5 * err


def _adamw(w, g, m, v):
    m = ADAM_B1 * m + (1.0 - ADAM_B1) * g
    v = ADAM_B2 * v + (1.0 - ADAM_B2) * _jnp.square(g)
    m_hat = m / (1.0 - ADAM_B1 ** ADAM_STEP)
    v_hat = v / (1.0 - ADAM_B2 ** ADAM_STEP)
    delta = -ADAM_LR * (m_hat / (_jnp.sqrt(v_hat) + ADAM_EPS) + ADAM_WD * w)
    return delta, m, v


def reference(x, mem, hgrn_lb, norm1_w, w_in, hgrn_norm_w, sconv_w, w_out, norm2_w, mem_norm_w, wq, wk, wv, wo, norm3_w, w_gate, w_up, ffn_conv_w, ffn_conv_b, w_down, final_norm_w, loss_target, m_hgrn_lb, m_norm1_w, m_w_in, m_hgrn_norm_w, m_sconv_w, m_w_out, m_norm2_w, m_mem_norm_w, m_wq, m_wk, m_wv, m_wo, m_norm3_w, m_w_gate, m_w_up, m_ffn_conv_w, m_ffn_conv_b, m_w_down, m_final_norm_w, v_hgrn_lb, v_norm1_w, v_w_in, v_hgrn_norm_w, v_sconv_w, v_w_out, v_norm2_w, v_mem_norm_w, v_wq, v_wk, v_wv, v_wo, v_norm3_w, v_w_gate, v_w_up, v_ffn_conv_w, v_ffn_conv_b, v_w_down, v_final_norm_w):
    given = dict(x=x, mem=mem, hgrn_lb=hgrn_lb, norm1_w=norm1_w, w_in=w_in, hgrn_norm_w=hgrn_norm_w, sconv_w=sconv_w, w_out=w_out, norm2_w=norm2_w, mem_norm_w=mem_norm_w, wq=wq, wk=wk, wv=wv, wo=wo, norm3_w=norm3_w, w_gate=w_gate, w_up=w_up, ffn_conv_w=ffn_conv_w, ffn_conv_b=ffn_conv_b, w_down=w_down, final_norm_w=final_norm_w, loss_target=loss_target, m_hgrn_lb=m_hgrn_lb, m_norm1_w=m_norm1_w, m_w_in=m_w_in, m_hgrn_norm_w=m_hgrn_norm_w, m_sconv_w=m_sconv_w, m_w_out=m_w_out, m_norm2_w=m_norm2_w, m_mem_norm_w=m_mem_norm_w, m_wq=m_wq, m_wk=m_wk, m_wv=m_wv, m_wo=m_wo, m_norm3_w=m_norm3_w, m_w_gate=m_w_gate, m_w_up=m_w_up, m_ffn_conv_w=m_ffn_conv_w, m_ffn_conv_b=m_ffn_conv_b, m_w_down=m_w_down, m_final_norm_w=m_final_norm_w, v_hgrn_lb=v_hgrn_lb, v_norm1_w=v_norm1_w, v_w_in=v_w_in, v_hgrn_norm_w=v_hgrn_norm_w, v_sconv_w=v_sconv_w, v_w_out=v_w_out, v_norm2_w=v_norm2_w, v_mem_norm_w=v_mem_norm_w, v_wq=v_wq, v_wk=v_wk, v_wv=v_wv, v_wo=v_wo, v_norm3_w=v_norm3_w, v_w_gate=v_w_gate, v_w_up=v_w_up, v_ffn_conv_w=v_ffn_conv_w, v_ffn_conv_b=v_ffn_conv_b, v_w_down=v_w_down, v_final_norm_w=v_final_norm_w)
    weights = {n: given[n] for n in TWIN_WEIGHTS}
    shared = {n: given[n] for n in SHARED_INPUTS}
    per_example = {n: given[n] for n in ['x', 'mem']}
    grad_fn = _jax.value_and_grad(_loss, argnums=(0, 1))

    def one_microbatch(ex, loss_target):
        ex = dict(ex)
        diff = ex.pop(TWIN_DIFF_INPUT)
        return grad_fn(weights, diff, {**shared, **ex}, loss_target)

    if N_MICROBATCH == 1:
        loss, (grad_w, grad_x) = one_microbatch(per_example, given["loss_target"])
    else:
        def body(carry, xs):
            loss_sum, grad_sum = carry
            l_k, (gw_k, gx_k) = one_microbatch(xs[0], xs[1])
            with _jax.named_scope("update"):
                return (loss_sum + l_k, _jax.tree.map(_jnp.add, grad_sum, gw_k)), gx_k

        init = (_jnp.zeros((), _jnp.float32), _jax.tree.map(_jnp.zeros_like, weights))
        (loss, grad_w), grad_x = _jax.lax.scan(body, init, (per_example, given["loss_target"]))
    with _jax.named_scope("update"):
        delta_w, new_m, new_v = {}, {}, {}
        for n in TWIN_WEIGHTS:
            delta_w[n], new_m[n], new_v[n] = _adamw(weights[n], grad_w[n], given["m_" + n], given["v_" + n])
    return (loss, grad_x, *[grad_w[n] for n in TWIN_WEIGHTS], *[delta_w[n] for n in TWIN_WEIGHTS],
            *[new_m[n] for n in TWIN_WEIGHTS], *[new_v[n] for n in TWIN_WEIGHTS])
```

```python
import functools
import math

import jax
import jax.numpy as jnp
from jax import lax
from jax.experimental import pallas as pl
from jax.experimental.pallas import tpu as pltpu

F32 = jnp.float32
BF16 = jnp.bfloat16
MESH = pl.DeviceIdType.MESH
N_DEV = 8
EPS = 1e-6
CHUNK = 64
HGRN_HEADS = 8
HEAD_DIM = 128
MEM_HEADS = 4
GROUP_W = HGRN_HEADS * HEAD_DIM
N_GROUPS = 7
HALO = 16
TS = 512
TR = 256
TM = 512
TN = 1024
TK = 1024
VMEM_LIMIT = 48 * 1024 * 1024

ADAM_LR = 0.001
ADAM_B1 = 0.9
ADAM_B2 = 0.999
ADAM_EPS = 1e-08
ADAM_WD = 0.01
ADAM_STEP = 10

_DIMS = {
    "nn": (((1,), (0,)), ((), ())),
    "nt": (((1,), (1,)), ((), ())),
    "tn": (((0,), (0,)), ((), ())),
}


def _params(n_axes):
    return pltpu.CompilerParams(dimension_semantics=("arbitrary",) * n_axes, vmem_limit_bytes=VMEM_LIMIT)


def _dot(a, b, mode):
    return lax.dot_general(a.astype(BF16), b.astype(BF16), _DIMS[mode], preferred_element_type=F32)


def _sigmoid(v):
    return 1.0 / (1.0 + jnp.exp(-v))


def _block_dims(spec):
    return tuple(d for d in spec.block_shape if d is not None)


def _matmul(name, grid, k_axis, pairs, outs, mode, residual=None):
    n_pairs, n_out, has_res = len(pairs), len(outs), residual is not None
    k_steps = grid[k_axis] if k_axis is not None else 1

    def body(*refs):
        ins = refs[: 2 * n_pairs]
        pos = 2 * n_pairs
        res_ref = refs[pos] if has_res else None
        pos += int(has_res)
        out_refs = refs[pos : pos + n_out]
        acc_refs = refs[pos + n_out :]

        def partial(o):
            tot = None
            for p in outs[o][3]:
                d = _dot(ins[2 * p][...], ins[2 * p + 1][...], mode)
                tot = d if tot is None else tot + d
            return tot

        if k_steps == 1:
            for o in range(n_out):
                val = partial(o)
                if has_res and o == 0:
                    val = val + res_ref[...]
                out_refs[o][...] = val.astype(out_refs[o].dtype)
        else:
            k = pl.program_id(k_axis)

            @pl.when(k == 0)
            def _():
                for o in range(n_out):
                    if has_res and o == 0:
                        acc_refs[o][...] = res_ref[...]
                    else:
                        acc_refs[o][...] = jnp.zeros_like(acc_refs[o])

            for o in range(n_out):
                acc_refs[o][...] += partial(o)

            @pl.when(k == k_steps - 1)
            def _():
                for o in range(n_out):
                    out_refs[o][...] = acc_refs[o][...].astype(out_refs[o].dtype)

    args, in_specs = [], []
    for a, a_spec, b, b_spec in pairs:
        args += [a, b]
        in_specs += [a_spec, b_spec]
    if has_res:
        args.append(residual[0])
        in_specs.append(residual[1])
    scratch = [pltpu.VMEM(_block_dims(o[2]), F32) for o in outs] if k_steps > 1 else []
    res = pl.pallas_call(
        body,
        name=name,
        grid=grid,
        in_specs=in_specs,
        out_specs=[o[2] for o in outs],
        out_shape=[jax.ShapeDtypeStruct(o[0], o[1]) for o in outs],
        scratch_shapes=scratch,
        compiler_params=_params(len(grid)),
    )(*args)
    return res


def _mm_nn(name, a, w, out_dtype, residual=None):
    m, k = a.shape
    n = w.shape[1]
    tm, tn = min(TM, m), min(TN, n)
    pairs = [(a, pl.BlockSpec((tm, k), lambda i, j: (i, 0)), w, pl.BlockSpec((k, tn), lambda i, j: (0, j)))]
    o_spec = pl.BlockSpec((tm, tn), lambda i, j: (i, j))
    res = (residual, o_spec) if residual is not None else None
    return _matmul(name, (m // tm, n // tn), None, pairs, [((m, n), out_dtype, o_spec, [0])], "nn", res)[0]


def _mm_nt(name, pairs_in, out_dtype):
    m, k = pairs_in[0][0].shape
    n = pairs_in[0][1].shape[0]
    tm, tn = min(TM, m), min(TN, n)
    pairs = [
        (a, pl.BlockSpec((tm, k), lambda i, j: (i, 0)), w, pl.BlockSpec((tn, k), lambda i, j: (j, 0)))
        for a, w in pairs_in
    ]
    o_spec = pl.BlockSpec((tm, tn), lambda i, j: (i, j))
    outs = [((m, n), out_dtype, o_spec, list(range(len(pairs))))]
    return _matmul(name, (m // tm, n // tn), None, pairs, outs, "nt")[0]


def _mm_tn(name, a, b):
    s, i_dim = a.shape
    n = b.shape[1]
    ts, ti, tn = min(TK, s), min(TN, i_dim), min(TN, n)
    pairs = [(a, pl.BlockSpec((ts, ti), lambda i, j, k: (k, i)), b, pl.BlockSpec((ts, tn), lambda i, j, k: (k, j)))]
    outs = [((i_dim, n), F32, pl.BlockSpec((ti, tn), lambda i, j, k: (i, j)), [0])]
    return _matmul(name, (i_dim // ti, n // tn, s // ts), 2, pairs, outs, "tn")[0]


def _mm_proj(h1, w_in):
    s, d = h1.shape
    g = w_in.shape[1] // GROUP_W
    tm = min(TM, s)
    pairs = [(h1, pl.BlockSpec((tm, d), lambda i, j: (i, 0)), w_in, pl.BlockSpec((d, GROUP_W), lambda i, j: (0, j)))]
    outs = [((g, s, GROUP_W), F32, pl.BlockSpec((None, tm, GROUP_W), lambda i, j: (j, i, 0)), [0])]
    return _matmul("mm_proj", (s // tm, g), None, pairs, outs, "nn")[0]


def _mm_gate_up(h3, wg_g, wu_g):
    s, d = h3.shape
    f = wg_g.shape[2]
    tm = min(TM, s)
    a_spec = pl.BlockSpec((tm, d), lambda i, j: (i, 0))
    w_spec = pl.BlockSpec((None, d, f), lambda i, j: (j, 0, 0))
    o_spec = pl.BlockSpec((None, tm, f), lambda i, j: (j, i, 0))
    pairs = [(h3, a_spec, wg_g, w_spec), (h3, a_spec, wu_g, w_spec)]
    outs = [((N_DEV, s, f), F32, o_spec, [0]), ((N_DEV, s, f), F32, o_spec, [1])]
    return _matmul("mm_gate_up", (s // tm, N_DEV), None, pairs, outs, "nn")


def _mm_down(act, wd_g, x2):
    _, s, f = act.shape
    d = wd_g.shape[2]
    tm, tn = min(TM, s), d
    pairs = [
        (
            act,
            pl.BlockSpec((None, tm, f), lambda i, j, k: (k, i, 0)),
            wd_g,
            pl.BlockSpec((None, f, tn), lambda i, j, k: (k, 0, j)),
        )
    ]
    o_spec = pl.BlockSpec((tm, tn), lambda i, j, k: (i, j))
    outs = [((s, d), F32, o_spec, [0])]
    return _matmul("mm_down", (s // tm, d // tn, N_DEV), 2, pairs, outs, "nn", (x2, o_spec))[0]


def _mm_dact(dx3b, wd_g):
    s, d = dx3b.shape
    f = wd_g.shape[1]
    tm = min(TM, s)
    pairs = [(dx3b, pl.BlockSpec((tm, d), lambda i, j: (i, 0)), wd_g, pl.BlockSpec((None, f, d), lambda i, j: (j, 0, 0)))]
    outs = [((N_DEV, s, f), BF16, pl.BlockSpec((None, tm, f), lambda i, j: (j, i, 0)), [0])]
    return _matmul("mm_dact", (s // tm, N_DEV), None, pairs, outs, "nt")[0]


def _mm_dwdown(act, dx3b):
    _, s, f = act.shape
    d = dx3b.shape[1]
    ts, tn = min(TK, s), min(TN, d)
    pairs = [
        (
            act,
            pl.BlockSpec((None, ts, f), lambda j, n, k: (j, k, 0)),
            dx3b,
            pl.BlockSpec((ts, tn), lambda j, n, k: (k, n)),
        )
    ]
    outs = [((N_DEV, f, d), F32, pl.BlockSpec((None, f, tn), lambda j, n, k: (j, 0, n)), [0])]
    return _matmul("mm_dwdown", (N_DEV, d // tn, s // ts), 2, pairs, outs, "tn")[0]


def _mm_dh3(dapre, dup, wg_g, wu_g):
    _, s, f = dapre.shape
    d = wg_g.shape[1]
    tm, tn = min(TM, s), d
    a_spec = pl.BlockSpec((None, tm, f), lambda i, j, k: (k, i, 0))
    w_spec = pl.BlockSpec((None, tn, f), lambda i, j, k: (k, j, 0))
    pairs = [(dapre, a_spec, wg_g, w_spec), (dup, a_spec, wu_g, w_spec)]
    outs = [((s, d), F32, pl.BlockSpec((tm, tn), lambda i, j, k: (i, j)), [0, 1])]
    return _matmul("mm_dh3", (s // tm, d // tn, N_DEV), 2, pairs, outs, "nt")[0]


def _mm_dw_cols(name, h, dcols_list):
    s, d = h.shape
    f = dcols_list[0].shape[2]
    ts, ti = min(TK, s), min(TN, d)
    a_spec = pl.BlockSpec((ts, ti), lambda j, i, k: (k, i))
    b_spec = pl.BlockSpec((None, ts, f), lambda j, i, k: (j, k, 0))
    o_spec = pl.BlockSpec((None, ti, f), lambda j, i, k: (j, i, 0))
    pairs = [(h, a_spec, dc, b_spec) for dc in dcols_list]
    outs = [((N_DEV, d, f), F32, o_spec, [p]) for p in range(len(pairs))]
    return _matmul(name, (N_DEV, d // ti, s // ts), 2, pairs, outs, "tn")


def _mm_dwin(h1, dproj, groups):
    s, d = h1.shape
    ts, ti = min(TK, s), min(TN, d)
    pairs = [
        (
            h1,
            pl.BlockSpec((ts, ti), lambda j, i, k: (k, i)),
            dproj,
            pl.BlockSpec((None, ts, GROUP_W), lambda j, i, k: (j, k, 0)),
        )
    ]
    outs = [((d, groups * GROUP_W), F32, pl.BlockSpec((ti, GROUP_W), lambda j, i, k: (i, j)), [0])]
    return _matmul("mm_dwin", (groups, d // ti, s // ts), 2, pairs, outs, "tn")[0]


def _mm_dh1(dproj, w_in, groups):
    s = dproj.shape[1]
    d = w_in.shape[0]
    tm, tn = min(TM, s), d
    pairs = [
        (
            dproj,
            pl.BlockSpec((None, tm, GROUP_W), lambda i, j, k: (k, i, 0)),
            w_in,
            pl.BlockSpec((tn, GROUP_W), lambda i, j, k: (j, k)),
        )
    ]
    outs = [((s, d), F32, pl.BlockSpec((tm, tn), lambda i, j, k: (i, j)), [0])]
    return _matmul("mm_dh1", (s // tm, d // tn, groups), 2, pairs, outs, "nt")[0]


def _rows_to_8(v):
    t, d = v.shape
    return jnp.sum(v.reshape(t // 8, 8, d), axis=0)


def _rmsnorm(name, x, w):
    s, d = x.shape
    t = min(TS, s)

    def body(x_ref, w_ref, o_ref):
        xv = x_ref[...]
        r = lax.rsqrt(jnp.mean(xv * xv, axis=-1, keepdims=True) + EPS)
        o_ref[...] = (xv * r * w_ref[...]).astype(o_ref.dtype)

    return pl.pallas_call(
        body,
        name=name,
        grid=(s // t,),
        in_specs=[pl.BlockSpec((t, d), lambda i: (i, 0)), pl.BlockSpec((1, d), lambda i: (0, 0))],
        out_specs=pl.BlockSpec((t, d), lambda i: (i, 0)),
        out_shape=jax.ShapeDtypeStruct((s, d), BF16),
        compiler_params=_params(1),
    )(x, w)


def _rmsnorm_bwd(name, dh, x, w, res=None):
    s, d = x.shape
    t = min(TR, s)
    steps = s // t
    has_res = res is not None

    def body(*refs):
        if has_res:
            dh_ref, x_ref, w_ref, res_ref, dx_ref, dxb_ref, dw_ref, acc = refs
        else:
            dh_ref, x_ref, w_ref, dx_ref, dxb_ref, dw_ref, acc = refs
        i = pl.program_id(0)
        xv = x_ref[...]
        r = lax.rsqrt(jnp.mean(xv * xv, axis=-1, keepdims=True) + EPS)
        xn = xv * r
        dhv = dh_ref[...].astype(F32)
        dxn = dhv * w_ref[...]
        dx = r * (dxn - xn * jnp.mean(dxn * xn, axis=-1, keepdims=True))
        if has_res:
            dx = dx + res_ref[...]
        dx_ref[...] = dx
        dxb_ref[...] = dx.astype(BF16)

        @pl.when(i == 0)
        def _():
            acc[...] = jnp.zeros_like(acc)

        acc[...] += _rows_to_8(dhv * xn)

        @pl.when(i == steps - 1)
        def _():
            dw_ref[...] = jnp.sum(acc[...], axis=0, keepdims=True)

    row = pl.BlockSpec((t, d), lambda i: (i, 0))
    vec = pl.BlockSpec((1, d), lambda i: (0, 0))
    args = [dh, x, w] + ([res] if has_res else [])
    return pl.pallas_call(
        body,
        name=name,
        grid=(steps,),
        in_specs=[row, row, vec] + ([row] if has_res else []),
        out_specs=[row, row, vec],
        out_shape=[
            jax.ShapeDtypeStruct((s, d), F32),
            jax.ShapeDtypeStruct((s, d), BF16),
            jax.ShapeDtypeStruct((1, d), F32),
        ],
        scratch_shapes=[pltpu.VMEM((8, d), F32)],
        compiler_params=_params(1),
    )(*args)


def _loss_head(x3, target, w):
    s, d = x3.shape
    t = min(TR, s)
    steps = s // t

    def body(x_ref, t_ref, w_ref, dx_ref, dxb_ref, loss_ref, dw_ref, acc_l, acc_w):
        i = pl.program_id(0)
        xv = x_ref[...]
        wv = w_ref[...]
        r = lax.rsqrt(jnp.mean(xv * xv, axis=-1, keepdims=True) + EPS)
        xn = xv * r
        err = xn * wv - t_ref[...]
        dy = err * (1.0 / d)
        dxn = dy * wv
        dx = r * (dxn - xn * jnp.mean(dxn * xn, axis=-1, keepdims=True))
        dx_ref[...] = dx
        dxb_ref[...] = dx.astype(BF16)

        @pl.when(i == 0)
        def _():
            acc_l[...] = jnp.zeros_like(acc_l)
            acc_w[...] = jnp.zeros_like(acc_w)

        acc_l[...] += _rows_to_8(err * err)
        acc_w[...] += _rows_to_8(dy * xn)

        @pl.when(i == steps - 1)
        def _():
            tot = jnp.sum(jnp.sum(acc_l[...], axis=0, keepdims=True), axis=1, keepdims=True) * (0.5 / d)
            loss_ref[...] = jnp.broadcast_to(tot, loss_ref.shape)
            dw_ref[...] = jnp.sum(acc_w[...], axis=0, keepdims=True)

    row = pl.BlockSpec((t, d), lambda i: (i, 0))
    vec = pl.BlockSpec((1, d), lambda i: (0, 0))
    return pl.pallas_call(
        body,
        name="loss_head",
        grid=(steps,),
        in_specs=[row, row, vec],
        out_specs=[row, row, pl.BlockSpec((1, 128), lambda i: (0, 0)), vec],
        out_shape=[
            jax.ShapeDtypeStruct((s, d), F32),
            jax.ShapeDtypeStruct((s, d), BF16),
            jax.ShapeDtypeStruct((1, 128), F32),
            jax.ShapeDtypeStruct((1, d), F32),
        ],
        scratch_shapes=[pltpu.VMEM((8, d), F32), pltpu.VMEM((8, d), F32)],
        compiler_params=_params(1),
    )(x3, target, w)


def _lb_fwd(hgrn_lb):
    def body(p_ref, lb_ref):
        p = p_ref[...]
        m = jnp.max(p, axis=0, keepdims=True)
        e = jnp.exp(p - m)
        lb_ref[...] = e[0:1, :] / jnp.sum(e, axis=0, keepdims=True)

    return pl.pallas_call(body, name="lb_fwd", out_shape=jax.ShapeDtypeStruct((1, hgrn_lb.shape[1]), F32))(hgrn_lb)


def _lb_bwd(hgrn_lb, dlb):
    def body(p_ref, d_ref, o_ref):
        p = p_ref[...]
        m = jnp.max(p, axis=0, keepdims=True)
        e = jnp.exp(p - m)
        sm = e / jnp.sum(e, axis=0, keepdims=True)
        p0 = sm[0:1, :]
        row = lax.broadcasted_iota(jnp.int32, sm.shape, 0)
        o_ref[...] = d_ref[...] * p0 * (jnp.where(row == 0, 1.0, 0.0) - sm)

    return pl.pallas_call(body, name="lb_bwd", out_shape=jax.ShapeDtypeStruct(hgrn_lb.shape, F32))(hgrn_lb, dlb)


def _split3(v):
    hi = v.astype(BF16)
    r1 = v - hi.astype(F32)
    mid = r1.astype(BF16)
    lo = (r1 - mid.astype(F32)).astype(BF16)
    return hi, mid, lo


def _pair(v):
    hi = v.astype(BF16)
    return hi, (v - hi.astype(F32)).astype(BF16)


def _dot_pairs(a2, b2, mode):
    dims = _DIMS[mode]
    out = lax.dot_general(a2[0], b2[1], dims, preferred_element_type=F32)
    out = out + lax.dot_general(a2[1], b2[0], dims, preferred_element_type=F32)
    return out + lax.dot_general(a2[0], b2[0], dims, preferred_element_type=F32)


def _tri_dot(tri, v):
    hi, mid, lo = _split3(v)
    dims = _DIMS["nn"]
    out = lax.dot_general(tri, lo, dims, preferred_element_type=F32)
    out = out + lax.dot_general(tri, mid, dims, preferred_element_type=F32)
    return out + lax.dot_general(tri, hi, dims, preferred_element_type=F32)


def _chunk_gates(qp, fp, lbv, tri):
    sf = _sigmoid(fp)
    f = lbv + (1.0 - lbv) * sf
    k = 1.0 - f
    sq = _sigmoid(qp)
    qf = qp * sq
    b = _tri_dot(tri, jnp.log(f))
    row = lax.broadcasted_iota(jnp.int32, b.shape, 0)
    bm = jnp.sum(jnp.where(row == CHUNK // 2 - 1, b, 0.0), axis=0, keepdims=True)
    bl = jnp.sum(jnp.where(row == CHUNK - 1, b, 0.0), axis=0, keepdims=True)
    e_b = jnp.exp(b)
    e_qm = jnp.exp(b - bm)
    e_km = jnp.exp(bm - b)
    e_kl = jnp.exp(bl - b)
    e_l = jnp.exp(bl)
    return sf, f, k, sq, qf, e_b, e_qm, e_km, e_kl, e_l


def _hgrn_fwd(proj, lb, nw):
    s = proj.shape[1]
    t = min(TS, s)
    nt, nch = s // t, t // CHUNK
    h_, k_ = HGRN_HEADS, HEAD_DIM

    def body(q_ref, f_ref, i_ref, g_ref, lb_ref, nw_ref, mix_ref, o_ref, st_ref, state):
        @pl.when(pl.program_id(1) == 0)
        def _():
            state[...] = jnp.zeros_like(state)

        lbv = lb_ref[...]
        nwv = nw_ref[...]
        rr = lax.broadcasted_iota(jnp.int32, (CHUNK, CHUNK), 0)
        cc = lax.broadcasted_iota(jnp.int32, (CHUNK, CHUNK), 1)
        causal = rr >= cc
        tri = jnp.where(causal, 1.0, 0.0).astype(BF16)

        def chunk(c, carry):
            rows = pl.ds(pl.multiple_of(c * CHUNK, CHUNK), CHUNK)
            qp, fp, v, gp = q_ref[rows, :], f_ref[rows, :], i_ref[rows, :], g_ref[rows, :]
            _, _, k, _, qf, e_b, e_qm, e_km, e_kl, e_l = _chunk_gates(qp, fp, lbv, tri)
            st = state[...]
            st_ref[c] = st
            vb = v.astype(BF16)
            a = jnp.where(causal, _dot(qf * e_qm, k * e_km, "nt"), 0.0)
            o = _dot(qf * e_b, st, "nt") + _dot(a, vb, "nn")
            state[...] = st * e_l + _dot(vb, k * e_kl, "tn")
            o_ref[rows, :] = o
            on = o * lax.rsqrt(jnp.mean(o * o, axis=-1, keepdims=True) + EPS) * nwv
            mix_ref[rows, :] = (on * (gp * _sigmoid(gp))).astype(BF16)
            return carry

        lax.fori_loop(0, nch, chunk, 0)

    def col(g):
        return pl.BlockSpec((None, t, k_), lambda h, i: (g, i, h))

    return pl.pallas_call(
        body,
        name="hgrn_fwd",
        grid=(h_, nt),
        in_specs=[
            col(0),
            col(1),
            col(2),
            col(3),
            pl.BlockSpec((1, k_), lambda h, i: (0, h)),
            pl.BlockSpec((1, k_), lambda h, i: (0, 0)),
        ],
        out_specs=[
            pl.BlockSpec((t, k_), lambda h, i: (i, h)),
            pl.BlockSpec((t, k_), lambda h, i: (i, h)),
            pl.BlockSpec((nch, None, k_, k_), lambda h, i: (i, h, 0, 0)),
        ],
        out_shape=[
            jax.ShapeDtypeStruct((s, 2 * h_ * k_), BF16),
            jax.ShapeDtypeStruct((s, h_ * k_), F32),
            jax.ShapeDtypeStruct((s // CHUNK, h_, k_, k_), F32),
        ],
        scratch_shapes=[pltpu.VMEM((k_, k_), F32)],
        compiler_params=_params(2),
    )(proj, proj, proj, proj, lb, nw)


def _hgrn_bwd(proj, lb, nw, o_pre, states, dmix):
    s = proj.shape[1]
    t = min(TS, s)
    nt, nch = s // t, t // CHUNK
    h_, k_ = HGRN_HEADS, HEAD_DIM

    def body(q_ref, f_ref, i_ref, g_ref, lb_ref, nw_ref, o_ref, st_ref, dm_ref, dp_ref, dlb_ref, dnw_ref, dstate, acc_lb, acc_nw):
        h, i = pl.program_id(0), pl.program_id(1)
        dq_ref, df_ref, di_ref, dg_ref = dp_ref.at[0], dp_ref.at[1], dp_ref.at[2], dp_ref.at[3]

        @pl.when(i == 0)
        def _():
            dstate[...] = jnp.zeros_like(dstate)
            acc_lb[...] = jnp.zeros_like(acc_lb)

        @pl.when((i == 0) & (h == 0))
        def _():
            acc_nw[...] = jnp.zeros_like(acc_nw)

        lbv = lb_ref[...]
        nwv = nw_ref[...]
        rr = lax.broadcasted_iota(jnp.int32, (CHUNK, CHUNK), 0)
        cc = lax.broadcasted_iota(jnp.int32, (CHUNK, CHUNK), 1)
        causal = rr >= cc
        tri = jnp.where(causal, 1.0, 0.0).astype(BF16)
        tri_t = jnp.where(cc >= rr, 1.0, 0.0).astype(BF16)
        last_row = lax.broadcasted_iota(jnp.int32, (CHUNK, k_), 0) == CHUNK - 1

        def chunk(cr, carry):
            c = nch - 1 - cr
            rows = pl.ds(pl.multiple_of(c * CHUNK, CHUNK), CHUNK)
            qp, fp, v, gp = q_ref[rows, :], f_ref[rows, :], i_ref[rows, :], g_ref[rows, :]
            sf, f, k, sq, qf, e_b, e_qm, e_km, e_kl, e_l = _chunk_gates(qp, fp, lbv, tri)
            st0 = st_ref[c]
            dst = dstate[...]
            dm = dm_ref[rows, :]
            o = o_ref[rows, :]
            rinv = lax.rsqrt(jnp.mean(o * o, axis=-1, keepdims=True) + EPS)
            ohat = o * rinv
            sg = _sigmoid(gp)
            dg_ref[rows, :] = (dm * (ohat * nwv) * (sg * (1.0 + gp * (1.0 - sg)))).astype(BF16)
            don = dm * (gp * sg)
            acc_nw[...] += _rows_to_8(don * ohat)
            dohat = don * nwv
            do = rinv * (dohat - ohat * jnp.mean(dohat * ohat, axis=-1, keepdims=True))
            do2, v2, dst2 = _pair(do), _pair(v), _pair(dst)
            qm2, km2 = _pair(qf * e_qm), _pair(k * e_km)
            a = jnp.where(causal, _dot(qm2[0], km2[0], "nt"), 0.0)
            da2 = _pair(jnp.where(causal, _dot_pairs(do2, v2, "nt"), 0.0))
            dq = e_b * _dot_pairs(do2, _pair(st0), "nn") + e_qm * _dot_pairs(da2, km2, "nn")
            dk_state = e_kl * _dot_pairs(v2, dst2, "nn")
            dk = e_km * _dot_pairs(da2, qm2, "tn") + dk_state
            dv = _dot(a, do2[0], "tn") + _dot(k * e_kl, dst2[0], "nt")
            d_bl = jnp.sum(k * dk_state, axis=0, keepdims=True) + e_l * jnp.sum(st0 * dst, axis=0, keepdims=True)
            db = qf * dq - k * dk + jnp.where(last_row, d_bl, 0.0)
            dlogf = _tri_dot(tri_t, db)
            dstate[...] = dst * e_l + _dot_pairs(do2, _pair(qf * e_b), "tn")
            dfv = dlogf / f - dk
            df_ref[rows, :] = (dfv * (1.0 - lbv) * (sf * (1.0 - sf))).astype(BF16)
            acc_lb[...] += _rows_to_8(dfv * (1.0 - sf))
            dq_ref[rows, :] = (dq * (sq * (1.0 + qp * (1.0 - sq)))).astype(BF16)
            di_ref[rows, :] = dv.astype(BF16)
            return carry

        lax.fori_loop(0, nch, chunk, 0)

        @pl.when(i == nt - 1)
        def _():
            dlb_ref[...] = jnp.sum(acc_lb[...], axis=0, keepdims=True)

        @pl.when((i == nt - 1) & (h == h_ - 1))
        def _():
            dnw_ref[...] = jnp.sum(acc_nw[...], axis=0, keepdims=True)

    def col(g):
        return pl.BlockSpec((None, t, k_), lambda h, i: (g, nt - 1 - i, h))

    row = pl.BlockSpec((t, k_), lambda h, i: (nt - 1 - i, h))
    return pl.pallas_call(
        body,
        name="hgrn_bwd",
        grid=(h_, nt),
        in_specs=[
            col(0),
            col(1),
            col(2),
            col(3),
            pl.BlockSpec((1, k_), lambda h, i: (0, h)),
            pl.BlockSpec((1, k_), lambda h, i: (0, 0)),
            row,
            pl.BlockSpec((nch, None, k_, k_), lambda h, i: (nt - 1 - i, h, 0, 0)),
            row,
        ],
        out_specs=[
            pl.BlockSpec((4, t, k_), lambda h, i: (0, nt - 1 - i, h)),
            pl.BlockSpec((1, k_), lambda h, i: (0, h)),
            pl.BlockSpec((1, k_), lambda h, i: (0, 0)),
        ],
        out_shape=[
            jax.ShapeDtypeStruct((8, s, h_ * k_), BF16),
            jax.ShapeDtypeStruct((1, h_ * k_), F32),
            jax.ShapeDtypeStruct((1, k_), F32),
        ],
        scratch_shapes=[pltpu.VMEM((k_, k_), F32), pltpu.VMEM((8, k_), F32), pltpu.VMEM((8, k_), F32)],
        compiler_params=_params(2),
    )(proj, proj, proj, proj, lb, nw, o_pre, states, dmix)


def _shift_rows(ext, k):
    n = ext.shape[0]
    return pltpu.roll(ext, k % n, axis=0)


def _row_select(parts):
    w = parts[0].shape[1]
    row = lax.broadcasted_iota(jnp.int32, (8, w), 0)
    out = jnp.zeros((8, w), F32)
    for r, p in enumerate(parts):
        out = out + jnp.where(row == r, p, 0.0)
    return out


def _sconv_fwd(proj, sw, mix):
    s = proj.shape[1]
    t = min(TS, s)
    nt, w = s // t, 512
    ncol, hb = GROUP_W // w, t // HALO

    def body(cb_ref, cc_ref, ch_ref, ccp_ref, chp_ref, sw_ref, mix_in, mix_ref):
        del mix_in
        i = pl.program_id(0)
        u = cc_ref[...] * ch_ref[...]
        up = jnp.where(i > 0, ccp_ref[...] * chp_ref[...], 0.0)
        ext = jnp.concatenate([up, u], axis=0)
        z = sw_ref[0:1, :] * _shift_rows(ext, 2)[HALO:] + sw_ref[1:2, :] * _shift_rows(ext, 1)[HALO:] + sw_ref[2:3, :] * u
        mix_ref[...] = (cb_ref[...] * z).astype(BF16)

    def cur(g):
        return pl.BlockSpec((None, t, w), lambda i, c: (g, i, c))

    def prev(g):
        return pl.BlockSpec((None, HALO, w), lambda i, c: (g, jnp.maximum(i * hb - 1, 0), c))

    return pl.pallas_call(
        body,
        name="sconv_fwd",
        grid=(nt, ncol),
        in_specs=[cur(4), cur(5), cur(6), prev(5), prev(6), pl.BlockSpec((3, w), lambda i, c: (0, c)), pl.BlockSpec(memory_space=pl.ANY)],
        out_specs=pl.BlockSpec((t, w), lambda i, c: (i, ncol + c)),
        out_shape=jax.ShapeDtypeStruct(mix.shape, mix.dtype),
        input_output_aliases={6: 0},
        compiler_params=_params(2),
    )(proj, proj, proj, proj, proj, sw, mix)


def _sconv_bwd(proj, sw, dmix, dproj):
    s = proj.shape[1]
    t = min(TS, s)
    nt, w = s // t, 512
    ncol, hb = GROUP_W // w, t // HALO

    def body(cb_ref, cbn_ref, cc_ref, ccp_ref, ch_ref, chp_ref, dy_ref, dyn_ref, sw_ref, dp_in, dp_ref, dsw_ref, acc):
        del dp_in
        i = pl.program_id(1)
        w0, w1, w2 = sw_ref[0:1, :], sw_ref[1:2, :], sw_ref[2:3, :]
        ccv, chv, cbv, dyv = cc_ref[...], ch_ref[...], cb_ref[...], dy_ref[...]
        u = ccv * chv
        up = jnp.where(i > 0, ccp_ref[...] * chp_ref[...], 0.0)
        ext = jnp.concatenate([up, u], axis=0)
        u1, u2 = _shift_rows(ext, 1)[HALO:], _shift_rows(ext, 2)[HALO:]
        z = w0 * u2 + w1 * u1 + w2 * u
        dz = dyv * cbv
        dzn = jnp.where(i < nt - 1, dyn_ref[...] * cbn_ref[...], 0.0)
        dze = jnp.concatenate([dz, dzn], axis=0)
        du = w2 * dz + w1 * _shift_rows(dze, -1)[:t] + w0 * _shift_rows(dze, -2)[:t]
        dp_ref[0] = (dyv * z).astype(BF16)
        dp_ref[1] = (du * chv).astype(BF16)
        dp_ref[2] = (du * ccv).astype(BF16)
        dp_ref[3] = jnp.zeros((t, w), BF16)

        @pl.when(i == 0)
        def _():
            acc[...] = jnp.zeros_like(acc)

        acc[...] += _row_select([jnp.sum(dz * uu, axis=0, keepdims=True) for uu in (u2, u1, u)])

        @pl.when(i == nt - 1)
        def _():
            dsw_ref[...] = acc[...]

    def cur(g):
        return pl.BlockSpec((None, t, w), lambda c, i: (g, i, c))

    def prev(g):
        return pl.BlockSpec((None, HALO, w), lambda c, i: (g, jnp.maximum(i * hb - 1, 0), c))

    def nxt(g):
        return pl.BlockSpec((None, HALO, w), lambda c, i: (g, jnp.minimum((i + 1) * hb, s // HALO - 1), c))

    return pl.pallas_call(
        body,
        name="sconv_bwd",
        grid=(ncol, nt),
        in_specs=[
            cur(4),
            nxt(4),
            cur(5),
            prev(5),
            cur(6),
            prev(6),
            pl.BlockSpec((t, w), lambda c, i: (i, ncol + c)),
            pl.BlockSpec((HALO, w), lambda c, i: (jnp.minimum((i + 1) * hb, s // HALO - 1), ncol + c)),
            pl.BlockSpec((3, w), lambda c, i: (0, c)),
            pl.BlockSpec(memory_space=pl.ANY),
        ],
        out_specs=[pl.BlockSpec((4, t, w), lambda c, i: (1, i, c)), pl.BlockSpec((8, w), lambda c, i: (0, c))],
        out_shape=[jax.ShapeDtypeStruct(dproj.shape, dproj.dtype), jax.ShapeDtypeStruct((8, GROUP_W), F32)],
        scratch_shapes=[pltpu.VMEM((8, w), F32)],
        input_output_aliases={9: 0},
        compiler_params=_params(2),
    )(proj, proj, proj, proj, proj, proj, dmix, dmix, sw, dproj)


def _ffn_act(a_pre, up, cw, cb):
    nd, s, f = a_pre.shape
    t = min(TS, s)
    nt, hb = s // t, t // HALO

    def body(a_ref, ap_ref, up_ref, cw_ref, cb_ref, o_ref):
        i = pl.program_id(1)
        av = a_ref[...]
        ext = jnp.concatenate([jnp.where(i > 0, ap_ref[...], 0.0), av], axis=0)
        a = cw_ref[0:1, :] * _shift_rows(ext, 2)[HALO:] + cw_ref[1:2, :] * _shift_rows(ext, 1)[HALO:] + cw_ref[2:3, :] * av
        a = a + cb_ref[...]
        o_ref[...] = (a * _sigmoid(a) * up_ref[...]).astype(BF16)

    cur = pl.BlockSpec((None, t, f), lambda j, i: (j, i, 0))
    prev = pl.BlockSpec((None, HALO, f), lambda j, i: (j, jnp.maximum(i * hb - 1, 0), 0))
    return pl.pallas_call(
        body,
        name="ffn_act",
        grid=(nd, nt),
        in_specs=[cur, prev, cur, pl.BlockSpec((None, 3, f), lambda j, i: (j, 0, 0)), pl.BlockSpec((None, 1, f), lambda j, i: (j, 0, 0))],
        out_specs=cur,
        out_shape=jax.ShapeDtypeStruct(a_pre.shape, BF16),
        compiler_params=_params(2),
    )(a_pre, a_pre, up, cw, cb)


def _ffn_act_bwd(a_pre, up, dact, cw, cb):
    nd, s, f = a_pre.shape
    t = min(TS, s)
    nt, hb = s // t, t // HALO

    def body(a_ref, ap_ref, an_ref, up_ref, upn_ref, da_ref, dan_ref, cw_ref, cb_ref, dap_ref, dup_ref, dsm_ref, acc):
        i = pl.program_id(1)
        w0, w1, w2 = cw_ref[0:1, :], cw_ref[1:2, :], cw_ref[2:3, :]
        av = a_ref[...]
        ext = jnp.concatenate([jnp.where(i > 0, ap_ref[...], 0.0), av, an_ref[...]], axis=0)
        e1, e2 = _shift_rows(ext, 1), _shift_rows(ext, 2)
        a = (w0 * e2 + w1 * e1 + w2 * ext)[HALO:] + cb_ref[...]
        sg = _sigmoid(a)
        dact_e = jnp.concatenate([da_ref[...], dan_ref[...]], axis=0).astype(F32)
        up_e = jnp.concatenate([up_ref[...], upn_ref[...]], axis=0)
        row = lax.broadcasted_iota(jnp.int32, a.shape, 0)
        live = (row < t) | (i < nt - 1)
        d_a = jnp.where(live, dact_e * up_e * (sg * (1.0 + a * (1.0 - sg))), 0.0)
        dup_ref[...] = (dact_e * (a * sg))[:t].astype(BF16)
        dap_ref[...] = (w2 * d_a + w1 * _shift_rows(d_a, -1) + w0 * _shift_rows(d_a, -2))[:t].astype(BF16)
        d_a_cur = d_a[:t]

        @pl.when(i == 0)
        def _():
            acc[...] = jnp.zeros_like(acc)

        sums = [jnp.sum(d_a_cur * e[HALO : HALO + t], axis=0, keepdims=True) for e in (e2, e1, ext)]
        acc[...] += _row_select(sums + [jnp.sum(d_a_cur, axis=0, keepdims=True)])

        @pl.when(i == nt - 1)
        def _():
            dsm_ref[...] = acc[...]

    cur = pl.BlockSpec((None, t, f), lambda j, i: (j, i, 0))
    prev = pl.BlockSpec((None, HALO, f), lambda j, i: (j, jnp.maximum(i * hb - 1, 0), 0))
    nxt = pl.BlockSpec((None, HALO, f), lambda j, i: (j, jnp.minimum((i + 1) * hb, s // HALO - 1), 0))
    return pl.pallas_call(
        body,
        name="ffn_act_bwd",
        grid=(nd, nt),
        in_specs=[cur, prev, nxt, cur, nxt, cur, nxt, pl.BlockSpec((None, 3, f), lambda j, i: (j, 0, 0)), pl.BlockSpec((None, 1, f), lambda j, i: (j, 0, 0))],
        out_specs=[cur, cur, pl.BlockSpec((None, 8, f), lambda j, i: (j, 0, 0))],
        out_shape=[jax.ShapeDtypeStruct(a_pre.shape, BF16), jax.ShapeDtypeStruct(a_pre.shape, BF16), jax.ShapeDtypeStruct((nd, 8, f), F32)],
        scratch_shapes=[pltpu.VMEM((8, f), F32)],
        compiler_params=_params(2),
    )(a_pre, a_pre, a_pre, up, up, dact, dact, cw, cb)


def _softmax_rows(sc):
    m = jnp.max(sc, axis=-1, keepdims=True)
    e = jnp.exp(sc - m)
    return e / jnp.sum(e, axis=-1, keepdims=True)


def _attn_fwd(q, km, vm):
    s, d = q.shape
    m = km.shape[0]
    t = min(TS, s)
    hd = d // MEM_HEADS
    scale = hd**-0.5

    def body(q_ref, k_ref, v_ref, o_ref):
        for h in range(MEM_HEADS):
            cols = slice(h * hd, (h + 1) * hd)
            p = _softmax_rows(_dot(q_ref[:, cols], k_ref[:, cols], "nt") * scale)
            o_ref[:, cols] = _dot(p, v_ref[:, cols], "nn").astype(BF16)

    row = pl.BlockSpec((t, d), lambda i: (i, 0))
    full = pl.BlockSpec((m, d), lambda i: (0, 0))
    return pl.pallas_call(
        body,
        name="attn_fwd",
        grid=(s // t,),
        in_specs=[row, full, full],
        out_specs=row,
        out_shape=jax.ShapeDtypeStruct((s, d), BF16),
        compiler_params=_params(1),
    )(q, km, vm)


def _attn_bwd(q, km, vm, do):
    s, d = q.shape
    m = km.shape[0]
    t = min(TS, s)
    steps = s // t
    hd = d // MEM_HEADS
    scale = hd**-0.5

    def body(q_ref, k_ref, v_ref, do_ref, dq_ref, dk_ref, dv_ref):
        @pl.when(pl.program_id(0) == 0)
        def _():
            dk_ref[...] = jnp.zeros_like(dk_ref)
            dv_ref[...] = jnp.zeros_like(dv_ref)

        for h in range(MEM_HEADS):
            cols = slice(h * hd, (h + 1) * hd)
            qh, kh, vh, doh = q_ref[:, cols], k_ref[:, cols], v_ref[:, cols], do_ref[:, cols]
            p = _softmax_rows(_dot(qh, kh, "nt") * scale)
            dp = _dot(doh, vh, "nt")
            ds = p * (dp - jnp.sum(dp * p, axis=-1, keepdims=True)) * scale
            dq_ref[:, cols] = _dot(ds, kh, "nn").astype(BF16)
            dk_ref[:, cols] += _dot(ds, qh, "tn")
            dv_ref[:, cols] += _dot(p, doh, "tn")

    row = pl.BlockSpec((t, d), lambda i: (i, 0))
    full = pl.BlockSpec((m, d), lambda i: (0, 0))
    return pl.pallas_call(
        body,
        name="attn_bwd",
        grid=(steps,),
        in_specs=[row, full, full, row],
        out_specs=[row, full, full],
        out_shape=[jax.ShapeDtypeStruct((s, d), BF16), jax.ShapeDtypeStruct((m, d), F32), jax.ShapeDtypeStruct((m, d), F32)],
        compiler_params=_params(1),
    )(q, km, vm, do)


HBM_SPEC = pl.BlockSpec(memory_space=pltpu.HBM)


def _place():
    x, y, c = lax.axis_index("x"), lax.axis_index("y"), lax.axis_index("c")
    return x, y, c, [(1 - x, y), (x, 1 - y), (1 - x, 1 - y)]


def _slab(p):
    return 4 * p[0] + 2 * p[1] + p[2]


def _allgather(name, shards):
    n = len(shards)

    def body(*refs):
        ins, outs = refs[:n], refs[n : 2 * n]
        send_sems, recv_sems, local_sems = refs[2 * n :]
        x, y, c, chips = _place()
        me, sibling = (x, y, c), (x, y, 1 - c)

        def copy(a, k, block, to, src=None):
            dst = outs[a].at[_slab(block)]
            return pltpu.make_async_remote_copy(
                src_ref=dst if src is None else src,
                dst_ref=dst,
                send_sem=send_sems.at[a, k],
                recv_sem=recv_sems.at[a, k],
                device_id=to,
                device_id_type=MESH,
            )

        mine = [pltpu.make_async_copy(ins[a], outs[a].at[_slab(me)], local_sems.at[a]) for a in range(n)]
        first = []
        for a in range(n):
            mine[a].start()
            first.append(copy(a, 0, me, sibling, src=ins[a]))
            first += [copy(a, 1 + j, me, (*chip, c), src=ins[a]) for j, chip in enumerate(chips)]
        for cp in first:
            cp.start()
        passed = []
        for j, chip in enumerate(chips):
            for a in range(n):
                copy(a, 1 + j, (*chip, c), me).wait_recv()
                passed.append(copy(a, 4 + j, (*chip, c), sibling))
                passed[-1].start()
        for a in range(n):
            copy(a, 0, sibling, me).wait_recv()
            for j, chip in enumerate(chips):
                copy(a, 4 + j, (*chip, 1 - c), me).wait_recv()
        for cp in first + passed:
            cp.wait_send()
        for cp in mine:
            cp.wait()

    return pl.pallas_call(
        body,
        name=name,
        in_specs=[HBM_SPEC] * n,
        out_specs=[HBM_SPEC] * n,
        out_shape=[jax.ShapeDtypeStruct((N_DEV, *a.shape), a.dtype) for a in shards],
        scratch_shapes=[pltpu.SemaphoreType.DMA((n, 7)), pltpu.SemaphoreType.DMA((n, 7)), pltpu.SemaphoreType.DMA((n,))],
    )(*shards)


def _rs_sibling(grads):
    n = len(grads)

    def body(*refs):
        ins, outs = refs[:n], refs[n : 2 * n]
        send_sems, recv_sems = refs[2 * n :]
        x, y, c, chips = _place()
        rel = [(x, y)] + chips
        copies = []
        for a in range(n):
            for r, chip in enumerate(rel):
                copies.append(
                    pltpu.make_async_remote_copy(
                        src_ref=ins[a].at[_slab((*chip, 1 - c))],
                        dst_ref=outs[a].at[r],
                        send_sem=send_sems.at[a, r],
                        recv_sem=recv_sems.at[a, r],
                        device_id=(x, y, 1 - c),
                        device_id_type=MESH,
                    )
                )
        for cp in copies:
            cp.start()
        for cp in copies:
            cp.wait_recv()
        for cp in copies:
            cp.wait_send()

    return pl.pallas_call(
        body,
        name="rs_sibling",
        in_specs=[HBM_SPEC] * n,
        out_specs=[HBM_SPEC] * n,
        out_shape=[jax.ShapeDtypeStruct((4, *g.shape[1:]), g.dtype) for g in grads],
        scratch_shapes=[pltpu.SemaphoreType.DMA((n, 4)), pltpu.SemaphoreType.DMA((n, 4))],
    )(*grads)


def _rs_chips(sums):
    n = len(sums)

    def body(*refs):
        ins, outs = refs[:n], refs[n : 2 * n]
        send_sems, recv_sems = refs[2 * n :]
        x, y, c, chips = _place()
        copies = []
        for a in range(n):
            for j, chip in enumerate(chips):
                copies.append(
                    pltpu.make_async_remote_copy(
                        src_ref=ins[a].at[1 + j],
                        dst_ref=outs[a].at[j],
                        send_sem=send_sems.at[a, j],
                        recv_sem=recv_sems.at[a, j],
                        device_id=(*chip, c),
                        device_id_type=MESH,
                    )
                )
        for cp in copies:
            cp.start()
        for cp in copies:
            cp.wait_recv()
        for cp in copies:
            cp.wait_send()

    return pl.pallas_call(
        body,
        name="rs_chips",
        in_specs=[HBM_SPEC] * n,
        out_specs=[HBM_SPEC] * n,
        out_shape=[jax.ShapeDtypeStruct((3, *g.shape[1:]), g.dtype) for g in sums],
        scratch_shapes=[pltpu.SemaphoreType.DMA((n, 3)), pltpu.SemaphoreType.DMA((n, 3))],
    )(*sums)


def _row_tile(rows, cols):
    best = 8
    for tr in range(8, rows + 1, 8):
        if rows % tr == 0 and tr * cols * 4 <= (1 << 20):
            best = tr
    return best


def _rs_add(name, slabs, grad, got):
    _, rows, cols = grad.shape
    tr = _row_tile(rows, cols)

    def body(slabs_ref, g_ref, r_ref, o_ref):
        del slabs_ref
        o_ref[...] = (g_ref[...] + r_ref[...]).astype(BF16)

    grid_spec = pltpu.PrefetchScalarGridSpec(
        num_scalar_prefetch=1,
        grid=(3, rows // tr),
        in_specs=[
            pl.BlockSpec((None, tr, cols), lambda r, i, sl: (sl[r + 1], i, 0)),
            pl.BlockSpec((None, tr, cols), lambda r, i, sl: (r + 1, i, 0)),
        ],
        out_specs=pl.BlockSpec((None, tr, cols), lambda r, i, sl: (r + 1, i, 0)),
    )
    return pl.pallas_call(
        body,
        name=name,
        grid_spec=grid_spec,
        out_shape=jax.ShapeDtypeStruct((4, rows, cols), BF16),
        compiler_params=_params(2),
    )(slabs, grad, got)


def _adamw(w, g, m, v):
    m = ADAM_B1 * m + (1.0 - ADAM_B1) * g
    v = ADAM_B2 * v + (1.0 - ADAM_B2) * (g * g)
    m_hat = m / (1.0 - ADAM_B1**ADAM_STEP)
    v_hat = v / (1.0 - ADAM_B2**ADAM_STEP)
    delta = -ADAM_LR * (m_hat / (jnp.sqrt(v_hat) + ADAM_EPS) + ADAM_WD * w)
    return delta, m, v


def _adam_shard(name, slabs, grad, got, far, w, m, v):
    rows, cols = w.shape
    tr = _row_tile(rows, cols)

    def body(slabs_ref, g_ref, r_ref, f0_ref, f1_ref, f2_ref, w_ref, m_ref, v_ref, go_ref, d_ref, mo_ref, vo_ref):
        del slabs_ref
        g = g_ref[...] + r_ref[...]
        g = g + f0_ref[...].astype(F32)
        g = g + f1_ref[...].astype(F32)
        g = g + f2_ref[...].astype(F32)
        go_ref[...] = g
        d_ref[...], mo_ref[...], vo_ref[...] = _adamw(w_ref[...], g, m_ref[...], v_ref[...])

    def far_spec(j):
        return pl.BlockSpec((None, tr, cols), lambda i, sl: (j, i, 0))

    flat = pl.BlockSpec((tr, cols), lambda i, sl: (i, 0))
    grid_spec = pltpu.PrefetchScalarGridSpec(
        num_scalar_prefetch=1,
        grid=(rows // tr,),
        in_specs=[
            pl.BlockSpec((None, tr, cols), lambda i, sl: (sl[0], i, 0)),
            pl.BlockSpec((None, tr, cols), lambda i, sl: (0, i, 0)),
            far_spec(0),
            far_spec(1),
            far_spec(2),
            flat,
            flat,
            flat,
        ],
        out_specs=[flat, flat, flat, flat],
    )
    return pl.pallas_call(
        body,
        name=name,
        grid_spec=grid_spec,
        out_shape=[jax.ShapeDtypeStruct((rows, cols), F32)] * 4,
        compiler_params=_params(1),
    )(slabs, grad, got, far, far, far, w, m, v)


def _sum_devices(gathered):
    def body(g_ref, o_ref):
        tot = g_ref[0]
        for dev in range(1, N_DEV):
            tot = tot + g_ref[dev]
        o_ref[...] = tot

    return pl.pallas_call(body, name="sum_devices", out_shape=jax.ShapeDtypeStruct(gathered.shape[1:], F32))(gathered)


def _adam_packed(w, g, m, v):
    def body(w_ref, g_ref, m_ref, v_ref, d_ref, mo_ref, vo_ref):
        d_ref[...], mo_ref[...], vo_ref[...] = _adamw(w_ref[...], g_ref[...], m_ref[...], v_ref[...])

    return pl.pallas_call(body, name="adam_packed", out_shape=[jax.ShapeDtypeStruct(w.shape, F32)] * 3)(w, g, m, v)


def _padded(size):
    return -(-size // 128) * 128


def _pack(parts):
    flat = []
    for p in parts:
        v = p.reshape(-1)
        flat.append(jnp.pad(v, (0, _padded(v.shape[0]) - v.shape[0])))
    return jnp.concatenate(flat).reshape(-1, 128)


def _unpack(packed, shapes):
    flat = packed.reshape(-1)
    out, pos = [], 0
    for shp in shapes:
        size = math.prod(shp)
        out.append(flat[pos : pos + size].reshape(shp))
        pos += _padded(size)
    return out


WEIGHTS = (
    "hgrn_lb norm1_w w_in hgrn_norm_w sconv_w w_out norm2_w mem_norm_w wq wk wv wo norm3_w w_gate w_up ffn_conv_w "
    "ffn_conv_b w_down final_norm_w"
).split()
BIG = ["w_in", "w_out", "wq", "wk", "wv", "wo", "w_gate", "w_up", "w_down"]
SMALL_SHARDED = ["sconv_w", "ffn_conv_w"]
SMALL = [n for n in WEIGHTS if n not in BIG]


def kernel(x, mem, hgrn_lb, norm1_w, w_in, hgrn_norm_w, sconv_w, w_out, norm2_w, mem_norm_w, wq, wk, wv, wo, norm3_w, w_gate, w_up, ffn_conv_w, ffn_conv_b, w_down, final_norm_w, loss_target, m_hgrn_lb, m_norm1_w, m_w_in, m_hgrn_norm_w, m_sconv_w, m_w_out, m_norm2_w, m_mem_norm_w, m_wq, m_wk, m_wv, m_wo, m_norm3_w, m_w_gate, m_w_up, m_ffn_conv_w, m_ffn_conv_b, m_w_down, m_final_norm_w, v_hgrn_lb, v_norm1_w, v_w_in, v_hgrn_norm_w, v_sconv_w, v_w_out, v_norm2_w, v_mem_norm_w, v_wq, v_wk, v_wv, v_wo, v_norm3_w, v_w_gate, v_w_up, v_ffn_conv_w, v_ffn_conv_b, v_w_down, v_final_norm_w):
    given = dict(locals())
    w_of = {n: given[n] for n in WEIGHTS}
    m_of = {n: given["m_" + n] for n in WEIGHTS}
    v_of = {n: given["v_" + n] for n in WEIGHTS}
    x2, mem2, target = x[0], mem[0], loss_target[0]
    d = x2.shape[1]
    xi, yi, ci = lax.axis_index("x"), lax.axis_index("y"), lax.axis_index("c")
    me = 4 * xi + 2 * yi + ci
    final_w = final_norm_w.reshape(1, d)

    small_shard = jnp.concatenate([sconv_w[0], ffn_conv_w[0]], axis=1)
    gathered = _allgather("gather_weights", [w_of[n][0].astype(BF16) for n in BIG] + [small_shard])
    w_in_g, w_out_g, wq_g, wk_g, wv_g, wo_g, wg_g, wu_g, wd_g, small_g = gathered
    w_in_f = jnp.transpose(w_in_g, (1, 0, 2)).reshape(d, -1)
    w_out_f, wq_f, wk_f, wv_f, wo_f = (a.reshape(d, d) for a in (w_out_g, wq_g, wk_g, wv_g, wo_g))
    conv_ch = sconv_w.shape[2]
    sw_f = jnp.transpose(small_g[:, :, :conv_ch], (1, 0, 2)).reshape(3, -1)
    cw_g = small_g[:, :, conv_ch:]
    cb_g = ffn_conv_b.reshape(N_DEV, 1, -1)

    lb = _lb_fwd(hgrn_lb)
    h1 = _rmsnorm("norm1", x2, norm1_w)
    proj = _mm_proj(h1, w_in_f)
    mix, o_pre, states = _hgrn_fwd(proj, lb, hgrn_norm_w)
    mix = _sconv_fwd(proj, sw_f, mix)
    x1 = _mm_nn("mm_out", mix, w_out_f, F32, residual=x2)
    h2 = _rmsnorm("norm2", x1, norm2_w)
    mem_n = _rmsnorm("norm_mem", mem2, mem_norm_w)
    q = _mm_nn("mm_q", h2, wq_f, BF16)
    km = _mm_nn("mm_k", mem_n, wk_f, BF16)
    vm = _mm_nn("mm_v", mem_n, wv_f, BF16)
    ao = _attn_fwd(q, km, vm)
    x2r = _mm_nn("mm_o", ao, wo_f, F32, residual=x1)
    h3 = _rmsnorm("norm3", x2r, norm3_w)
    a_pre, up = _mm_gate_up(h3, wg_g, wu_g)
    act = _ffn_act(a_pre, up, cw_g, cb_g)
    x3 = _mm_down(act, wd_g, x2r)

    dx3, dx3b, loss_blk, d_final = _loss_head(x3, target, final_w)
    dact = _mm_dact(dx3b, wd_g)
    dwd = _mm_dwdown(act, dx3b)
    dapre, dup, d_ffn_small = _ffn_act_bwd(a_pre, up, dact, cw_g, cb_g)
    dh3 = _mm_dh3(dapre, dup, wg_g, wu_g)
    dwg, dwu = _mm_dw_cols("mm_dw_gate_up", h3, [dapre, dup])
    dx2, dx2b, d_n3 = _rmsnorm_bwd("norm3_bwd", dh3, x2r, norm3_w, dx3)
    dao = _mm_nt("mm_dao", [(dx2b, wo_f)], BF16)
    dwo = _mm_tn("mm_dwo", ao, dx2b)
    dq, dkm, dvm = _attn_bwd(q, km, vm, dao)
    dwq = _mm_tn("mm_dwq", h2, dq)
    dh2 = _mm_nt("mm_dh2", [(dq, wq_f)], F32)
    dwk = _mm_tn("mm_dwk", mem_n, dkm)
    dwv = _mm_tn("mm_dwv", mem_n, dvm)
    dmem_n = _mm_nt("mm_dmem", [(dkm, wk_f), (dvm, wv_f)], F32)
    _, _, d_nm = _rmsnorm_bwd("norm_mem_bwd", dmem_n, mem2, mem_norm_w)
    dx1, dx1b, d_n2 = _rmsnorm_bwd("norm2_bwd", dh2, x1, norm2_w, dx2)
    dmix = _mm_nt("mm_dmix", [(dx1b, w_out_f)], F32)
    dwout = _mm_tn("mm_dwout", mix, dx1b)
    dproj, d_lb, d_nw = _hgrn_bwd(proj, lb, hgrn_norm_w, o_pre, states, dmix)
    dproj, d_sw = _sconv_bwd(proj, sw_f, dmix, dproj)
    dwin = _mm_dwin(h1, dproj, N_GROUPS)
    dh1 = _mm_dh1(dproj, w_in_f, N_GROUPS)
    grad_x, _, d_n1 = _rmsnorm_bwd("norm1_bwd", dh1, x2, norm1_w, dx1)
    d_hlb = _lb_bwd(hgrn_lb, d_lb)

    small_parts = [d_hlb, d_n1, d_nw, d_sw[0:3], d_n2, d_nm, d_n3, d_ffn_small[:, 0:3, :], d_ffn_small[:, 3, :], d_final, loss_blk]
    small_shapes = [p.shape for p in small_parts]
    total = _sum_devices(_allgather("gather_small", [_pack(small_parts)])[0])
    g_hlb, g_n1, g_nw, g_sw, g_n2, g_nm, g_n3, g_cw, g_cb, g_final, loss_row = _unpack(total, small_shapes)
    grads = {
        "hgrn_lb": g_hlb,
        "norm1_w": g_n1,
        "hgrn_norm_w": g_nw,
        "sconv_w": lax.dynamic_slice_in_dim(g_sw, me * conv_ch, conv_ch, axis=1).reshape(sconv_w.shape),
        "norm2_w": g_n2,
        "mem_norm_w": g_nm,
        "norm3_w": g_n3,
        "ffn_conv_w": lax.dynamic_index_in_dim(g_cw, me, axis=0, keepdims=True),
        "ffn_conv_b": g_cb.reshape(ffn_conv_b.shape),
        "final_norm_w": g_final.reshape(final_norm_w.shape),
    }
    shapes = [w_of[n].shape for n in SMALL]
    deltas_p, new_m_p, new_v_p = _adam_packed(
        _pack([w_of[n] for n in SMALL]), _pack([grads[n] for n in SMALL]), _pack([m_of[n] for n in SMALL]), _pack([v_of[n] for n in SMALL])
    )
    deltas = dict(zip(SMALL, _unpack(deltas_p, shapes)))
    new_m = dict(zip(SMALL, _unpack(new_m_p, shapes)))
    new_v = dict(zip(SMALL, _unpack(new_v_p, shapes)))

    dwin_g = jnp.transpose(dwin.reshape(d, N_DEV, -1), (1, 0, 2))
    big_grads = [dwin_g] + [a.reshape(N_DEV, d // N_DEV, d) for a in (dwout, dwq, dwk, dwv, dwo)] + [dwg, dwu, dwd]
    rel = [(xi, yi), (1 - xi, yi), (xi, 1 - yi), (1 - xi, 1 - yi)]
    slabs = jnp.stack([4 * cx + 2 * cy + ci for cx, cy in rel]).astype(jnp.int32)
    got = _rs_sibling(big_grads)
    sums = [_rs_add("rs_add_" + n, slabs, g, r) for n, g, r in zip(BIG, big_grads, got)]
    far = _rs_chips(sums)
    for n, g, r, f in zip(BIG, big_grads, got, far):
        shp = w_of[n].shape
        gw, dw, mw, vw = _adam_shard("adam_" + n, slabs, g, r, f, w_of[n][0], m_of[n][0], v_of[n][0])
        grads[n], deltas[n], new_m[n], new_v[n] = (a.reshape(shp) for a in (gw, dw, mw, vw))

    loss = loss_row[0, 0]
    return (
        loss,
        grad_x.reshape(x.shape),
        *[grads[n] for n in WEIGHTS],
        *[deltas[n] for n in WEIGHTS],
        *[new_m[n] for n in WEIGHTS],
        *[new_v[n] for n in WEIGHTS],
    )
```

```python
import functools
import math

import jax
import jax.numpy as jnp
from jax import lax
from jax.experimental import pallas as pl
from jax.experimental.pallas import tpu as pltpu

F32 = jnp.float32
BF16 = jnp.bfloat16
MESH = pl.DeviceIdType.MESH
N_DEV = 8
EPS = 1e-6
CHUNK = 64
HGRN_HEADS = 8
HEAD_DIM = 128
MEM_HEADS = 4
GROUP_W = HGRN_HEADS * HEAD_DIM
N_GROUPS = 7
HALO = 16
TS = 512
TR = 256
TM = 1024
TN = 1024
TK = 1024
VMEM_LIMIT = 48 * 1024 * 1024

ADAM_LR = 0.001
ADAM_B1 = 0.9
ADAM_B2 = 0.999
ADAM_EPS = 1e-08
ADAM_WD = 0.01
ADAM_STEP = 10

_DIMS = {
    "nn": (((1,), (0,)), ((), ())),
    "nt": (((1,), (1,)), ((), ())),
    "tn": (((0,), (0,)), ((), ())),
}


def _params(n_axes):
    return pltpu.CompilerParams(dimension_semantics=("arbitrary",) * n_axes, vmem_limit_bytes=VMEM_LIMIT)


def _dot(a, b, mode):
    return lax.dot_general(a.astype(BF16), b.astype(BF16), _DIMS[mode], preferred_element_type=F32)


def _sigmoid(v):
    return 1.0 / (1.0 + jnp.exp(-v))


def _block_dims(spec):
    return tuple(d for d in spec.block_shape if d is not None)


def _matmul(name, grid, k_axis, pairs, outs, mode, residual=None):
    n_pairs, n_out, has_res = len(pairs), len(outs), residual is not None
    k_steps = grid[k_axis] if k_axis is not None else 1

    def body(*refs):
        ins = refs[: 2 * n_pairs]
        pos = 2 * n_pairs
        res_ref = refs[pos] if has_res else None
        pos += int(has_res)
        out_refs = refs[pos : pos + n_out]
        acc_refs = refs[pos + n_out :]

        def partial(o):
            tot = None
            for p in outs[o][3]:
                d = _dot(ins[2 * p][...], ins[2 * p + 1][...], mode)
                tot = d if tot is None else tot + d
            return tot

        if k_steps == 1:
            for o in range(n_out):
                val = partial(o)
                if has_res and o == 0:
                    val = val + res_ref[...]
                out_refs[o][...] = val.astype(out_refs[o].dtype)
        else:
            k = pl.program_id(k_axis)

            @pl.when(k == 0)
            def _():
                for o in range(n_out):
                    if has_res and o == 0:
                        acc_refs[o][...] = res_ref[...]
                    else:
                        acc_refs[o][...] = jnp.zeros_like(acc_refs[o])

            for o in range(n_out):
                acc_refs[o][...] += partial(o)

            @pl.when(k == k_steps - 1)
            def _():
                for o in range(n_out):
                    out_refs[o][...] = acc_refs[o][...].astype(out_refs[o].dtype)

    args, in_specs = [], []
    for a, a_spec, b, b_spec in pairs:
        args += [a, b]
        in_specs += [a_spec, b_spec]
    if has_res:
        args.append(residual[0])
        in_specs.append(residual[1])
    scratch = [pltpu.VMEM(_block_dims(o[2]), F32) for o in outs] if k_steps > 1 else []
    res = pl.pallas_call(
        body,
        name=name,
        grid=grid,
        in_specs=in_specs,
        out_specs=[o[2] for o in outs],
        out_shape=[jax.ShapeDtypeStruct(o[0], o[1]) for o in outs],
        scratch_shapes=scratch,
        compiler_params=_params(len(grid)),
    )(*args)
    return res


def _mm_nn(name, a, w, out_dtype, residual=None):
    m, k = a.shape
    n = w.shape[1]
    tm, tn = min(TM, m), min(TN, n)
    pairs = [(a, pl.BlockSpec((tm, k), lambda i, j: (i, 0)), w, pl.BlockSpec((k, tn), lambda i, j: (0, j)))]
    o_spec = pl.BlockSpec((tm, tn), lambda i, j: (i, j))
    res = (residual, o_spec) if residual is not None else None
    return _matmul(name, (m // tm, n // tn), None, pairs, [((m, n), out_dtype, o_spec, [0])], "nn", res)[0]


def _mm_nt(name, pairs_in, out_dtype):
    m, k = pairs_in[0][0].shape
    n = pairs_in[0][1].shape[0]
    tm, tn = min(TM, m), min(TN, n)
    pairs = [
        (a, pl.BlockSpec((tm, k), lambda i, j: (i, 0)), w, pl.BlockSpec((tn, k), lambda i, j: (j, 0)))
        for a, w in pairs_in
    ]
    o_spec = pl.BlockSpec((tm, tn), lambda i, j: (i, j))
    outs = [((m, n), out_dtype, o_spec, list(range(len(pairs))))]
    return _matmul(name, (m // tm, n // tn), None, pairs, outs, "nt")[0]


def _mm_tn(name, a, b):
    s, i_dim = a.shape
    n = b.shape[1]
    ts, ti, tn = min(TK, s), min(TN, i_dim), min(TN, n)
    pairs = [(a, pl.BlockSpec((ts, ti), lambda i, j, k: (k, i)), b, pl.BlockSpec((ts, tn), lambda i, j, k: (k, j)))]
    outs = [((i_dim, n), F32, pl.BlockSpec((ti, tn), lambda i, j, k: (i, j)), [0])]
    return _matmul(name, (i_dim // ti, n // tn, s // ts), 2, pairs, outs, "tn")[0]


def _mm_proj(h1, w_in):
    s, d = h1.shape
    g = w_in.shape[1] // GROUP_W
    tm = min(TM, s)
    pairs = [(h1, pl.BlockSpec((tm, d), lambda i, j: (i, 0)), w_in, pl.BlockSpec((d, GROUP_W), lambda i, j: (0, j)))]
    outs = [((g, s, GROUP_W), BF16, pl.BlockSpec((None, tm, GROUP_W), lambda i, j: (j, i, 0)), [0])]
    return _matmul("mm_proj", (s // tm, g), None, pairs, outs, "nn")[0]


def _mm_gate_up(h3, wg_g, wu_g):
    s, d = h3.shape
    f = wg_g.shape[2]
    tm = min(TM, s)
    a_spec = pl.BlockSpec((tm, d), lambda i, j: (i, 0))
    w_spec = pl.BlockSpec((None, d, f), lambda i, j: (j, 0, 0))
    o_spec = pl.BlockSpec((None, tm, f), lambda i, j: (j, i, 0))
    pairs = [(h3, a_spec, wg_g, w_spec), (h3, a_spec, wu_g, w_spec)]
    outs = [((N_DEV, s, f), BF16, o_spec, [0]), ((N_DEV, s, f), BF16, o_spec, [1])]
    return _matmul("mm_gate_up", (s // tm, N_DEV), None, pairs, outs, "nn")


def _mm_down(act, wd_g, x2):
    _, s, f = act.shape
    d = wd_g.shape[2]
    tm, tn = min(TM, s), min(TN, d)
    pairs = [
        (
            act,
            pl.BlockSpec((None, tm, f), lambda i, j, k: (k, i, 0)),
            wd_g,
            pl.BlockSpec((None, f, tn), lambda i, j, k: (k, 0, j)),
        )
    ]
    o_spec = pl.BlockSpec((tm, tn), lambda i, j, k: (i, j))
    outs = [((s, d), F32, o_spec, [0])]
    return _matmul("mm_down", (s // tm, d // tn, N_DEV), 2, pairs, outs, "nn", (x2, o_spec))[0]


def _mm_dact(dx3b, wd_g):
    s, d = dx3b.shape
    f = wd_g.shape[1]
    tm = min(TM, s)
    pairs = [(dx3b, pl.BlockSpec((tm, d), lambda i, j: (i, 0)), wd_g, pl.BlockSpec((None, f, d), lambda i, j: (j, 0, 0)))]
    outs = [((N_DEV, s, f), BF16, pl.BlockSpec((None, tm, f), lambda i, j: (j, i, 0)), [0])]
    return _matmul("mm_dact", (s // tm, N_DEV), None, pairs, outs, "nt")[0]


def _mm_dwdown(act, dx3b):
    _, s, f = act.shape
    d = dx3b.shape[1]
    ts, tn = min(TK, s), min(TN, d)
    pairs = [
        (
            act,
            pl.BlockSpec((None, ts, f), lambda j, n, k: (j, k, 0)),
            dx3b,
            pl.BlockSpec((ts, tn), lambda j, n, k: (k, n)),
        )
    ]
    outs = [((N_DEV, f, d), F32, pl.BlockSpec((None, f, tn), lambda j, n, k: (j, 0, n)), [0])]
    return _matmul("mm_dwdown", (N_DEV, d // tn, s // ts), 2, pairs, outs, "tn")[0]


def _mm_dh3(dapre, dup, wg_g, wu_g):
    _, s, f = dapre.shape
    d = wg_g.shape[1]
    tm, tn = min(TM, s), min(TN, d)
    a_spec = pl.BlockSpec((None, tm, f), lambda i, j, k: (k, i, 0))
    w_spec = pl.BlockSpec((None, tn, f), lambda i, j, k: (k, j, 0))
    pairs = [(dapre, a_spec, wg_g, w_spec), (dup, a_spec, wu_g, w_spec)]
    outs = [((s, d), F32, pl.BlockSpec((tm, tn), lambda i, j, k: (i, j)), [0, 1])]
    return _matmul("mm_dh3", (s // tm, d // tn, N_DEV), 2, pairs, outs, "nt")[0]


def _mm_dw_cols(name, h, dcols_list):
    s, d = h.shape
    f = dcols_list[0].shape[2]
    ts, ti = min(TK, s), min(TN, d)
    a_spec = pl.BlockSpec((ts, ti), lambda j, i, k: (k, i))
    b_spec = pl.BlockSpec((None, ts, f), lambda j, i, k: (j, k, 0))
    o_spec = pl.BlockSpec((None, ti, f), lambda j, i, k: (j, i, 0))
    pairs = [(h, a_spec, dc, b_spec) for dc in dcols_list]
    outs = [((N_DEV, d, f), F32, o_spec, [p]) for p in range(len(pairs))]
    return _matmul(name, (N_DEV, d // ti, s // ts), 2, pairs, outs, "tn")


def _mm_dwin(h1, dproj, groups):
    s, d = h1.shape
    ts, ti = min(TK, s), min(TN, d)
    pairs = [
        (
            h1,
            pl.BlockSpec((ts, ti), lambda j, i, k: (k, i)),
            dproj,
            pl.BlockSpec((None, ts, GROUP_W), lambda j, i, k: (j, k, 0)),
        )
    ]
    outs = [((d, groups * GROUP_W), F32, pl.BlockSpec((ti, GROUP_W), lambda j, i, k: (i, j)), [0])]
    return _matmul("mm_dwin", (groups, d // ti, s // ts), 2, pairs, outs, "tn")[0]


def _mm_dh1(dproj, w_in, groups):
    s = dproj.shape[1]
    d = w_in.shape[0]
    tm, tn = min(TM, s), min(TN, d)
    pairs = [
        (
            dproj,
            pl.BlockSpec((None, tm, GROUP_W), lambda i, j, k: (k, i, 0)),
            w_in,
            pl.BlockSpec((tn, GROUP_W), lambda i, j, k: (j, k)),
        )
    ]
    outs = [((s, d), F32, pl.BlockSpec((tm, tn), lambda i, j, k: (i, j)), [0])]
    return _matmul("mm_dh1", (s // tm, d // tn, groups), 2, pairs, outs, "nt")[0]


def _rows_to_8(v):
    t, d = v.shape
    return jnp.sum(v.reshape(t // 8, 8, d), axis=0)


def _rmsnorm(name, x, w):
    s, d = x.shape
    t = min(TS, s)

    def body(x_ref, w_ref, o_ref):
        xv = x_ref[...]
        r = lax.rsqrt(jnp.mean(xv * xv, axis=-1, keepdims=True) + EPS)
        o_ref[...] = (xv * r * w_ref[...]).astype(o_ref.dtype)

    return pl.pallas_call(
        body,
        name=name,
        grid=(s // t,),
        in_specs=[pl.BlockSpec((t, d), lambda i: (i, 0)), pl.BlockSpec((1, d), lambda i: (0, 0))],
        out_specs=pl.BlockSpec((t, d), lambda i: (i, 0)),
        out_shape=jax.ShapeDtypeStruct((s, d), BF16),
        compiler_params=_params(1),
    )(x, w)


def _rmsnorm_bwd(name, dh, x, w, res=None):
    s, d = x.shape
    t = min(TR, s)
    steps = s // t
    has_res = res is not None

    def body(*refs):
        if has_res:
            dh_ref, x_ref, w_ref, res_ref, dx_ref, dxb_ref, dw_ref, acc = refs
        else:
            dh_ref, x_ref, w_ref, dx_ref, dxb_ref, dw_ref, acc = refs
        i = pl.program_id(0)
        xv = x_ref[...]
        r = lax.rsqrt(jnp.mean(xv * xv, axis=-1, keepdims=True) + EPS)
        xn = xv * r
        dhv = dh_ref[...].astype(F32)
        dxn = dhv * w_ref[...]
        dx = r * (dxn - xn * jnp.mean(dxn * xn, axis=-1, keepdims=True))
        if has_res:
            dx = dx + res_ref[...]
        dx_ref[...] = dx
        dxb_ref[...] = dx.astype(BF16)

        @pl.when(i == 0)
        def _():
            acc[...] = jnp.zeros_like(acc)

        acc[...] += _rows_to_8(dhv * xn)

        @pl.when(i == steps - 1)
        def _():
            dw_ref[...] = jnp.sum(acc[...], axis=0, keepdims=True)

    row = pl.BlockSpec((t, d), lambda i: (i, 0))
    vec = pl.BlockSpec((1, d), lambda i: (0, 0))
    args = [dh, x, w] + ([res] if has_res else [])
    return pl.pallas_call(
        body,
        name=name,
        grid=(steps,),
        in_specs=[row, row, vec] + ([row] if has_res else []),
        out_specs=[row, row, vec],
        out_shape=[
            jax.ShapeDtypeStruct((s, d), F32),
            jax.ShapeDtypeStruct((s, d), BF16),
            jax.ShapeDtypeStruct((1, d), F32),
        ],
        scratch_shapes=[pltpu.VMEM((8, d), F32)],
        compiler_params=_params(1),
    )(*args)


def _loss_head(x3, target, w):
    s, d = x3.shape
    t = min(TR, s)
    steps = s // t

    def body(x_ref, t_ref, w_ref, dx_ref, dxb_ref, loss_ref, dw_ref, acc_l, acc_w):
        i = pl.program_id(0)
        xv = x_ref[...]
        wv = w_ref[...]
        r = lax.rsqrt(jnp.mean(xv * xv, axis=-1, keepdims=True) + EPS)
        xn = xv * r
        err = xn * wv - t_ref[...]
        dy = err * (1.0 / d)
        dxn = dy * wv
        dx = r * (dxn - xn * jnp.mean(dxn * xn, axis=-1, keepdims=True))
        dx_ref[...] = dx
        dxb_ref[...] = dx.astype(BF16)

        @pl.when(i == 0)
        def _():
            acc_l[...] = jnp.zeros_like(acc_l)
            acc_w[...] = jnp.zeros_like(acc_w)

        acc_l[...] += _rows_to_8(err * err)
        acc_w[...] += _rows_to_8(dy * xn)

        @pl.when(i == steps - 1)
        def _():
            tot = jnp.sum(jnp.sum(acc_l[...], axis=0, keepdims=True), axis=1, keepdims=True) * (0.5 / d)
            loss_ref[...] = jnp.broadcast_to(tot, loss_ref.shape)
            dw_ref[...] = jnp.sum(acc_w[...], axis=0, keepdims=True)

    row = pl.BlockSpec((t, d), lambda i: (i, 0))
    vec = pl.BlockSpec((1, d), lambda i: (0, 0))
    return pl.pallas_call(
        body,
        name="loss_head",
        grid=(steps,),
        in_specs=[row, row, vec],
        out_specs=[row, row, pl.BlockSpec((1, 128), lambda i: (0, 0)), vec],
        out_shape=[
            jax.ShapeDtypeStruct((s, d), F32),
            jax.ShapeDtypeStruct((s, d), BF16),
            jax.ShapeDtypeStruct((1, 128), F32),
            jax.ShapeDtypeStruct((1, d), F32),
        ],
        scratch_shapes=[pltpu.VMEM((8, d), F32), pltpu.VMEM((8, d), F32)],
        compiler_params=_params(1),
    )(x3, target, w)


def _lb_fwd(hgrn_lb):
    def body(p_ref, lb_ref):
        p = p_ref[...]
        m = jnp.max(p, axis=0, keepdims=True)
        e = jnp.exp(p - m)
        lb_ref[...] = e[0:1, :] / jnp.sum(e, axis=0, keepdims=True)

    return pl.pallas_call(body, name="lb_fwd", out_shape=jax.ShapeDtypeStruct((1, hgrn_lb.shape[1]), F32))(hgrn_lb)


def _lb_bwd(hgrn_lb, dlb):
    def body(p_ref, d_ref, o_ref):
        p = p_ref[...]
        m = jnp.max(p, axis=0, keepdims=True)
        e = jnp.exp(p - m)
        sm = e / jnp.sum(e, axis=0, keepdims=True)
        p0 = sm[0:1, :]
        row = lax.broadcasted_iota(jnp.int32, sm.shape, 0)
        o_ref[...] = d_ref[...] * p0 * (jnp.where(row == 0, 1.0, 0.0) - sm)

    return pl.pallas_call(body, name="lb_bwd", out_shape=jax.ShapeDtypeStruct(hgrn_lb.shape, F32))(hgrn_lb, dlb)


def _split3(v):
    hi = v.astype(BF16)
    r1 = v - hi.astype(F32)
    mid = r1.astype(BF16)
    lo = (r1 - mid.astype(F32)).astype(BF16)
    return hi, mid, lo


def _pair(v):
    hi = v.astype(BF16)
    return hi, (v - hi.astype(F32)).astype(BF16)


def _dot_pairs(a2, b2, mode):
    dims = _DIMS[mode]
    out = lax.dot_general(a2[0], b2[0], dims, preferred_element_type=F32)
    if a2[1] is not None:
        out = lax.dot_general(a2[1], b2[0], dims, preferred_element_type=F32) + out
    if b2[1] is not None:
        out = lax.dot_general(a2[0], b2[1], dims, preferred_element_type=F32) + out
    return out


def _tri_dot(tri, v):
    hi, mid, lo = _split3(v)
    dims = _DIMS["nn"]
    out = lax.dot_general(tri, lo, dims, preferred_element_type=F32)
    out = out + lax.dot_general(tri, mid, dims, preferred_element_type=F32)
    return out + lax.dot_general(tri, hi, dims, preferred_element_type=F32)


def _chunk_gates(qp, fp, lbv, tri):
    sf = _sigmoid(fp)
    f = lbv + (1.0 - lbv) * sf
    k = 1.0 - f
    sq = _sigmoid(qp)
    qf = qp * sq
    b = _tri_dot(tri, jnp.log(f))
    row = lax.broadcasted_iota(jnp.int32, b.shape, 0)
    bm = jnp.sum(jnp.where(row == CHUNK // 2 - 1, b, 0.0), axis=0, keepdims=True)
    bl = jnp.sum(jnp.where(row == CHUNK - 1, b, 0.0), axis=0, keepdims=True)
    e_b = jnp.exp(b)
    e_qm = jnp.exp(b - bm)
    e_km = jnp.exp(bm - b)
    e_kl = jnp.exp(bl - b)
    e_l = jnp.exp(bl)
    return sf, f, k, sq, qf, e_b, e_qm, e_km, e_kl, e_l


def _hgrn_fwd(proj, lb, nw):
    s = proj.shape[1]
    t = min(TS, s)
    nt, nch = s // t, t // CHUNK
    h_, k_ = HGRN_HEADS, HEAD_DIM

    def body(q_ref, f_ref, i_ref, g_ref, lb_ref, nw_ref, mix_ref, o_ref, st_ref, state):
        @pl.when(pl.program_id(1) == 0)
        def _():
            state[...] = jnp.zeros_like(state)

        lbv = lb_ref[...]
        nwv = nw_ref[...]
        rr = lax.broadcasted_iota(jnp.int32, (CHUNK, CHUNK), 0)
        cc = lax.broadcasted_iota(jnp.int32, (CHUNK, CHUNK), 1)
        causal = rr >= cc
        tri = jnp.where(causal, 1.0, 0.0).astype(BF16)

        st = state[...]
        for c in range(nch):
            rows = slice(c * CHUNK, (c + 1) * CHUNK)
            qp, fp, gp = q_ref[rows, :].astype(F32), f_ref[rows, :].astype(F32), g_ref[rows, :].astype(F32)
            vb = i_ref[rows, :].astype(BF16)
            _, _, k, _, qf, e_b, e_qm, e_km, e_kl, e_l = _chunk_gates(qp, fp, lbv, tri)
            st_ref[c] = st
            a = jnp.where(causal, _dot(qf * e_qm, k * e_km, "nt"), 0.0)
            o = _dot(qf * e_b, st, "nt") + _dot(a, vb, "nn")
            st = st * e_l + _dot(vb, k * e_kl, "tn")
            o_ref[rows, :] = o
            on = o * lax.rsqrt(jnp.mean(o * o, axis=-1, keepdims=True) + EPS) * nwv
            mix_ref[rows, :] = (on * (gp * _sigmoid(gp))).astype(BF16)
        state[...] = st

    def col(g):
        return pl.BlockSpec((None, t, k_), lambda h, i: (g, i, h))

    return pl.pallas_call(
        body,
        name="hgrn_fwd",
        grid=(h_, nt),
        in_specs=[
            col(0),
            col(1),
            col(2),
            col(3),
            pl.BlockSpec((1, k_), lambda h, i: (0, h)),
            pl.BlockSpec((1, k_), lambda h, i: (0, 0)),
        ],
        out_specs=[
            pl.BlockSpec((t, k_), lambda h, i: (i, h)),
            pl.BlockSpec((t, k_), lambda h, i: (i, h)),
            pl.BlockSpec((nch, None, k_, k_), lambda h, i: (i, h, 0, 0)),
        ],
        out_shape=[
            jax.ShapeDtypeStruct((s, 2 * h_ * k_), BF16),
            jax.ShapeDtypeStruct((s, h_ * k_), F32),
            jax.ShapeDtypeStruct((s // CHUNK, h_, k_, k_), F32),
        ],
        scratch_shapes=[pltpu.VMEM((k_, k_), F32)],
        compiler_params=_params(2),
    )(proj, proj, proj, proj, lb, nw)


def _hgrn_bwd(proj, lb, nw, o_pre, states, dmix):
    s = proj.shape[1]
    t = min(TS, s)
    nt, nch = s // t, t // CHUNK
    h_, k_ = HGRN_HEADS, HEAD_DIM

    def body(q_ref, f_ref, i_ref, g_ref, lb_ref, nw_ref, o_ref, st_ref, dm_ref, dp_ref, dlb_ref, dnw_ref, dstate, acc_lb, acc_nw):
        h, i = pl.program_id(0), pl.program_id(1)
        dq_ref, df_ref, di_ref, dg_ref = dp_ref.at[0], dp_ref.at[1], dp_ref.at[2], dp_ref.at[3]

        @pl.when(i == 0)
        def _():
            dstate[...] = jnp.zeros_like(dstate)
            acc_lb[...] = jnp.zeros_like(acc_lb)

        @pl.when((i == 0) & (h == 0))
        def _():
            acc_nw[...] = jnp.zeros_like(acc_nw)

        lbv = lb_ref[...]
        nwv = nw_ref[...]
        rr = lax.broadcasted_iota(jnp.int32, (CHUNK, CHUNK), 0)
        cc = lax.broadcasted_iota(jnp.int32, (CHUNK, CHUNK), 1)
        causal = rr >= cc
        tri = jnp.where(causal, 1.0, 0.0).astype(BF16)
        tri_t = jnp.where(cc >= rr, 1.0, 0.0).astype(BF16)
        last_row = lax.broadcasted_iota(jnp.int32, (CHUNK, k_), 0) == CHUNK - 1

        dst = dstate[...]
        sum_nw = jnp.zeros((8, k_), F32)
        sum_lb = jnp.zeros((8, k_), F32)
        for c in reversed(range(nch)):
            rows = slice(c * CHUNK, (c + 1) * CHUNK)
            qp, fp, gp = q_ref[rows, :].astype(F32), f_ref[rows, :].astype(F32), g_ref[rows, :].astype(F32)
            vb = i_ref[rows, :].astype(BF16)
            sf, f, k, sq, qf, e_b, e_qm, e_km, e_kl, e_l = _chunk_gates(qp, fp, lbv, tri)
            st0 = st_ref[c]
            dm = dm_ref[rows, :]
            o = o_ref[rows, :]
            rinv = lax.rsqrt(jnp.mean(o * o, axis=-1, keepdims=True) + EPS)
            ohat = o * rinv
            sg = _sigmoid(gp)
            dg_ref[rows, :] = (dm * (ohat * nwv) * (sg * (1.0 + gp * (1.0 - sg)))).astype(BF16)
            don = dm * (gp * sg)
            sum_nw = sum_nw + _rows_to_8(don * ohat)
            dohat = don * nwv
            do = rinv * (dohat - ohat * jnp.mean(dohat * ohat, axis=-1, keepdims=True))
            do2, dst2 = _pair(do), _pair(dst)
            qm2, km2 = _pair(qf * e_qm), _pair(k * e_km)
            a = jnp.where(causal, _dot(qm2[0], km2[0], "nt"), 0.0)
            da2 = _pair(jnp.where(causal, _dot_pairs(do2, (vb, None), "nt"), 0.0))
            dq = e_b * _dot_pairs(do2, _pair(st0), "nn") + e_qm * _dot_pairs(da2, km2, "nn")
            dk_state = e_kl * _dot_pairs((vb, None), dst2, "nn")
            dk = e_km * _dot_pairs(da2, qm2, "tn") + dk_state
            dv = _dot(a, do2[0], "tn") + _dot(k * e_kl, dst2[0], "nt")
            d_bl = jnp.sum(k * dk_state, axis=0, keepdims=True) + e_l * jnp.sum(st0 * dst, axis=0, keepdims=True)
            db = qf * dq - k * dk + jnp.where(last_row, d_bl, 0.0)
            dlogf = _tri_dot(tri_t, db)
            dst = dst * e_l + _dot_pairs(do2, _pair(qf * e_b), "tn")
            dfv = dlogf / f - dk
            df_ref[rows, :] = (dfv * (1.0 - lbv) * (sf * (1.0 - sf))).astype(BF16)
            sum_lb = sum_lb + _rows_to_8(dfv * (1.0 - sf))
            dq_ref[rows, :] = (dq * (sq * (1.0 + qp * (1.0 - sq)))).astype(BF16)
            di_ref[rows, :] = dv.astype(BF16)
        dstate[...] = dst
        acc_nw[...] += sum_nw
        acc_lb[...] += sum_lb

        @pl.when(i == nt - 1)
        def _():
            dlb_ref[...] = jnp.sum(acc_lb[...], axis=0, keepdims=True)

        @pl.when((i == nt - 1) & (h == h_ - 1))
        def _():
            dnw_ref[...] = jnp.sum(acc_nw[...], axis=0, keepdims=True)

    def col(g):
        return pl.BlockSpec((None, t, k_), lambda h, i: (g, nt - 1 - i, h))

    row = pl.BlockSpec((t, k_), lambda h, i: (nt - 1 - i, h))
    return pl.pallas_call(
        body,
        name="hgrn_bwd",
        grid=(h_, nt),
        in_specs=[
            col(0),
            col(1),
            col(2),
            col(3),
            pl.BlockSpec((1, k_), lambda h, i: (0, h)),
            pl.BlockSpec((1, k_), lambda h, i: (0, 0)),
            row,
            pl.BlockSpec((nch, None, k_, k_), lambda h, i: (nt - 1 - i, h, 0, 0)),
            row,
        ],
        out_specs=[
            pl.BlockSpec((4, t, k_), lambda h, i: (0, nt - 1 - i, h)),
            pl.BlockSpec((1, k_), lambda h, i: (0, h)),
            pl.BlockSpec((1, k_), lambda h, i: (0, 0)),
        ],
        out_shape=[
            jax.ShapeDtypeStruct((8, s, h_ * k_), BF16),
            jax.ShapeDtypeStruct((1, h_ * k_), F32),
            jax.ShapeDtypeStruct((1, k_), F32),
        ],
        scratch_shapes=[pltpu.VMEM((k_, k_), F32), pltpu.VMEM((8, k_), F32), pltpu.VMEM((8, k_), F32)],
        compiler_params=_params(2),
    )(proj, proj, proj, proj, lb, nw, o_pre, states, dmix)


def _shift_rows(ext, k):
    n = ext.shape[0]
    return pltpu.roll(ext, k % n, axis=0)


def _row_select(parts):
    w = parts[0].shape[1]
    row = lax.broadcasted_iota(jnp.int32, (8, w), 0)
    out = jnp.zeros((8, w), F32)
    for r, p in enumerate(parts):
        out = out + jnp.where(row == r, p, 0.0)
    return out


def _sconv_fwd(proj, sw, mix):
    s = proj.shape[1]
    t = min(TS, s)
    nt, w = s // t, 512
    ncol, hb = GROUP_W // w, t // HALO

    def body(cb_ref, cc_ref, ch_ref, ccp_ref, chp_ref, sw_ref, mix_in, mix_ref):
        del mix_in
        i = pl.program_id(0)
        u = cc_ref[...].astype(F32) * ch_ref[...].astype(F32)
        up = jnp.where(i > 0, ccp_ref[...].astype(F32) * chp_ref[...].astype(F32), 0.0)
        ext = jnp.concatenate([up, u], axis=0)
        z = sw_ref[0:1, :] * _shift_rows(ext, 2)[HALO:] + sw_ref[1:2, :] * _shift_rows(ext, 1)[HALO:] + sw_ref[2:3, :] * u
        mix_ref[...] = (cb_ref[...].astype(F32) * z).astype(BF16)

    def cur(g):
        return pl.BlockSpec((None, t, w), lambda i, c: (g, i, c))

    def prev(g):
        return pl.BlockSpec((None, HALO, w), lambda i, c: (g, jnp.maximum(i * hb - 1, 0), c))

    return pl.pallas_call(
        body,
        name="sconv_fwd",
        grid=(nt, ncol),
        in_specs=[cur(4), cur(5), cur(6), prev(5), prev(6), pl.BlockSpec((3, w), lambda i, c: (0, c)), pl.BlockSpec(memory_space=pl.ANY)],
        out_specs=pl.BlockSpec((t, w), lambda i, c: (i, ncol + c)),
        out_shape=jax.ShapeDtypeStruct(mix.shape, mix.dtype),
        input_output_aliases={6: 0},
        compiler_params=_params(2),
    )(proj, proj, proj, proj, proj, sw, mix)


def _sconv_bwd(proj, sw, dmix, dproj):
    s = proj.shape[1]
    t = min(TS, s)
    nt, w = s // t, 512
    ncol, hb = GROUP_W // w, t // HALO

    def body(cb_ref, cbn_ref, cc_ref, ccp_ref, ch_ref, chp_ref, dy_ref, dyn_ref, sw_ref, dp_in, dp_ref, dsw_ref, acc):
        del dp_in
        i = pl.program_id(1)
        w0, w1, w2 = sw_ref[0:1, :], sw_ref[1:2, :], sw_ref[2:3, :]
        ccv, chv, cbv, dyv = cc_ref[...].astype(F32), ch_ref[...].astype(F32), cb_ref[...].astype(F32), dy_ref[...]
        u = ccv * chv
        up = jnp.where(i > 0, ccp_ref[...].astype(F32) * chp_ref[...].astype(F32), 0.0)
        ext = jnp.concatenate([up, u], axis=0)
        u1, u2 = _shift_rows(ext, 1)[HALO:], _shift_rows(ext, 2)[HALO:]
        z = w0 * u2 + w1 * u1 + w2 * u
        dz = dyv * cbv
        dzn = jnp.where(i < nt - 1, dyn_ref[...] * cbn_ref[...].astype(F32), 0.0)
        dze = jnp.concatenate([dz, dzn], axis=0)
        du = w2 * dz + w1 * _shift_rows(dze, -1)[:t] + w0 * _shift_rows(dze, -2)[:t]
        dp_ref[0] = (dyv * z).astype(BF16)
        dp_ref[1] = (du * chv).astype(BF16)
        dp_ref[2] = (du * ccv).astype(BF16)
        dp_ref[3] = jnp.zeros((t, w), BF16)

        @pl.when(i == 0)
        def _():
            acc[...] = jnp.zeros_like(acc)

        acc[...] += _row_select([jnp.sum(dz * uu, axis=0, keepdims=True) for uu in (u2, u1, u)])

        @pl.when(i == nt - 1)
        def _():
            dsw_ref[...] = acc[...]

    def cur(g):
        return pl.BlockSpec((None, t, w), lambda c, i: (g, i, c))

    def prev(g):
        return pl.BlockSpec((None, HALO, w), lambda c, i: (g, jnp.maximum(i * hb - 1, 0), c))

    def nxt(g):
        return pl.BlockSpec((None, HALO, w), lambda c, i: (g, jnp.minimum((i + 1) * hb, s // HALO - 1), c))

    return pl.pallas_call(
        body,
        name="sconv_bwd",
        grid=(ncol, nt),
        in_specs=[
            cur(4),
            nxt(4),
            cur(5),
            prev(5),
            cur(6),
            prev(6),
            pl.BlockSpec((t, w), lambda c, i: (i, ncol + c)),
            pl.BlockSpec((HALO, w), lambda c, i: (jnp.minimum((i + 1) * hb, s // HALO - 1), ncol + c)),
            pl.BlockSpec((3, w), lambda c, i: (0, c)),
            pl.BlockSpec(memory_space=pl.ANY),
        ],
        out_specs=[pl.BlockSpec((4, t, w), lambda c, i: (1, i, c)), pl.BlockSpec((8, w), lambda c, i: (0, c))],
        out_shape=[jax.ShapeDtypeStruct(dproj.shape, dproj.dtype), jax.ShapeDtypeStruct((8, GROUP_W), F32)],
        scratch_shapes=[pltpu.VMEM((8, w), F32)],
        input_output_aliases={9: 0},
        compiler_params=_params(2),
    )(proj, proj, proj, proj, proj, proj, dmix, dmix, sw, dproj)


def _ffn_act(a_pre, up, cw, cb):
    nd, s, f = a_pre.shape
    t = min(TS, s)
    nt, hb = s // t, t // HALO

    def body(a_ref, ap_ref, up_ref, cw_ref, cb_ref, o_ref):
        i = pl.program_id(1)
        av = a_ref[...].astype(F32)
        ext = jnp.concatenate([jnp.where(i > 0, ap_ref[...].astype(F32), 0.0), av], axis=0)
        a = cw_ref[0:1, :] * _shift_rows(ext, 2)[HALO:] + cw_ref[1:2, :] * _shift_rows(ext, 1)[HALO:] + cw_ref[2:3, :] * av
        a = a + cb_ref[...]
        o_ref[...] = (a * _sigmoid(a) * up_ref[...].astype(F32)).astype(BF16)

    cur = pl.BlockSpec((None, t, f), lambda j, i: (j, i, 0))
    prev = pl.BlockSpec((None, HALO, f), lambda j, i: (j, jnp.maximum(i * hb - 1, 0), 0))
    return pl.pallas_call(
        body,
        name="ffn_act",
        grid=(nd, nt),
        in_specs=[cur, prev, cur, pl.BlockSpec((None, 3, f), lambda j, i: (j, 0, 0)), pl.BlockSpec((None, 1, f), lambda j, i: (j, 0, 0))],
        out_specs=cur,
        out_shape=jax.ShapeDtypeStruct(a_pre.shape, BF16),
        compiler_params=_params(2),
    )(a_pre, a_pre, up, cw, cb)


def _ffn_act_bwd(a_pre, up, dact, cw, cb):
    nd, s, f = a_pre.shape
    t = min(TS, s)
    nt, hb = s // t, t // HALO

    def body(a_ref, ap_ref, an_ref, up_ref, upn_ref, da_ref, dan_ref, cw_ref, cb_ref, dap_ref, dup_ref, dsm_ref, acc):
        i = pl.program_id(1)
        w0, w1, w2 = cw_ref[0:1, :], cw_ref[1:2, :], cw_ref[2:3, :]
        av = a_ref[...].astype(F32)
        ext = jnp.concatenate([jnp.where(i > 0, ap_ref[...].astype(F32), 0.0), av, an_ref[...].astype(F32)], axis=0)
        e1, e2 = _shift_rows(ext, 1), _shift_rows(ext, 2)
        a = (w0 * e2 + w1 * e1 + w2 * ext)[HALO:] + cb_ref[...]
        sg = _sigmoid(a)
        dact_e = jnp.concatenate([da_ref[...], dan_ref[...]], axis=0).astype(F32)
        up_e = jnp.concatenate([up_ref[...], upn_ref[...]], axis=0).astype(F32)
        row = lax.broadcasted_iota(jnp.int32, a.shape, 0)
        live = (row < t) | (i < nt - 1)
        d_a = jnp.where(live, dact_e * up_e * (sg * (1.0 + a * (1.0 - sg))), 0.0)
        dup_ref[...] = (dact_e * (a * sg))[:t].astype(BF16)
        dap_ref[...] = (w2 * d_a + w1 * _shift_rows(d_a, -1) + w0 * _shift_rows(d_a, -2))[:t].astype(BF16)
        d_a_cur = d_a[:t]

        @pl.when(i == 0)
        def _():
            acc[...] = jnp.zeros_like(acc)

        sums = [jnp.sum(d_a_cur * e[HALO : HALO + t], axis=0, keepdims=True) for e in (e2, e1, ext)]
        acc[...] += _row_select(sums + [jnp.sum(d_a_cur, axis=0, keepdims=True)])

        @pl.when(i == nt - 1)
        def _():
            dsm_ref[...] = acc[...]

    cur = pl.BlockSpec((None, t, f), lambda j, i: (j, i, 0))
    prev = pl.BlockSpec((None, HALO, f), lambda j, i: (j, jnp.maximum(i * hb - 1, 0), 0))
    nxt = pl.BlockSpec((None, HALO, f), lambda j, i: (j, jnp.minimum((i + 1) * hb, s // HALO - 1), 0))
    return pl.pallas_call(
        body,
        name="ffn_act_bwd",
        grid=(nd, nt),
        in_specs=[cur, prev, nxt, cur, nxt, cur, nxt, pl.BlockSpec((None, 3, f), lambda j, i: (j, 0, 0)), pl.BlockSpec((None, 1, f), lambda j, i: (j, 0, 0))],
        out_specs=[cur, cur, pl.BlockSpec((None, 8, f), lambda j, i: (j, 0, 0))],
        out_shape=[jax.ShapeDtypeStruct(a_pre.shape, BF16), jax.ShapeDtypeStruct(a_pre.shape, BF16), jax.ShapeDtypeStruct((nd, 8, f), F32)],
        scratch_shapes=[pltpu.VMEM((8, f), F32)],
        compiler_params=_params(2),
    )(a_pre, a_pre, a_pre, up, up, dact, dact, cw, cb)


def _softmax_rows(sc):
    m = jnp.max(sc, axis=-1, keepdims=True)
    e = jnp.exp(sc - m)
    return e / jnp.sum(e, axis=-1, keepdims=True)


def _attn_fwd(q, km, vm):
    s, d = q.shape
    m = km.shape[0]
    t = min(TS, s)
    hd = d // MEM_HEADS
    scale = hd**-0.5

    def body(q_ref, k_ref, v_ref, o_ref):
        for h in range(MEM_HEADS):
            cols = slice(h * hd, (h + 1) * hd)
            p = _softmax_rows(_dot(q_ref[:, cols], k_ref[:, cols], "nt") * scale)
            o_ref[:, cols] = _dot(p, v_ref[:, cols], "nn").astype(BF16)

    row = pl.BlockSpec((t, d), lambda i: (i, 0))
    full = pl.BlockSpec((m, d), lambda i: (0, 0))
    return pl.pallas_call(
        body,
        name="attn_fwd",
        grid=(s // t,),
        in_specs=[row, full, full],
        out_specs=row,
        out_shape=jax.ShapeDtypeStruct((s, d), BF16),
        compiler_params=_params(1),
    )(q, km, vm)


def _attn_bwd(q, km, vm, do):
    s, d = q.shape
    m = km.shape[0]
    t = min(TS, s)
    steps = s // t
    hd = d // MEM_HEADS
    scale = hd**-0.5

    def body(q_ref, k_ref, v_ref, do_ref, dq_ref, dk_ref, dv_ref):
        @pl.when(pl.program_id(0) == 0)
        def _():
            dk_ref[...] = jnp.zeros_like(dk_ref)
            dv_ref[...] = jnp.zeros_like(dv_ref)

        for h in range(MEM_HEADS):
            cols = slice(h * hd, (h + 1) * hd)
            qh, kh, vh, doh = q_ref[:, cols], k_ref[:, cols], v_ref[:, cols], do_ref[:, cols]
            p = _softmax_rows(_dot(qh, kh, "nt") * scale)
            dp = _dot(doh, vh, "nt")
            ds = p * (dp - jnp.sum(dp * p, axis=-1, keepdims=True)) * scale
            dq_ref[:, cols] = _dot(ds, kh, "nn").astype(BF16)
            dk_ref[:, cols] += _dot(ds, qh, "tn")
            dv_ref[:, cols] += _dot(p, doh, "tn")

    row = pl.BlockSpec((t, d), lambda i: (i, 0))
    full = pl.BlockSpec((m, d), lambda i: (0, 0))
    return pl.pallas_call(
        body,
        name="attn_bwd",
        grid=(steps,),
        in_specs=[row, full, full, row],
        out_specs=[row, full, full],
        out_shape=[jax.ShapeDtypeStruct((s, d), BF16), jax.ShapeDtypeStruct((m, d), F32), jax.ShapeDtypeStruct((m, d), F32)],
        compiler_params=_params(1),
    )(q, km, vm, do)


HBM_SPEC = pl.BlockSpec(memory_space=pltpu.HBM)


def _place():
    x, y, c = lax.axis_index("x"), lax.axis_index("y"), lax.axis_index("c")
    return x, y, c, [(1 - x, y), (x, 1 - y), (1 - x, 1 - y)]


def _slab(p):
    return 4 * p[0] + 2 * p[1] + p[2]


def _allgather(name, shards):
    n = len(shards)

    def body(*refs):
        ins, outs = refs[:n], refs[n : 2 * n]
        send_sems, recv_sems, local_sems = refs[2 * n :]
        x, y, c, chips = _place()
        me, sibling = (x, y, c), (x, y, 1 - c)

        def copy(a, k, block, to, src=None):
            dst = outs[a].at[_slab(block)]
            return pltpu.make_async_remote_copy(
                src_ref=dst if src is None else src,
                dst_ref=dst,
                send_sem=send_sems.at[a, k],
                recv_sem=recv_sems.at[a, k],
                device_id=to,
                device_id_type=MESH,
            )

        mine = [pltpu.make_async_copy(ins[a], outs[a].at[_slab(me)], local_sems.at[a]) for a in range(n)]
        first = []
        for a in range(n):
            mine[a].start()
            first.append(copy(a, 0, me, sibling, src=ins[a]))
            first += [copy(a, 1 + j, me, (*chip, c), src=ins[a]) for j, chip in enumerate(chips)]
        for cp in first:
            cp.start()
        passed = []
        for j, chip in enumerate(chips):
            for a in range(n):
                copy(a, 1 + j, (*chip, c), me).wait_recv()
                passed.append(copy(a, 4 + j, (*chip, c), sibling))
                passed[-1].start()
        for a in range(n):
            copy(a, 0, sibling, me).wait_recv()
            for j, chip in enumerate(chips):
                copy(a, 4 + j, (*chip, 1 - c), me).wait_recv()
        for cp in first + passed:
            cp.wait_send()
        for cp in mine:
            cp.wait()

    return pl.pallas_call(
        body,
        name=name,
        in_specs=[HBM_SPEC] * n,
        out_specs=[HBM_SPEC] * n,
        out_shape=[jax.ShapeDtypeStruct((N_DEV, *a.shape), a.dtype) for a in shards],
        scratch_shapes=[pltpu.SemaphoreType.DMA((n, 7)), pltpu.SemaphoreType.DMA((n, 7)), pltpu.SemaphoreType.DMA((n,))],
    )(*shards)


def _rs_sibling(grads):
    n = len(grads)

    def body(*refs):
        ins, outs = refs[:n], refs[n : 2 * n]
        send_sems, recv_sems = refs[2 * n :]
        x, y, c, chips = _place()
        rel = [(x, y)] + chips
        copies = []
        for a in range(n):
            for r, chip in enumerate(rel):
                copies.append(
                    pltpu.make_async_remote_copy(
                        src_ref=ins[a].at[_slab((*chip, 1 - c))],
                        dst_ref=outs[a].at[r],
                        send_sem=send_sems.at[a, r],
                        recv_sem=recv_sems.at[a, r],
                        device_id=(x, y, 1 - c),
                        device_id_type=MESH,
                    )
                )
        for cp in copies:
            cp.start()
        for cp in copies:
            cp.wait_recv()
        for cp in copies:
            cp.wait_send()

    return pl.pallas_call(
        body,
        name="rs_sibling",
        in_specs=[HBM_SPEC] * n,
        out_specs=[HBM_SPEC] * n,
        out_shape=[jax.ShapeDtypeStruct((4, *g.shape[1:]), g.dtype) for g in grads],
        scratch_shapes=[pltpu.SemaphoreType.DMA((n, 4)), pltpu.SemaphoreType.DMA((n, 4))],
    )(*grads)


def _rs_chips(sums):
    n = len(sums)

    def body(*refs):
        ins, outs = refs[:n], refs[n : 2 * n]
        send_sems, recv_sems = refs[2 * n :]
        x, y, c, chips = _place()
        copies = []
        for a in range(n):
            for j, chip in enumerate(chips):
                copies.append(
                    pltpu.make_async_remote_copy(
                        src_ref=ins[a].at[1 + j],
                        dst_ref=outs[a].at[j],
                        send_sem=send_sems.at[a, j],
                        recv_sem=recv_sems.at[a, j],
                        device_id=(*chip, c),
                        device_id_type=MESH,
                    )
                )
        for cp in copies:
            cp.start()
        for cp in copies:
            cp.wait_recv()
        for cp in copies:
            cp.wait_send()

    return pl.pallas_call(
        body,
        name="rs_chips",
        in_specs=[HBM_SPEC] * n,
        out_specs=[HBM_SPEC] * n,
        out_shape=[jax.ShapeDtypeStruct((3, *g.shape[1:]), g.dtype) for g in sums],
        scratch_shapes=[pltpu.SemaphoreType.DMA((n, 3)), pltpu.SemaphoreType.DMA((n, 3))],
    )(*sums)


def _row_tile(rows, cols):
    best = 8
    for tr in range(8, rows + 1, 8):
        if rows % tr == 0 and tr * cols * 4 <= (1 << 20):
            best = tr
    return best


def _rs_add(name, slabs, grad, got):
    _, rows, cols = grad.shape
    tr = _row_tile(rows, cols)

    def body(slabs_ref, g_ref, r_ref, o_ref):
        del slabs_ref
        o_ref[...] = (g_ref[...] + r_ref[...]).astype(BF16)

    grid_spec = pltpu.PrefetchScalarGridSpec(
        num_scalar_prefetch=1,
        grid=(3, rows // tr),
        in_specs=[
            pl.BlockSpec((None, tr, cols), lambda r, i, sl: (sl[r + 1], i, 0)),
            pl.BlockSpec((None, tr, cols), lambda r, i, sl: (r + 1, i, 0)),
        ],
        out_specs=pl.BlockSpec((None, tr, cols), lambda r, i, sl: (r + 1, i, 0)),
    )
    return pl.pallas_call(
        body,
        name=name,
        grid_spec=grid_spec,
        out_shape=jax.ShapeDtypeStruct((4, rows, cols), BF16),
        compiler_params=_params(2),
    )(slabs, grad, got)


def _adamw(w, g, m, v):
    m = ADAM_B1 * m + (1.0 - ADAM_B1) * g
    v = ADAM_B2 * v + (1.0 - ADAM_B2) * (g * g)
    m_hat = m / (1.0 - ADAM_B1**ADAM_STEP)
    v_hat = v / (1.0 - ADAM_B2**ADAM_STEP)
    delta = -ADAM_LR * (m_hat / (jnp.sqrt(v_hat) + ADAM_EPS) + ADAM_WD * w)
    return delta, m, v


def _adam_shard(name, slabs, grad, got, far, w, m, v):
    rows, cols = w.shape
    tr = _row_tile(rows, cols)

    def body(slabs_ref, g_ref, r_ref, f0_ref, f1_ref, f2_ref, w_ref, m_ref, v_ref, go_ref, d_ref, mo_ref, vo_ref):
        del slabs_ref
        g = g_ref[...] + r_ref[...]
        g = g + f0_ref[...].astype(F32)
        g = g + f1_ref[...].astype(F32)
        g = g + f2_ref[...].astype(F32)
        go_ref[...] = g
        d_ref[...], mo_ref[...], vo_ref[...] = _adamw(w_ref[...], g, m_ref[...], v_ref[...])

    def far_spec(j):
        return pl.BlockSpec((None, tr, cols), lambda i, sl: (j, i, 0))

    flat = pl.BlockSpec((tr, cols), lambda i, sl: (i, 0))
    grid_spec = pltpu.PrefetchScalarGridSpec(
        num_scalar_prefetch=1,
        grid=(rows // tr,),
        in_specs=[
            pl.BlockSpec((None, tr, cols), lambda i, sl: (sl[0], i, 0)),
            pl.BlockSpec((None, tr, cols), lambda i, sl: (0, i, 0)),
            far_spec(0),
            far_spec(1),
            far_spec(2),
            flat,
            flat,
            flat,
        ],
        out_specs=[flat, flat, flat, flat],
    )
    return pl.pallas_call(
        body,
        name=name,
        grid_spec=grid_spec,
        out_shape=[jax.ShapeDtypeStruct((rows, cols), F32)] * 4,
        compiler_params=_params(1),
    )(slabs, grad, got, far, far, far, w, m, v)


def _sum_devices(gathered):
    def body(g_ref, o_ref):
        tot = g_ref[0]
        for dev in range(1, N_DEV):
            tot = tot + g_ref[dev]
        o_ref[...] = tot

    return pl.pallas_call(body, name="sum_devices", out_shape=jax.ShapeDtypeStruct(gathered.shape[1:], F32))(gathered)


def _adam_packed(w, g, m, v):
    def body(w_ref, g_ref, m_ref, v_ref, d_ref, mo_ref, vo_ref):
        d_ref[...], mo_ref[...], vo_ref[...] = _adamw(w_ref[...], g_ref[...], m_ref[...], v_ref[...])

    return pl.pallas_call(body, name="adam_packed", out_shape=[jax.ShapeDtypeStruct(w.shape, F32)] * 3)(w, g, m, v)


def _padded(size):
    return -(-size // 128) * 128


def _pack(parts):
    flat = []
    for p in parts:
        v = p.reshape(-1)
        flat.append(jnp.pad(v, (0, _padded(v.shape[0]) - v.shape[0])))
    return jnp.concatenate(flat).reshape(-1, 128)


def _unpack(packed, shapes):
    flat = packed.reshape(-1)
    out, pos = [], 0
    for shp in shapes:
        size = math.prod(shp)
        out.append(flat[pos : pos + size].reshape(shp))
        pos += _padded(size)
    return out


WEIGHTS = (
    "hgrn_lb norm1_w w_in hgrn_norm_w sconv_w w_out norm2_w mem_norm_w wq wk wv wo norm3_w w_gate w_up ffn_conv_w "
    "ffn_conv_b w_down final_norm_w"
).split()
BIG = ["w_in", "w_out", "wq", "wk", "wv", "wo", "w_gate", "w_up", "w_down"]
SMALL_SHARDED = ["sconv_w", "ffn_conv_w"]
SMALL = [n for n in WEIGHTS if n not in BIG]


def kernel(x, mem, hgrn_lb, norm1_w, w_in, hgrn_norm_w, sconv_w, w_out, norm2_w, mem_norm_w, wq, wk, wv, wo, norm3_w, w_gate, w_up, ffn_conv_w, ffn_conv_b, w_down, final_norm_w, loss_target, m_hgrn_lb, m_norm1_w, m_w_in, m_hgrn_norm_w, m_sconv_w, m_w_out, m_norm2_w, m_mem_norm_w, m_wq, m_wk, m_wv, m_wo, m_norm3_w, m_w_gate, m_w_up, m_ffn_conv_w, m_ffn_conv_b, m_w_down, m_final_norm_w, v_hgrn_lb, v_norm1_w, v_w_in, v_hgrn_norm_w, v_sconv_w, v_w_out, v_norm2_w, v_mem_norm_w, v_wq, v_wk, v_wv, v_wo, v_norm3_w, v_w_gate, v_w_up, v_ffn_conv_w, v_ffn_conv_b, v_w_down, v_final_norm_w):
    given = dict(locals())
    w_of = {n: given[n] for n in WEIGHTS}
    m_of = {n: given["m_" + n] for n in WEIGHTS}
    v_of = {n: given["v_" + n] for n in WEIGHTS}
    x2, mem2, target = x[0], mem[0], loss_target[0]
    d = x2.shape[1]
    xi, yi, ci = lax.axis_index("x"), lax.axis_index("y"), lax.axis_index("c")
    me = 4 * xi + 2 * yi + ci
    final_w = final_norm_w.reshape(1, d)

    small_shard = jnp.concatenate([sconv_w[0], ffn_conv_w[0]], axis=1)
    gathered = _allgather("gather_weights", [w_of[n][0].astype(BF16) for n in BIG] + [small_shard])
    w_in_g, w_out_g, wq_g, wk_g, wv_g, wo_g, wg_g, wu_g, wd_g, small_g = gathered
    w_in_f = jnp.transpose(w_in_g, (1, 0, 2)).reshape(d, -1)
    w_out_f, wq_f, wk_f, wv_f, wo_f = (a.reshape(d, d) for a in (w_out_g, wq_g, wk_g, wv_g, wo_g))
    conv_ch = sconv_w.shape[2]
    sw_f = jnp.transpose(small_g[:, :, :conv_ch], (1, 0, 2)).reshape(3, -1)
    cw_g = small_g[:, :, conv_ch:]
    cb_g = ffn_conv_b.reshape(N_DEV, 1, -1)

    lb = _lb_fwd(hgrn_lb)
    h1 = _rmsnorm("norm1", x2, norm1_w)
    proj = _mm_proj(h1, w_in_f)
    mix, o_pre, states = _hgrn_fwd(proj, lb, hgrn_norm_w)
    mix = _sconv_fwd(proj, sw_f, mix)
    x1 = _mm_nn("mm_out", mix, w_out_f, F32, residual=x2)
    h2 = _rmsnorm("norm2", x1, norm2_w)
    mem_n = _rmsnorm("norm_mem", mem2, mem_norm_w)
    q = _mm_nn("mm_q", h2, wq_f, BF16)
    km = _mm_nn("mm_k", mem_n, wk_f, BF16)
    vm = _mm_nn("mm_v", mem_n, wv_f, BF16)
    ao = _attn_fwd(q, km, vm)
    x2r = _mm_nn("mm_o", ao, wo_f, F32, residual=x1)
    h3 = _rmsnorm("norm3", x2r, norm3_w)
    a_pre, up = _mm_gate_up(h3, wg_g, wu_g)
    act = _ffn_act(a_pre, up, cw_g, cb_g)
    x3 = _mm_down(act, wd_g, x2r)

    dx3, dx3b, loss_blk, d_final = _loss_head(x3, target, final_w)
    dact = _mm_dact(dx3b, wd_g)
    dwd = _mm_dwdown(act, dx3b)
    dapre, dup, d_ffn_small = _ffn_act_bwd(a_pre, up, dact, cw_g, cb_g)
    dh3 = _mm_dh3(dapre, dup, wg_g, wu_g)
    dwg, dwu = _mm_dw_cols("mm_dw_gate_up", h3, [dapre, dup])
    dx2, dx2b, d_n3 = _rmsnorm_bwd("norm3_bwd", dh3, x2r, norm3_w, dx3)
    dao = _mm_nt("mm_dao", [(dx2b, wo_f)], BF16)
    dwo = _mm_tn("mm_dwo", ao, dx2b)
    dq, dkm, dvm = _attn_bwd(q, km, vm, dao)
    dwq = _mm_tn("mm_dwq", h2, dq)
    dh2 = _mm_nt("mm_dh2", [(dq, wq_f)], F32)
    dwk = _mm_tn("mm_dwk", mem_n, dkm)
    dwv = _mm_tn("mm_dwv", mem_n, dvm)
    dmem_n = _mm_nt("mm_dmem", [(dkm, wk_f), (dvm, wv_f)], F32)
    _, _, d_nm = _rmsnorm_bwd("norm_mem_bwd", dmem_n, mem2, mem_norm_w)
    dx1, dx1b, d_n2 = _rmsnorm_bwd("norm2_bwd", dh2, x1, norm2_w, dx2)
    dmix = _mm_nt("mm_dmix", [(dx1b, w_out_f)], F32)
    dwout = _mm_tn("mm_dwout", mix, dx1b)
    dproj, d_lb, d_nw = _hgrn_bwd(proj, lb, hgrn_norm_w, o_pre, states, dmix)
    dproj, d_sw = _sconv_bwd(proj, sw_f, dmix, dproj)
    dwin = _mm_dwin(h1, dproj, N_GROUPS)
    dh1 = _mm_dh1(dproj, w_in_f, N_GROUPS)
    grad_x, _, d_n1 = _rmsnorm_bwd("norm1_bwd", dh1, x2, norm1_w, dx1)
    d_hlb = _lb_bwd(hgrn_lb, d_lb)

    small_parts = [d_hlb, d_n1, d_nw, d_sw[0:3], d_n2, d_nm, d_n3, d_ffn_small[:, 0:3, :], d_ffn_small[:, 3, :], d_final, loss_blk]
    small_shapes = [p.shape for p in small_parts]
    total = _sum_devices(_allgather("gather_small", [_pack(small_parts)])[0])
    g_hlb, g_n1, g_nw, g_sw, g_n2, g_nm, g_n3, g_cw, g_cb, g_final, loss_row = _unpack(total, small_shapes)
    grads = {
        "hgrn_lb": g_hlb,
        "norm1_w": g_n1,
        "hgrn_norm_w": g_nw,
        "sconv_w": lax.dynamic_slice_in_dim(g_sw, me * conv_ch, conv_ch, axis=1).reshape(sconv_w.shape),
        "norm2_w": g_n2,
        "mem_norm_w": g_nm,
        "norm3_w": g_n3,
        "ffn_conv_w": lax.dynamic_index_in_dim(g_cw, me, axis=0, keepdims=True),
        "ffn_conv_b": g_cb.reshape(ffn_conv_b.shape),
        "final_norm_w": g_final.reshape(final_norm_w.shape),
    }
    shapes = [w_of[n].shape for n in SMALL]
    deltas_p, new_m_p, new_v_p = _adam_packed(
        _pack([w_of[n] for n in SMALL]), _pack([grads[n] for n in SMALL]), _pack([m_of[n] for n in SMALL]), _pack([v_of[n] for n in SMALL])
    )
    deltas = dict(zip(SMALL, _unpack(deltas_p, shapes)))
    new_m = dict(zip(SMALL, _unpack(new_m_p, shapes)))
    new_v = dict(zip(SMALL, _unpack(new_v_p, shapes)))

    dwin_g = jnp.transpose(dwin.reshape(d, N_DEV, -1), (1, 0, 2))
    big_grads = [dwin_g] + [a.reshape(N_DEV, d // N_DEV, d) for a in (dwout, dwq, dwk, dwv, dwo)] + [dwg, dwu, dwd]
    rel = [(xi, yi), (1 - xi, yi), (xi, 1 - yi), (1 - xi, 1 - yi)]
    slabs = jnp.stack([4 * cx + 2 * cy + ci for cx, cy in rel]).astype(jnp.int32)
    got = _rs_sibling(big_grads)
    sums = [_rs_add("rs_add_" + n, slabs, g, r) for n, g, r in zip(BIG, big_grads, got)]
    far = _rs_chips(sums)
    for n, g, r, f in zip(BIG, big_grads, got, far):
        shp = w_of[n].shape
        gw, dw, mw, vw = _adam_shard("adam_" + n, slabs, g, r, f, w_of[n][0], m_of[n][0], v_of[n][0])
        grads[n], deltas[n], new_m[n], new_v[n] = (a.reshape(shp) for a in (gw, dw, mw, vw))

    loss = loss_row[0, 0]
    return (
        loss,
        grad_x.reshape(x.shape),
        *[grads[n] for n in WEIGHTS],
        *[deltas[n] for n in WEIGHTS],
        *[new_m[n] for n in WEIGHTS],
        *[new_v[n] for n in WEIGHTS],
    )
```

```python
import functools
import math

import jax
import jax.numpy as jnp
from jax import lax
from jax.experimental import pallas as pl
from jax.experimental.pallas import tpu as pltpu

F32 = jnp.float32
BF16 = jnp.bfloat16
MESH = pl.DeviceIdType.MESH
N_DEV = 8
EPS = 1e-6
CHUNK = 64
HGRN_HEADS = 8
HEAD_DIM = 128
MEM_HEADS = 4
GROUP_W = HGRN_HEADS * HEAD_DIM
N_GROUPS = 7
HALO = 16
TS = 512
TR = 256
TM = 1024
TN = 1024
TK = 1024
VMEM_LIMIT = 48 * 1024 * 1024

ADAM_LR = 0.001
ADAM_B1 = 0.9
ADAM_B2 = 0.999
ADAM_EPS = 1e-08
ADAM_WD = 0.01
ADAM_STEP = 10

_DIMS = {
    "nn": (((1,), (0,)), ((), ())),
    "nt": (((1,), (1,)), ((), ())),
    "tn": (((0,), (0,)), ((), ())),
}


def _params(n_axes):
    return pltpu.CompilerParams(dimension_semantics=("arbitrary",) * n_axes, vmem_limit_bytes=VMEM_LIMIT)


def _dot(a, b, mode):
    return lax.dot_general(a.astype(BF16), b.astype(BF16), _DIMS[mode], preferred_element_type=F32)


def _sigmoid(v):
    return 1.0 / (1.0 + jnp.exp(-v))


def _block_dims(spec):
    return tuple(d for d in spec.block_shape if d is not None)


HBM_SPEC = pl.BlockSpec(memory_space=pltpu.HBM)


def _place():
    x, y, c = lax.axis_index("x"), lax.axis_index("y"), lax.axis_index("c")
    return x, y, c, [(1 - x, y), (x, 1 - y), (1 - x, 1 - y)]


def _slab(p):
    return 4 * p[0] + 2 * p[1] + p[2]


def _remote(src, dst, send_sem, recv_sem, to):
    return pltpu.make_async_remote_copy(src_ref=src, dst_ref=dst, send_sem=send_sem, recv_sem=recv_sem, device_id=to, device_id_type=MESH)


class _GatherRide:
    def __init__(self, shards):
        self.inputs = list(shards)
        self.n = len(shards)
        self.out_shape = [jax.ShapeDtypeStruct((N_DEV, *a.shape), a.dtype) for a in shards]
        self.scratch = [pltpu.SemaphoreType.DMA((self.n, 7)), pltpu.SemaphoreType.DMA((self.n, 7)), pltpu.SemaphoreType.DMA((self.n,))]

    @staticmethod
    def _copy(outs, sems, a, k, block, to, src=None):
        dst = outs[a].at[_slab(block)]
        return _remote(dst if src is None else src, dst, sems[0].at[a, k], sems[1].at[a, k], to)

    def _mine(self, ins, outs, sems, a):
        x, y, c, chips = _place()
        local = pltpu.make_async_copy(ins[a], outs[a].at[_slab((x, y, c))], sems[2].at[a])
        first = [self._copy(outs, sems, a, 0, (x, y, c), (x, y, 1 - c), src=ins[a])]
        first += [self._copy(outs, sems, a, 1 + j, (x, y, c), (*chip, c), src=ins[a]) for j, chip in enumerate(chips)]
        return local, first

    def start(self, ins, outs, sems):
        for a in range(self.n):
            local, first = self._mine(ins, outs, sems, a)
            local.start()
            for cp in first:
                cp.start()

    def forward(self, ins, outs, sems):
        x, y, c, chips = _place()
        for j, chip in enumerate(chips):
            for a in range(self.n):
                self._copy(outs, sems, a, 1 + j, (*chip, c), (x, y, c)).wait_recv()
                self._copy(outs, sems, a, 4 + j, (*chip, c), (x, y, 1 - c)).start()

    def finish(self, ins, outs, sems):
        x, y, c, chips = _place()
        for a in range(self.n):
            self._copy(outs, sems, a, 0, (x, y, 1 - c), (x, y, c)).wait_recv()
            for j, chip in enumerate(chips):
                self._copy(outs, sems, a, 4 + j, (*chip, 1 - c), (x, y, c)).wait_recv()
        for a in range(self.n):
            local, first = self._mine(ins, outs, sems, a)
            for cp in first:
                cp.wait_send()
            for j, chip in enumerate(chips):
                self._copy(outs, sems, a, 4 + j, (*chip, c), (x, y, 1 - c)).wait_send()
            local.wait()


class _ScatterRide:
    def __init__(self, grads):
        self.inputs = list(grads)
        self.n = len(grads)
        self.out_shape = [jax.ShapeDtypeStruct(g.shape, g.dtype) for g in grads]
        self.scratch = [pltpu.SemaphoreType.DMA((self.n, 7)), pltpu.SemaphoreType.DMA((self.n, 7)), pltpu.SemaphoreType.DMA((self.n,))]

    def _copies(self, ins, outs, sems, a):
        x, y, c, chips = _place()
        me = _slab((x, y, c))
        peers = [(x, y, 1 - c)] + [(*chip, c) for chip in chips] + [(*chip, 1 - c) for chip in chips]
        local = pltpu.make_async_copy(ins[a].at[me], outs[a].at[me], sems[2].at[a])
        sends = [_remote(ins[a].at[_slab(p)], outs[a].at[me], sems[0].at[a, k], sems[1].at[a, k], p) for k, p in enumerate(peers)]
        recvs = [_remote(ins[a].at[me], outs[a].at[_slab(p)], sems[0].at[a, k], sems[1].at[a, k], p) for k, p in enumerate(peers)]
        return local, sends, recvs

    def start(self, ins, outs, sems):
        for a in range(self.n):
            local, sends, _ = self._copies(ins, outs, sems, a)
            local.start()
            for cp in sends:
                cp.start()

    def forward(self, ins, outs, sems):
        pass

    def finish(self, ins, outs, sems):
        for a in range(self.n):
            local, sends, recvs = self._copies(ins, outs, sems, a)
            for cp in recvs:
                cp.wait_recv()
            for cp in sends:
                cp.wait_send()
            local.wait()


def _grid_step(grid):
    step = 0
    for axis, size in enumerate(grid):
        step = step * size + pl.program_id(axis)
    return step


def _ride_begin(ride, grid, ins, outs, sems):
    step, n_steps = _grid_step(grid), math.prod(grid)

    @pl.when(step == 0)
    def _():
        ride.start(ins, outs, sems)

    @pl.when(step == (3 * n_steps) // 4)
    def _():
        ride.forward(ins, outs, sems)


def _ride_end(ride, grid, ins, outs, sems):
    @pl.when(_grid_step(grid) == math.prod(grid) - 1)
    def _():
        ride.finish(ins, outs, sems)


def _allgather(name, shards):
    ride = _GatherRide(shards)
    n = ride.n

    def body(*refs):
        ins, outs, sems = refs[:n], refs[n : 2 * n], refs[2 * n :]
        ride.start(ins, outs, sems)
        ride.forward(ins, outs, sems)
        ride.finish(ins, outs, sems)

    return pl.pallas_call(
        body,
        name=name,
        in_specs=[HBM_SPEC] * n,
        out_specs=[HBM_SPEC] * n,
        out_shape=ride.out_shape,
        scratch_shapes=ride.scratch,
    )(*shards)


def _matmul(name, grid, k_axis, pairs, outs, mode, residual=None, ride=None):
    n_pairs, n_out, has_res = len(pairs), len(outs), residual is not None
    k_steps = grid[k_axis] if k_axis is not None else 1
    n_acc = n_out if k_steps > 1 else 0
    n_ride = ride.n if ride is not None else 0

    def body(*refs):
        ins = refs[: 2 * n_pairs]
        pos = 2 * n_pairs
        res_ref = refs[pos] if has_res else None
        pos += int(has_res)
        ride_ins = refs[pos : pos + n_ride]
        pos += n_ride
        out_refs = refs[pos : pos + n_out]
        pos += n_out
        ride_outs = refs[pos : pos + n_ride]
        pos += n_ride
        acc_refs = refs[pos : pos + n_acc]
        ride_sems = refs[pos + n_acc :]
        if ride is not None:
            _ride_begin(ride, grid, ride_ins, ride_outs, ride_sems)

        def partial(o):
            tot = None
            for p in outs[o][3]:
                d = _dot(ins[2 * p][...], ins[2 * p + 1][...], mode)
                tot = d if tot is None else tot + d
            return tot

        if k_steps == 1:
            for o in range(n_out):
                val = partial(o)
                if has_res and o == 0:
                    val = val + res_ref[...]
                out_refs[o][...] = val.astype(out_refs[o].dtype)
        else:
            k = pl.program_id(k_axis)

            @pl.when(k == 0)
            def _():
                for o in range(n_out):
                    if has_res and o == 0:
                        acc_refs[o][...] = res_ref[...]
                    else:
                        acc_refs[o][...] = jnp.zeros_like(acc_refs[o])

            for o in range(n_out):
                acc_refs[o][...] += partial(o)

            @pl.when(k == k_steps - 1)
            def _():
                for o in range(n_out):
                    out_refs[o][...] = acc_refs[o][...].astype(out_refs[o].dtype)

        if ride is not None:
            _ride_end(ride, grid, ride_ins, ride_outs, ride_sems)

    args, in_specs = [], []
    for a, a_spec, b, b_spec in pairs:
        args += [a, b]
        in_specs += [a_spec, b_spec]
    if has_res:
        args.append(residual[0])
        in_specs.append(residual[1])
    out_specs = [o[2] for o in outs]
    out_shape = [jax.ShapeDtypeStruct(o[0], o[1]) for o in outs]
    scratch = [pltpu.VMEM(_block_dims(o[2]), F32) for o in outs] if k_steps > 1 else []
    if ride is not None:
        args += ride.inputs
        in_specs += [HBM_SPEC] * n_ride
        out_specs += [HBM_SPEC] * n_ride
        out_shape += ride.out_shape
        scratch += ride.scratch
    return pl.pallas_call(
        body,
        name=name,
        grid=grid,
        in_specs=in_specs,
        out_specs=out_specs,
        out_shape=out_shape,
        scratch_shapes=scratch,
        compiler_params=_params(len(grid)),
    )(*args)


def _mm_nn(name, a, w, out_dtype, residual=None):
    m, k = a.shape
    n = w.shape[1]
    tm, tn = min(TM, m), min(TN, n)
    pairs = [(a, pl.BlockSpec((tm, k), lambda i, j: (i, 0)), w, pl.BlockSpec((k, tn), lambda i, j: (0, j)))]
    o_spec = pl.BlockSpec((tm, tn), lambda i, j: (i, j))
    res = (residual, o_spec) if residual is not None else None
    return _matmul(name, (m // tm, n // tn), None, pairs, [((m, n), out_dtype, o_spec, [0])], "nn", res)[0]


def _mm_nt(name, pairs_in, out_dtype):
    m, k = pairs_in[0][0].shape
    n = pairs_in[0][1].shape[0]
    tm, tn = min(TM, m), min(TN, n)
    pairs = [
        (a, pl.BlockSpec((tm, k), lambda i, j: (i, 0)), w, pl.BlockSpec((tn, k), lambda i, j: (j, 0)))
        for a, w in pairs_in
    ]
    o_spec = pl.BlockSpec((tm, tn), lambda i, j: (i, j))
    outs = [((m, n), out_dtype, o_spec, list(range(len(pairs))))]
    return _matmul(name, (m // tm, n // tn), None, pairs, outs, "nt")[0]


def _mm_tn(name, a, b):
    s, i_dim = a.shape
    n = b.shape[1]
    ts, ti, tn = min(TK, s), min(TN, i_dim), min(TN, n)
    pairs = [(a, pl.BlockSpec((ts, ti), lambda i, j, k: (k, i)), b, pl.BlockSpec((ts, tn), lambda i, j, k: (k, j)))]
    outs = [((i_dim, n), BF16, pl.BlockSpec((ti, tn), lambda i, j, k: (i, j)), [0])]
    return _matmul(name, (i_dim // ti, n // tn, s // ts), 2, pairs, outs, "tn")[0]


def _mm_proj(h1, w_in, ride):
    s, d = h1.shape
    g = w_in.shape[1] // GROUP_W
    tm = min(TM, s)
    pairs = [(h1, pl.BlockSpec((tm, d), lambda i, j: (i, 0)), w_in, pl.BlockSpec((d, GROUP_W), lambda i, j: (0, j)))]
    outs = [((g, s, GROUP_W), BF16, pl.BlockSpec((None, tm, GROUP_W), lambda i, j: (j, i, 0)), [0])]
    return _matmul("mm_proj", (s // tm, g), None, pairs, outs, "nn", ride=ride)


def _mm_gate_up(h3, wg_g, wu_g):
    s, d = h3.shape
    f = wg_g.shape[2]
    tm = min(TM, s)
    a_spec = pl.BlockSpec((tm, d), lambda i, j: (i, 0))
    w_spec = pl.BlockSpec((None, d, f), lambda i, j: (j, 0, 0))
    o_spec = pl.BlockSpec((None, tm, f), lambda i, j: (j, i, 0))
    pairs = [(h3, a_spec, wg_g, w_spec), (h3, a_spec, wu_g, w_spec)]
    outs = [((N_DEV, s, f), BF16, o_spec, [0]), ((N_DEV, s, f), BF16, o_spec, [1])]
    return _matmul("mm_gate_up", (s // tm, N_DEV), None, pairs, outs, "nn")


def _mm_down(act, wd_g, x2):
    _, s, f = act.shape
    d = wd_g.shape[2]
    tm, tn = min(TM, s), min(TN, d)
    pairs = [
        (
            act,
            pl.BlockSpec((None, tm, f), lambda i, j, k: (k, i, 0)),
            wd_g,
            pl.BlockSpec((None, f, tn), lambda i, j, k: (k, 0, j)),
        )
    ]
    o_spec = pl.BlockSpec((tm, tn), lambda i, j, k: (i, j))
    outs = [((s, d), F32, o_spec, [0])]
    return _matmul("mm_down", (s // tm, d // tn, N_DEV), 2, pairs, outs, "nn", (x2, o_spec))[0]


def _mm_dact(dx3b, wd_g):
    s, d = dx3b.shape
    f = wd_g.shape[1]
    tm = min(TM, s)
    pairs = [(dx3b, pl.BlockSpec((tm, d), lambda i, j: (i, 0)), wd_g, pl.BlockSpec((None, f, d), lambda i, j: (j, 0, 0)))]
    outs = [((N_DEV, s, f), BF16, pl.BlockSpec((None, tm, f), lambda i, j: (j, i, 0)), [0])]
    return _matmul("mm_dact", (s // tm, N_DEV), None, pairs, outs, "nt")[0]


def _mm_dwdown(act, dx3b):
    _, s, f = act.shape
    d = dx3b.shape[1]
    ts, tn = min(TK, s), min(TN, d)
    pairs = [
        (
            act,
            pl.BlockSpec((None, ts, f), lambda j, n, k: (j, k, 0)),
            dx3b,
            pl.BlockSpec((ts, tn), lambda j, n, k: (k, n)),
        )
    ]
    outs = [((N_DEV, f, d), BF16, pl.BlockSpec((None, f, tn), lambda j, n, k: (j, 0, n)), [0])]
    return _matmul("mm_dwdown", (N_DEV, d // tn, s // ts), 2, pairs, outs, "tn")[0]


def _mm_dh3(dapre, dup, wg_g, wu_g, ride):
    _, s, f = dapre.shape
    d = wg_g.shape[1]
    tm, tn = min(TM, s), min(TN, d)
    a_spec = pl.BlockSpec((None, tm, f), lambda i, j, k: (k, i, 0))
    w_spec = pl.BlockSpec((None, tn, f), lambda i, j, k: (k, j, 0))
    pairs = [(dapre, a_spec, wg_g, w_spec), (dup, a_spec, wu_g, w_spec)]
    outs = [((s, d), F32, pl.BlockSpec((tm, tn), lambda i, j, k: (i, j)), [0, 1])]
    return _matmul("mm_dh3", (s // tm, d // tn, N_DEV), 2, pairs, outs, "nt", ride=ride)


def _mm_dw_cols(name, h, dcols_list):
    s, d = h.shape
    f = dcols_list[0].shape[2]
    ts, ti = min(TK, s), min(TN, d)
    a_spec = pl.BlockSpec((ts, ti), lambda j, i, k: (k, i))
    b_spec = pl.BlockSpec((None, ts, f), lambda j, i, k: (j, k, 0))
    o_spec = pl.BlockSpec((None, ti, f), lambda j, i, k: (j, i, 0))
    pairs = [(h, a_spec, dc, b_spec) for dc in dcols_list]
    outs = [((N_DEV, d, f), BF16, o_spec, [p]) for p in range(len(pairs))]
    return _matmul(name, (N_DEV, d // ti, s // ts), 2, pairs, outs, "tn")


def _mm_dwin(h1, dproj, groups):
    s, d = h1.shape
    ts, ti = min(TK, s), min(TN, d)
    pairs = [
        (
            h1,
            pl.BlockSpec((ts, ti), lambda j, i, k: (k, i)),
            dproj,
            pl.BlockSpec((None, ts, GROUP_W), lambda j, i, k: (j, k, 0)),
        )
    ]
    outs = [((d, groups * GROUP_W), BF16, pl.BlockSpec((ti, GROUP_W), lambda j, i, k: (i, j)), [0])]
    return _matmul("mm_dwin", (groups, d // ti, s // ts), 2, pairs, outs, "tn")[0]


def _mm_dh1(dproj, w_in, groups, ride):
    s = dproj.shape[1]
    d = w_in.shape[0]
    tm, tn = min(TM, s), min(TN, d)
    pairs = [
        (
            dproj,
            pl.BlockSpec((None, tm, GROUP_W), lambda i, j, k: (k, i, 0)),
            w_in,
            pl.BlockSpec((tn, GROUP_W), lambda i, j, k: (j, k)),
        )
    ]
    outs = [((s, d), F32, pl.BlockSpec((tm, tn), lambda i, j, k: (i, j)), [0])]
    return _matmul("mm_dh1", (s // tm, d // tn, groups), 2, pairs, outs, "nt", ride=ride)


def _rows_to_8(v):
    t, d = v.shape
    return jnp.sum(v.reshape(t // 8, 8, d), axis=0)


def _rmsnorm(name, x, w):
    s, d = x.shape
    t = min(TS, s)

    def body(x_ref, w_ref, o_ref):
        xv = x_ref[...]
        r = lax.rsqrt(jnp.mean(xv * xv, axis=-1, keepdims=True) + EPS)
        o_ref[...] = (xv * r * w_ref[...]).astype(o_ref.dtype)

    return pl.pallas_call(
        body,
        name=name,
        grid=(s // t,),
        in_specs=[pl.BlockSpec((t, d), lambda i: (i, 0)), pl.BlockSpec((1, d), lambda i: (0, 0))],
        out_specs=pl.BlockSpec((t, d), lambda i: (i, 0)),
        out_shape=jax.ShapeDtypeStruct((s, d), BF16),
        compiler_params=_params(1),
    )(x, w)


def _rmsnorm_bwd(name, dh, x, w, res=None):
    s, d = x.shape
    t = min(TR, s)
    steps = s // t
    has_res = res is not None

    def body(*refs):
        if has_res:
            dh_ref, x_ref, w_ref, res_ref, dx_ref, dxb_ref, dw_ref, acc = refs
        else:
            dh_ref, x_ref, w_ref, dx_ref, dxb_ref, dw_ref, acc = refs
        i = pl.program_id(0)
        xv = x_ref[...]
        r = lax.rsqrt(jnp.mean(xv * xv, axis=-1, keepdims=True) + EPS)
        xn = xv * r
        dhv = dh_ref[...].astype(F32)
        dxn = dhv * w_ref[...]
        dx = r * (dxn - xn * jnp.mean(dxn * xn, axis=-1, keepdims=True))
        if has_res:
            dx = dx + res_ref[...]
        dx_ref[...] = dx
        dxb_ref[...] = dx.astype(BF16)

        @pl.when(i == 0)
        def _():
            acc[...] = jnp.zeros_like(acc)

        acc[...] += _rows_to_8(dhv * xn)

        @pl.when(i == steps - 1)
        def _():
            dw_ref[...] = jnp.sum(acc[...], axis=0, keepdims=True)

    row = pl.BlockSpec((t, d), lambda i: (i, 0))
    vec = pl.BlockSpec((1, d), lambda i: (0, 0))
    args = [dh, x, w] + ([res] if has_res else [])
    return pl.pallas_call(
        body,
        name=name,
        grid=(steps,),
        in_specs=[row, row, vec] + ([row] if has_res else []),
        out_specs=[row, row, vec],
        out_shape=[
            jax.ShapeDtypeStruct((s, d), F32),
            jax.ShapeDtypeStruct((s, d), BF16),
            jax.ShapeDtypeStruct((1, d), F32),
        ],
        scratch_shapes=[pltpu.VMEM((8, d), F32)],
        compiler_params=_params(1),
    )(*args)


def _loss_head(x3, target, w):
    s, d = x3.shape
    t = min(TR, s)
    steps = s // t

    def body(x_ref, t_ref, w_ref, dx_ref, dxb_ref, loss_ref, dw_ref, acc_l, acc_w):
        i = pl.program_id(0)
        xv = x_ref[...]
        wv = w_ref[...]
        r = lax.rsqrt(jnp.mean(xv * xv, axis=-1, keepdims=True) + EPS)
        xn = xv * r
        err = xn * wv - t_ref[...]
        dy = err * (1.0 / d)
        dxn = dy * wv
        dx = r * (dxn - xn * jnp.mean(dxn * xn, axis=-1, keepdims=True))
        dx_ref[...] = dx
        dxb_ref[...] = dx.astype(BF16)

        @pl.when(i == 0)
        def _():
            acc_l[...] = jnp.zeros_like(acc_l)
            acc_w[...] = jnp.zeros_like(acc_w)

        acc_l[...] += _rows_to_8(err * err)
        acc_w[...] += _rows_to_8(dy * xn)

        @pl.when(i == steps - 1)
        def _():
            tot = jnp.sum(jnp.sum(acc_l[...], axis=0, keepdims=True), axis=1, keepdims=True) * (0.5 / d)
            loss_ref[...] = jnp.broadcast_to(tot, loss_ref.shape)
            dw_ref[...] = jnp.sum(acc_w[...], axis=0, keepdims=True)

    row = pl.BlockSpec((t, d), lambda i: (i, 0))
    vec = pl.BlockSpec((1, d), lambda i: (0, 0))
    return pl.pallas_call(
        body,
        name="loss_head",
        grid=(steps,),
        in_specs=[row, row, vec],
        out_specs=[row, row, pl.BlockSpec((1, 128), lambda i: (0, 0)), vec],
        out_shape=[
            jax.ShapeDtypeStruct((s, d), F32),
            jax.ShapeDtypeStruct((s, d), BF16),
            jax.ShapeDtypeStruct((1, 128), F32),
            jax.ShapeDtypeStruct((1, d), F32),
        ],
        scratch_shapes=[pltpu.VMEM((8, d), F32), pltpu.VMEM((8, d), F32)],
        compiler_params=_params(1),
    )(x3, target, w)


def _lb_fwd(hgrn_lb):
    def body(p_ref, lb_ref):
        p = p_ref[...]
        m = jnp.max(p, axis=0, keepdims=True)
        e = jnp.exp(p - m)
        lb_ref[...] = e[0:1, :] / jnp.sum(e, axis=0, keepdims=True)

    return pl.pallas_call(body, name="lb_fwd", out_shape=jax.ShapeDtypeStruct((1, hgrn_lb.shape[1]), F32))(hgrn_lb)


def _lb_bwd(hgrn_lb, dlb):
    def body(p_ref, d_ref, o_ref):
        p = p_ref[...]
        m = jnp.max(p, axis=0, keepdims=True)
        e = jnp.exp(p - m)
        sm = e / jnp.sum(e, axis=0, keepdims=True)
        p0 = sm[0:1, :]
        row = lax.broadcasted_iota(jnp.int32, sm.shape, 0)
        o_ref[...] = d_ref[...] * p0 * (jnp.where(row == 0, 1.0, 0.0) - sm)

    return pl.pallas_call(body, name="lb_bwd", out_shape=jax.ShapeDtypeStruct(hgrn_lb.shape, F32))(hgrn_lb, dlb)


def _split3(v):
    hi = v.astype(BF16)
    r1 = v - hi.astype(F32)
    mid = r1.astype(BF16)
    lo = (r1 - mid.astype(F32)).astype(BF16)
    return hi, mid, lo


def _pair(v):
    hi = v.astype(BF16)
    return hi, (v - hi.astype(F32)).astype(BF16)


def _dot_pairs(a2, b2, mode):
    dims = _DIMS[mode]
    out = lax.dot_general(a2[0], b2[0], dims, preferred_element_type=F32)
    if a2[1] is not None:
        out = lax.dot_general(a2[1], b2[0], dims, preferred_element_type=F32) + out
    if b2[1] is not None:
        out = lax.dot_general(a2[0], b2[1], dims, preferred_element_type=F32) + out
    return out


def _tri_dot(tri, v):
    hi, mid, lo = _split3(v)
    dims = _DIMS["nn"]
    out = lax.dot_general(tri, lo, dims, preferred_element_type=F32)
    out = out + lax.dot_general(tri, mid, dims, preferred_element_type=F32)
    return out + lax.dot_general(tri, hi, dims, preferred_element_type=F32)


def _chunk_gates(qp, fp, lbv, tri):
    sf = _sigmoid(fp)
    f = lbv + (1.0 - lbv) * sf
    k = 1.0 - f
    sq = _sigmoid(qp)
    qf = qp * sq
    b = _tri_dot(tri, jnp.log(f))
    row = lax.broadcasted_iota(jnp.int32, b.shape, 0)
    bm = jnp.sum(jnp.where(row == CHUNK // 2 - 1, b, 0.0), axis=0, keepdims=True)
    bl = jnp.sum(jnp.where(row == CHUNK - 1, b, 0.0), axis=0, keepdims=True)
    e_b = jnp.exp(b)
    e_qm = jnp.exp(b - bm)
    e_km = jnp.exp(bm - b)
    e_kl = jnp.exp(bl - b)
    e_l = jnp.exp(bl)
    return sf, f, k, sq, qf, e_b, e_qm, e_km, e_kl, e_l


def _hosted(compute, n_in, n_out, n_scratch, ride, grid):
    n_ride = ride.n

    def body(*refs):
        ins, ride_ins = refs[:n_in], refs[n_in : n_in + n_ride]
        pos = n_in + n_ride
        outs, ride_outs = refs[pos : pos + n_out], refs[pos + n_out : pos + n_out + n_ride]
        pos += n_out + n_ride
        scratch, sems = refs[pos : pos + n_scratch], refs[pos + n_scratch :]
        _ride_begin(ride, grid, ride_ins, ride_outs, sems)
        compute(*ins, *outs, *scratch)
        _ride_end(ride, grid, ride_ins, ride_outs, sems)

    return body


def _hgrn_fwd(proj, lb, nw, ride):
    s = proj.shape[1]
    t = min(TS, s)
    nt, nch = s // t, t // CHUNK
    h_, k_ = HGRN_HEADS, HEAD_DIM

    def compute(q_ref, f_ref, i_ref, g_ref, lb_ref, nw_ref, mix_ref, o_ref, st_ref, state):
        @pl.when(pl.program_id(1) == 0)
        def _():
            state[...] = jnp.zeros_like(state)

        lbv = lb_ref[...]
        nwv = nw_ref[...]
        rr = lax.broadcasted_iota(jnp.int32, (CHUNK, CHUNK), 0)
        cc = lax.broadcasted_iota(jnp.int32, (CHUNK, CHUNK), 1)
        causal = rr >= cc
        tri = jnp.where(causal, 1.0, 0.0).astype(BF16)

        st = state[...]
        for c in range(nch):
            rows = slice(c * CHUNK, (c + 1) * CHUNK)
            qp, fp, gp = q_ref[rows, :].astype(F32), f_ref[rows, :].astype(F32), g_ref[rows, :].astype(F32)
            vb = i_ref[rows, :].astype(BF16)
            _, _, k, _, qf, e_b, e_qm, e_km, e_kl, e_l = _chunk_gates(qp, fp, lbv, tri)
            st_ref[c] = st
            a = jnp.where(causal, _dot(qf * e_qm, k * e_km, "nt"), 0.0)
            o = _dot(qf * e_b, st, "nt") + _dot(a, vb, "nn")
            st = st * e_l + _dot(vb, k * e_kl, "tn")
            o_ref[rows, :] = o
            on = o * lax.rsqrt(jnp.mean(o * o, axis=-1, keepdims=True) + EPS) * nwv
            mix_ref[rows, :] = (on * (gp * _sigmoid(gp))).astype(BF16)
        state[...] = st

    def col(g):
        return pl.BlockSpec((None, t, k_), lambda h, i: (g, i, h))

    return pl.pallas_call(
        _hosted(compute, 6, 3, 1, ride, (h_, nt)),
        name="hgrn_fwd",
        grid=(h_, nt),
        in_specs=[
            col(0),
            col(1),
            col(2),
            col(3),
            pl.BlockSpec((1, k_), lambda h, i: (0, h)),
            pl.BlockSpec((1, k_), lambda h, i: (0, 0)),
        ]
        + [HBM_SPEC] * ride.n,
        out_specs=[
            pl.BlockSpec((t, k_), lambda h, i: (i, h)),
            pl.BlockSpec((t, k_), lambda h, i: (i, h)),
            pl.BlockSpec((nch, None, k_, k_), lambda h, i: (i, h, 0, 0)),
        ]
        + [HBM_SPEC] * ride.n,
        out_shape=[
            jax.ShapeDtypeStruct((s, 2 * h_ * k_), BF16),
            jax.ShapeDtypeStruct((s, h_ * k_), F32),
            jax.ShapeDtypeStruct((s // CHUNK, h_, k_, k_), F32),
        ]
        + ride.out_shape,
        scratch_shapes=[pltpu.VMEM((k_, k_), F32)] + ride.scratch,
        compiler_params=_params(2),
    )(proj, proj, proj, proj, lb, nw, *ride.inputs)


def _hgrn_bwd(proj, lb, nw, o_pre, states, dmix, ride):
    s = proj.shape[1]
    t = min(TS, s)
    nt, nch = s // t, t // CHUNK
    h_, k_ = HGRN_HEADS, HEAD_DIM

    def compute(q_ref, f_ref, i_ref, g_ref, lb_ref, nw_ref, o_ref, st_ref, dm_ref, dp_ref, dlb_ref, dnw_ref, dstate, acc_lb, acc_nw):
        h, i = pl.program_id(0), pl.program_id(1)
        dq_ref, df_ref, di_ref, dg_ref = dp_ref.at[0], dp_ref.at[1], dp_ref.at[2], dp_ref.at[3]

        @pl.when(i == 0)
        def _():
            dstate[...] = jnp.zeros_like(dstate)
            acc_lb[...] = jnp.zeros_like(acc_lb)

        @pl.when((i == 0) & (h == 0))
        def _():
            acc_nw[...] = jnp.zeros_like(acc_nw)

        lbv = lb_ref[...]
        nwv = nw_ref[...]
        rr = lax.broadcasted_iota(jnp.int32, (CHUNK, CHUNK), 0)
        cc = lax.broadcasted_iota(jnp.int32, (CHUNK, CHUNK), 1)
        causal = rr >= cc
        tri = jnp.where(causal, 1.0, 0.0).astype(BF16)
        tri_t = jnp.where(cc >= rr, 1.0, 0.0).astype(BF16)
        last_row = lax.broadcasted_iota(jnp.int32, (CHUNK, k_), 0) == CHUNK - 1

        dst = dstate[...]
        sum_nw = jnp.zeros((8, k_), F32)
        sum_lb = jnp.zeros((8, k_), F32)
        for c in reversed(range(nch)):
            rows = slice(c * CHUNK, (c + 1) * CHUNK)
            qp, fp, gp = q_ref[rows, :].astype(F32), f_ref[rows, :].astype(F32), g_ref[rows, :].astype(F32)
            vb = i_ref[rows, :].astype(BF16)
            sf, f, k, sq, qf, e_b, e_qm, e_km, e_kl, e_l = _chunk_gates(qp, fp, lbv, tri)
            st0 = st_ref[c]
            dm = dm_ref[rows, :]
            o = o_ref[rows, :]
            rinv = lax.rsqrt(jnp.mean(o * o, axis=-1, keepdims=True) + EPS)
            ohat = o * rinv
            sg = _sigmoid(gp)
            dg_ref[rows, :] = (dm * (ohat * nwv) * (sg * (1.0 + gp * (1.0 - sg)))).astype(BF16)
            don = dm * (gp * sg)
            sum_nw = sum_nw + _rows_to_8(don * ohat)
            dohat = don * nwv
            do = rinv * (dohat - ohat * jnp.mean(dohat * ohat, axis=-1, keepdims=True))
            do2, dst2 = _pair(do), _pair(dst)
            qm2, km2 = _pair(qf * e_qm), _pair(k * e_km)
            a = jnp.where(causal, _dot(qm2[0], km2[0], "nt"), 0.0)
            da2 = _pair(jnp.where(causal, _dot_pairs(do2, (vb, None), "nt"), 0.0))
            dq = e_b * _dot_pairs(do2, _pair(st0), "nn") + e_qm * _dot_pairs(da2, km2, "nn")
            dk_state = e_kl * _dot_pairs((vb, None), dst2, "nn")
            dk = e_km * _dot_pairs(da2, qm2, "tn") + dk_state
            dv = _dot(a, do2[0], "tn") + _dot(k * e_kl, dst2[0], "nt")
            d_bl = jnp.sum(k * dk_state, axis=0, keepdims=True) + e_l * jnp.sum(st0 * dst, axis=0, keepdims=True)
            db = qf * dq - k * dk + jnp.where(last_row, d_bl, 0.0)
            dlogf = _tri_dot(tri_t, db)
            dst = dst * e_l + _dot_pairs(do2, _pair(qf * e_b), "tn")
            dfv = dlogf / f - dk
            df_ref[rows, :] = (dfv * (1.0 - lbv) * (sf * (1.0 - sf))).astype(BF16)
            sum_lb = sum_lb + _rows_to_8(dfv * (1.0 - sf))
            dq_ref[rows, :] = (dq * (sq * (1.0 + qp * (1.0 - sq)))).astype(BF16)
            di_ref[rows, :] = dv.astype(BF16)
        dstate[...] = dst
        acc_nw[...] += sum_nw
        acc_lb[...] += sum_lb

        @pl.when(i == nt - 1)
        def _():
            dlb_ref[...] = jnp.sum(acc_lb[...], axis=0, keepdims=True)

        @pl.when((i == nt - 1) & (h == h_ - 1))
        def _():
            dnw_ref[...] = jnp.sum(acc_nw[...], axis=0, keepdims=True)

    def col(g):
        return pl.BlockSpec((None, t, k_), lambda h, i: (g, nt - 1 - i, h))

    row = pl.BlockSpec((t, k_), lambda h, i: (nt - 1 - i, h))
    return pl.pallas_call(
        _hosted(compute, 9, 3, 3, ride, (h_, nt)),
        name="hgrn_bwd",
        grid=(h_, nt),
        in_specs=[
            col(0),
            col(1),
            col(2),
            col(3),
            pl.BlockSpec((1, k_), lambda h, i: (0, h)),
            pl.BlockSpec((1, k_), lambda h, i: (0, 0)),
            row,
            pl.BlockSpec((nch, None, k_, k_), lambda h, i: (nt - 1 - i, h, 0, 0)),
            row,
        ]
        + [HBM_SPEC] * ride.n,
        out_specs=[
            pl.BlockSpec((4, t, k_), lambda h, i: (0, nt - 1 - i, h)),
            pl.BlockSpec((1, k_), lambda h, i: (0, h)),
            pl.BlockSpec((1, k_), lambda h, i: (0, 0)),
        ]
        + [HBM_SPEC] * ride.n,
        out_shape=[
            jax.ShapeDtypeStruct((8, s, h_ * k_), BF16),
            jax.ShapeDtypeStruct((1, h_ * k_), F32),
            jax.ShapeDtypeStruct((1, k_), F32),
        ]
        + ride.out_shape,
        scratch_shapes=[pltpu.VMEM((k_, k_), F32), pltpu.VMEM((8, k_), F32), pltpu.VMEM((8, k_), F32)] + ride.scratch,
        compiler_params=_params(2),
    )(proj, proj, proj, proj, lb, nw, o_pre, states, dmix, *ride.inputs)


def _shift_rows(ext, k):
    n = ext.shape[0]
    return pltpu.roll(ext, k % n, axis=0)


def _row_select(parts):
    w = parts[0].shape[1]
    row = lax.broadcasted_iota(jnp.int32, (8, w), 0)
    out = jnp.zeros((8, w), F32)
    for r, p in enumerate(parts):
        out = out + jnp.where(row == r, p, 0.0)
    return out


def _sconv_fwd(proj, sw, mix):
    s = proj.shape[1]
    t = min(TS, s)
    nt, w = s // t, 512
    ncol, hb = GROUP_W // w, t // HALO

    def body(cb_ref, cc_ref, ch_ref, ccp_ref, chp_ref, sw_ref, mix_in, mix_ref):
        del mix_in
        i = pl.program_id(0)
        u = cc_ref[...].astype(F32) * ch_ref[...].astype(F32)
        up = jnp.where(i > 0, ccp_ref[...].astype(F32) * chp_ref[...].astype(F32), 0.0)
        ext = jnp.concatenate([up, u], axis=0)
        z = sw_ref[0:1, :] * _shift_rows(ext, 2)[HALO:] + sw_ref[1:2, :] * _shift_rows(ext, 1)[HALO:] + sw_ref[2:3, :] * u
        mix_ref[...] = (cb_ref[...].astype(F32) * z).astype(BF16)

    def cur(g):
        return pl.BlockSpec((None, t, w), lambda i, c: (g, i, c))

    def prev(g):
        return pl.BlockSpec((None, HALO, w), lambda i, c: (g, jnp.maximum(i * hb - 1, 0), c))

    return pl.pallas_call(
        body,
        name="sconv_fwd",
        grid=(nt, ncol),
        in_specs=[cur(4), cur(5), cur(6), prev(5), prev(6), pl.BlockSpec((3, w), lambda i, c: (0, c)), pl.BlockSpec(memory_space=pl.ANY)],
        out_specs=pl.BlockSpec((t, w), lambda i, c: (i, ncol + c)),
        out_shape=jax.ShapeDtypeStruct(mix.shape, mix.dtype),
        input_output_aliases={6: 0},
        compiler_params=_params(2),
    )(proj, proj, proj, proj, proj, sw, mix)


def _sconv_bwd(proj, sw, dmix, dproj):
    s = proj.shape[1]
    t = min(TS, s)
    nt, w = s // t, 512
    ncol, hb = GROUP_W // w, t // HALO

    def body(cb_ref, cbn_ref, cc_ref, ccp_ref, ch_ref, chp_ref, dy_ref, dyn_ref, sw_ref, dp_in, dp_ref, dsw_ref, acc):
        del dp_in
        i = pl.program_id(1)
        w0, w1, w2 = sw_ref[0:1, :], sw_ref[1:2, :], sw_ref[2:3, :]
        ccv, chv, cbv, dyv = cc_ref[...].astype(F32), ch_ref[...].astype(F32), cb_ref[...].astype(F32), dy_ref[...]
        u = ccv * chv
        up = jnp.where(i > 0, ccp_ref[...].astype(F32) * chp_ref[...].astype(F32), 0.0)
        ext = jnp.concatenate([up, u], axis=0)
        u1, u2 = _shift_rows(ext, 1)[HALO:], _shift_rows(ext, 2)[HALO:]
        z = w0 * u2 + w1 * u1 + w2 * u
        dz = dyv * cbv
        dzn = jnp.where(i < nt - 1, dyn_ref[...] * cbn_ref[...].astype(F32), 0.0)
        dze = jnp.concatenate([dz, dzn], axis=0)
        du = w2 * dz + w1 * _shift_rows(dze, -1)[:t] + w0 * _shift_rows(dze, -2)[:t]
        dp_ref[0] = (dyv * z).astype(BF16)
        dp_ref[1] = (du * chv).astype(BF16)
        dp_ref[2] = (du * ccv).astype(BF16)
        dp_ref[3] = jnp.zeros((t, w), BF16)

        @pl.when(i == 0)
        def _():
            acc[...] = jnp.zeros_like(acc)

        acc[...] += _row_select([jnp.sum(dz * uu, axis=0, keepdims=True) for uu in (u2, u1, u)])

        @pl.when(i == nt - 1)
        def _():
            dsw_ref[...] = acc[...]

    def cur(g):
        return pl.BlockSpec((None, t, w), lambda c, i: (g, i, c))

    def prev(g):
        return pl.BlockSpec((None, HALO, w), lambda c, i: (g, jnp.maximum(i * hb - 1, 0), c))

    def nxt(g):
        return pl.BlockSpec((None, HALO, w), lambda c, i: (g, jnp.minimum((i + 1) * hb, s // HALO - 1), c))

    return pl.pallas_call(
        body,
        name="sconv_bwd",
        grid=(ncol, nt),
        in_specs=[
            cur(4),
            nxt(4),
            cur(5),
            prev(5),
            cur(6),
            prev(6),
            pl.BlockSpec((t, w), lambda c, i: (i, ncol + c)),
            pl.BlockSpec((HALO, w), lambda c, i: (jnp.minimum((i + 1) * hb, s // HALO - 1), ncol + c)),
            pl.BlockSpec((3, w), lambda c, i: (0, c)),
            pl.BlockSpec(memory_space=pl.ANY),
        ],
        out_specs=[pl.BlockSpec((4, t, w), lambda c, i: (1, i, c)), pl.BlockSpec((8, w), lambda c, i: (0, c))],
        out_shape=[jax.ShapeDtypeStruct(dproj.shape, dproj.dtype), jax.ShapeDtypeStruct((8, GROUP_W), F32)],
        scratch_shapes=[pltpu.VMEM((8, w), F32)],
        input_output_aliases={9: 0},
        compiler_params=_params(2),
    )(proj, proj, proj, proj, proj, proj, dmix, dmix, sw, dproj)


def _ffn_act(a_pre, up, cw, cb):
    nd, s, f = a_pre.shape
    t = min(TS, s)
    nt, hb = s // t, t // HALO

    def body(a_ref, ap_ref, up_ref, cw_ref, cb_ref, o_ref):
        i = pl.program_id(1)
        av = a_ref[...].astype(F32)
        ext = jnp.concatenate([jnp.where(i > 0, ap_ref[...].astype(F32), 0.0), av], axis=0)
        a = cw_ref[0:1, :] * _shift_rows(ext, 2)[HALO:] + cw_ref[1:2, :] * _shift_rows(ext, 1)[HALO:] + cw_ref[2:3, :] * av
        a = a + cb_ref[...]
        o_ref[...] = (a * _sigmoid(a) * up_ref[...].astype(F32)).astype(BF16)

    cur = pl.BlockSpec((None, t, f), lambda j, i: (j, i, 0))
    prev = pl.BlockSpec((None, HALO, f), lambda j, i: (j, jnp.maximum(i * hb - 1, 0), 0))
    return pl.pallas_call(
        body,
        name="ffn_act",
        grid=(nd, nt),
        in_specs=[cur, prev, cur, pl.BlockSpec((None, 3, f), lambda j, i: (j, 0, 0)), pl.BlockSpec((None, 1, f), lambda j, i: (j, 0, 0))],
        out_specs=cur,
        out_shape=jax.ShapeDtypeStruct(a_pre.shape, BF16),
        compiler_params=_params(2),
    )(a_pre, a_pre, up, cw, cb)


def _ffn_act_bwd(a_pre, up, dact, cw, cb, ride):
    nd, s, f = a_pre.shape
    t = min(TS, s)
    nt, hb = s // t, t // HALO

    def compute(a_ref, ap_ref, an_ref, up_ref, upn_ref, da_ref, dan_ref, cw_ref, cb_ref, dap_ref, dup_ref, dsm_ref, acc):
        i = pl.program_id(1)
        w0, w1, w2 = cw_ref[0:1, :], cw_ref[1:2, :], cw_ref[2:3, :]
        av = a_ref[...].astype(F32)
        ext = jnp.concatenate([jnp.where(i > 0, ap_ref[...].astype(F32), 0.0), av, an_ref[...].astype(F32)], axis=0)
        e1, e2 = _shift_rows(ext, 1), _shift_rows(ext, 2)
        a = (w0 * e2 + w1 * e1 + w2 * ext)[HALO:] + cb_ref[...]
        sg = _sigmoid(a)
        dact_e = jnp.concatenate([da_ref[...], dan_ref[...]], axis=0).astype(F32)
        up_e = jnp.concatenate([up_ref[...], upn_ref[...]], axis=0).astype(F32)
        row = lax.broadcasted_iota(jnp.int32, a.shape, 0)
        live = (row < t) | (i < nt - 1)
        d_a = jnp.where(live, dact_e * up_e * (sg * (1.0 + a * (1.0 - sg))), 0.0)
        dup_ref[...] = (dact_e * (a * sg))[:t].astype(BF16)
        dap_ref[...] = (w2 * d_a + w1 * _shift_rows(d_a, -1) + w0 * _shift_rows(d_a, -2))[:t].astype(BF16)
        d_a_cur = d_a[:t]

        @pl.when(i == 0)
        def _():
            acc[...] = jnp.zeros_like(acc)

        sums = [jnp.sum(d_a_cur * e[HALO : HALO + t], axis=0, keepdims=True) for e in (e2, e1, ext)]
        acc[...] += _row_select(sums + [jnp.sum(d_a_cur, axis=0, keepdims=True)])

        @pl.when(i == nt - 1)
        def _():
            dsm_ref[...] = acc[...]

    cur = pl.BlockSpec((None, t, f), lambda j, i: (j, i, 0))
    prev = pl.BlockSpec((None, HALO, f), lambda j, i: (j, jnp.maximum(i * hb - 1, 0), 0))
    nxt = pl.BlockSpec((None, HALO, f), lambda j, i: (j, jnp.minimum((i + 1) * hb, s // HALO - 1), 0))
    return pl.pallas_call(
        _hosted(compute, 9, 3, 1, ride, (nd, nt)),
        name="ffn_act_bwd",
        grid=(nd, nt),
        in_specs=[cur, prev, nxt, cur, nxt, cur, nxt, pl.BlockSpec((None, 3, f), lambda j, i: (j, 0, 0)), pl.BlockSpec((None, 1, f), lambda j, i: (j, 0, 0))]
        + [HBM_SPEC] * ride.n,
        out_specs=[cur, cur, pl.BlockSpec((None, 8, f), lambda j, i: (j, 0, 0))] + [HBM_SPEC] * ride.n,
        out_shape=[jax.ShapeDtypeStruct(a_pre.shape, BF16), jax.ShapeDtypeStruct(a_pre.shape, BF16), jax.ShapeDtypeStruct((nd, 8, f), F32)]
        + ride.out_shape,
        scratch_shapes=[pltpu.VMEM((8, f), F32)] + ride.scratch,
        compiler_params=_params(2),
    )(a_pre, a_pre, a_pre, up, up, dact, dact, cw, cb, *ride.inputs)


def _softmax_rows(sc):
    m = jnp.max(sc, axis=-1, keepdims=True)
    e = jnp.exp(sc - m)
    return e / jnp.sum(e, axis=-1, keepdims=True)


def _attn_fwd(q, km, vm):
    s, d = q.shape
    m = km.shape[0]
    t = min(TS, s)
    hd = d // MEM_HEADS
    scale = hd**-0.5

    def body(q_ref, k_ref, v_ref, o_ref):
        for h in range(MEM_HEADS):
            cols = slice(h * hd, (h + 1) * hd)
            p = _softmax_rows(_dot(q_ref[:, cols], k_ref[:, cols], "nt") * scale)
            o_ref[:, cols] = _dot(p, v_ref[:, cols], "nn").astype(BF16)

    row = pl.BlockSpec((t, d), lambda i: (i, 0))
    full = pl.BlockSpec((m, d), lambda i: (0, 0))
    return pl.pallas_call(
        body,
        name="attn_fwd",
        grid=(s // t,),
        in_specs=[row, full, full],
        out_specs=row,
        out_shape=jax.ShapeDtypeStruct((s, d), BF16),
        compiler_params=_params(1),
    )(q, km, vm)


def _attn_bwd(q, km, vm, do):
    s, d = q.shape
    m = km.shape[0]
    t = min(TS, s)
    steps = s // t
    hd = d // MEM_HEADS
    scale = hd**-0.5

    def body(q_ref, k_ref, v_ref, do_ref, dq_ref, dk_ref, dv_ref):
        @pl.when(pl.program_id(0) == 0)
        def _():
            dk_ref[...] = jnp.zeros_like(dk_ref)
            dv_ref[...] = jnp.zeros_like(dv_ref)

        for h in range(MEM_HEADS):
            cols = slice(h * hd, (h + 1) * hd)
            qh, kh, vh, doh = q_ref[:, cols], k_ref[:, cols], v_ref[:, cols], do_ref[:, cols]
            p = _softmax_rows(_dot(qh, kh, "nt") * scale)
            dp = _dot(doh, vh, "nt")
            ds = p * (dp - jnp.sum(dp * p, axis=-1, keepdims=True)) * scale
            dq_ref[:, cols] = _dot(ds, kh, "nn").astype(BF16)
            dk_ref[:, cols] += _dot(ds, qh, "tn")
            dv_ref[:, cols] += _dot(p, doh, "tn")

    row = pl.BlockSpec((t, d), lambda i: (i, 0))
    full = pl.BlockSpec((m, d), lambda i: (0, 0))
    return pl.pallas_call(
        body,
        name="attn_bwd",
        grid=(steps,),
        in_specs=[row, full, full, row],
        out_specs=[row, full, full],
        out_shape=[jax.ShapeDtypeStruct((s, d), BF16), jax.ShapeDtypeStruct((m, d), F32), jax.ShapeDtypeStruct((m, d), F32)],
        compiler_params=_params(1),
    )(q, km, vm, do)


def _row_tile(rows, cols):
    best = 8
    for tr in range(8, rows + 1, 8):
        if rows % tr == 0 and tr * cols * 4 <= (1 << 20):
            best = tr
    return best


def _adamw(w, g, m, v):
    m = ADAM_B1 * m + (1.0 - ADAM_B1) * g
    v = ADAM_B2 * v + (1.0 - ADAM_B2) * (g * g)
    m_hat = m / (1.0 - ADAM_B1**ADAM_STEP)
    v_hat = v / (1.0 - ADAM_B2**ADAM_STEP)
    delta = -ADAM_LR * (m_hat / (jnp.sqrt(v_hat) + ADAM_EPS) + ADAM_WD * w)
    return delta, m, v


def _adam_shard(name, parts, w, m, v):
    rows, cols = w.shape
    tr = _row_tile(rows, cols)

    def body(*refs):
        part_refs = refs[:N_DEV]
        w_ref, m_ref, v_ref, go_ref, d_ref, mo_ref, vo_ref = refs[N_DEV:]
        g = part_refs[0][...].astype(F32)
        for dev in range(1, N_DEV):
            g = g + part_refs[dev][...].astype(F32)
        go_ref[...] = g
        d_ref[...], mo_ref[...], vo_ref[...] = _adamw(w_ref[...], g, m_ref[...], v_ref[...])

    def part(dev):
        return pl.BlockSpec((None, tr, cols), lambda i: (dev, i, 0))

    flat = pl.BlockSpec((tr, cols), lambda i: (i, 0))
    return pl.pallas_call(
        body,
        name=name,
        grid=(rows // tr,),
        in_specs=[part(dev) for dev in range(N_DEV)] + [flat, flat, flat],
        out_specs=[flat, flat, flat, flat],
        out_shape=[jax.ShapeDtypeStruct((rows, cols), F32)] * 4,
        compiler_params=_params(1),
    )(*([parts] * N_DEV), w, m, v)


def _sum_devices(gathered):
    def body(g_ref, o_ref):
        tot = g_ref[0]
        for dev in range(1, N_DEV):
            tot = tot + g_ref[dev]
        o_ref[...] = tot

    return pl.pallas_call(body, name="sum_devices", out_shape=jax.ShapeDtypeStruct(gathered.shape[1:], F32))(gathered)


def _adam_packed(w, g, m, v):
    def body(w_ref, g_ref, m_ref, v_ref, d_ref, mo_ref, vo_ref):
        d_ref[...], mo_ref[...], vo_ref[...] = _adamw(w_ref[...], g_ref[...], m_ref[...], v_ref[...])

    return pl.pallas_call(body, name="adam_packed", out_shape=[jax.ShapeDtypeStruct(w.shape, F32)] * 3)(w, g, m, v)


def _padded(size):
    return -(-size // 128) * 128


def _pack(parts):
    flat = []
    for p in parts:
        v = p.reshape(-1)
        flat.append(jnp.pad(v, (0, _padded(v.shape[0]) - v.shape[0])))
    return jnp.concatenate(flat).reshape(-1, 128)


def _unpack(packed, shapes):
    flat = packed.reshape(-1)
    out, pos = [], 0
    for shp in shapes:
        size = math.prod(shp)
        out.append(flat[pos : pos + size].reshape(shp))
        pos += _padded(size)
    return out


WEIGHTS = (
    "hgrn_lb norm1_w w_in hgrn_norm_w sconv_w w_out norm2_w mem_norm_w wq wk wv wo norm3_w w_gate w_up ffn_conv_w "
    "ffn_conv_b w_down final_norm_w"
).split()
BIG = ["w_in", "w_out", "wq", "wk", "wv", "wo", "w_gate", "w_up", "w_down"]
SMALL_SHARDED = ["sconv_w", "ffn_conv_w"]
SMALL = [n for n in WEIGHTS if n not in BIG]


def kernel(x, mem, hgrn_lb, norm1_w, w_in, hgrn_norm_w, sconv_w, w_out, norm2_w, mem_norm_w, wq, wk, wv, wo, norm3_w, w_gate, w_up, ffn_conv_w, ffn_conv_b, w_down, final_norm_w, loss_target, m_hgrn_lb, m_norm1_w, m_w_in, m_hgrn_norm_w, m_sconv_w, m_w_out, m_norm2_w, m_mem_norm_w, m_wq, m_wk, m_wv, m_wo, m_norm3_w, m_w_gate, m_w_up, m_ffn_conv_w, m_ffn_conv_b, m_w_down, m_final_norm_w, v_hgrn_lb, v_norm1_w, v_w_in, v_hgrn_norm_w, v_sconv_w, v_w_out, v_norm2_w, v_mem_norm_w, v_wq, v_wk, v_wv, v_wo, v_norm3_w, v_w_gate, v_w_up, v_ffn_conv_w, v_ffn_conv_b, v_w_down, v_final_norm_w):
    given = dict(locals())
    w_of = {n: given[n] for n in WEIGHTS}
    m_of = {n: given["m_" + n] for n in WEIGHTS}
    v_of = {n: given["v_" + n] for n in WEIGHTS}
    x2, mem2, target = x[0], mem[0], loss_target[0]
    d = x2.shape[1]
    xi, yi, ci = lax.axis_index("x"), lax.axis_index("y"), lax.axis_index("c")
    me = 4 * xi + 2 * yi + ci
    final_w = final_norm_w.reshape(1, d)

    shard = {n: w_of[n][0].astype(BF16) for n in BIG}
    small_shard = jnp.concatenate([sconv_w[0], ffn_conv_w[0]], axis=1)
    w_in_g, small_g = _allgather("gather_w_in", [shard["w_in"], small_shard])
    w_in_f = jnp.transpose(w_in_g, (1, 0, 2)).reshape(d, -1)
    conv_ch = sconv_w.shape[2]
    sw_f = jnp.transpose(small_g[:, :, :conv_ch], (1, 0, 2)).reshape(3, -1)
    cw_g = small_g[:, :, conv_ch:]
    cb_g = ffn_conv_b.reshape(N_DEV, 1, -1)

    lb = _lb_fwd(hgrn_lb)
    h1 = _rmsnorm("norm1", x2, norm1_w)
    proj, *attn_w = _mm_proj(h1, w_in_f, _GatherRide([shard[n] for n in ("w_out", "wq", "wk", "wv", "wo")]))
    w_out_f, wq_f, wk_f, wv_f, wo_f = (a.reshape(d, d) for a in attn_w)
    mix, o_pre, states, wg_g, wu_g, wd_g = _hgrn_fwd(proj, lb, hgrn_norm_w, _GatherRide([shard[n] for n in ("w_gate", "w_up", "w_down")]))
    mix = _sconv_fwd(proj, sw_f, mix)
    x1 = _mm_nn("mm_out", mix, w_out_f, F32, residual=x2)
    h2 = _rmsnorm("norm2", x1, norm2_w)
    mem_n = _rmsnorm("norm_mem", mem2, mem_norm_w)
    q = _mm_nn("mm_q", h2, wq_f, BF16)
    km = _mm_nn("mm_k", mem_n, wk_f, BF16)
    vm = _mm_nn("mm_v", mem_n, wv_f, BF16)
    ao = _attn_fwd(q, km, vm)
    x2r = _mm_nn("mm_o", ao, wo_f, F32, residual=x1)
    h3 = _rmsnorm("norm3", x2r, norm3_w)
    a_pre, up = _mm_gate_up(h3, wg_g, wu_g)
    act = _ffn_act(a_pre, up, cw_g, cb_g)
    x3 = _mm_down(act, wd_g, x2r)

    dx3, dx3b, loss_blk, d_final = _loss_head(x3, target, final_w)
    parts = {}
    dact = _mm_dact(dx3b, wd_g)
    dwd = _mm_dwdown(act, dx3b)
    dapre, dup, d_ffn_small, parts["w_down"] = _ffn_act_bwd(a_pre, up, dact, cw_g, cb_g, _ScatterRide([dwd]))
    dwg, dwu = _mm_dw_cols("mm_dw_gate_up", h3, [dapre, dup])
    dh3, parts["w_gate"], parts["w_up"] = _mm_dh3(dapre, dup, wg_g, wu_g, _ScatterRide([dwg, dwu]))
    dx2, dx2b, d_n3 = _rmsnorm_bwd("norm3_bwd", dh3, x2r, norm3_w, dx3)
    dao = _mm_nt("mm_dao", [(dx2b, wo_f)], BF16)
    dwo = _mm_tn("mm_dwo", ao, dx2b)
    dq, dkm, dvm = _attn_bwd(q, km, vm, dao)
    dwq = _mm_tn("mm_dwq", h2, dq)
    dh2 = _mm_nt("mm_dh2", [(dq, wq_f)], F32)
    dwk = _mm_tn("mm_dwk", mem_n, dkm)
    dwv = _mm_tn("mm_dwv", mem_n, dvm)
    dmem_n = _mm_nt("mm_dmem", [(dkm, wk_f), (dvm, wv_f)], F32)
    _, _, d_nm = _rmsnorm_bwd("norm_mem_bwd", dmem_n, mem2, mem_norm_w)
    dx1, dx1b, d_n2 = _rmsnorm_bwd("norm2_bwd", dh2, x1, norm2_w, dx2)
    dmix = _mm_nt("mm_dmix", [(dx1b, w_out_f)], F32)
    dwout = _mm_tn("mm_dwout", mix, dx1b)
    attn_names = ["w_out", "wq", "wk", "wv", "wo"]
    attn_grads = [a.reshape(N_DEV, d // N_DEV, d) for a in (dwout, dwq, dwk, dwv, dwo)]
    dproj, d_lb, d_nw, *attn_parts = _hgrn_bwd(proj, lb, hgrn_norm_w, o_pre, states, dmix, _ScatterRide(attn_grads))
    parts.update(zip(attn_names, attn_parts))
    dproj, d_sw = _sconv_bwd(proj, sw_f, dmix, dproj)
    dwin = _mm_dwin(h1, dproj, N_GROUPS)
    dwin_g = jnp.transpose(dwin.reshape(d, N_DEV, -1), (1, 0, 2))
    dh1, parts["w_in"] = _mm_dh1(dproj, w_in_f, N_GROUPS, _ScatterRide([dwin_g]))
    grad_x, _, d_n1 = _rmsnorm_bwd("norm1_bwd", dh1, x2, norm1_w, dx1)
    d_hlb = _lb_bwd(hgrn_lb, d_lb)

    small_parts = [d_hlb, d_n1, d_nw, d_sw[0:3], d_n2, d_nm, d_n3, d_ffn_small[:, 0:3, :], d_ffn_small[:, 3, :], d_final, loss_blk]
    small_shapes = [p.shape for p in small_parts]
    total = _sum_devices(_allgather("gather_small", [_pack(small_parts)])[0])
    g_hlb, g_n1, g_nw, g_sw, g_n2, g_nm, g_n3, g_cw, g_cb, g_final, loss_row = _unpack(total, small_shapes)
    grads = {
        "hgrn_lb": g_hlb,
        "norm1_w": g_n1,
        "hgrn_norm_w": g_nw,
        "sconv_w": lax.dynamic_slice_in_dim(g_sw, me * conv_ch, conv_ch, axis=1).reshape(sconv_w.shape),
        "norm2_w": g_n2,
        "mem_norm_w": g_nm,
        "norm3_w": g_n3,
        "ffn_conv_w": lax.dynamic_index_in_dim(g_cw, me, axis=0, keepdims=True),
        "ffn_conv_b": g_cb.reshape(ffn_conv_b.shape),
        "final_norm_w": g_final.reshape(final_norm_w.shape),
    }
    shapes = [w_of[n].shape for n in SMALL]
    deltas_p, new_m_p, new_v_p = _adam_packed(
        _pack([w_of[n] for n in SMALL]), _pack([grads[n] for n in SMALL]), _pack([m_of[n] for n in SMALL]), _pack([v_of[n] for n in SMALL])
    )
    deltas = dict(zip(SMALL, _unpack(deltas_p, shapes)))
    new_m = dict(zip(SMALL, _unpack(new_m_p, shapes)))
    new_v = dict(zip(SMALL, _unpack(new_v_p, shapes)))

    for n in BIG:
        shp = w_of[n].shape
        gw, dw, mw, vw = _adam_shard("adam_" + n, parts[n], w_of[n][0], m_of[n][0], v_of[n][0])
        grads[n], deltas[n], new_m[n], new_v[n] = (a.reshape(shp) for a in (gw, dw, mw, vw))

    loss = loss_row[0, 0]
    return (
        loss,
        grad_x.reshape(x.shape),
        *[grads[n] for n in WEIGHTS],
        *[deltas[n] for n in WEIGHTS],
        *[new_m[n] for n in WEIGHTS],
        *[new_v[n] for n in WEIGHTS],
    )
```

```python
import functools
import math

import jax
import jax.numpy as jnp
from jax import lax
from jax.experimental import pallas as pl
from jax.experimental.pallas import tpu as pltpu

F32 = jnp.float32
BF16 = jnp.bfloat16
MESH = pl.DeviceIdType.MESH
N_DEV = 8
EPS = 1e-6
CHUNK = 64
HGRN_HEADS = 8
HEAD_DIM = 128
HEADS_PER_STEP = 2
MEM_HEADS = 4
GROUP_W = HGRN_HEADS * HEAD_DIM
N_GROUPS = 7
HALO = 16
TS = 512
TR = 256
TM = 1024
TN = 1024
TK = 1024
VMEM_LIMIT = 48 * 1024 * 1024

ADAM_LR = 0.001
ADAM_B1 = 0.9
ADAM_B2 = 0.999
ADAM_EPS = 1e-08
ADAM_WD = 0.01
ADAM_STEP = 10

_DIMS = {
    "nn": (((1,), (0,)), ((), ())),
    "nt": (((1,), (1,)), ((), ())),
    "tn": (((0,), (0,)), ((), ())),
}


def _params(n_axes):
    return pltpu.CompilerParams(dimension_semantics=("arbitrary",) * n_axes, vmem_limit_bytes=VMEM_LIMIT)


def _dot(a, b, mode):
    return lax.dot_general(a.astype(BF16), b.astype(BF16), _DIMS[mode], preferred_element_type=F32)


def _sigmoid(v):
    return 0.5 * jnp.tanh(0.5 * v) + 0.5


def _block_dims(spec):
    return tuple(d for d in spec.block_shape if d is not None)


HBM_SPEC = pl.BlockSpec(memory_space=pltpu.HBM)


def _place():
    x, y, c = lax.axis_index("x"), lax.axis_index("y"), lax.axis_index("c")
    return x, y, c, [(1 - x, y), (x, 1 - y), (1 - x, 1 - y)]


def _slab(p):
    return 4 * p[0] + 2 * p[1] + p[2]


def _remote(src, dst, send_sem, recv_sem, to):
    return pltpu.make_async_remote_copy(src_ref=src, dst_ref=dst, send_sem=send_sem, recv_sem=recv_sem, device_id=to, device_id_type=MESH)


class _GatherRide:
    def __init__(self, shards):
        self.inputs = list(shards)
        self.n = len(shards)
        self.out_shape = [jax.ShapeDtypeStruct((N_DEV, *a.shape), a.dtype) for a in shards]
        self.scratch = [pltpu.SemaphoreType.DMA((self.n, 7)), pltpu.SemaphoreType.DMA((self.n, 7)), pltpu.SemaphoreType.DMA((self.n,))]

    @staticmethod
    def _copy(outs, sems, a, k, block, to, src=None):
        dst = outs[a].at[_slab(block)]
        return _remote(dst if src is None else src, dst, sems[0].at[a, k], sems[1].at[a, k], to)

    def _mine(self, ins, outs, sems, a):
        x, y, c, chips = _place()
        local = pltpu.make_async_copy(ins[a], outs[a].at[_slab((x, y, c))], sems[2].at[a])
        first = [self._copy(outs, sems, a, 0, (x, y, c), (x, y, 1 - c), src=ins[a])]
        first += [self._copy(outs, sems, a, 1 + j, (x, y, c), (*chip, c), src=ins[a]) for j, chip in enumerate(chips)]
        return local, first

    def start(self, ins, outs, sems):
        for a in range(self.n):
            local, first = self._mine(ins, outs, sems, a)
            local.start()
            for cp in first:
                cp.start()

    def forward(self, ins, outs, sems):
        x, y, c, chips = _place()
        for j, chip in enumerate(chips):
            for a in range(self.n):
                self._copy(outs, sems, a, 1 + j, (*chip, c), (x, y, c)).wait_recv()
                self._copy(outs, sems, a, 4 + j, (*chip, c), (x, y, 1 - c)).start()

    def finish(self, ins, outs, sems):
        x, y, c, chips = _place()
        for a in range(self.n):
            self._copy(outs, sems, a, 0, (x, y, 1 - c), (x, y, c)).wait_recv()
            for j, chip in enumerate(chips):
                self._copy(outs, sems, a, 4 + j, (*chip, 1 - c), (x, y, c)).wait_recv()
        for a in range(self.n):
            local, first = self._mine(ins, outs, sems, a)
            for cp in first:
                cp.wait_send()
            for j, chip in enumerate(chips):
                self._copy(outs, sems, a, 4 + j, (*chip, c), (x, y, 1 - c)).wait_send()
            local.wait()


class _ScatterRide:
    def __init__(self, grads):
        self.inputs = list(grads)
        self.n = len(grads)
        self.out_shape = [jax.ShapeDtypeStruct(g.shape, g.dtype) for g in grads]
        self.scratch = [pltpu.SemaphoreType.DMA((self.n, 7)), pltpu.SemaphoreType.DMA((self.n, 7)), pltpu.SemaphoreType.DMA((self.n,))]

    @staticmethod
    def _peers():
        x, y, c, chips = _place()
        return _slab((x, y, c)), [(x, y, 1 - c)] + [(*chip, c) for chip in chips] + [(*chip, 1 - c) for chip in chips]

    def _sends(self, ins, outs, sems, a):
        me, peers = self._peers()
        local = pltpu.make_async_copy(ins[a].at[me], outs[a].at[me], sems[2].at[a])
        return [local] + [_remote(ins[a].at[_slab(p)], outs[a].at[me], sems[0].at[a, k], sems[1].at[a, k], p) for k, p in enumerate(peers)]

    def start(self, ins, outs, sems):
        for a in range(self.n):
            for cp in self._sends(ins, outs, sems, a):
                cp.start()

    def forward(self, ins, outs, sems):
        pass

    def finish(self, ins, outs, sems):
        me, peers = self._peers()
        for a in range(self.n):
            for k, p in enumerate(peers):
                _remote(ins[a].at[me], outs[a].at[_slab(p)], sems[0].at[a, k], sems[1].at[a, k], p).wait_recv()
            local, *sends = self._sends(ins, outs, sems, a)
            for cp in sends:
                cp.wait_send()
            local.wait()


def _grid_step(grid):
    step = 0
    for axis, size in enumerate(grid):
        step = step * size + pl.program_id(axis)
    return step


def _ride_begin(ride, grid, ins, outs, sems):
    step, n_steps = _grid_step(grid), math.prod(grid)

    @pl.when(step == 0)
    def _():
        ride.start(ins, outs, sems)

    @pl.when(step == (3 * n_steps) // 4)
    def _():
        ride.forward(ins, outs, sems)


def _ride_end(ride, grid, ins, outs, sems):
    @pl.when(_grid_step(grid) == math.prod(grid) - 1)
    def _():
        ride.finish(ins, outs, sems)


def _allgather(name, shards):
    ride = _GatherRide(shards)
    n = ride.n

    def body(*refs):
        ins, outs, sems = refs[:n], refs[n : 2 * n], refs[2 * n :]
        ride.start(ins, outs, sems)
        ride.forward(ins, outs, sems)
        ride.finish(ins, outs, sems)

    return pl.pallas_call(
        body,
        name=name,
        in_specs=[HBM_SPEC] * n,
        out_specs=[HBM_SPEC] * n,
        out_shape=ride.out_shape,
        scratch_shapes=ride.scratch,
    )(*shards)


def _matmul(name, grid, k_axis, pairs, outs, mode, residual=None, ride=None):
    n_pairs, n_out, has_res = len(pairs), len(outs), residual is not None
    k_steps = grid[k_axis] if k_axis is not None else 1
    n_acc = n_out if k_steps > 1 else 0
    n_ride = ride.n if ride is not None else 0

    def body(*refs):
        ins = refs[: 2 * n_pairs]
        pos = 2 * n_pairs
        res_ref = refs[pos] if has_res else None
        pos += int(has_res)
        ride_ins = refs[pos : pos + n_ride]
        pos += n_ride
        out_refs = refs[pos : pos + n_out]
        pos += n_out
        ride_outs = refs[pos : pos + n_ride]
        pos += n_ride
        acc_refs = refs[pos : pos + n_acc]
        ride_sems = refs[pos + n_acc :]
        if ride is not None:
            _ride_begin(ride, grid, ride_ins, ride_outs, ride_sems)

        def partial(o):
            tot = None
            for p in outs[o][3]:
                d = _dot(ins[2 * p][...], ins[2 * p + 1][...], mode)
                tot = d if tot is None else tot + d
            return tot

        if k_steps == 1:
            for o in range(n_out):
                val = partial(o)
                if has_res and o == 0:
                    val = val + res_ref[...]
                out_refs[o][...] = val.astype(out_refs[o].dtype)
        else:
            k = pl.program_id(k_axis)

            @pl.when(k == 0)
            def _():
                for o in range(n_out):
                    if has_res and o == 0:
                        acc_refs[o][...] = res_ref[...]
                    else:
                        acc_refs[o][...] = jnp.zeros_like(acc_refs[o])

            for o in range(n_out):
                acc_refs[o][...] += partial(o)

            @pl.when(k == k_steps - 1)
            def _():
                for o in range(n_out):
                    out_refs[o][...] = acc_refs[o][...].astype(out_refs[o].dtype)

        if ride is not None:
            _ride_end(ride, grid, ride_ins, ride_outs, ride_sems)

    args, in_specs = [], []
    for a, a_spec, b, b_spec in pairs:
        args += [a, b]
        in_specs += [a_spec, b_spec]
    if has_res:
        args.append(residual[0])
        in_specs.append(residual[1])
    out_specs = [o[2] for o in outs]
    out_shape = [jax.ShapeDtypeStruct(o[0], o[1]) for o in outs]
    scratch = [pltpu.VMEM(_block_dims(o[2]), F32) for o in outs] if k_steps > 1 else []
    if ride is not None:
        args += ride.inputs
        in_specs += [HBM_SPEC] * n_ride
        out_specs += [HBM_SPEC] * n_ride
        out_shape += ride.out_shape
        scratch += ride.scratch
    return pl.pallas_call(
        body,
        name=name,
        grid=grid,
        in_specs=in_specs,
        out_specs=out_specs,
        out_shape=out_shape,
        scratch_shapes=scratch,
        compiler_params=_params(len(grid)),
    )(*args)


def _out_tiles(m, n, out_dtype):
    if out_dtype == F32 and m >= TM:
        return TM // 2, n
    return min(TM, m), min(TN, n)


def _mm_nn(name, a, w, out_dtype, residual=None):
    m, k = a.shape
    n = w.shape[1]
    tm, tn = _out_tiles(m, n, out_dtype)
    pairs = [(a, pl.BlockSpec((tm, k), lambda i, j: (i, 0)), w, pl.BlockSpec((k, tn), lambda i, j: (0, j)))]
    o_spec = pl.BlockSpec((tm, tn), lambda i, j: (i, j))
    res = (residual, o_spec) if residual is not None else None
    return _matmul(name, (m // tm, n // tn), None, pairs, [((m, n), out_dtype, o_spec, [0])], "nn", res)[0]


def _mm_nt(name, pairs_in, out_dtype):
    m, k = pairs_in[0][0].shape
    n = pairs_in[0][1].shape[0]
    tm, tn = _out_tiles(m, n, out_dtype)
    pairs = [
        (a, pl.BlockSpec((tm, k), lambda i, j: (i, 0)), w, pl.BlockSpec((tn, k), lambda i, j: (j, 0)))
        for a, w in pairs_in
    ]
    o_spec = pl.BlockSpec((tm, tn), lambda i, j: (i, j))
    outs = [((m, n), out_dtype, o_spec, list(range(len(pairs))))]
    return _matmul(name, (m // tm, n // tn), None, pairs, outs, "nt")[0]


def _mm_tn(name, a, b):
    s, i_dim = a.shape
    n = b.shape[1]
    ts, ti, tn = min(TK, s), min(TN, i_dim), min(TN, n)
    pairs = [(a, pl.BlockSpec((ts, ti), lambda i, j, k: (k, i)), b, pl.BlockSpec((ts, tn), lambda i, j, k: (k, j)))]
    outs = [((i_dim, n), BF16, pl.BlockSpec((ti, tn), lambda i, j, k: (i, j)), [0])]
    return _matmul(name, (i_dim // ti, n // tn, s // ts), 2, pairs, outs, "tn")[0]


def _mm_proj(h1, w_in, ride):
    s, d = h1.shape
    g = w_in.shape[1] // GROUP_W
    tm = min(TM, s)
    pairs = [(h1, pl.BlockSpec((tm, d), lambda i, j: (i, 0)), w_in, pl.BlockSpec((d, GROUP_W), lambda i, j: (0, j)))]
    outs = [((g, s, GROUP_W), BF16, pl.BlockSpec((None, tm, GROUP_W), lambda i, j: (j, i, 0)), [0])]
    return _matmul("mm_proj", (s // tm, g), None, pairs, outs, "nn", ride=ride)


def _mm_gate_up(h3, wg_g, wu_g, ride):
    s, d = h3.shape
    f = wg_g.shape[2]
    tm = min(TM, s)
    a_spec = pl.BlockSpec((tm, d), lambda i, j: (i, 0))
    w_spec = pl.BlockSpec((None, d, f), lambda i, j: (j, 0, 0))
    o_spec = pl.BlockSpec((None, tm, f), lambda i, j: (j, i, 0))
    pairs = [(h3, a_spec, wg_g, w_spec), (h3, a_spec, wu_g, w_spec)]
    outs = [((N_DEV, s, f), BF16, o_spec, [0]), ((N_DEV, s, f), BF16, o_spec, [1])]
    return _matmul("mm_gate_up", (s // tm, N_DEV), None, pairs, outs, "nn", ride=ride)


def _mm_down(act, wd_g, x2):
    _, s, f = act.shape
    d = wd_g.shape[2]
    tm, tn = min(TM, s), min(TN, d)
    pairs = [
        (
            act,
            pl.BlockSpec((None, tm, f), lambda i, j, k: (k, i, 0)),
            wd_g,
            pl.BlockSpec((None, f, tn), lambda i, j, k: (k, 0, j)),
        )
    ]
    o_spec = pl.BlockSpec((tm, tn), lambda i, j, k: (i, j))
    outs = [((s, d), F32, o_spec, [0])]
    return _matmul("mm_down", (s // tm, d // tn, N_DEV), 2, pairs, outs, "nn", (x2, o_spec))[0]


def _mm_dact(dx3b, wd_g):
    s, d = dx3b.shape
    f = wd_g.shape[1]
    tm = min(TM, s)
    pairs = [(dx3b, pl.BlockSpec((tm, d), lambda i, j: (i, 0)), wd_g, pl.BlockSpec((None, f, d), lambda i, j: (j, 0, 0)))]
    outs = [((N_DEV, s, f), BF16, pl.BlockSpec((None, tm, f), lambda i, j: (j, i, 0)), [0])]
    return _matmul("mm_dact", (s // tm, N_DEV), None, pairs, outs, "nt")[0]


def _mm_dwdown(act, dx3b):
    _, s, f = act.shape
    d = dx3b.shape[1]
    ts, tn = min(TK, s), min(TN, d)
    pairs = [
        (
            act,
            pl.BlockSpec((None, ts, f), lambda j, n, k: (j, k, 0)),
            dx3b,
            pl.BlockSpec((ts, tn), lambda j, n, k: (k, n)),
        )
    ]
    outs = [((N_DEV, f, d), BF16, pl.BlockSpec((None, f, tn), lambda j, n, k: (j, 0, n)), [0])]
    return _matmul("mm_dwdown", (N_DEV, d // tn, s // ts), 2, pairs, outs, "tn")[0]


def _mm_dh3(dapre, dup, wg_g, wu_g, ride):
    _, s, f = dapre.shape
    d = wg_g.shape[1]
    tm, tn = min(TM, s), min(TN, d)
    a_spec = pl.BlockSpec((None, tm, f), lambda i, j, k: (k, i, 0))
    w_spec = pl.BlockSpec((None, tn, f), lambda i, j, k: (k, j, 0))
    pairs = [(dapre, a_spec, wg_g, w_spec), (dup, a_spec, wu_g, w_spec)]
    outs = [((s, d), F32, pl.BlockSpec((tm, tn), lambda i, j, k: (i, j)), [0, 1])]
    return _matmul("mm_dh3", (s // tm, d // tn, N_DEV), 2, pairs, outs, "nt", ride=ride)


def _mm_dw_cols(name, h, dcols_list):
    s, d = h.shape
    f = dcols_list[0].shape[2]
    ts, ti = min(TK, s), min(TN, d)
    a_spec = pl.BlockSpec((ts, ti), lambda j, i, k: (k, i))
    b_spec = pl.BlockSpec((None, ts, f), lambda j, i, k: (j, k, 0))
    o_spec = pl.BlockSpec((None, ti, f), lambda j, i, k: (j, i, 0))
    pairs = [(h, a_spec, dc, b_spec) for dc in dcols_list]
    outs = [((N_DEV, d, f), BF16, o_spec, [p]) for p in range(len(pairs))]
    return _matmul(name, (N_DEV, d // ti, s // ts), 2, pairs, outs, "tn")


def _mm_dwin(h1, dproj, groups):
    s, d = h1.shape
    ts, ti = min(TK, s), min(TN, d)
    pairs = [
        (
            h1,
            pl.BlockSpec((ts, ti), lambda j, i, k: (k, i)),
            dproj,
            pl.BlockSpec((None, ts, GROUP_W), lambda j, i, k: (j, k, 0)),
        )
    ]
    outs = [((d, groups * GROUP_W), BF16, pl.BlockSpec((ti, GROUP_W), lambda j, i, k: (i, j)), [0])]
    return _matmul("mm_dwin", (groups, d // ti, s // ts), 2, pairs, outs, "tn")[0]


def _mm_dh1(dproj, w_in, groups, ride):
    s = dproj.shape[1]
    d = w_in.shape[0]
    tm, tn = min(TM, s), min(TN, d)
    pairs = [
        (
            dproj,
            pl.BlockSpec((None, tm, GROUP_W), lambda i, j, k: (k, i, 0)),
            w_in,
            pl.BlockSpec((tn, GROUP_W), lambda i, j, k: (j, k)),
        )
    ]
    outs = [((s, d), F32, pl.BlockSpec((tm, tn), lambda i, j, k: (i, j)), [0])]
    return _matmul("mm_dh1", (s // tm, d // tn, groups), 2, pairs, outs, "nt", ride=ride)


def _rows_to_8(v):
    t, d = v.shape
    return jnp.sum(v.reshape(t // 8, 8, d), axis=0)


def _rmsnorm(name, x, w, ride=None):
    s, d = x.shape
    t = min(TS, s)
    grid = (s // t,)

    def compute(x_ref, w_ref, o_ref):
        xv = x_ref[...]
        r = lax.rsqrt(jnp.mean(xv * xv, axis=-1, keepdims=True) + EPS)
        o_ref[...] = (xv * r * w_ref[...]).astype(o_ref.dtype)

    in_specs = [pl.BlockSpec((t, d), lambda i: (i, 0)), pl.BlockSpec((1, d), lambda i: (0, 0))]
    out_spec = pl.BlockSpec((t, d), lambda i: (i, 0))
    out_shape = jax.ShapeDtypeStruct((s, d), BF16)
    if ride is None:
        return pl.pallas_call(compute, name=name, grid=grid, in_specs=in_specs, out_specs=out_spec, out_shape=out_shape, compiler_params=_params(1))(x, w)
    return pl.pallas_call(
        _hosted(compute, 2, 1, 0, ride, grid),
        name=name,
        grid=grid,
        in_specs=in_specs + [HBM_SPEC] * ride.n,
        out_specs=[out_spec] + [HBM_SPEC] * ride.n,
        out_shape=[out_shape] + ride.out_shape,
        scratch_shapes=ride.scratch,
        compiler_params=_params(1),
    )(x, w, *ride.inputs)


def _rmsnorm_bwd(name, dh, x, w, res=None):
    s, d = x.shape
    t = min(TR, s)
    steps = s // t
    has_res = res is not None

    def body(*refs):
        if has_res:
            dh_ref, x_ref, w_ref, res_ref, dx_ref, dxb_ref, dw_ref, acc = refs
        else:
            dh_ref, x_ref, w_ref, dx_ref, dxb_ref, dw_ref, acc = refs
        i = pl.program_id(0)
        xv = x_ref[...]
        r = lax.rsqrt(jnp.mean(xv * xv, axis=-1, keepdims=True) + EPS)
        xn = xv * r
        dhv = dh_ref[...].astype(F32)
        dxn = dhv * w_ref[...]
        dx = r * (dxn - xn * jnp.mean(dxn * xn, axis=-1, keepdims=True))
        if has_res:
            dx = dx + res_ref[...]
        dx_ref[...] = dx
        dxb_ref[...] = dx.astype(BF16)

        @pl.when(i == 0)
        def _():
            acc[...] = jnp.zeros_like(acc)

        acc[...] += _rows_to_8(dhv * xn)

        @pl.when(i == steps - 1)
        def _():
            dw_ref[...] = jnp.sum(acc[...], axis=0, keepdims=True)

    row = pl.BlockSpec((t, d), lambda i: (i, 0))
    vec = pl.BlockSpec((1, d), lambda i: (0, 0))
    args = [dh, x, w] + ([res] if has_res else [])
    return pl.pallas_call(
        body,
        name=name,
        grid=(steps,),
        in_specs=[row, row, vec] + ([row] if has_res else []),
        out_specs=[row, row, vec],
        out_shape=[
            jax.ShapeDtypeStruct((s, d), F32),
            jax.ShapeDtypeStruct((s, d), BF16),
            jax.ShapeDtypeStruct((1, d), F32),
        ],
        scratch_shapes=[pltpu.VMEM((8, d), F32)],
        compiler_params=_params(1),
    )(*args)


def _loss_head(x3, target, w):
    s, d = x3.shape
    t = min(TR, s)
    steps = s // t

    def body(x_ref, t_ref, w_ref, dx_ref, dxb_ref, loss_ref, dw_ref, acc_l, acc_w):
        i = pl.program_id(0)
        xv = x_ref[...]
        wv = w_ref[...]
        r = lax.rsqrt(jnp.mean(xv * xv, axis=-1, keepdims=True) + EPS)
        xn = xv * r
        err = xn * wv - t_ref[...]
        dy = err * (1.0 / d)
        dxn = dy * wv
        dx = r * (dxn - xn * jnp.mean(dxn * xn, axis=-1, keepdims=True))
        dx_ref[...] = dx
        dxb_ref[...] = dx.astype(BF16)

        @pl.when(i == 0)
        def _():
            acc_l[...] = jnp.zeros_like(acc_l)
            acc_w[...] = jnp.zeros_like(acc_w)

        acc_l[...] += _rows_to_8(err * err)
        acc_w[...] += _rows_to_8(dy * xn)

        @pl.when(i == steps - 1)
        def _():
            tot = jnp.sum(jnp.sum(acc_l[...], axis=0, keepdims=True), axis=1, keepdims=True) * (0.5 / d)
            loss_ref[...] = jnp.broadcast_to(tot, loss_ref.shape)
            dw_ref[...] = jnp.sum(acc_w[...], axis=0, keepdims=True)

    row = pl.BlockSpec((t, d), lambda i: (i, 0))
    vec = pl.BlockSpec((1, d), lambda i: (0, 0))
    return pl.pallas_call(
        body,
        name="loss_head",
        grid=(steps,),
        in_specs=[row, row, vec],
        out_specs=[row, row, pl.BlockSpec((1, 128), lambda i: (0, 0)), vec],
        out_shape=[
            jax.ShapeDtypeStruct((s, d), F32),
            jax.ShapeDtypeStruct((s, d), BF16),
            jax.ShapeDtypeStruct((1, 128), F32),
            jax.ShapeDtypeStruct((1, d), F32),
        ],
        scratch_shapes=[pltpu.VMEM((8, d), F32), pltpu.VMEM((8, d), F32)],
        compiler_params=_params(1),
    )(x3, target, w)


def _lb_fwd(hgrn_lb):
    def body(p_ref, lb_ref):
        p = p_ref[...]
        m = jnp.max(p, axis=0, keepdims=True)
        e = jnp.exp(p - m)
        lb_ref[...] = e[0:1, :] / jnp.sum(e, axis=0, keepdims=True)

    return pl.pallas_call(body, name="lb_fwd", out_shape=jax.ShapeDtypeStruct((1, hgrn_lb.shape[1]), F32))(hgrn_lb)


def _lb_bwd(hgrn_lb, dlb):
    def body(p_ref, d_ref, o_ref):
        p = p_ref[...]
        m = jnp.max(p, axis=0, keepdims=True)
        e = jnp.exp(p - m)
        sm = e / jnp.sum(e, axis=0, keepdims=True)
        p0 = sm[0:1, :]
        row = lax.broadcasted_iota(jnp.int32, sm.shape, 0)
        o_ref[...] = d_ref[...] * p0 * (jnp.where(row == 0, 1.0, 0.0) - sm)

    return pl.pallas_call(body, name="lb_bwd", out_shape=jax.ShapeDtypeStruct(hgrn_lb.shape, F32))(hgrn_lb, dlb)


def _split3(v):
    hi = v.astype(BF16)
    r1 = v - hi.astype(F32)
    mid = r1.astype(BF16)
    lo = (r1 - mid.astype(F32)).astype(BF16)
    return hi, mid, lo


def _pair(v):
    hi = v.astype(BF16)
    return hi, (v - hi.astype(F32)).astype(BF16)


def _dot_pairs(a2, b2, mode):
    dims = _DIMS[mode]
    out = lax.dot_general(a2[0], b2[0], dims, preferred_element_type=F32)
    if a2[1] is not None:
        out = lax.dot_general(a2[1], b2[0], dims, preferred_element_type=F32) + out
    if b2[1] is not None:
        out = lax.dot_general(a2[0], b2[1], dims, preferred_element_type=F32) + out
    return out


def _tri_dot(tri, v):
    hi, mid, lo = _split3(v)
    dims = _DIMS["nn"]
    out = lax.dot_general(tri, lo, dims, preferred_element_type=F32)
    out = out + lax.dot_general(tri, mid, dims, preferred_element_type=F32)
    return out + lax.dot_general(tri, hi, dims, preferred_element_type=F32)


def _chunk_gates(qp, fp, lbv, tri):
    sf = _sigmoid(fp)
    f = lbv + (1.0 - lbv) * sf
    k = 1.0 - f
    sq = _sigmoid(qp)
    qf = qp * sq
    b = _tri_dot(tri, jnp.log(f))
    row = lax.broadcasted_iota(jnp.int32, b.shape, 0)
    bm = jnp.sum(jnp.where(row == CHUNK // 2 - 1, b, 0.0), axis=0, keepdims=True)
    bl = jnp.sum(jnp.where(row == CHUNK - 1, b, 0.0), axis=0, keepdims=True)
    e_b = jnp.exp(b)
    e_qm = jnp.exp(b - bm)
    e_km = jnp.exp(bm - b)
    e_kl = jnp.exp(bl - b)
    e_l = jnp.exp(bl)
    return sf, f, k, sq, qf, e_b, e_qm, e_km, e_kl, e_l


def _hosted(compute, n_in, n_out, n_scratch, ride, grid):
    n_ride = ride.n

    def body(*refs):
        ins, ride_ins = refs[:n_in], refs[n_in : n_in + n_ride]
        pos = n_in + n_ride
        outs, ride_outs = refs[pos : pos + n_out], refs[pos + n_out : pos + n_out + n_ride]
        pos += n_out + n_ride
        scratch, sems = refs[pos : pos + n_scratch], refs[pos + n_scratch :]
        _ride_begin(ride, grid, ride_ins, ride_outs, sems)
        compute(*ins, *outs, *scratch)
        _ride_end(ride, grid, ride_ins, ride_outs, sems)

    return body


def _hgrn_fwd(proj, lb, nw, ride):
    s = proj.shape[1]
    t = min(TS, s)
    nt, nch = s // t, t // CHUNK
    h_, k_, hb = HGRN_HEADS, HEAD_DIM, HEADS_PER_STEP

    def compute(q_ref, f_ref, i_ref, g_ref, lb_ref, nw_ref, mix_ref, o_ref, st_ref, state):
        @pl.when(pl.program_id(1) == 0)
        def _():
            state[...] = jnp.zeros_like(state)

        nwv = nw_ref[...]
        rr = lax.broadcasted_iota(jnp.int32, (CHUNK, CHUNK), 0)
        cc = lax.broadcasted_iota(jnp.int32, (CHUNK, CHUNK), 1)
        causal = rr >= cc
        tri = jnp.where(causal, 1.0, 0.0).astype(BF16)

        st = [state[hh] for hh in range(hb)]
        for c in range(nch):
            rows = slice(c * CHUNK, (c + 1) * CHUNK)
            for hh in range(hb):
                lanes = slice(hh * k_, (hh + 1) * k_)
                qp, fp, gp = q_ref[rows, lanes].astype(F32), f_ref[rows, lanes].astype(F32), g_ref[rows, lanes].astype(F32)
                vb = i_ref[rows, lanes].astype(BF16)
                _, _, k, _, qf, e_b, e_qm, e_km, e_kl, e_l = _chunk_gates(qp, fp, lb_ref[:, lanes], tri)
                st_ref[c, hh] = st[hh]
                a = jnp.where(causal, _dot(qf * e_qm, k * e_km, "nt"), 0.0)
                o = _dot(qf * e_b, st[hh], "nt") + _dot(a, vb, "nn")
                st[hh] = st[hh] * e_l + _dot(vb, k * e_kl, "tn")
                o_ref[rows, lanes] = o
                on = o * lax.rsqrt(jnp.mean(o * o, axis=-1, keepdims=True) + EPS) * nwv
                mix_ref[rows, lanes] = (on * (gp * _sigmoid(gp))).astype(BF16)
        for hh in range(hb):
            state[hh] = st[hh]

    def col(g):
        return pl.BlockSpec((None, t, hb * k_), lambda h, i: (g, i, h))

    grid = (h_ // hb, nt)
    return pl.pallas_call(
        _hosted(compute, 6, 3, 1, ride, grid),
        name="hgrn_fwd",
        grid=grid,
        in_specs=[
            col(0),
            col(1),
            col(2),
            col(3),
            pl.BlockSpec((1, hb * k_), lambda h, i: (0, h)),
            pl.BlockSpec((1, k_), lambda h, i: (0, 0)),
        ]
        + [HBM_SPEC] * ride.n,
        out_specs=[
            pl.BlockSpec((t, hb * k_), lambda h, i: (i, h)),
            pl.BlockSpec((t, hb * k_), lambda h, i: (i, h)),
            pl.BlockSpec((nch, hb, k_, k_), lambda h, i: (i, h, 0, 0)),
        ]
        + [HBM_SPEC] * ride.n,
        out_shape=[
            jax.ShapeDtypeStruct((s, 2 * h_ * k_), BF16),
            jax.ShapeDtypeStruct((s, h_ * k_), F32),
            jax.ShapeDtypeStruct((s // CHUNK, h_, k_, k_), F32),
        ]
        + ride.out_shape,
        scratch_shapes=[pltpu.VMEM((hb, k_, k_), F32)] + ride.scratch,
        compiler_params=_params(2),
    )(proj, proj, proj, proj, lb, nw, *ride.inputs)


def _hgrn_bwd(proj, lb, nw, o_pre, states, dmix, ride):
    s = proj.shape[1]
    t = min(TS, s)
    nt, nch = s // t, t // CHUNK
    h_, k_, hb = HGRN_HEADS, HEAD_DIM, HEADS_PER_STEP

    def compute(q_ref, f_ref, i_ref, g_ref, lb_ref, nw_ref, o_ref, st_ref, dm_ref, dp_ref, dlb_ref, dnw_ref, dstate, acc_lb, acc_nw):
        h, i = pl.program_id(0), pl.program_id(1)
        dq_ref, df_ref, di_ref, dg_ref = dp_ref.at[0], dp_ref.at[1], dp_ref.at[2], dp_ref.at[3]

        @pl.when(i == 0)
        def _():
            dstate[...] = jnp.zeros_like(dstate)
            acc_lb[...] = jnp.zeros_like(acc_lb)

        @pl.when((i == 0) & (h == 0))
        def _():
            acc_nw[...] = jnp.zeros_like(acc_nw)

        nwv = nw_ref[...]
        rr = lax.broadcasted_iota(jnp.int32, (CHUNK, CHUNK), 0)
        cc = lax.broadcasted_iota(jnp.int32, (CHUNK, CHUNK), 1)
        causal = rr >= cc
        tri = jnp.where(causal, 1.0, 0.0).astype(BF16)
        tri_t =jnp.where(cc >= rr, 1.0, 0.0).astype(BF16)
        last_row = lax.broadcasted_iota(jnp.int32, (CHUNK, k_), 0) == CHUNK - 1

        dst = [dstate[hh] for hh in range(hb)]
        sum_nw = jnp.zeros((8, k_), F32)
        sum_lb = [jnp.zeros((8, k_), F32) for _ in range(hb)]
        for c in reversed(range(nch)):
            rows = slice(c * CHUNK, (c + 1) * CHUNK)
            for hh in range(hb):
                lanes = slice(hh * k_, (hh + 1) * k_)
                lbv = lb_ref[:, lanes]
                qp, fp, gp = q_ref[rows, lanes].astype(F32), f_ref[rows, lanes].astype(F32), g_ref[rows, lanes].astype(F32)
                vb = i_ref[rows, lanes].astype(BF16)
                sf, f, k, sq, qf, e_b, e_qm, e_km, e_kl, e_l = _chunk_gates(qp, fp, lbv, tri)
                st0 = st_ref[c, hh]
                dm = dm_ref[rows, lanes]
                o = o_ref[rows, lanes]
                rinv = lax.rsqrt(jnp.mean(o * o, axis=-1, keepdims=True) + EPS)
                ohat = o * rinv
                sg = _sigmoid(gp)
                dg_ref[rows, lanes] = (dm * (ohat * nwv) * (sg * (1.0 + gp * (1.0 - sg)))).astype(BF16)
                don = dm * (gp * sg)
                sum_nw = sum_nw + _rows_to_8(don * ohat)
                dohat = don * nwv
                do = rinv * (dohat - ohat * jnp.mean(dohat * ohat, axis=-1, keepdims=True))
                do2, dst2 = _pair(do), _pair(dst[hh])
                qm2, km2 = _pair(qf * e_qm), _pair(k * e_km)
                a = jnp.where(causal, _dot(qm2[0], km2[0], "nt"), 0.0)
                da2 = _pair(jnp.where(causal, _dot_pairs(do2, (vb, None), "nt"), 0.0))
                dq = e_b * _dot_pairs(do2, _pair(st0), "nn") + e_qm * _dot_pairs(da2, km2, "nn")
                dk_state = e_kl * _dot_pairs((vb, None), dst2, "nn")
                dk = e_km * _dot_pairs(da2, qm2, "tn") + dk_state
                dv = _dot(a, do2[0], "tn") + _dot(k * e_kl, dst2[0], "nt")
                d_bl = jnp.sum(k * dk_state, axis=0, keepdims=True) + e_l * jnp.sum(st0 * dst[hh], axis=0, keepdims=True)
                db = qf * dq - k * dk + jnp.where(last_row, d_bl, 0.0)
                dlogf = _tri_dot(tri_t, db)
                dst[hh] = dst[hh] * e_l + _dot_pairs(do2, _pair(qf * e_b), "tn")
                dfv = dlogf / f - dk
                df_ref[rows, lanes] = (dfv * (1.0 - lbv) * (sf * (1.0 - sf))).astype(BF16)
                sum_lb[hh] = sum_lb[hh] + _rows_to_8(dfv * (1.0 - sf))
                dq_ref[rows, lanes] = (dq * (sq * (1.0 + qp * (1.0 - sq)))).astype(BF16)
                di_ref[rows, lanes] = dv.astype(BF16)
        acc_nw[...] += sum_nw
        for hh in range(hb):
            dstate[hh] = dst[hh]
            acc_lb[:, hh * k_ : (hh + 1) * k_] += sum_lb[hh]

        @pl.when(i == nt - 1)
        def _():
            dlb_ref[...] = jnp.sum(acc_lb[...], axis=0, keepdims=True)

        @pl.when((i == nt - 1) & (h == h_ // hb - 1))
        def _():
            dnw_ref[...] = jnp.sum(acc_nw[...], axis=0, keepdims=True)

    def col(g):
        return pl.BlockSpec((None, t, hb * k_), lambda h, i: (g, nt - 1 - i, h))

    row = pl.BlockSpec((t, hb * k_), lambda h, i: (nt - 1 - i, h))
    grid = (h_ // hb, nt)
    return pl.pallas_call(
        _hosted(compute, 9, 3, 3, ride, grid),
        name="hgrn_bwd",
        grid=grid,
        in_specs=[
            col(0),
            col(1),
            col(2),
            col(3),
            pl.BlockSpec((1, hb * k_), lambda h, i: (0, h)),
            pl.BlockSpec((1, k_), lambda h, i: (0, 0)),
            row,
            pl.BlockSpec((nch, hb, k_, k_), lambda h, i: (nt - 1 - i, h, 0, 0)),
            row,
        ]
        + [HBM_SPEC] * ride.n,
        out_specs=[
            pl.BlockSpec((4, t, hb * k_), lambda h, i: (0, nt - 1 - i, h)),
            pl.BlockSpec((1, hb * k_), lambda h, i: (0, h)),
            pl.BlockSpec((1, k_), lambda h, i: (0, 0)),
        ]
        + [HBM_SPEC] * ride.n,
        out_shape=[
            jax.ShapeDtypeStruct((8, s, h_ * k_), BF16),
            jax.ShapeDtypeStruct((1, h_ * k_), F32),
            jax.ShapeDtypeStruct((1, k_), F32),
        ]
        + ride.out_shape,
        scratch_shapes=[pltpu.VMEM((hb, k_, k_), F32), pltpu.VMEM((8, hb * k_), F32), pltpu.VMEM((8, k_), F32)] + ride.scratch,
        compiler_params=_params(2),
    )(proj, proj, proj, proj, lb, nw, o_pre, states, dmix, *ride.inputs)


def _shift_rows(ext, k):
    n = ext.shape[0]
    return pltpu.roll(ext, k % n, axis=0)


def _row_select(parts):
    w = parts[0].shape[1]
    row = lax.broadcasted_iota(jnp.int32, (8, w), 0)
    out = jnp.zeros((8, w), F32)
    for r, p in enumerate(parts):
        out = out + jnp.where(row == r, p, 0.0)
    return out


def _sconv_fwd(proj, sw, mix):
    s = proj.shape[1]
    t = min(TS, s)
    nt, w = s // t, 512
    ncol, hb = GROUP_W // w, t // HALO

    def body(cb_ref, cc_ref, ch_ref, ccp_ref, chp_ref, sw_ref, mix_in, mix_ref):
        del mix_in
        i = pl.program_id(0)
        u = cc_ref[...].astype(F32) * ch_ref[...].astype(F32)
        up = jnp.where(i > 0, ccp_ref[...].astype(F32) * chp_ref[...].astype(F32), 0.0)
        ext = jnp.concatenate([up, u], axis=0)
        z = sw_ref[0:1, :] * _shift_rows(ext, 2)[HALO:] + sw_ref[1:2, :] * _shift_rows(ext, 1)[HALO:] + sw_ref[2:3, :] * u
        mix_ref[...] = (cb_ref[...].astype(F32) * z).astype(BF16)

    def cur(g):
        return pl.BlockSpec((None, t, w), lambda i, c: (g, i, c))

    def prev(g):
        return pl.BlockSpec((None, HALO, w), lambda i, c: (g, jnp.maximum(i * hb - 1, 0), c))

    return pl.pallas_call(
        body,
        name="sconv_fwd",
        grid=(nt, ncol),
        in_specs=[cur(4), cur(5), cur(6), prev(5), prev(6), pl.BlockSpec((3, w), lambda i, c: (0, c)), pl.BlockSpec(memory_space=pl.ANY)],
        out_specs=pl.BlockSpec((t, w), lambda i, c: (i, ncol + c)),
        out_shape=jax.ShapeDtypeStruct(mix.shape, mix.dtype),
        input_output_aliases={6: 0},
        compiler_params=_params(2),
    )(proj, proj, proj, proj, proj, sw, mix)


def _sconv_bwd(proj, sw, dmix, dproj):
    s = proj.shape[1]
    t = min(TS, s)
    nt, w = s // t, 512
    ncol, hb = GROUP_W // w, t // HALO

    def body(cb_ref, cbn_ref, cc_ref, ccp_ref, ch_ref, chp_ref, dy_ref, dyn_ref, sw_ref, dp_in, dp_ref, dsw_ref, acc):
        del dp_in
        i = pl.program_id(1)
        w0, w1, w2 = sw_ref[0:1, :], sw_ref[1:2, :], sw_ref[2:3, :]
        ccv, chv, cbv, dyv = cc_ref[...].astype(F32), ch_ref[...].astype(F32), cb_ref[...].astype(F32), dy_ref[...]
        u = ccv * chv
        up = jnp.where(i > 0, ccp_ref[...].astype(F32) * chp_ref[...].astype(F32), 0.0)
        ext = jnp.concatenate([up, u], axis=0)
        u1, u2 = _shift_rows(ext, 1)[HALO:], _shift_rows(ext, 2)[HALO:]
        z = w0 * u2 + w1 * u1 + w2 * u
        dz = dyv * cbv
        dzn = jnp.where(i < nt - 1, dyn_ref[...] * cbn_ref[...].astype(F32), 0.0)
        dze = jnp.concatenate([dz, dzn], axis=0)
        du = w2 * dz + w1 * _shift_rows(dze, -1)[:t] + w0 * _shift_rows(dze, -2)[:t]
        dp_ref[0] = (dyv * z).astype(BF16)
        dp_ref[1] = (du * chv).astype(BF16)
        dp_ref[2] = (du * ccv).astype(BF16)
        dp_ref[3] = jnp.zeros((t, w), BF16)

        @pl.when(i == 0)
        def _():
            acc[...] = jnp.zeros_like(acc)

        acc[...] += _row_select([jnp.sum(dz * uu, axis=0, keepdims=True) for uu in (u2, u1, u)])

        @pl.when(i == nt - 1)
        def _():
            dsw_ref[...] = acc[...]

    def cur(g):
        return pl.BlockSpec((None, t, w), lambda c, i: (g, i, c))

    def prev(g):
        return pl.BlockSpec((None, HALO, w), lambda c, i: (g, jnp.maximum(i * hb - 1, 0), c))

    def nxt(g):
        return pl.BlockSpec((None, HALO, w), lambda c, i: (g, jnp.minimum((i + 1) * hb, s // HALO - 1), c))

    return pl.pallas_call(
        body,
        name="sconv_bwd",
        grid=(ncol, nt),
        in_specs=[
            cur(4),
            nxt(4),
            cur(5),
            prev(5),
            cur(6),
            prev(6),
            pl.BlockSpec((t, w), lambda c, i: (i, ncol + c)),
            pl.BlockSpec((HALO, w), lambda c, i: (jnp.minimum((i + 1) * hb, s // HALO - 1), ncol + c)),
            pl.BlockSpec((3, w), lambda c, i: (0, c)),
            pl.BlockSpec(memory_space=pl.ANY),
        ],
        out_specs=[pl.BlockSpec((4, t, w), lambda c, i: (1, i, c)), pl.BlockSpec((8, w), lambda c, i: (0, c))],
        out_shape=[jax.ShapeDtypeStruct(dproj.shape, dproj.dtype), jax.ShapeDtypeStruct((8, GROUP_W), F32)],
        scratch_shapes=[pltpu.VMEM((8, w), F32)],
        input_output_aliases={9: 0},
        compiler_params=_params(2),
    )(proj, proj, proj, proj, proj, proj, dmix, dmix, sw, dproj)


def _ffn_act(a_pre, up, cw, cb):
    nd, s, f = a_pre.shape
    t = min(TS, s)
    nt, hb = s // t, t // HALO

    def body(a_ref, ap_ref, up_ref, cw_ref, cb_ref, o_ref):
        i = pl.program_id(1)
        av = a_ref[...].astype(F32)
        ext = jnp.concatenate([jnp.where(i > 0, ap_ref[...].astype(F32), 0.0), av], axis=0)
        a = cw_ref[0:1, :] * _shift_rows(ext, 2)[HALO:] + cw_ref[1:2, :] * _shift_rows(ext, 1)[HALO:] + cw_ref[2:3, :] * av
        a = a + cb_ref[...]
        o_ref[...] = (a * _sigmoid(a) * up_ref[...].astype(F32)).astype(BF16)

    cur = pl.BlockSpec((None, t, f), lambda j, i: (j, i, 0))
    prev = pl.BlockSpec((None, HALO, f), lambda j, i: (j, jnp.maximum(i * hb - 1, 0), 0))
    return pl.pallas_call(
        body,
        name="ffn_act",
        grid=(nd, nt),
        in_specs=[cur, prev, cur, pl.BlockSpec((None, 3, f), lambda j, i: (j, 0, 0)), pl.BlockSpec((None, 1, f), lambda j, i: (j, 0, 0))],
        out_specs=cur,
        out_shape=jax.ShapeDtypeStruct(a_pre.shape, BF16),
        compiler_params=_params(2),
    )(a_pre, a_pre, up, cw, cb)


def _ffn_act_bwd(a_pre, up, dact, cw, cb, ride):
    nd, s, f = a_pre.shape
    t = min(TS, s)
    nt, hb = s // t, t // HALO

    def compute(a_ref, ap_ref, an_ref, up_ref, upn_ref, da_ref, dan_ref, cw_ref, cb_ref, dap_ref, dup_ref, dsm_ref, acc):
        i = pl.program_id(1)
        w0, w1, w2 = cw_ref[0:1, :], cw_ref[1:2, :], cw_ref[2:3, :]
        av = a_ref[...].astype(F32)
        ext = jnp.concatenate([jnp.where(i > 0, ap_ref[...].astype(F32), 0.0), av, an_ref[...].astype(F32)], axis=0)
        e1, e2 = _shift_rows(ext, 1), _shift_rows(ext, 2)
        a = (w0 * e2 + w1 * e1 + w2 * ext)[HALO:] + cb_ref[...]
        sg = _sigmoid(a)
        dact_e = jnp.concatenate([da_ref[...], dan_ref[...]], axis=0).astype(F32)
        up_e = jnp.concatenate([up_ref[...], upn_ref[...]], axis=0).astype(F32)
        row = lax.broadcasted_iota(jnp.int32, a.shape, 0)
        live = (row < t) | (i < nt - 1)
        d_a = jnp.where(live, dact_e * up_e * (sg * (1.0 + a * (1.0 - sg))), 0.0)
        dup_ref[...] = (dact_e * (a * sg))[:t].astype(BF16)
        dap_ref[...] = (w2 * d_a + w1 * _shift_rows(d_a, -1) + w0 * _shift_rows(d_a, -2))[:t].astype(BF16)
        d_a_cur = d_a[:t]

        @pl.when(i == 0)
        def _():
            acc[...] = jnp.zeros_like(acc)

        sums = [jnp.sum(d_a_cur * e[HALO : HALO + t], axis=0, keepdims=True) for e in (e2, e1, ext)]
        acc[...] += _row_select(sums + [jnp.sum(d_a_cur, axis=0, keepdims=True)])

        @pl.when(i == nt - 1)
        def _():
            dsm_ref[...] = acc[...]

    cur = pl.BlockSpec((None, t, f), lambda j, i: (j, i, 0))
    prev = pl.BlockSpec((None, HALO, f), lambda j, i: (j, jnp.maximum(i * hb - 1, 0), 0))
    nxt = pl.BlockSpec((None, HALO, f), lambda j, i: (j, jnp.minimum((i + 1) * hb, s // HALO - 1), 0))
    return pl.pallas_call(
        _hosted(compute, 9, 3, 1, ride, (nd, nt)),
        name="ffn_act_bwd",
        grid=(nd, nt),
        in_specs=[cur, prev, nxt, cur, nxt, cur, nxt, pl.BlockSpec((None, 3, f), lambda j, i: (j, 0, 0)), pl.BlockSpec((None, 1, f), lambda j, i: (j, 0, 0))]
        + [HBM_SPEC] * ride.n,
        out_specs=[cur, cur, pl.BlockSpec((None, 8, f), lambda j, i: (j, 0, 0))] + [HBM_SPEC] * ride.n,
        out_shape=[jax.ShapeDtypeStruct(a_pre.shape, BF16), jax.ShapeDtypeStruct(a_pre.shape, BF16), jax.ShapeDtypeStruct((nd, 8, f), F32)]
        + ride.out_shape,
        scratch_shapes=[pltpu.VMEM((8, f), F32)] + ride.scratch,
        compiler_params=_params(2),
    )(a_pre, a_pre, a_pre, up, up, dact, dact, cw, cb, *ride.inputs)


def _softmax_rows(sc):
    m = jnp.max(sc, axis=-1, keepdims=True)
    e = jnp.exp(sc - m)
    return e / jnp.sum(e, axis=-1, keepdims=True)


def _attn_fwd(q, km, vm):
    s, d = q.shape
    m = km.shape[0]
    t = min(TS, s)
    hd = d // MEM_HEADS
    scale = hd**-0.5

    def body(q_ref, k_ref, v_ref, o_ref):
        for h in range(MEM_HEADS):
            cols = slice(h * hd, (h + 1) * hd)
            p = _softmax_rows(_dot(q_ref[:, cols], k_ref[:, cols], "nt") * scale)
            o_ref[:, cols] = _dot(p, v_ref[:, cols], "nn").astype(BF16)

    row = pl.BlockSpec((t, d), lambda i: (i, 0))
    full = pl.BlockSpec((m, d), lambda i: (0, 0))
    return pl.pallas_call(
        body,
        name="attn_fwd",
        grid=(s // t,),
        in_specs=[row, full, full],
        out_specs=row,
        out_shape=jax.ShapeDtypeStruct((s, d), BF16),
        compiler_params=_params(1),
    )(q, km, vm)


def _attn_bwd(q, km, vm, do):
    s, d = q.shape
    m = km.shape[0]
    t = min(TS, s)
    steps = s // t
    hd = d // MEM_HEADS
    scale = hd**-0.5

    def body(q_ref, k_ref, v_ref, do_ref, dq_ref, dk_ref, dv_ref):
        @pl.when(pl.program_id(0) == 0)
        def _():
            dk_ref[...] = jnp.zeros_like(dk_ref)
            dv_ref[...] = jnp.zeros_like(dv_ref)

        for h in range(MEM_HEADS):
            cols = slice(h * hd, (h + 1) * hd)
            qh, kh, vh, doh = q_ref[:, cols], k_ref[:, cols], v_ref[:, cols], do_ref[:, cols]
            p = _softmax_rows(_dot(qh, kh, "nt") * scale)
            dp = _dot(doh, vh, "nt")
            ds = p * (dp - jnp.sum(dp * p, axis=-1, keepdims=True)) * scale
            dq_ref[:, cols] = _dot(ds, kh, "nn").astype(BF16)
            dk_ref[:, cols] += _dot(ds, qh, "tn")
            dv_ref[:, cols] += _dot(p, doh, "tn")

    row = pl.BlockSpec((t, d), lambda i: (i, 0))
    full = pl.BlockSpec((m, d), lambda i: (0, 0))
    return pl.pallas_call(
        body,
        name="attn_bwd",
        grid=(steps,),
        in_specs=[row, full, full, row],
        out_specs=[row, full, full],
        out_shape=[jax.ShapeDtypeStruct((s, d), BF16), jax.ShapeDtypeStruct((m, d), F32), jax.ShapeDtypeStruct((m, d), F32)],
        compiler_params=_params(1),
    )(q, km, vm, do)


def _row_tile(rows, cols):
    best = 8
    for tr in range(8, rows + 1, 8):
        if rows % tr == 0 and tr * cols * 4 <= (1 << 20):
            best = tr
    return best


def _adamw(w, g, m, v):
    m = ADAM_B1 * m + (1.0 - ADAM_B1) * g
    v = ADAM_B2 * v + (1.0 - ADAM_B2) * (g * g)
    m_hat = m / (1.0 - ADAM_B1**ADAM_STEP)
    v_hat = v / (1.0 - ADAM_B2**ADAM_STEP)
    delta = -ADAM_LR * (m_hat / (jnp.sqrt(v_hat) + ADAM_EPS) + ADAM_WD * w)
    return delta, m, v


def _adam_shard(name, parts, w, m, v):
    rows, cols = w.shape
    tr = _row_tile(rows, cols)

    def body(*refs):
        part_refs = refs[:N_DEV]
        w_ref, m_ref, v_ref, go_ref, d_ref, mo_ref, vo_ref = refs[N_DEV:]
        g = part_refs[0][...].astype(F32)
        for dev in range(1, N_DEV):
            g = g + part_refs[dev][...].astype(F32)
        go_ref[...] = g
        d_ref[...], mo_ref[...], vo_ref[...] = _adamw(w_ref[...], g, m_ref[...], v_ref[...])

    def part(dev):
        return pl.BlockSpec((None, tr, cols), lambda i: (dev, i, 0))

    flat = pl.BlockSpec((tr, cols), lambda i: (i, 0))
    return pl.pallas_call(
        body,
        name=name,
        grid=(rows // tr,),
        in_specs=[part(dev) for dev in range(N_DEV)] + [flat, flat, flat],
        out_specs=[flat, flat, flat, flat],
        out_shape=[jax.ShapeDtypeStruct((rows, cols), F32)] * 4,
        compiler_params=_params(1),
    )(*([parts] * N_DEV), w, m, v)


def _sum_devices(gathered):
    def body(g_ref, o_ref):
        tot = g_ref[0]
        for dev in range(1, N_DEV):
            tot = tot + g_ref[dev]
        o_ref[...] = tot

    return pl.pallas_call(body, name="sum_devices", out_shape=jax.ShapeDtypeStruct(gathered.shape[1:], F32))(gathered)


def _adam_packed(w, g, m, v):
    def body(w_ref, g_ref, m_ref, v_ref, d_ref, mo_ref, vo_ref):
        d_ref[...], mo_ref[...], vo_ref[...] = _adamw(w_ref[...], g_ref[...], m_ref[...], v_ref[...])

    return pl.pallas_call(body, name="adam_packed", out_shape=[jax.ShapeDtypeStruct(w.shape, F32)] * 3)(w, g, m, v)


def _padded(size):
    return -(-size // 128) * 128


def _pack(parts):
    flat = []
    for p in parts:
        v = p.reshape(-1)
        flat.append(jnp.pad(v, (0, _padded(v.shape[0]) - v.shape[0])))
    return jnp.concatenate(flat).reshape(-1, 128)


def _unpack(packed, shapes):
    flat = packed.reshape(-1)
    out, pos = [], 0
    for shp in shapes:
        size = math.prod(shp)
        out.append(flat[pos : pos + size].reshape(shp))
        pos += _padded(size)
    return out


WEIGHTS = (
    "hgrn_lb norm1_w w_in hgrn_norm_w sconv_w w_out norm2_w mem_norm_w wq wk wv wo norm3_w w_gate w_up ffn_conv_w "
    "ffn_conv_b w_down final_norm_w"
).split()
BIG = ["w_in", "w_out", "wq", "wk", "wv", "wo", "w_gate", "w_up", "w_down"]
SMALL_SHARDED = ["sconv_w", "ffn_conv_w"]
SMALL = [n for n in WEIGHTS if n not in BIG]


def kernel(x, mem, hgrn_lb, norm1_w, w_in, hgrn_norm_w, sconv_w, w_out, norm2_w, mem_norm_w, wq, wk, wv, wo, norm3_w, w_gate, w_up, ffn_conv_w, ffn_conv_b, w_down, final_norm_w, loss_target, m_hgrn_lb, m_norm1_w, m_w_in, m_hgrn_norm_w, m_sconv_w, m_w_out, m_norm2_w, m_mem_norm_w, m_wq, m_wk, m_wv, m_wo, m_norm3_w, m_w_gate, m_w_up, m_ffn_conv_w, m_ffn_conv_b, m_w_down, m_final_norm_w, v_hgrn_lb, v_norm1_w, v_w_in, v_hgrn_norm_w, v_sconv_w, v_w_out, v_norm2_w, v_mem_norm_w, v_wq, v_wk, v_wv, v_wo, v_norm3_w, v_w_gate, v_w_up, v_ffn_conv_w, v_ffn_conv_b, v_w_down, v_final_norm_w):
    given = dict(locals())
    w_of = {n: given[n] for n in WEIGHTS}
    m_of = {n: given["m_" + n] for n in WEIGHTS}
    v_of = {n: given["v_" + n] for n in WEIGHTS}
    x2, mem2, target = x[0], mem[0], loss_target[0]
    d = x2.shape[1]
    xi, yi, ci = lax.axis_index("x"), lax.axis_index("y"), lax.axis_index("c")
    me = 4 * xi + 2 * yi + ci
    final_w = final_norm_w.reshape(1, d)

    shard = {n: w_of[n][0].astype(BF16) for n in BIG}
    small_shard = jnp.concatenate([sconv_w[0], ffn_conv_w[0]], axis=1)
    h1, w_in_g, small_g = _rmsnorm("norm1", x2, norm1_w, _GatherRide([shard["w_in"], small_shard]))
    w_in_f = jnp.transpose(w_in_g, (1, 0, 2)).reshape(d, -1)
    conv_ch = sconv_w.shape[2]
    sw_f = jnp.transpose(small_g[:, :, :conv_ch], (1, 0, 2)).reshape(3, -1)
    cw_g = small_g[:, :, conv_ch:]
    cb_g = ffn_conv_b.reshape(N_DEV, 1, -1)

    lb = _lb_fwd(hgrn_lb)
    proj, *attn_w = _mm_proj(h1, w_in_f, _GatherRide([shard[n] for n in ("w_out", "wq", "wk", "wv", "wo")]))
    w_out_f, wq_f, wk_f, wv_f, wo_f = (a.reshape(d, d) for a in attn_w)
    mix, o_pre, states, wg_g, wu_g = _hgrn_fwd(proj, lb, hgrn_norm_w, _GatherRide([shard["w_gate"], shard["w_up"]]))
    mix = _sconv_fwd(proj, sw_f, mix)
    x1 = _mm_nn("mm_out", mix, w_out_f, F32, residual=x2)
    h2 = _rmsnorm("norm2", x1, norm2_w)
    mem_n = _rmsnorm("norm_mem", mem2, mem_norm_w)
    q = _mm_nn("mm_q", h2, wq_f, BF16)
    km = _mm_nn("mm_k", mem_n, wk_f, BF16)
    vm = _mm_nn("mm_v", mem_n, wv_f, BF16)
    ao = _attn_fwd(q, km, vm)
    x2r = _mm_nn("mm_o", ao, wo_f, F32, residual=x1)
    h3 = _rmsnorm("norm3", x2r, norm3_w)
    a_pre, up, wd_g = _mm_gate_up(h3, wg_g, wu_g, _GatherRide([shard["w_down"]]))
    act = _ffn_act(a_pre, up, cw_g, cb_g)
    x3 = _mm_down(act, wd_g, x2r)

    dx3, dx3b, loss_blk, d_final = _loss_head(x3, target, final_w)
    parts = {}
    dact = _mm_dact(dx3b, wd_g)
    dwd = _mm_dwdown(act, dx3b)
    dapre, dup, d_ffn_small, parts["w_down"] = _ffn_act_bwd(a_pre, up, dact, cw_g, cb_g, _ScatterRide([dwd]))
    dwg, dwu = _mm_dw_cols("mm_dw_gate_up", h3, [dapre, dup])
    dh3, parts["w_gate"], parts["w_up"] = _mm_dh3(dapre, dup, wg_g, wu_g, _ScatterRide([dwg, dwu]))
    dx2, dx2b, d_n3 = _rmsnorm_bwd("norm3_bwd", dh3, x2r, norm3_w, dx3)
    dao = _mm_nt("mm_dao", [(dx2b, wo_f)], BF16)
    dwo = _mm_tn("mm_dwo", ao, dx2b)
    dq, dkm, dvm = _attn_bwd(q, km, vm, dao)
    dwq = _mm_tn("mm_dwq", h2, dq)
    dh2 = _mm_nt("mm_dh2", [(dq, wq_f)], F32)
    dwk = _mm_tn("mm_dwk", mem_n, dkm)
    dwv = _mm_tn("mm_dwv", mem_n, dvm)
    dmem_n = _mm_nt("mm_dmem", [(dkm, wk_f), (dvm, wv_f)], F32)
    _, _, d_nm = _rmsnorm_bwd("norm_mem_bwd", dmem_n, mem2, mem_norm_w)
    dx1, dx1b, d_n2 = _rmsnorm_bwd("norm2_bwd", dh2, x1, norm2_w, dx2)
    dmix = _mm_nt("mm_dmix", [(dx1b, w_out_f)], F32)
    dwout = _mm_tn("mm_dwout", mix, dx1b)
    attn_names = ["w_out", "wq", "wk", "wv", "wo"]
    attn_grads = [a.reshape(N_DEV, d // N_DEV, d) for a in (dwout, dwq, dwk, dwv, dwo)]
    dproj, d_lb, d_nw, *attn_parts = _hgrn_bwd(proj, lb, hgrn_norm_w, o_pre, states, dmix, _ScatterRide(attn_grads))
    parts.update(zip(attn_names, attn_parts))
    dproj, d_sw = _sconv_bwd(proj, sw_f, dmix, dproj)
    dwin = _mm_dwin(h1, dproj, N_GROUPS)
    dwin_g = jnp.transpose(dwin.reshape(d, N_DEV, -1), (1, 0, 2))
    dh1, parts["w_in"] = _mm_dh1(dproj, w_in_f, N_GROUPS, _ScatterRide([dwin_g]))
    grad_x, _, d_n1 = _rmsnorm_bwd("norm1_bwd", dh1, x2, norm1_w, dx1)
    d_hlb = _lb_bwd(hgrn_lb, d_lb)

    small_parts = [d_hlb, d_n1, d_nw, d_sw[0:3], d_n2, d_nm, d_n3, d_ffn_small[:, 0:3, :], d_ffn_small[:, 3, :], d_final, loss_blk]
    small_shapes = [p.shape for p in small_parts]
    total = _sum_devices(_allgather("gather_small", [_pack(small_parts)])[0])
    g_hlb, g_n1, g_nw, g_sw, g_n2, g_nm, g_n3, g_cw, g_cb, g_final, loss_row = _unpack(total, small_shapes)
    grads = {
        "hgrn_lb": g_hlb,
        "norm1_w": g_n1,
        "hgrn_norm_w": g_nw,
        "sconv_w": lax.dynamic_slice_in_dim(g_sw, me * conv_ch, conv_ch, axis=1).reshape(sconv_w.shape),
        "norm2_w": g_n2,
        "mem_norm_w": g_nm,
        "norm3_w": g_n3,
        "ffn_conv_w": lax.dynamic_index_in_dim(g_cw, me, axis=0, keepdims=True),
        "ffn_conv_b": g_cb.reshape(ffn_conv_b.shape),
        "final_norm_w": g_final.reshape(final_norm_w.shape),
    }
    shapes = [w_of[n].shape for n in SMALL]
    deltas_p, new_m_p, new_v_p = _adam_packed(
        _pack([w_of[n] for n in SMALL]), _pack([grads[n] for n in SMALL]), _pack([m_of[n] for n in SMALL]), _pack([v_of[n] for n in SMALL])
    )
    deltas = dict(zip(SMALL, _unpack(deltas_p, shapes)))
    new_m = dict(zip(SMALL, _unpack(new_m_p, shapes)))
    new_v = dict(zip(SMALL, _unpack(new_v_p, shapes)))

    for n in BIG:
        shp = w_of[n].shape
        gw, dw, mw, vw = _adam_shard("adam_" + n, parts[n], w_of[n][0], m_of[n][0], v_of[n][0])
        grads[n], deltas[n], new_m[n], new_v[n] = (a.reshape(shp) for a in (gw, dw, mw, vw))

    loss = loss_row[0, 0]
    return (
        loss,
        grad_x.reshape(x.shape),
        *[grads[n] for n in WEIGHTS],
        *[deltas[n] for n in WEIGHTS],
        *[new_m[n] for n in WEIGHTS],
        *[new_v[n] for n in WEIGHTS],
    )
```

```python
import functools
import math

import jax
import jax.numpy as jnp
from jax import lax
from jax.experimental import pallas as pl
from jax.experimental.pallas import tpu as pltpu

F32 = jnp.float32
BF16 = jnp.bfloat16
MESH = pl.DeviceIdType.MESH
N_DEV = 8
EPS = 1e-6
CHUNK = 64
HGRN_HEADS = 8
HEAD_DIM = 128
HEADS_PER_STEP = 2
MEM_HEADS = 4
GROUP_W = HGRN_HEADS * HEAD_DIM
N_GROUPS = 7
HALO = 16
TS = 512
TR = 256
TM = 1024
TN = 1024
TK = 1024
VMEM_LIMIT = 48 * 1024 * 1024

ADAM_LR = 0.001
ADAM_B1 = 0.9
ADAM_B2 = 0.999
ADAM_EPS = 1e-08
ADAM_WD = 0.01
ADAM_STEP = 10

_DIMS = {
    "nn": (((1,), (0,)), ((), ())),
    "nt": (((1,), (1,)), ((), ())),
    "tn": (((0,), (0,)), ((), ())),
}


def _params(n_axes):
    return pltpu.CompilerParams(dimension_semantics=("arbitrary",) * n_axes, vmem_limit_bytes=VMEM_LIMIT)


def _dot(a, b, mode):
    return lax.dot_general(a.astype(BF16), b.astype(BF16), _DIMS[mode], preferred_element_type=F32)


def _sigmoid(v):
    return 0.5 * jnp.tanh(0.5 * v) + 0.5


def _block_dims(spec):
    return tuple(d for d in spec.block_shape if d is not None)


HBM_SPEC = pl.BlockSpec(memory_space=pltpu.HBM)


def _place():
    x, y, c = lax.axis_index("x"), lax.axis_index("y"), lax.axis_index("c")
    return x, y, c, [(1 - x, y), (x, 1 - y), (1 - x, 1 - y)]


def _slab(p):
    return 4 * p[0] + 2 * p[1] + p[2]


def _remote(src, dst, send_sem, recv_sem, to):
    return pltpu.make_async_remote_copy(src_ref=src, dst_ref=dst, send_sem=send_sem, recv_sem=recv_sem, device_id=to, device_id_type=MESH)


class _GatherRide:
    def __init__(self, shards):
        self.inputs = list(shards)
        self.n = len(shards)
        self.out_shape = [jax.ShapeDtypeStruct((N_DEV, *a.shape), a.dtype) for a in shards]
        self.scratch = [pltpu.SemaphoreType.DMA((self.n, 7)), pltpu.SemaphoreType.DMA((self.n, 7)), pltpu.SemaphoreType.DMA((self.n,))]

    @staticmethod
    def _copy(outs, sems, a, k, block, to, src=None):
        dst = outs[a].at[_slab(block)]
        return _remote(dst if src is None else src, dst, sems[0].at[a, k], sems[1].at[a, k], to)

    def _mine(self, ins, outs, sems, a):
        x, y, c, chips = _place()
        local = pltpu.make_async_copy(ins[a], outs[a].at[_slab((x, y, c))], sems[2].at[a])
        first = [self._copy(outs, sems, a, 0, (x, y, c), (x, y, 1 - c), src=ins[a])]
        first += [self._copy(outs, sems, a, 1 + j, (x, y, c), (*chip, c), src=ins[a]) for j, chip in enumerate(chips)]
        return local, first

    def start(self, ins, outs, sems):
        for a in range(self.n):
            local, first = self._mine(ins, outs, sems, a)
            local.start()
            for cp in first:
                cp.start()

    def forward(self, ins, outs, sems):
        x, y, c, chips = _place()
        for j, chip in enumerate(chips):
            for a in range(self.n):
                self._copy(outs, sems, a, 1 + j, (*chip, c), (x, y, c)).wait_recv()
                self._copy(outs, sems, a, 4 + j, (*chip, c), (x, y, 1 - c)).start()

    def finish(self, ins, outs, sems):
        x, y, c, chips = _place()
        for a in range(self.n):
            self._copy(outs, sems, a, 0, (x, y, 1 - c), (x, y, c)).wait_recv()
            for j, chip in enumerate(chips):
                self._copy(outs, sems, a, 4 + j, (*chip, 1 - c), (x, y, c)).wait_recv()
        for a in range(self.n):
            local, first = self._mine(ins, outs, sems, a)
            for cp in first:
                cp.wait_send()
            for j, chip in enumerate(chips):
                self._copy(outs, sems, a, 4 + j, (*chip, c), (x, y, 1 - c)).wait_send()
            local.wait()


class _ScatterRide:
    def __init__(self, grads):
        self.inputs = list(grads)
        self.n = len(grads)
        self.out_shape = [jax.ShapeDtypeStruct(g.shape, g.dtype) for g in grads]
        self.scratch = [pltpu.SemaphoreType.DMA((self.n, 7)), pltpu.SemaphoreType.DMA((self.n, 7)), pltpu.SemaphoreType.DMA((self.n,))]

    @staticmethod
    def _peers():
        x, y, c, chips = _place()
        return _slab((x, y, c)), [(x, y, 1 - c)] + [(*chip, c) for chip in chips] + [(*chip, 1 - c) for chip in chips]

    def _sends(self, ins, outs, sems, a):
        me, peers = self._peers()
        local = pltpu.make_async_copy(ins[a].at[me], outs[a].at[me], sems[2].at[a])
        return [local] + [_remote(ins[a].at[_slab(p)], outs[a].at[me], sems[0].at[a, k], sems[1].at[a, k], p) for k, p in enumerate(peers)]

    def start(self, ins, outs, sems):
        for a in range(self.n):
            for cp in self._sends(ins, outs, sems, a):
                cp.start()

    def forward(self, ins, outs, sems):
        pass

    def finish(self, ins, outs, sems):
        me, peers = self._peers()
        for a in range(self.n):
            for k, p in enumerate(peers):
                _remote(ins[a].at[me], outs[a].at[_slab(p)], sems[0].at[a, k], sems[1].at[a, k], p).wait_recv()
            local, *sends = self._sends(ins, outs, sems, a)
            for cp in sends:
                cp.wait_send()
            local.wait()


def _grid_step(grid):
    step = 0
    for axis, size in enumerate(grid):
        step = step * size + pl.program_id(axis)
    return step


def _ride_begin(ride, grid, ins, outs, sems):
    step, n_steps = _grid_step(grid), math.prod(grid)

    @pl.when(step == 0)
    def _():
        ride.start(ins, outs, sems)

    @pl.when(step == (3 * n_steps) // 4)
    def _():
        ride.forward(ins, outs, sems)


def _ride_end(ride, grid, ins, outs, sems):
    @pl.when(_grid_step(grid) == math.prod(grid) - 1)
    def _():
        ride.finish(ins, outs, sems)


def _allgather(name, shards):
    ride = _GatherRide(shards)
    n = ride.n

    def body(*refs):
        ins, outs, sems = refs[:n], refs[n : 2 * n], refs[2 * n :]
        ride.start(ins, outs, sems)
        ride.forward(ins, outs, sems)
        ride.finish(ins, outs, sems)

    return pl.pallas_call(
        body,
        name=name,
        in_specs=[HBM_SPEC] * n,
        out_specs=[HBM_SPEC] * n,
        out_shape=ride.out_shape,
        scratch_shapes=ride.scratch,
    )(*shards)


def _matmul(name, grid, k_axis, pairs, outs, mode, residual=None, ride=None):
    n_pairs, n_out, has_res = len(pairs), len(outs), residual is not None
    k_steps = grid[k_axis] if k_axis is not None else 1
    n_acc = n_out if k_steps > 1 else 0
    n_ride = ride.n if ride is not None else 0

    def body(*refs):
        ins = refs[: 2 * n_pairs]
        pos = 2 * n_pairs
        res_ref = refs[pos] if has_res else None
        pos += int(has_res)
        ride_ins = refs[pos : pos + n_ride]
        pos += n_ride
        out_refs = refs[pos : pos + n_out]
        pos += n_out
        ride_outs = refs[pos : pos + n_ride]
        pos += n_ride
        acc_refs = refs[pos : pos + n_acc]
        ride_sems = refs[pos + n_acc :]
        if ride is not None:
            _ride_begin(ride, grid, ride_ins, ride_outs, ride_sems)

        def partial(o):
            tot = None
            for p in outs[o][3]:
                d = _dot(ins[2 * p][...], ins[2 * p + 1][...], mode)
                tot = d if tot is None else tot + d
            return tot

        if k_steps == 1:
            for o in range(n_out):
                val = partial(o)
                if has_res and o == 0:
                    val = val + res_ref[...]
                out_refs[o][...] = val.astype(out_refs[o].dtype)
        else:
            k = pl.program_id(k_axis)

            @pl.when(k == 0)
            def _():
                for o in range(n_out):
                    if has_res and o == 0:
                        acc_refs[o][...] = res_ref[...]
                    else:
                        acc_refs[o][...] = jnp.zeros_like(acc_refs[o])

            for o in range(n_out):
                acc_refs[o][...] += partial(o)

            @pl.when(k == k_steps - 1)
            def _():
                for o in range(n_out):
                    out_refs[o][...] = acc_refs[o][...].astype(out_refs[o].dtype)

        if ride is not None:
            _ride_end(ride, grid, ride_ins, ride_outs, ride_sems)

    args, in_specs = [], []
    for a, a_spec, b, b_spec in pairs:
        args += [a, b]
        in_specs += [a_spec, b_spec]
    if has_res:
        args.append(residual[0])
        in_specs.append(residual[1])
    out_specs = [o[2] for o in outs]
    out_shape = [jax.ShapeDtypeStruct(o[0], o[1]) for o in outs]
    scratch = [pltpu.VMEM(_block_dims(o[2]), F32) for o in outs] if k_steps > 1 else []
    if ride is not None:
        args += ride.inputs
        in_specs += [HBM_SPEC] * n_ride
        out_specs += [HBM_SPEC] * n_ride
        out_shape += ride.out_shape
        scratch += ride.scratch
    return pl.pallas_call(
        body,
        name=name,
        grid=grid,
        in_specs=in_specs,
        out_specs=out_specs,
        out_shape=out_shape,
        scratch_shapes=scratch,
        compiler_params=_params(len(grid)),
    )(*args)


def _out_tiles(m, n, out_dtype):
    if out_dtype == F32 and m >= TM:
        return TM // 2, n
    return min(TM, m), min(TN, n)


def _mm_nn(name, a, w, out_dtype, residual=None):
    m, k = a.shape
    n = w.shape[1]
    tm, tn = _out_tiles(m, n, out_dtype)
    pairs = [(a, pl.BlockSpec((tm, k), lambda i, j: (i, 0)), w, pl.BlockSpec((k, tn), lambda i, j: (0, j)))]
    o_spec = pl.BlockSpec((tm, tn), lambda i, j: (i, j))
    res = (residual, o_spec) if residual is not None else None
    return _matmul(name, (m // tm, n // tn), None, pairs, [((m, n), out_dtype, o_spec, [0])], "nn", res)[0]


def _mm_nt(name, pairs_in, out_dtype):
    m, k = pairs_in[0][0].shape
    n = pairs_in[0][1].shape[0]
    tm, tn = _out_tiles(m, n, out_dtype)
    pairs = [
        (a, pl.BlockSpec((tm, k), lambda i, j: (i, 0)), w, pl.BlockSpec((tn, k), lambda i, j: (j, 0)))
        for a, w in pairs_in
    ]
    o_spec = pl.BlockSpec((tm, tn), lambda i, j: (i, j))
    outs = [((m, n), out_dtype, o_spec, list(range(len(pairs))))]
    return _matmul(name, (m // tm, n // tn), None, pairs, outs, "nt")[0]


def _mm_tn(name, a, b):
    s, i_dim = a.shape
    n = b.shape[1]
    ts, ti, tn = min(TK, s), min(TN, i_dim), min(TN, n)
    pairs = [(a, pl.BlockSpec((ts, ti), lambda i, j, k: (k, i)), b, pl.BlockSpec((ts, tn), lambda i, j, k: (k, j)))]
    outs = [((i_dim, n), BF16, pl.BlockSpec((ti, tn), lambda i, j, k: (i, j)), [0])]
    return _matmul(name, (i_dim // ti, n // tn, s // ts), 2, pairs, outs, "tn")[0]


def _mm_proj(h1, w_in, ride):
    s, d = h1.shape
    g = w_in.shape[1] // GROUP_W
    tm = min(TM, s)
    pairs = [(h1, pl.BlockSpec((tm, d), lambda i, j: (i, 0)), w_in, pl.BlockSpec((d, GROUP_W), lambda i, j: (0, j)))]
    outs = [((g, s, GROUP_W), BF16, pl.BlockSpec((None, tm, GROUP_W), lambda i, j: (j, i, 0)), [0])]
    return _matmul("mm_proj", (s // tm, g), None, pairs, outs, "nn", ride=ride)


def _mm_gate_up(h3, wg_g, wu_g, ride):
    s, d = h3.shape
    f = wg_g.shape[2]
    tm = min(TM, s)
    a_spec = pl.BlockSpec((tm, d), lambda i, j: (i, 0))
    w_spec = pl.BlockSpec((None, d, f), lambda i, j: (j, 0, 0))
    o_spec = pl.BlockSpec((None, tm, f), lambda i, j: (j, i, 0))
    pairs = [(h3, a_spec, wg_g, w_spec), (h3, a_spec, wu_g, w_spec)]
    outs = [((N_DEV, s, f), BF16, o_spec, [0]), ((N_DEV, s, f), BF16, o_spec, [1])]
    return _matmul("mm_gate_up", (s // tm, N_DEV), None, pairs, outs, "nn", ride=ride)


def _mm_dact(dx3b, wd_g):
    s, d = dx3b.shape
    f = wd_g.shape[1]
    tm = min(TM, s)
    pairs = [(dx3b, pl.BlockSpec((tm, d), lambda i, j: (i, 0)), wd_g, pl.BlockSpec((None, f, d), lambda i, j: (j, 0, 0)))]
    outs = [((N_DEV, s, f), BF16, pl.BlockSpec((None, tm, f), lambda i, j: (j, i, 0)), [0])]
    return _matmul("mm_dact", (s // tm, N_DEV), None, pairs, outs, "nt")[0]


def _mm_dwdown(act, dx3b):
    _, s, f = act.shape
    d = dx3b.shape[1]
    ts, tn = min(TK, s), min(TN, d)
    pairs = [
        (
            act,
            pl.BlockSpec((None, ts, f), lambda j, n, k: (j, k, 0)),
            dx3b,
            pl.BlockSpec((ts, tn), lambda j, n, k: (k, n)),
        )
    ]
    outs = [((N_DEV, f, d), BF16, pl.BlockSpec((None, f, tn), lambda j, n, k: (j, 0, n)), [0])]
    return _matmul("mm_dwdown", (N_DEV, d // tn, s // ts), 2, pairs, outs, "tn")[0]


def _mm_dw_cols(name, h, dcols_list):
    s, d = h.shape
    f = dcols_list[0].shape[2]
    ts, ti = min(TK, s), min(TN, d)
    a_spec = pl.BlockSpec((ts, ti), lambda j, i, k: (k, i))
    b_spec = pl.BlockSpec((None, ts, f), lambda j, i, k: (j, k, 0))
    o_spec = pl.BlockSpec((None, ti, f), lambda j, i, k: (j, i, 0))
    pairs = [(h, a_spec, dc, b_spec) for dc in dcols_list]
    outs = [((N_DEV, d, f), BF16, o_spec, [p]) for p in range(len(pairs))]
    return _matmul(name, (N_DEV, d // ti, s // ts), 2, pairs, outs, "tn")


def _mm_dwin(h1, dproj, groups, ride):
    s, d = h1.shape
    ts, ti = min(TK, s), min(TN, d)
    pairs = [
        (
            h1,
            pl.BlockSpec((ts, ti), lambda j, i, k: (k, i)),
            dproj,
            pl.BlockSpec((None, ts, GROUP_W), lambda j, i, k: (j, k, 0)),
        )
    ]
    outs = [((d, groups * GROUP_W), BF16, pl.BlockSpec((ti, GROUP_W), lambda j, i, k: (i, j)), [0])]
    return _matmul("mm_dwin", (groups, d // ti, s // ts), 2, pairs, outs, "tn", ride=ride)


def _mm_dh1(dproj, w_in, groups, ride):
    s = dproj.shape[1]
    d = w_in.shape[0]
    tm, tn = min(TM, s), min(TN, d)
    pairs = [
        (
            dproj,
            pl.BlockSpec((None, tm, GROUP_W), lambda i, j, k: (k, i, 0)),
            w_in,
            pl.BlockSpec((tn, GROUP_W), lambda i, j, k: (j, k)),
        )
    ]
    outs = [((s, d), F32, pl.BlockSpec((tm, tn), lambda i, j, k: (i, j)), [0])]
    return _matmul("mm_dh1", (s // tm, d // tn, groups), 2, pairs, outs, "nt", ride=ride)


def _rows_to_8(v):
    t, d = v.shape
    return jnp.sum(v.reshape(t // 8, 8, d), axis=0)


def _rmsnorm(name, x, w, ride=None):
    s, d = x.shape
    t = min(TS, s)
    grid = (s // t,)

    def compute(x_ref, w_ref, o_ref):
        xv = x_ref[...]
        r = lax.rsqrt(jnp.mean(xv * xv, axis=-1, keepdims=True) + EPS)
        o_ref[...] = (xv * r * w_ref[...]).astype(o_ref.dtype)

    in_specs = [pl.BlockSpec((t, d), lambda i: (i, 0)), pl.BlockSpec((1, d), lambda i: (0, 0))]
    out_spec = pl.BlockSpec((t, d), lambda i: (i, 0))
    out_shape = jax.ShapeDtypeStruct((s, d), BF16)
    if ride is None:
        return pl.pallas_call(compute, name=name, grid=grid, in_specs=in_specs, out_specs=out_spec, out_shape=out_shape, compiler_params=_params(1))(x, w)
    return pl.pallas_call(
        _hosted(compute, 2, 1, 0, ride, grid),
        name=name,
        grid=grid,
        in_specs=in_specs + [HBM_SPEC] * ride.n,
        out_specs=[out_spec] + [HBM_SPEC] * ride.n,
        out_shape=[out_shape] + ride.out_shape,
        scratch_shapes=ride.scratch,
        compiler_params=_params(1),
    )(x, w, *ride.inputs)


def _rmsnorm_bwd(name, dh, x, w, res=None):
    s, d = x.shape
    t = min(TR, s)
    steps = s // t
    has_res = res is not None

    def body(*refs):
        if has_res:
            dh_ref, x_ref, w_ref, res_ref, dx_ref, dxb_ref, dw_ref, acc = refs
        else:
            dh_ref, x_ref, w_ref, dx_ref, dxb_ref, dw_ref, acc = refs
        i = pl.program_id(0)
        xv = x_ref[...]
        r = lax.rsqrt(jnp.mean(xv * xv, axis=-1, keepdims=True) + EPS)
        xn = xv * r
        dhv = dh_ref[...].astype(F32)
        dxn = dhv * w_ref[...]
        dx = r * (dxn - xn * jnp.mean(dxn * xn, axis=-1, keepdims=True))
        if has_res:
            dx = dx + res_ref[...]
        dx_ref[...] = dx
        dxb_ref[...] = dx.astype(BF16)

        @pl.when(i == 0)
        def _():
            acc[...] = jnp.zeros_like(acc)

        acc[...] += _rows_to_8(dhv * xn)

        @pl.when(i == steps - 1)
        def _():
            dw_ref[...] = jnp.sum(acc[...], axis=0, keepdims=True)

    row = pl.BlockSpec((t, d), lambda i: (i, 0))
    vec = pl.BlockSpec((1, d), lambda i: (0, 0))
    args = [dh, x, w] + ([res] if has_res else [])
    return pl.pallas_call(
        body,
        name=name,
        grid=(steps,),
        in_specs=[row, row, vec] + ([row] if has_res else []),
        out_specs=[row, row, vec],
        out_shape=[
            jax.ShapeDtypeStruct((s, d), F32),
            jax.ShapeDtypeStruct((s, d), BF16),
            jax.ShapeDtypeStruct((1, d), F32),
        ],
        scratch_shapes=[pltpu.VMEM((8, d), F32)],
        compiler_params=_params(1),
    )(*args)


def _loss_head(x3, target, w):
    s, d = x3.shape
    t = min(TR, s)
    steps = s // t

    def body(x_ref, t_ref, w_ref, dx_ref, dxb_ref, loss_ref, dw_ref, acc_l, acc_w):
        i = pl.program_id(0)
        xv = x_ref[...]
        wv = w_ref[...]
        r = lax.rsqrt(jnp.mean(xv * xv, axis=-1, keepdims=True) + EPS)
        xn = xv * r
        err = xn * wv - t_ref[...]
        dy = err * (1.0 / d)
        dxn = dy * wv
        dx = r * (dxn - xn * jnp.mean(dxn * xn, axis=-1, keepdims=True))
        dx_ref[...] = dx
        dxb_ref[...] = dx.astype(BF16)

        @pl.when(i == 0)
        def _():
            acc_l[...] = jnp.zeros_like(acc_l)
            acc_w[...] = jnp.zeros_like(acc_w)

        acc_l[...] += _rows_to_8(err * err)
        acc_w[...] += _rows_to_8(dy * xn)

        @pl.when(i == steps - 1)
        def _():
            tot = jnp.sum(jnp.sum(acc_l[...], axis=0, keepdims=True), axis=1, keepdims=True) * (0.5 / d)
            loss_ref[...] = jnp.broadcast_to(tot, loss_ref.shape)
            dw_ref[...] = jnp.sum(acc_w[...], axis=0, keepdims=True)

    row = pl.BlockSpec((t, d), lambda i: (i, 0))
    vec = pl.BlockSpec((1, d), lambda i: (0, 0))
    return pl.pallas_call(
        body,
        name="loss_head",
        grid=(steps,),
        in_specs=[row, row, vec],
        out_specs=[row, row, pl.BlockSpec((1, 128), lambda i: (0, 0)), vec],
        out_shape=[
            jax.ShapeDtypeStruct((s, d), F32),
            jax.ShapeDtypeStruct((s, d), BF16),
            jax.ShapeDtypeStruct((1, 128), F32),
            jax.ShapeDtypeStruct((1, d), F32),
        ],
        scratch_shapes=[pltpu.VMEM((8, d), F32), pltpu.VMEM((8, d), F32)],
        compiler_params=_params(1),
    )(x3, target, w)


def _lb_fwd(hgrn_lb):
    def body(p_ref, lb_ref):
        p = p_ref[...]
        m = jnp.max(p, axis=0, keepdims=True)
        e = jnp.exp(p - m)
        lb_ref[...] = e[0:1, :] / jnp.sum(e, axis=0, keepdims=True)

    return pl.pallas_call(body, name="lb_fwd", out_shape=jax.ShapeDtypeStruct((1, hgrn_lb.shape[1]), F32))(hgrn_lb)


def _lb_bwd(hgrn_lb, dlb):
    def body(p_ref, d_ref, o_ref):
        p = p_ref[...]
        m = jnp.max(p, axis=0, keepdims=True)
        e = jnp.exp(p - m)
        sm = e / jnp.sum(e, axis=0, keepdims=True)
        p0 = sm[0:1, :]
        row = lax.broadcasted_iota(jnp.int32, sm.shape, 0)
        o_ref[...] = d_ref[...] * p0 * (jnp.where(row == 0, 1.0, 0.0) - sm)

    return pl.pallas_call(body, name="lb_bwd", out_shape=jax.ShapeDtypeStruct(hgrn_lb.shape, F32))(hgrn_lb, dlb)


def _split3(v):
    hi = v.astype(BF16)
    r1 = v - hi.astype(F32)
    mid = r1.astype(BF16)
    lo = (r1 - mid.astype(F32)).astype(BF16)
    return hi, mid, lo


def _pair(v):
    hi = v.astype(BF16)
    return hi, (v - hi.astype(F32)).astype(BF16)


def _dot_pairs(a2, b2, mode):
    dims = _DIMS[mode]
    out = lax.dot_general(a2[0], b2[0], dims, preferred_element_type=F32)
    if a2[1] is not None:
        out = lax.dot_general(a2[1], b2[0], dims, preferred_element_type=F32) + out
    if b2[1] is not None:
        out = lax.dot_general(a2[0], b2[1], dims, preferred_element_type=F32) + out
    return out


def _tri_dot(tri, v):
    hi, mid, lo = _split3(v)
    dims = _DIMS["nn"]
    out = lax.dot_general(tri, lo, dims, preferred_element_type=F32)
    out = out + lax.dot_general(tri, mid, dims, preferred_element_type=F32)
    return out + lax.dot_general(tri, hi, dims, preferred_element_type=F32)


def _chunk_gates(qp, fp, lbv):
    sf = _sigmoid(fp)
    f = lbv + (1.0 - lbv) * sf
    k = 1.0 - f
    sq = _sigmoid(qp)
    qf = qp * sq
    return sf, f, k, sq, qf, jnp.log(f)


def _chunk_decays(b):
    row = lax.broadcasted_iota(jnp.int32, b.shape, 0)
    bm = jnp.sum(jnp.where(row == CHUNK // 2 - 1, b, 0.0), axis=0, keepdims=True)
    bl = jnp.sum(jnp.where(row == CHUNK - 1, b, 0.0), axis=0, keepdims=True)
    return jnp.exp(b), jnp.exp(b - bm), jnp.exp(bm - b), jnp.exp(bl - b), jnp.exp(bl)


def _emit_pipelined(tasks, n_stages):
    tasks = list(tasks)
    for step in range(len(tasks) + n_stages - 1):
        for t in range(max(step - n_stages + 1, 0), min(step, len(tasks) - 1) + 1):
            next(tasks[t], None)


def _hosted(compute, n_in, n_out, n_scratch, ride, grid):
    n_ride = ride.n

    def body(*refs):
        ins, ride_ins = refs[:n_in], refs[n_in : n_in + n_ride]
        pos = n_in + n_ride
        outs, ride_outs = refs[pos : pos + n_out], refs[pos + n_out : pos + n_out + n_ride]
        pos += n_out + n_ride
        scratch, sems = refs[pos : pos + n_scratch], refs[pos + n_scratch :]
        _ride_begin(ride, grid, ride_ins, ride_outs, sems)
        compute(*ins, *outs, *scratch)
        _ride_end(ride, grid, ride_ins, ride_outs, sems)

    return body


def _hgrn_fwd(proj, lb, nw, ride):
    s = proj.shape[1]
    t = min(TS, s)
    nt, nch = s // t, t // CHUNK
    h_, k_, hb = HGRN_HEADS, HEAD_DIM, HEADS_PER_STEP

    def compute(q_ref, f_ref, i_ref, g_ref, lb_ref, nw_ref, mix_ref, o_ref, st_ref, state):
        @pl.when(pl.program_id(1) == 0)
        def _():
            state[...] = jnp.zeros_like(state)

        nwv = nw_ref[...]
        rr = lax.broadcasted_iota(jnp.int32, (CHUNK, CHUNK), 0)
        cc = lax.broadcasted_iota(jnp.int32, (CHUNK, CHUNK), 1)
        causal = rr >= cc
        tri = jnp.where(causal, 1.0, 0.0).astype(BF16)

        st = [state[hh] for hh in range(hb)]

        def task(c, hh):
            rows, lanes = slice(c * CHUNK, (c + 1) * CHUNK), slice(hh * k_, (hh + 1) * k_)
            qp, fp, gp = q_ref[rows, lanes].astype(F32), f_ref[rows, lanes].astype(F32), g_ref[rows, lanes].astype(F32)
            vb = i_ref[rows, lanes].astype(BF16)
            _, _, k, _, qf, logf = _chunk_gates(qp, fp, lb_ref[:, lanes])
            yield
            b = _tri_dot(tri, logf)
            yield
            e_b, e_qm, e_km, e_kl, e_l = _chunk_decays(b)
            qm, km, qe, kl = (qf * e_qm).astype(BF16), (k * e_km).astype(BF16), (qf * e_b).astype(BF16), (k * e_kl).astype(BF16)
            yield
            a = jnp.where(causal, _dot(qm, km, "nt"), 0.0).astype(BF16)
            update = _dot(vb, kl, "tn")
            yield
            st_ref[c, hh] = st[hh]
            o = _dot(qe, st[hh], "nt") + _dot(a, vb, "nn")
            st[hh] = st[hh] * e_l + update
            yield
            o_ref[rows, lanes] = o
            on = o * lax.rsqrt(jnp.mean(o * o, axis=-1, keepdims=True) + EPS) * nwv
            mix_ref[rows, lanes] = (on * (gp * _sigmoid(gp))).astype(BF16)

        _emit_pipelined((task(c, hh) for c in range(nch) for hh in range(hb)), 6)
        for hh in range(hb):
            state[hh] = st[hh]

    def col(g):
        return pl.BlockSpec((None, t, hb * k_), lambda h, i: (g, i, h))

    grid = (h_ // hb, nt)
    return pl.pallas_call(
        _hosted(compute, 6, 3, 1, ride, grid),
        name="hgrn_fwd",
        grid=grid,
        in_specs=[
            col(0),
            col(1),
            col(2),
            col(3),
            pl.BlockSpec((1, hb * k_), lambda h, i: (0, h)),
            pl.BlockSpec((1, k_), lambda h, i: (0, 0)),
        ]
        + [HBM_SPEC] * ride.n,
        out_specs=[
            pl.BlockSpec((t, hb * k_), lambda h, i: (i, h)),
            pl.BlockSpec((t, hb * k_), lambda h, i: (i, h)),
            pl.BlockSpec((nch, hb, k_, k_), lambda h, i: (i, h, 0, 0)),
        ]
        + [HBM_SPEC] * ride.n,
        out_shape=[
            jax.ShapeDtypeStruct((s, 2 * h_ * k_), BF16),
            jax.ShapeDtypeStruct((s, h_ * k_), F32),
            jax.ShapeDtypeStruct((s // CHUNK, h_, k_, k_), F32),
        ]
        + ride.out_shape,
        scratch_shapes=[pltpu.VMEM((hb, k_, k_), F32)] + ride.scratch,
        compiler_params=_params(2),
    )(proj, proj, proj, proj, lb, nw, *ride.inputs)


def _hgrn_bwd(proj, lb, nw, o_pre, states, dmix, ride):
    s = proj.shape[1]
    t = min(TS, s)
    nt, nch = s // t, t // CHUNK
    h_, k_, hb = HGRN_HEADS, HEAD_DIM, HEADS_PER_STEP

    def compute(q_ref, f_ref, i_ref, g_ref, lb_ref, nw_ref, o_ref, st_ref, dm_ref, dp_ref, dlb_ref, dnw_ref, dstate, acc_lb, acc_nw):
        h, i = pl.program_id(0), pl.program_id(1)
        dq_ref, df_ref, di_ref, dg_ref = dp_ref.at[0], dp_ref.at[1], dp_ref.at[2], dp_ref.at[3]

        @pl.when(i == 0)
        def _():
            dstate[...] = jnp.zeros_like(dstate)
            acc_lb[...] = jnp.zeros_like(acc_lb)

        @pl.when((i == 0) & (h == 0))
        def _():
            acc_nw[...] = jnp.zeros_like(acc_nw)

        nwv = nw_ref[...]
        rr = lax.broadcasted_iota(jnp.int32, (CHUNK, CHUNK), 0)
        cc = lax.broadcasted_iota(jnp.int32, (CHUNK, CHUNK), 1)
        causal = rr >= cc
        tri = jnp.where(causal, 1.0, 0.0).astype(BF16)
        tri_t =jnp.where(cc >= rr, 1.0, 0.0).astype(BF16)
        last_row = lax.broadcasted_iota(jnp.int32, (CHUNK, k_), 0) == CHUNK - 1

        dst = [dstate[hh] for hh in range(hb)]
        sum_nw = [jnp.zeros((8, k_), F32)]
        sum_lb = [jnp.zeros((8, k_), F32) for _ in range(hb)]

        def task(c, hh):
            rows, lanes = slice(c * CHUNK, (c + 1) * CHUNK), slice(hh * k_, (hh + 1) * k_)
            lbv = lb_ref[:, lanes]
            qp, fp, gp = q_ref[rows, lanes].astype(F32), f_ref[rows, lanes].astype(F32), g_ref[rows, lanes].astype(F32)
            vb = i_ref[rows, lanes].astype(BF16)
            sf, f, k, sq, qf, logf = _chunk_gates(qp, fp, lbv)
            yield
            b = _tri_dot(tri, logf)
            dm = dm_ref[rows, lanes]
            o = o_ref[rows, lanes]
            rinv = lax.rsqrt(jnp.mean(o * o, axis=-1, keepdims=True) + EPS)
            ohat = o * rinv
            sg = _sigmoid(gp)
            dg_ref[rows, lanes] = (dm * (ohat * nwv) * (sg * (1.0 + gp * (1.0 - sg)))).astype(BF16)
            don = dm * (gp * sg)
            sum_nw[0] = sum_nw[0] + _rows_to_8(don * ohat)
            dohat = don * nwv
            do = rinv * (dohat - ohat * jnp.mean(dohat * ohat, axis=-1, keepdims=True))
            yield
            e_b, e_qm, e_km, e_kl, e_l = _chunk_decays(b)
            st0 = st_ref[c, hh]
            do2 = _pair(do)
            qm2, km2 = _pair(qf * e_qm), _pair(k * e_km)
            st02, qe2 = _pair(st0), _pair(qf * e_b)
            yield
            a = jnp.where(causal, _dot(qm2[0], km2[0], "nt"), 0.0)
            da2 = _pair(jnp.where(causal, _dot_pairs(do2, (vb, None), "nt"), 0.0))
            dq_state = _dot_pairs(do2, st02, "nn")
            update = _dot_pairs(do2, qe2, "tn")
            yield
            dst2 = _pair(dst[hh])
            dq = e_b * dq_state + e_qm * _dot_pairs(da2, km2, "nn")
            dk_state = e_kl * _dot_pairs((vb, None), dst2, "nn")
            dk = e_km * _dot_pairs(da2, qm2, "tn") + dk_state
            dv = _dot(a, do2[0], "tn") + _dot(k * e_kl, dst2[0], "nt")
            d_bl = jnp.sum(k * dk_state, axis=0, keepdims=True) + e_l * jnp.sum(st0 * dst[hh], axis=0, keepdims=True)
            dst[hh] = dst[hh] * e_l + update
            yield
            dlogf = _tri_dot(tri_t, qf * dq - k * dk + jnp.where(last_row, d_bl, 0.0))
            yield
            dfv = dlogf / f - dk
            df_ref[rows, lanes] = (dfv * (1.0 - lbv) * (sf * (1.0 - sf))).astype(BF16)
            sum_lb[hh] = sum_lb[hh] + _rows_to_8(dfv * (1.0 - sf))
            dq_ref[rows, lanes] = (dq * (sq * (1.0 + qp * (1.0 - sq)))).astype(BF16)
            di_ref[rows, lanes] = dv.astype(BF16)

        _emit_pipelined((task(c, hh) for c in reversed(range(nch)) for hh in range(hb)), 7)
        acc_nw[...] += sum_nw[0]
        for hh in range(hb):
            dstate[hh] = dst[hh]
            acc_lb[:, hh * k_ : (hh + 1) * k_] += sum_lb[hh]

        @pl.when(i == nt - 1)
        def _():
            dlb_ref[...] = jnp.sum(acc_lb[...], axis=0, keepdims=True)

        @pl.when((i == nt - 1) & (h == h_ // hb - 1))
        def _():
            dnw_ref[...] = jnp.sum(acc_nw[...], axis=0, keepdims=True)

    def col(g):
        return pl.BlockSpec((None, t, hb * k_), lambda h, i: (g, nt - 1 - i, h))

    row = pl.BlockSpec((t, hb * k_), lambda h, i: (nt - 1 - i, h))
    grid = (h_ // hb, nt)
    return pl.pallas_call(
        _hosted(compute, 9, 3, 3, ride, grid),
        name="hgrn_bwd",
        grid=grid,
        in_specs=[
            col(0),
            col(1),
            col(2),
            col(3),
            pl.BlockSpec((1, hb * k_), lambda h, i: (0, h)),
            pl.BlockSpec((1, k_), lambda h, i: (0, 0)),
            row,
            pl.BlockSpec((nch, hb, k_, k_), lambda h, i: (nt - 1 - i, h, 0, 0)),
            row,
        ]
        + [HBM_SPEC] * ride.n,
        out_specs=[
            pl.BlockSpec((4, t, hb * k_), lambda h, i: (0, nt - 1 - i, h)),
            pl.BlockSpec((1, hb * k_), lambda h, i: (0, h)),
            pl.BlockSpec((1, k_), lambda h, i: (0, 0)),
        ]
        + [HBM_SPEC] * ride.n,
        out_shape=[
            jax.ShapeDtypeStruct((8, s, h_ * k_), BF16),
            jax.ShapeDtypeStruct((1, h_ * k_), F32),
            jax.ShapeDtypeStruct((1, k_), F32),
        ]
        + ride.out_shape,
        scratch_shapes=[pltpu.VMEM((hb, k_, k_), F32), pltpu.VMEM((8, hb * k_), F32), pltpu.VMEM((8, k_), F32)] + ride.scratch,
        compiler_params=_params(2),
    )(proj, proj, proj, proj, lb, nw, o_pre, states, dmix, *ride.inputs)


def _shift_rows(ext, k):
    n = ext.shape[0]
    return pltpu.roll(ext, k % n, axis=0)


def _row_select(parts):
    w = parts[0].shape[1]
    row = lax.broadcasted_iota(jnp.int32, (8, w), 0)
    out = jnp.zeros((8, w), F32)
    for r, p in enumerate(parts):
        out = out + jnp.where(row == r, p, 0.0)
    return out


def _sconv_fwd(proj, sw, mix):
    s = proj.shape[1]
    t = min(TS, s)
    nt, w = s // t, 512
    ncol, hb = GROUP_W // w, t // HALO

    def body(cb_ref, cc_ref, ch_ref, ccp_ref, chp_ref, sw_ref, mix_in, mix_ref):
        del mix_in
        i = pl.program_id(0)
        u = cc_ref[...].astype(F32) * ch_ref[...].astype(F32)
        up = jnp.where(i > 0, ccp_ref[...].astype(F32) * chp_ref[...].astype(F32), 0.0)
        ext = jnp.concatenate([up, u], axis=0)
        z = sw_ref[0:1, :] * _shift_rows(ext, 2)[HALO:] + sw_ref[1:2, :] * _shift_rows(ext, 1)[HALO:] + sw_ref[2:3, :] * u
        mix_ref[...] = (cb_ref[...].astype(F32) * z).astype(BF16)

    def cur(g):
        return pl.BlockSpec((None, t, w), lambda i, c: (g, i, c))

    def prev(g):
        return pl.BlockSpec((None, HALO, w), lambda i, c: (g, jnp.maximum(i * hb - 1, 0), c))

    return pl.pallas_call(
        body,
        name="sconv_fwd",
        grid=(nt, ncol),
        in_specs=[cur(4), cur(5), cur(6), prev(5), prev(6), pl.BlockSpec((3, w), lambda i, c: (0, c)), pl.BlockSpec(memory_space=pl.ANY)],
        out_specs=pl.BlockSpec((t, w), lambda i, c: (i, ncol + c)),
        out_shape=jax.ShapeDtypeStruct(mix.shape, mix.dtype),
        input_output_aliases={6: 0},
        compiler_params=_params(2),
    )(proj, proj, proj, proj, proj, sw, mix)


def _sconv_bwd(proj, sw, dmix, dproj):
    s = proj.shape[1]
    t = min(TS, s)
    nt, w = s // t, 512
    ncol, hb = GROUP_W // w, t // HALO

    def body(cb_ref, cbn_ref, cc_ref, ccp_ref, ch_ref, chp_ref, dy_ref, dyn_ref, sw_ref, dp_in, dp_ref, dsw_ref, acc):
        del dp_in
        i = pl.program_id(1)
        w0, w1, w2 = sw_ref[0:1, :], sw_ref[1:2, :], sw_ref[2:3, :]
        ccv, chv, cbv, dyv = cc_ref[...].astype(F32), ch_ref[...].astype(F32), cb_ref[...].astype(F32), dy_ref[...]
        u = ccv * chv
        up = jnp.where(i > 0, ccp_ref[...].astype(F32) * chp_ref[...].astype(F32), 0.0)
        ext = jnp.concatenate([up, u], axis=0)
        u1, u2 = _shift_rows(ext, 1)[HALO:], _shift_rows(ext, 2)[HALO:]
        z = w0 * u2 + w1 * u1 + w2 * u
        dz = dyv * cbv
        dzn = jnp.where(i < nt - 1, dyn_ref[...] * cbn_ref[...].astype(F32), 0.0)
        dze = jnp.concatenate([dz, dzn], axis=0)
        du = w2 * dz + w1 * _shift_rows(dze, -1)[:t] + w0 * _shift_rows(dze, -2)[:t]
        dp_ref[0] = (dyv * z).astype(BF16)
        dp_ref[1] = (du * chv).astype(BF16)
        dp_ref[2] = (du * ccv).astype(BF16)
        dp_ref[3] = jnp.zeros((t, w), BF16)

        @pl.when(i == 0)
        def _():
            acc[...] = jnp.zeros_like(acc)

        acc[...] += _row_select([jnp.sum(dz * uu, axis=0, keepdims=True) for uu in (u2, u1, u)])

        @pl.when(i == nt - 1)
        def _():
            dsw_ref[...] = acc[...]

    def cur(g):
        return pl.BlockSpec((None, t, w), lambda c, i: (g, i, c))

    def prev(g):
        return pl.BlockSpec((None, HALO, w), lambda c, i: (g, jnp.maximum(i * hb - 1, 0), c))

    def nxt(g):
        return pl.BlockSpec((None, HALO, w), lambda c, i: (g, jnp.minimum((i + 1) * hb, s // HALO - 1), c))

    return pl.pallas_call(
        body,
        name="sconv_bwd",
        grid=(ncol, nt),
        in_specs=[
            cur(4),
            nxt(4),
            cur(5),
            prev(5),
            cur(6),
            prev(6),
            pl.BlockSpec((t, w), lambda c, i: (i, ncol + c)),
            pl.BlockSpec((HALO, w), lambda c, i: (jnp.minimum((i + 1) * hb, s // HALO - 1), ncol + c)),
            pl.BlockSpec((3, w), lambda c, i: (0, c)),
            pl.BlockSpec(memory_space=pl.ANY),
        ],
        out_specs=[pl.BlockSpec((4, t, w), lambda c, i: (1, i, c)), pl.BlockSpec((8, w), lambda c, i: (0, c))],
        out_shape=[jax.ShapeDtypeStruct(dproj.shape, dproj.dtype), jax.ShapeDtypeStruct((8, GROUP_W), F32)],
        scratch_shapes=[pltpu.VMEM((8, w), F32)],
        input_output_aliases={9: 0},
        compiler_params=_params(2),
    )(proj, proj, proj, proj, proj, proj, dmix, dmix, sw, dproj)


def _ffn_down(a_pre, up, cw, cb, wd_g, x2):
    nd, s, f = a_pre.shape
    d = wd_g.shape[2]
    t = min(TS, s)
    nt, hb = s // t, t // HALO

    def body(a_ref, ap_ref, up_ref, cw_ref, cb_ref, w_ref, res_ref, o_ref, act_ref):
        i, j = pl.program_id(0), pl.program_id(1)
        av = a_ref[...].astype(F32)
        ext = jnp.concatenate([jnp.where(i > 0, ap_ref[...].astype(F32), 0.0), av], axis=0)
        a = cw_ref[0:1, :] * _shift_rows(ext, 2)[HALO:] + cw_ref[1:2, :] * _shift_rows(ext, 1)[HALO:] + cw_ref[2:3, :] * av
        a = a + cb_ref[...]
        act = (a * _sigmoid(a) * up_ref[...].astype(F32)).astype(BF16)
        act_ref[...] = act

        @pl.when(j == 0)
        def _():
            o_ref[...] = res_ref[...]

        o_ref[...] += _dot(act, w_ref[...], "nn")

    cur = pl.BlockSpec((None, t, f), lambda i, j: (j, i, 0))
    prev = pl.BlockSpec((None, HALO, f), lambda i, j: (j, jnp.maximum(i * hb - 1, 0), 0))
    row = pl.BlockSpec((t, d), lambda i, j: (i, 0))
    return pl.pallas_call(
        body,
        name="ffn_down",
        grid=(nt, nd),
        in_specs=[
            cur,
            prev,
            cur,
            pl.BlockSpec((None, 3, f), lambda i, j: (j, 0, 0)),
            pl.BlockSpec((None, 1, f), lambda i, j: (j, 0, 0)),
            pl.BlockSpec((None, f, d), lambda i, j: (j, 0, 0)),
            row,
        ],
        out_specs=[row, cur],
        out_shape=[jax.ShapeDtypeStruct((s, d), F32), jax.ShapeDtypeStruct(a_pre.shape, BF16)],
        compiler_params=_params(2),
    )(a_pre, a_pre, up, cw, cb, wd_g, x2)


def _ffn_dh3(a_pre, up, dact, cw, cb, wg_g, wu_g, ride):
    nd, s, f = a_pre.shape
    d = wg_g.shape[1]
    t = min(TS, s)
    nt, hb = s // t, t // HALO

    def compute(a_ref, ap_ref, an_ref, up_ref, upn_ref, da_ref, dan_ref, cw_ref, cb_ref, wg_ref, wu_ref, dh_ref, dap_ref, dup_ref, dsm_ref):
        i, j = pl.program_id(0), pl.program_id(1)
        w0, w1, w2 = cw_ref[0:1, :], cw_ref[1:2, :], cw_ref[2:3, :]
        av = a_ref[...].astype(F32)
        ext = jnp.concatenate([jnp.where(i > 0, ap_ref[...].astype(F32), 0.0), av, an_ref[...].astype(F32)], axis=0)
        e1, e2 = _shift_rows(ext, 1), _shift_rows(ext, 2)
        a = (w0 * e2 + w1 * e1 + w2 * ext)[HALO:] + cb_ref[...]
        sg = _sigmoid(a)
        dact_e = jnp.concatenate([da_ref[...], dan_ref[...]], axis=0).astype(F32)
        up_e = jnp.concatenate([up_ref[...], upn_ref[...]], axis=0).astype(F32)
        row = lax.broadcasted_iota(jnp.int32, a.shape, 0)
        live = (row < t) | (i < nt - 1)
        d_a = jnp.where(live, dact_e * up_e * (sg * (1.0 + a * (1.0 - sg))), 0.0)
        dup = (dact_e * (a * sg))[:t].astype(BF16)
        dap = (w2 * d_a + w1 * _shift_rows(d_a, -1) + w0 * _shift_rows(d_a, -2))[:t].astype(BF16)
        dup_ref[...] = dup
        dap_ref[...] = dap

        @pl.when(j == 0)
        def _():
            dh_ref[...] = jnp.zeros_like(dh_ref)

        dh_ref[...] += _dot(dap, wg_ref[...], "nt") + _dot(dup, wu_ref[...], "nt")

        d_a_cur = d_a[:t]
        sums = [jnp.sum(d_a_cur * e[HALO : HALO + t], axis=0, keepdims=True) for e in (e2, e1, ext)]
        part = _row_select(sums + [jnp.sum(d_a_cur, axis=0, keepdims=True)])

        @pl.when(i == 0)
        def _():
            dsm_ref[j] = part

        @pl.when(i > 0)
        def _():
            dsm_ref[j] += part

    cur = pl.BlockSpec((None, t, f), lambda i, j: (j, i, 0))
    prev = pl.BlockSpec((None, HALO, f), lambda i, j: (j, jnp.maximum(i * hb - 1, 0), 0))
    nxt = pl.BlockSpec((None, HALO, f), lambda i, j: (j, jnp.minimum((i + 1) * hb, s // HALO - 1), 0))
    w_spec = pl.BlockSpec((None, d, f), lambda i, j: (j, 0, 0))
    grid = (nt, nd)
    return pl.pallas_call(
        _hosted(compute, 11, 4, 0, ride, grid),
        name="ffn_dh3",
        grid=grid,
        in_specs=[cur, prev, nxt, cur, nxt, cur, nxt, pl.BlockSpec((None, 3, f), lambda i, j: (j, 0, 0)), pl.BlockSpec((None, 1, f), lambda i, j: (j, 0, 0)), w_spec, w_spec]
        + [HBM_SPEC] * ride.n,
        out_specs=[pl.BlockSpec((t, d), lambda i, j: (i, 0)), cur, cur, pl.BlockSpec((nd, 8, f), lambda i, j: (0, 0, 0))] + [HBM_SPEC] * ride.n,
        out_shape=[
            jax.ShapeDtypeStruct((s, d), F32),
            jax.ShapeDtypeStruct(a_pre.shape, BF16),
            jax.ShapeDtypeStruct(a_pre.shape, BF16),
            jax.ShapeDtypeStruct((nd, 8, f), F32),
        ]
        + ride.out_shape,
        scratch_shapes=ride.scratch,
        compiler_params=_params(2),
    )(a_pre, a_pre, a_pre, up, up, dact, dact, cw, cb, wg_g, wu_g, *ride.inputs)


def _softmax_rows(sc):
    m = jnp.max(sc, axis=-1, keepdims=True)
    e = jnp.exp(sc - m)
    return e / jnp.sum(e, axis=-1, keepdims=True)


def _attn_fwd(q, km, vm):
    s, d = q.shape
    m = km.shape[0]
    t = min(TS, s)
    hd = d // MEM_HEADS
    scale = hd**-0.5

    def body(q_ref, k_ref, v_ref, o_ref):
        for h in range(MEM_HEADS):
            cols = slice(h * hd, (h + 1) * hd)
            p = _softmax_rows(_dot(q_ref[:, cols], k_ref[:, cols], "nt") * scale)
            o_ref[:, cols] = _dot(p, v_ref[:, cols], "nn").astype(BF16)

    row = pl.BlockSpec((t, d), lambda i: (i, 0))
    full = pl.BlockSpec((m, d), lambda i: (0, 0))
    return pl.pallas_call(
        body,
        name="attn_fwd",
        grid=(s // t,),
        in_specs=[row, full, full],
        out_specs=row,
        out_shape=jax.ShapeDtypeStruct((s, d), BF16),
        compiler_params=_params(1),
    )(q, km, vm)


def _attn_bwd(q, km, vm, do):
    s, d = q.shape
    m = km.shape[0]
    t = min(TS, s)
    steps = s // t
    hd = d // MEM_HEADS
    scale = hd**-0.5

    def body(q_ref, k_ref, v_ref, do_ref, dq_ref, dk_ref, dv_ref):
        @pl.when(pl.program_id(0) == 0)
        def _():
            dk_ref[...] = jnp.zeros_like(dk_ref)
            dv_ref[...] = jnp.zeros_like(dv_ref)

        for h in range(MEM_HEADS):
            cols = slice(h * hd, (h + 1) * hd)
            qh, kh, vh, doh = q_ref[:, cols], k_ref[:, cols], v_ref[:, cols], do_ref[:, cols]
            p = _softmax_rows(_dot(qh, kh, "nt") * scale)
            dp = _dot(doh, vh, "nt")
            ds = p * (dp - jnp.sum(dp * p, axis=-1, keepdims=True)) * scale
            dq_ref[:, cols] = _dot(ds, kh, "nn").astype(BF16)
            dk_ref[:, cols] += _dot(ds, qh, "tn")
            dv_ref[:, cols] += _dot(p, doh, "tn")

    row = pl.BlockSpec((t, d), lambda i: (i, 0))
    full = pl.BlockSpec((m, d), lambda i: (0, 0))
    return pl.pallas_call(
        body,
        name="attn_bwd",
        grid=(steps,),
        in_specs=[row, full, full, row],
        out_specs=[row, full, full],
        out_shape=[jax.ShapeDtypeStruct((s, d), BF16), jax.ShapeDtypeStruct((m, d), F32), jax.ShapeDtypeStruct((m, d), F32)],
        compiler_params=_params(1),
    )(q, km, vm, do)


def _row_tile(rows, cols):
    best = 8
    for tr in range(8, rows + 1, 8):
        if rows % tr == 0 and tr * cols * 4 <= (1 << 20):
            best = tr
    return best


def _adamw(w, g, m, v):
    m = ADAM_B1 * m + (1.0 - ADAM_B1) * g
    v = ADAM_B2 * v + (1.0 - ADAM_B2) * (g * g)
    m_hat = m / (1.0 - ADAM_B1**ADAM_STEP)
    v_hat = v / (1.0 - ADAM_B2**ADAM_STEP)
    delta = -ADAM_LR * (m_hat / (jnp.sqrt(v_hat) + ADAM_EPS) + ADAM_WD * w)
    return delta, m, v


def _adam_shard(name, parts, w, m, v):
    rows, cols = w.shape
    tr = _row_tile(rows, cols)

    def body(*refs):
        part_refs = refs[:N_DEV]
        w_ref, m_ref, v_ref, go_ref, d_ref, mo_ref, vo_ref = refs[N_DEV:]
        g = part_refs[0][...].astype(F32)
        for dev in range(1, N_DEV):
            g = g + part_refs[dev][...].astype(F32)
        go_ref[...] = g
        d_ref[...], mo_ref[...], vo_ref[...] = _adamw(w_ref[...], g, m_ref[...], v_ref[...])

    def part(dev):
        return pl.BlockSpec((None, tr, cols), lambda i: (dev, i, 0))

    flat = pl.BlockSpec((tr, cols), lambda i: (i, 0))
    return pl.pallas_call(
        body,
        name=name,
        grid=(rows // tr,),
        in_specs=[part(dev) for dev in range(N_DEV)] + [flat, flat, flat],
        out_specs=[flat, flat, flat, flat],
        out_shape=[jax.ShapeDtypeStruct((rows, cols), F32)] * 4,
        compiler_params=_params(1),
    )(*([parts] * N_DEV), w, m, v)


def _sum_devices(gathered):
    def body(g_ref, o_ref):
        tot = g_ref[0]
        for dev in range(1, N_DEV):
            tot = tot + g_ref[dev]
        o_ref[...] = tot

    return pl.pallas_call(body, name="sum_devices", out_shape=jax.ShapeDtypeStruct(gathered.shape[1:], F32))(gathered)


def _adam_packed(w, g, m, v):
    def body(w_ref, g_ref, m_ref, v_ref, d_ref, mo_ref, vo_ref):
        d_ref[...], mo_ref[...], vo_ref[...] = _adamw(w_ref[...], g_ref[...], m_ref[...], v_ref[...])

    return pl.pallas_call(body, name="adam_packed", out_shape=[jax.ShapeDtypeStruct(w.shape, F32)] * 3)(w, g, m, v)


def _padded(size):
    return -(-size // 128) * 128


def _pack(parts):
    flat = []
    for p in parts:
        v = p.reshape(-1)
        flat.append(jnp.pad(v, (0, _padded(v.shape[0]) - v.shape[0])))
    return jnp.concatenate(flat).reshape(-1, 128)


def _unpack(packed, shapes):
    flat = packed.reshape(-1)
    out, pos = [], 0
    for shp in shapes:
        size = math.prod(shp)
        out.append(flat[pos : pos + size].reshape(shp))
        pos += _padded(size)
    return out


WEIGHTS = (
    "hgrn_lb norm1_w w_in hgrn_norm_w sconv_w w_out norm2_w mem_norm_w wq wk wv wo norm3_w w_gate w_up ffn_conv_w "
    "ffn_conv_b w_down final_norm_w"
).split()
BIG = ["w_in", "w_out", "wq", "wk", "wv", "wo", "w_gate", "w_up", "w_down"]
SMALL_SHARDED = ["sconv_w", "ffn_conv_w"]
SMALL = [n for n in WEIGHTS if n not in BIG]


def kernel(x, mem, hgrn_lb, norm1_w, w_in, hgrn_norm_w, sconv_w, w_out, norm2_w, mem_norm_w, wq, wk, wv, wo, norm3_w, w_gate, w_up, ffn_conv_w, ffn_conv_b, w_down, final_norm_w, loss_target, m_hgrn_lb, m_norm1_w, m_w_in, m_hgrn_norm_w, m_sconv_w, m_w_out, m_norm2_w, m_mem_norm_w, m_wq, m_wk, m_wv, m_wo, m_norm3_w, m_w_gate, m_w_up, m_ffn_conv_w, m_ffn_conv_b, m_w_down, m_final_norm_w, v_hgrn_lb, v_norm1_w, v_w_in, v_hgrn_norm_w, v_sconv_w, v_w_out, v_norm2_w, v_mem_norm_w, v_wq, v_wk, v_wv, v_wo, v_norm3_w, v_w_gate, v_w_up, v_ffn_conv_w, v_ffn_conv_b, v_w_down, v_final_norm_w):
    given = dict(locals())
    w_of = {n: given[n] for n in WEIGHTS}
    m_of = {n: given["m_" + n] for n in WEIGHTS}
    v_of = {n: given["v_" + n] for n in WEIGHTS}
    x2, mem2, target = x[0], mem[0], loss_target[0]
    d = x2.shape[1]
    xi, yi, ci = lax.axis_index("x"), lax.axis_index("y"), lax.axis_index("c")
    me = 4 * xi + 2 * yi + ci
    final_w = final_norm_w.reshape(1, d)

    shard = {n: w_of[n][0].astype(BF16) for n in BIG}
    small_shard = jnp.concatenate([sconv_w[0], ffn_conv_w[0]], axis=1)
    h1, w_in_g, small_g = _rmsnorm("norm1", x2, norm1_w, _GatherRide([shard["w_in"], small_shard]))
    w_in_f = jnp.transpose(w_in_g, (1, 0, 2)).reshape(d, -1)
    conv_ch = sconv_w.shape[2]
    sw_f = jnp.transpose(small_g[:, :, :conv_ch], (1, 0, 2)).reshape(3, -1)
    cw_g = small_g[:, :, conv_ch:]
    cb_g = ffn_conv_b.reshape(N_DEV, 1, -1)

    lb = _lb_fwd(hgrn_lb)
    proj, *attn_w = _mm_proj(h1, w_in_f, _GatherRide([shard[n] for n in ("w_out", "wq", "wk", "wv", "wo")]))
    w_out_f, wq_f, wk_f, wv_f, wo_f = (a.reshape(d, d) for a in attn_w)
    mix, o_pre, states, wg_g, wu_g = _hgrn_fwd(proj, lb, hgrn_norm_w, _GatherRide([shard["w_gate"], shard["w_up"]]))
    mix = _sconv_fwd(proj, sw_f, mix)
    x1 = _mm_nn("mm_out", mix, w_out_f, F32, residual=x2)
    h2 = _rmsnorm("norm2", x1, norm2_w)
    mem_n = _rmsnorm("norm_mem", mem2, mem_norm_w)
    q = _mm_nn("mm_q", h2, wq_f, BF16)
    km = _mm_nn("mm_k", mem_n, wk_f, BF16)
    vm = _mm_nn("mm_v", mem_n, wv_f, BF16)
    ao = _attn_fwd(q, km, vm)
    x2r = _mm_nn("mm_o", ao, wo_f, F32, residual=x1)
    h3 = _rmsnorm("norm3", x2r, norm3_w)
    a_pre, up, wd_g = _mm_gate_up(h3, wg_g, wu_g, _GatherRide([shard["w_down"]]))
    x3, act = _ffn_down(a_pre, up, cw_g, cb_g, wd_g, x2r)

    dx3, dx3b, loss_blk, d_final = _loss_head(x3, target, final_w)
    parts = {}
    dact = _mm_dact(dx3b, wd_g)
    dwd = _mm_dwdown(act, dx3b)
    dh3, dapre, dup, d_ffn_small, parts["w_down"] = _ffn_dh3(a_pre, up, dact, cw_g, cb_g, wg_g, wu_g, _ScatterRide([dwd]))
    dwg, dwu = _mm_dw_cols("mm_dw_gate_up", h3, [dapre, dup])
    dx2, dx2b, d_n3 = _rmsnorm_bwd("norm3_bwd", dh3, x2r, norm3_w, dx3)
    dao = _mm_nt("mm_dao", [(dx2b, wo_f)], BF16)
    dwo = _mm_tn("mm_dwo", ao, dx2b)
    dq, dkm, dvm = _attn_bwd(q, km, vm, dao)
    dwq = _mm_tn("mm_dwq", h2, dq)
    dh2 = _mm_nt("mm_dh2", [(dq, wq_f)], F32)
    dwk = _mm_tn("mm_dwk", mem_n, dkm)
    dwv = _mm_tn("mm_dwv", mem_n, dvm)
    dmem_n = _mm_nt("mm_dmem", [(dkm, wk_f), (dvm, wv_f)], F32)
    _, _, d_nm = _rmsnorm_bwd("norm_mem_bwd", dmem_n, mem2, mem_norm_w)
    dx1, dx1b, d_n2 = _rmsnorm_bwd("norm2_bwd", dh2, x1, norm2_w, dx2)
    dmix = _mm_nt("mm_dmix", [(dx1b, w_out_f)], F32)
    dwout = _mm_tn("mm_dwout", mix, dx1b)
    attn_names = ["w_out", "wq", "wk", "wv", "wo"]
    attn_grads = [a.reshape(N_DEV, d // N_DEV, d) for a in (dwout, dwq, dwk, dwv, dwo)]
    dproj, d_lb, d_nw, parts["w_gate"], parts["w_up"] = _hgrn_bwd(proj, lb, hgrn_norm_w, o_pre, states, dmix, _ScatterRide([dwg, dwu]))
    dproj, d_sw = _sconv_bwd(proj, sw_f, dmix, dproj)
    dwin, *attn_parts = _mm_dwin(h1, dproj, N_GROUPS, _ScatterRide(attn_grads))
    parts.update(zip(attn_names, attn_parts))
    dwin_g = jnp.transpose(dwin.reshape(d, N_DEV, -1), (1, 0, 2))
    dh1, parts["w_in"] = _mm_dh1(dproj, w_in_f, N_GROUPS, _ScatterRide([dwin_g]))
    grad_x, _, d_n1 = _rmsnorm_bwd("norm1_bwd", dh1, x2, norm1_w, dx1)
    d_hlb = _lb_bwd(hgrn_lb, d_lb)

    small_parts = [d_hlb, d_n1, d_nw, d_sw[0:3], d_n2, d_nm, d_n3, d_ffn_small[:, 0:3, :], d_ffn_small[:, 3, :], d_final, loss_blk]
    small_shapes = [p.shape for p in small_parts]
    total = _sum_devices(_allgather("gather_small", [_pack(small_parts)])[0])
    g_hlb, g_n1, g_nw, g_sw, g_n2, g_nm, g_n3, g_cw, g_cb, g_final, loss_row = _unpack(total, small_shapes)
    grads = {
        "hgrn_lb": g_hlb,
        "norm1_w": g_n1,
        "hgrn_norm_w": g_nw,
        "sconv_w": lax.dynamic_slice_in_dim(g_sw, me * conv_ch, conv_ch, axis=1).reshape(sconv_w.shape),
        "norm2_w": g_n2,
        "mem_norm_w": g_nm,
        "norm3_w": g_n3,
        "ffn_conv_w": lax.dynamic_index_in_dim(g_cw, me, axis=0, keepdims=True),
        "ffn_conv_b": g_cb.reshape(ffn_conv_b.shape),
        "final_norm_w": g_final.reshape(final_norm_w.shape),
    }
    shapes = [w_of[n].shape for n in SMALL]
    deltas_p, new_m_p, new_v_p = _adam_packed(
        _pack([w_of[n] for n in SMALL]), _pack([grads[n] for n in SMALL]), _pack([m_of[n] for n in SMALL]), _pack([v_of[n] for n in SMALL])
    )
    deltas = dict(zip(SMALL, _unpack(deltas_p, shapes)))
    new_m = dict(zip(SMALL, _unpack(new_m_p, shapes)))
    new_v = dict(zip(SMALL, _unpack(new_v_p, shapes)))

    for n in BIG:
        shp = w_of[n].shape
        gw, dw, mw, vw = _adam_shard("adam_" + n, parts[n], w_of[n][0], m_of[n][0], v_of[n][0])
        grads[n], deltas[n], new_m[n], new_v[n] = (a.reshape(shp) for a in (gw, dw, mw, vw))

    loss = loss_row[0, 0]
    return (
        loss,
        grad_x.reshape(x.shape),
        *[grads[n] for n in WEIGHTS],
        *[deltas[n] for n in WEIGHTS],
        *[new_m[n] for n in WEIGHTS],
        *[new_v[n] for n in WEIGHTS],
    )
```

```python
import functools
import math

import jax
import jax.numpy as jnp
from jax import lax
from jax.experimental import pallas as pl
from jax.experimental.pallas import tpu as pltpu

F32 = jnp.float32
BF16 = jnp.bfloat16
MESH = pl.DeviceIdType.MESH
N_DEV = 8
EPS = 1e-6
CHUNK = 64
HGRN_HEADS = 8
HEAD_DIM = 128
HEADS_PER_STEP = 2
MEM_HEADS = 4
GROUP_W = HGRN_HEADS * HEAD_DIM
N_GROUPS = 7
HALO = 16
TS = 512
TR = 256
TM = 1024
TN = 1024
TK = 1024
VMEM_LIMIT = 48 * 1024 * 1024

ADAM_LR = 0.001
ADAM_B1 = 0.9
ADAM_B2 = 0.999
ADAM_EPS = 1e-08
ADAM_WD = 0.01
ADAM_STEP = 10

_DIMS = {
    "nn": (((1,), (0,)), ((), ())),
    "nt": (((1,), (1,)), ((), ())),
    "tn": (((0,), (0,)), ((), ())),
}


def _params(n_axes):
    return pltpu.CompilerParams(dimension_semantics=("arbitrary",) * n_axes, vmem_limit_bytes=VMEM_LIMIT)


def _dot(a, b, mode):
    return lax.dot_general(a.astype(BF16), b.astype(BF16), _DIMS[mode], preferred_element_type=F32)


def _sigmoid(v):
    return 0.5 * jnp.tanh(0.5 * v) + 0.5


def _block_dims(spec):
    return tuple(d for d in spec.block_shape if d is not None)


HBM_SPEC = pl.BlockSpec(memory_space=pltpu.HBM)


def _place():
    x, y, c = lax.axis_index("x"), lax.axis_index("y"), lax.axis_index("c")
    return x, y, c, [(1 - x, y), (x, 1 - y), (1 - x, 1 - y)]


def _slab(p):
    return 4 * p[0] + 2 * p[1] + p[2]


def _remote(src, dst, send_sem, recv_sem, to):
    return pltpu.make_async_remote_copy(src_ref=src, dst_ref=dst, send_sem=send_sem, recv_sem=recv_sem, device_id=to, device_id_type=MESH)


class _GatherRide:
    def __init__(self, shards):
        self.inputs = list(shards)
        self.n = len(shards)
        self.out_shape = [jax.ShapeDtypeStruct((N_DEV, *a.shape), a.dtype) for a in shards]
        self.scratch = [pltpu.SemaphoreType.DMA((self.n, 7)), pltpu.SemaphoreType.DMA((self.n, 7)), pltpu.SemaphoreType.DMA((self.n,))]

    @staticmethod
    def _copy(outs, sems, a, k, block, to, src=None):
        dst = outs[a].at[_slab(block)]
        return _remote(dst if src is None else src, dst, sems[0].at[a, k], sems[1].at[a, k], to)

    def _mine(self, ins, outs, sems, a):
        x, y, c, chips = _place()
        local = pltpu.make_async_copy(ins[a], outs[a].at[_slab((x, y, c))], sems[2].at[a])
        first = [self._copy(outs, sems, a, 0, (x, y, c), (x, y, 1 - c), src=ins[a])]
        first += [self._copy(outs, sems, a, 1 + j, (x, y, c), (*chip, c), src=ins[a]) for j, chip in enumerate(chips)]
        return local, first

    def start(self, ins, outs, sems):
        for a in range(self.n):
            local, first = self._mine(ins, outs, sems, a)
            local.start()
            for cp in first:
                cp.start()

    def forward(self, ins, outs, sems):
        x, y, c, chips = _place()
        for j, chip in enumerate(chips):
            for a in range(self.n):
                self._copy(outs, sems, a, 1 + j, (*chip, c), (x, y, c)).wait_recv()
                self._copy(outs, sems, a, 4 + j, (*chip, c), (x, y, 1 - c)).start()

    def finish(self, ins, outs, sems):
        x, y, c, chips = _place()
        for a in range(self.n):
            self._copy(outs, sems, a, 0, (x, y, 1 - c), (x, y, c)).wait_recv()
            for j, chip in enumerate(chips):
                self._copy(outs, sems, a, 4 + j, (*chip, 1 - c), (x, y, c)).wait_recv()
        for a in range(self.n):
            local, first = self._mine(ins, outs, sems, a)
            for cp in first:
                cp.wait_send()
            for j, chip in enumerate(chips):
                self._copy(outs, sems, a, 4 + j, (*chip, c), (x, y, 1 - c)).wait_send()
            local.wait()


class _ScatterRide:
    def __init__(self, grads):
        self.inputs = list(grads)
        self.n = len(grads)
        self.out_shape = [jax.ShapeDtypeStruct(g.shape, g.dtype) for g in grads]
        self.scratch = [pltpu.SemaphoreType.DMA((self.n, 7)), pltpu.SemaphoreType.DMA((self.n, 7)), pltpu.SemaphoreType.DMA((self.n,))]

    @staticmethod
    def _peers():
        x, y, c, chips = _place()
        return _slab((x, y, c)), [(x, y, 1 - c)] + [(*chip, c) for chip in chips] + [(*chip, 1 - c) for chip in chips]

    def _sends(self, ins, outs, sems, a):
        me, peers = self._peers()
        local = pltpu.make_async_copy(ins[a].at[me], outs[a].at[me], sems[2].at[a])
        return [local] + [_remote(ins[a].at[_slab(p)], outs[a].at[me], sems[0].at[a, k], sems[1].at[a, k], p) for k, p in enumerate(peers)]

    def start(self, ins, outs, sems):
        for a in range(self.n):
            for cp in self._sends(ins, outs, sems, a):
                cp.start()

    def forward(self, ins, outs, sems):
        pass

    def finish(self, ins, outs, sems):
        me, peers = self._peers()
        for a in range(self.n):
            for k, p in enumerate(peers):
                _remote(ins[a].at[me], outs[a].at[_slab(p)], sems[0].at[a, k], sems[1].at[a, k], p).wait_recv()
            local, *sends = self._sends(ins, outs, sems, a)
            for cp in sends:
                cp.wait_send()
            local.wait()


def _grid_step(grid):
    step = 0
    for axis, size in enumerate(grid):
        step = step * size + pl.program_id(axis)
    return step


def _ride_begin(ride, grid, ins, outs, sems):
    step, n_steps = _grid_step(grid), math.prod(grid)

    @pl.when(step == 0)
    def _():
        ride.start(ins, outs, sems)

    @pl.when(step == (3 * n_steps) // 4)
    def _():
        ride.forward(ins, outs, sems)


def _ride_end(ride, grid, ins, outs, sems):
    @pl.when(_grid_step(grid) == math.prod(grid) - 1)
    def _():
        ride.finish(ins, outs, sems)


def _allgather(name, shards):
    ride = _GatherRide(shards)
    n = ride.n

    def body(*refs):
        ins, outs, sems = refs[:n], refs[n : 2 * n], refs[2 * n :]
        ride.start(ins, outs, sems)
        ride.forward(ins, outs, sems)
        ride.finish(ins, outs, sems)

    return pl.pallas_call(
        body,
        name=name,
        in_specs=[HBM_SPEC] * n,
        out_specs=[HBM_SPEC] * n,
        out_shape=ride.out_shape,
        scratch_shapes=ride.scratch,
    )(*shards)


def _matmul(name, grid, k_axis, pairs, outs, mode, residual=None, ride=None):
    n_pairs, n_out, has_res = len(pairs), len(outs), residual is not None
    k_steps = grid[k_axis] if k_axis is not None else 1
    n_acc = n_out if k_steps > 1 else 0
    n_ride = ride.n if ride is not None else 0

    def body(*refs):
        ins = refs[: 2 * n_pairs]
        pos = 2 * n_pairs
        res_ref = refs[pos] if has_res else None
        pos += int(has_res)
        ride_ins = refs[pos : pos + n_ride]
        pos += n_ride
        out_refs = refs[pos : pos + n_out]
        pos += n_out
        ride_outs = refs[pos : pos + n_ride]
        pos += n_ride
        acc_refs = refs[pos : pos + n_acc]
        ride_sems = refs[pos + n_acc :]
        if ride is not None:
            _ride_begin(ride, grid, ride_ins, ride_outs, ride_sems)

        def partial(o):
            tot = None
            for p in outs[o][3]:
                d = _dot(ins[2 * p][...], ins[2 * p + 1][...], mode)
                tot = d if tot is None else tot + d
            return tot

        if k_steps == 1:
            for o in range(n_out):
                val = partial(o)
                if has_res and o == 0:
                    val = val + res_ref[...]
                out_refs[o][...] = val.astype(out_refs[o].dtype)
        else:
            k = pl.program_id(k_axis)

            @pl.when(k == 0)
            def _():
                for o in range(n_out):
                    if has_res and o == 0:
                        acc_refs[o][...] = res_ref[...]
                    else:
                        acc_refs[o][...] = jnp.zeros_like(acc_refs[o])

            for o in range(n_out):
                acc_refs[o][...] += partial(o)

            @pl.when(k == k_steps - 1)
            def _():
                for o in range(n_out):
                    out_refs[o][...] = acc_refs[o][...].astype(out_refs[o].dtype)

        if ride is not None:
            _ride_end(ride, grid, ride_ins, ride_outs, ride_sems)

    args, in_specs = [], []
    for a, a_spec, b, b_spec in pairs:
        args += [a, b]
        in_specs += [a_spec, b_spec]
    if has_res:
        args.append(residual[0])
        in_specs.append(residual[1])
    out_specs = [o[2] for o in outs]
    out_shape = [jax.ShapeDtypeStruct(o[0], o[1]) for o in outs]
    scratch = [pltpu.VMEM(_block_dims(o[2]), F32) for o in outs] if k_steps > 1 else []
    if ride is not None:
        args += ride.inputs
        in_specs += [HBM_SPEC] * n_ride
        out_specs += [HBM_SPEC] * n_ride
        out_shape += ride.out_shape
        scratch += ride.scratch
    return pl.pallas_call(
        body,
        name=name,
        grid=grid,
        in_specs=in_specs,
        out_specs=out_specs,
        out_shape=out_shape,
        scratch_shapes=scratch,
        compiler_params=_params(len(grid)),
    )(*args)


def _out_tiles(m, n, out_dtype):
    if out_dtype == F32 and m >= TM:
        return TM // 2, n
    return min(TM, m), min(TN, n)


def _mm_nn(name, a, w, out_dtype, residual=None):
    m, k = a.shape
    n = w.shape[1]
    tm, tn = _out_tiles(m, n, out_dtype)
    pairs = [(a, pl.BlockSpec((tm, k), lambda i, j: (i, 0)), w, pl.BlockSpec((k, tn), lambda i, j: (0, j)))]
    o_spec = pl.BlockSpec((tm, tn), lambda i, j: (i, j))
    res = (residual, o_spec) if residual is not None else None
    return _matmul(name, (m // tm, n // tn), None, pairs, [((m, n), out_dtype, o_spec, [0])], "nn", res)[0]


def _mm_nt(name, pairs_in, out_dtype):
    m, k = pairs_in[0][0].shape
    n = pairs_in[0][1].shape[0]
    tm, tn = _out_tiles(m, n, out_dtype)
    pairs = [
        (a, pl.BlockSpec((tm, k), lambda i, j: (i, 0)), w, pl.BlockSpec((tn, k), lambda i, j: (j, 0)))
        for a, w in pairs_in
    ]
    o_spec = pl.BlockSpec((tm, tn), lambda i, j: (i, j))
    outs = [((m, n), out_dtype, o_spec, list(range(len(pairs))))]
    return _matmul(name, (m // tm, n // tn), None, pairs, outs, "nt")[0]


def _mm_tn(name, a, b):
    s, i_dim = a.shape
    n = b.shape[1]
    ts, ti, tn = min(TK, s), min(TN, i_dim), min(TN, n)
    pairs = [(a, pl.BlockSpec((ts, ti), lambda i, j, k: (k, i)), b, pl.BlockSpec((ts, tn), lambda i, j, k: (k, j)))]
    outs = [((i_dim, n), BF16, pl.BlockSpec((ti, tn), lambda i, j, k: (i, j)), [0])]
    return _matmul(name, (i_dim // ti, n // tn, s // ts), 2, pairs, outs, "tn")[0]


def _mm_proj(h1, w_in, ride):
    s, d = h1.shape
    g = w_in.shape[1] // GROUP_W
    tm = min(TM, s)
    pairs = [(h1, pl.BlockSpec((tm, d), lambda i, j: (i, 0)), w_in, pl.BlockSpec((d, GROUP_W), lambda i, j: (0, j)))]
    outs = [((g, s, GROUP_W), BF16, pl.BlockSpec((None, tm, GROUP_W), lambda i, j: (j, i, 0)), [0])]
    return _matmul("mm_proj", (s // tm, g), None, pairs, outs, "nn", ride=ride)


def _mm_gate_up(h3, wg_g, wu_g, ride):
    s, d = h3.shape
    f = wg_g.shape[2]
    tm = min(TM, s)
    a_spec = pl.BlockSpec((tm, d), lambda i, j: (i, 0))
    w_spec = pl.BlockSpec((None, d, f), lambda i, j: (j, 0, 0))
    o_spec = pl.BlockSpec((None, tm, f), lambda i, j: (j, i, 0))
    pairs = [(h3, a_spec, wg_g, w_spec), (h3, a_spec, wu_g, w_spec)]
    outs = [((N_DEV, s, f), BF16, o_spec, [0]), ((N_DEV, s, f), BF16, o_spec, [1])]
    return _matmul("mm_gate_up", (s // tm, N_DEV), None, pairs, outs, "nn", ride=ride)


def _mm_dact(dx3b, wd_g):
    s, d = dx3b.shape
    f = wd_g.shape[1]
    tm = min(TM, s)
    pairs = [(dx3b, pl.BlockSpec((tm, d), lambda i, j: (i, 0)), wd_g, pl.BlockSpec((None, f, d), lambda i, j: (j, 0, 0)))]
    outs = [((N_DEV, s, f), BF16, pl.BlockSpec((None, tm, f), lambda i, j: (j, i, 0)), [0])]
    return _matmul("mm_dact", (s // tm, N_DEV), None, pairs, outs, "nt")[0]


def _mm_dwdown(act, dx3b):
    _, s, f = act.shape
    d = dx3b.shape[1]
    ts, tn = min(TK, s), min(TN, d)
    pairs = [
        (
            act,
            pl.BlockSpec((None, ts, f), lambda j, n, k: (j, k, 0)),
            dx3b,
            pl.BlockSpec((ts, tn), lambda j, n, k: (k, n)),
        )
    ]
    outs = [((N_DEV, f, d), BF16, pl.BlockSpec((None, f, tn), lambda j, n, k: (j, 0, n)), [0])]
    return _matmul("mm_dwdown", (N_DEV, d // tn, s // ts), 2, pairs, outs, "tn")[0]


def _mm_dh3(dapre, dup, wg_g, wu_g, ride):
    _, s, f = dapre.shape
    d = wg_g.shape[1]
    tm, tn = min(TM, s), min(TN, d)
    a_spec = pl.BlockSpec((None, tm, f), lambda i, j, k: (k, i, 0))
    w_spec = pl.BlockSpec((None, tn, f), lambda i, j, k: (k, j, 0))
    pairs = [(dapre, a_spec, wg_g, w_spec), (dup, a_spec, wu_g, w_spec)]
    outs = [((s, d), F32, pl.BlockSpec((tm, tn), lambda i, j, k: (i, j)), [0, 1])]
    return _matmul("mm_dh3", (s // tm, d // tn, N_DEV), 2, pairs, outs, "nt", ride=ride)


def _mm_dw_cols(name, h, dcols_list):
    s, d = h.shape
    f = dcols_list[0].shape[2]
    ts, ti = min(TK, s), min(TN, d)
    a_spec = pl.BlockSpec((ts, ti), lambda j, i, k: (k, i))
    b_spec = pl.BlockSpec((None, ts, f), lambda j, i, k: (j, k, 0))
    o_spec = pl.BlockSpec((None, ti, f), lambda j, i, k: (j, i, 0))
    pairs = [(h, a_spec, dc, b_spec) for dc in dcols_list]
    outs = [((N_DEV, d, f), BF16, o_spec, [p]) for p in range(len(pairs))]
    return _matmul(name, (N_DEV, d // ti, s // ts), 2, pairs, outs, "tn")


def _mm_dwin(h1, dproj, groups, ride):
    s, d = h1.shape
    ts, ti = min(TK, s), min(TN, d)
    pairs = [
        (
            h1,
            pl.BlockSpec((ts, ti), lambda j, i, k: (k, i)),
            dproj,
            pl.BlockSpec((None, ts, GROUP_W), lambda j, i, k: (j, k, 0)),
        )
    ]
    outs = [((d, groups * GROUP_W), BF16, pl.BlockSpec((ti, GROUP_W), lambda j, i, k: (i, j)), [0])]
    return _matmul("mm_dwin", (groups, d // ti, s // ts), 2, pairs, outs, "tn", ride=ride)


def _mm_dh1(dproj, w_in, groups, ride):
    s = dproj.shape[1]
    d = w_in.shape[0]
    tm, tn = min(TM, s), min(TN, d)
    pairs = [
        (
            dproj,
            pl.BlockSpec((None, tm, GROUP_W), lambda i, j, k: (k, i, 0)),
            w_in,
            pl.BlockSpec((tn, GROUP_W), lambda i, j, k: (j, k)),
        )
    ]
    outs = [((s, d), F32, pl.BlockSpec((tm, tn), lambda i, j, k: (i, j)), [0])]
    return _matmul("mm_dh1", (s // tm, d // tn, groups), 2, pairs, outs, "nt", ride=ride)


def _rows_to_8(v):
    t, d = v.shape
    return jnp.sum(v.reshape(t // 8, 8, d), axis=0)


def _rmsnorm(name, x, w, ride=None):
    s, d = x.shape
    t = min(TS, s)
    grid = (s // t,)

    def compute(x_ref, w_ref, o_ref):
        xv = x_ref[...]
        r = lax.rsqrt(jnp.mean(xv * xv, axis=-1, keepdims=True) + EPS)
        o_ref[...] = (xv * r * w_ref[...]).astype(o_ref.dtype)

    in_specs = [pl.BlockSpec((t, d), lambda i: (i, 0)), pl.BlockSpec((1, d), lambda i: (0, 0))]
    out_spec = pl.BlockSpec((t, d), lambda i: (i, 0))
    out_shape = jax.ShapeDtypeStruct((s, d), BF16)
    if ride is None:
        return pl.pallas_call(compute, name=name, grid=grid, in_specs=in_specs, out_specs=out_spec, out_shape=out_shape, compiler_params=_params(1))(x, w)
    return pl.pallas_call(
        _hosted(compute, 2, 1, 0, ride, grid),
        name=name,
        grid=grid,
        in_specs=in_specs + [HBM_SPEC] * ride.n,
        out_specs=[out_spec] + [HBM_SPEC] * ride.n,
        out_shape=[out_shape] + ride.out_shape,
        scratch_shapes=ride.scratch,
        compiler_params=_params(1),
    )(x, w, *ride.inputs)


def _rmsnorm_bwd(name, dh, x, w, res=None):
    s, d = x.shape
    t = min(TR, s)
    steps = s // t
    has_res = res is not None

    def body(*refs):
        if has_res:
            dh_ref, x_ref, w_ref, res_ref, dx_ref, dxb_ref, dw_ref, acc = refs
        else:
            dh_ref, x_ref, w_ref, dx_ref, dxb_ref, dw_ref, acc = refs
        i = pl.program_id(0)
        xv = x_ref[...]
        r = lax.rsqrt(jnp.mean(xv * xv, axis=-1, keepdims=True) + EPS)
        xn = xv * r
        dhv = dh_ref[...].astype(F32)
        dxn = dhv * w_ref[...]
        dx = r * (dxn - xn * jnp.mean(dxn * xn, axis=-1, keepdims=True))
        if has_res:
            dx = dx + res_ref[...]
        dx_ref[...] = dx
        dxb_ref[...] = dx.astype(BF16)

        @pl.when(i == 0)
        def _():
            acc[...] = jnp.zeros_like(acc)

        acc[...] += _rows_to_8(dhv * xn)

        @pl.when(i == steps - 1)
        def _():
            dw_ref[...] = jnp.sum(acc[...], axis=0, keepdims=True)

    row = pl.BlockSpec((t, d), lambda i: (i, 0))
    vec = pl.BlockSpec((1, d), lambda i: (0, 0))
    args = [dh, x, w] + ([res] if has_res else [])
    return pl.pallas_call(
        body,
        name=name,
        grid=(steps,),
        in_specs=[row, row, vec] + ([row] if has_res else []),
        out_specs=[row, row, vec],
        out_shape=[
            jax.ShapeDtypeStruct((s, d), F32),
            jax.ShapeDtypeStruct((s, d), BF16),
            jax.ShapeDtypeStruct((1, d), F32),
        ],
        scratch_shapes=[pltpu.VMEM((8, d), F32)],
        compiler_params=_params(1),
    )(*args)


def _loss_head(x3, target, w):
    s, d = x3.shape
    t = min(TR, s)
    steps = s // t

    def body(x_ref, t_ref, w_ref, dx_ref, dxb_ref, loss_ref, dw_ref, acc_l, acc_w):
        i = pl.program_id(0)
        xv = x_ref[...]
        wv = w_ref[...]
        r = lax.rsqrt(jnp.mean(xv * xv, axis=-1, keepdims=True) + EPS)
        xn = xv * r
        err = xn * wv - t_ref[...]
        dy = err * (1.0 / d)
        dxn = dy * wv
        dx = r * (dxn - xn * jnp.mean(dxn * xn, axis=-1, keepdims=True))
        dx_ref[...] = dx
        dxb_ref[...] = dx.astype(BF16)

        @pl.when(i == 0)
        def _():
            acc_l[...] = jnp.zeros_like(acc_l)
            acc_w[...] = jnp.zeros_like(acc_w)

        acc_l[...] += _rows_to_8(err * err)
        acc_w[...] += _rows_to_8(dy * xn)

        @pl.when(i == steps - 1)
        def _():
            tot = jnp.sum(jnp.sum(acc_l[...], axis=0, keepdims=True), axis=1, keepdims=True) * (0.5 / d)
            loss_ref[...] = jnp.broadcast_to(tot, loss_ref.shape)
            dw_ref[...] = jnp.sum(acc_w[...], axis=0, keepdims=True)

    row = pl.BlockSpec((t, d), lambda i: (i, 0))
    vec = pl.BlockSpec((1, d), lambda i: (0, 0))
    return pl.pallas_call(
        body,
        name="loss_head",
        grid=(steps,),
        in_specs=[row, row, vec],
        out_specs=[row, row, pl.BlockSpec((1, 128), lambda i: (0, 0)), vec],
        out_shape=[
            jax.ShapeDtypeStruct((s, d), F32),
            jax.ShapeDtypeStruct((s, d), BF16),
            jax.ShapeDtypeStruct((1, 128), F32),
            jax.ShapeDtypeStruct((1, d), F32),
        ],
        scratch_shapes=[pltpu.VMEM((8, d), F32), pltpu.VMEM((8, d), F32)],
        compiler_params=_params(1),
    )(x3, target, w)


def _lb_fwd(hgrn_lb):
    def body(p_ref, lb_ref):
        p = p_ref[...]
        m = jnp.max(p, axis=0, keepdims=True)
        e = jnp.exp(p - m)
        lb_ref[...] = e[0:1, :] / jnp.sum(e, axis=0, keepdims=True)

    return pl.pallas_call(body, name="lb_fwd", out_shape=jax.ShapeDtypeStruct((1, hgrn_lb.shape[1]), F32))(hgrn_lb)


def _lb_bwd(hgrn_lb, dlb):
    def body(p_ref, d_ref, o_ref):
        p = p_ref[...]
        m = jnp.max(p, axis=0, keepdims=True)
        e = jnp.exp(p - m)
        sm = e / jnp.sum(e, axis=0, keepdims=True)
        p0 = sm[0:1, :]
        row = lax.broadcasted_iota(jnp.int32, sm.shape, 0)
        o_ref[...] = d_ref[...] * p0 * (jnp.where(row == 0, 1.0, 0.0) - sm)

    return pl.pallas_call(body, name="lb_bwd", out_shape=jax.ShapeDtypeStruct(hgrn_lb.shape, F32))(hgrn_lb, dlb)


def _split3(v):
    hi = v.astype(BF16)
    r1 = v - hi.astype(F32)
    mid = r1.astype(BF16)
    lo = (r1 - mid.astype(F32)).astype(BF16)
    return hi, mid, lo


def _pair(v):
    hi = v.astype(BF16)
    return hi, (v - hi.astype(F32)).astype(BF16)


def _dot_pairs(a2, b2, mode):
    dims = _DIMS[mode]
    out = lax.dot_general(a2[0], b2[0], dims, preferred_element_type=F32)
    if a2[1] is not None:
        out = lax.dot_general(a2[1], b2[0], dims, preferred_element_type=F32) + out
    if b2[1] is not None:
        out = lax.dot_general(a2[0], b2[1], dims, preferred_element_type=F32) + out
    return out


def _tri_dot(tri, v):
    hi, mid, lo = _split3(v)
    dims = _DIMS["nn"]
    out = lax.dot_general(tri, lo, dims, preferred_element_type=F32)
    out = out + lax.dot_general(tri, mid, dims, preferred_element_type=F32)
    return out + lax.dot_general(tri, hi, dims, preferred_element_type=F32)


def _chunk_gates(qp, fp, lbv):
    sf = _sigmoid(fp)
    f = lbv + (1.0 - lbv) * sf
    k = 1.0 - f
    sq = _sigmoid(qp)
    qf = qp * sq
    return sf, f, k, sq, qf, jnp.log(f)


def _chunk_decays(b):
    row = lax.broadcasted_iota(jnp.int32, b.shape, 0)
    bm = jnp.sum(jnp.where(row == CHUNK // 2 - 1, b, 0.0), axis=0, keepdims=True)
    bl = jnp.sum(jnp.where(row == CHUNK - 1, b, 0.0), axis=0, keepdims=True)
    return jnp.exp(b), jnp.exp(b - bm), jnp.exp(bm - b), jnp.exp(bl - b), jnp.exp(bl)


def _emit_pipelined(tasks, n_stages):
    tasks = list(tasks)
    for step in range(len(tasks) + n_stages - 1):
        for t in range(max(step - n_stages + 1, 0), min(step, len(tasks) - 1) + 1):
            next(tasks[t], None)


def _hosted(compute, n_in, n_out, n_scratch, ride, grid):
    n_ride = ride.n

    def body(*refs):
        ins, ride_ins = refs[:n_in], refs[n_in : n_in + n_ride]
        pos = n_in + n_ride
        outs, ride_outs = refs[pos : pos + n_out], refs[pos + n_out : pos + n_out + n_ride]
        pos += n_out + n_ride
        scratch, sems = refs[pos : pos + n_scratch], refs[pos + n_scratch :]
        _ride_begin(ride, grid, ride_ins, ride_outs, sems)
        compute(*ins, *outs, *scratch)
        _ride_end(ride, grid, ride_ins, ride_outs, sems)

    return body


def _hgrn_fwd(proj, lb, nw, ride):
    s = proj.shape[1]
    t = min(TS, s)
    nt, nch = s // t, t // CHUNK
    h_, k_, hb = HGRN_HEADS, HEAD_DIM, HEADS_PER_STEP

    def compute(q_ref, f_ref, i_ref, g_ref, lb_ref, nw_ref, mix_ref, o_ref, st_ref, state):
        @pl.when(pl.program_id(1) == 0)
        def _():
            state[...] = jnp.zeros_like(state)

        nwv = nw_ref[...]
        rr = lax.broadcasted_iota(jnp.int32, (CHUNK, CHUNK), 0)
        cc = lax.broadcasted_iota(jnp.int32, (CHUNK, CHUNK), 1)
        causal = rr >= cc
        tri = jnp.where(causal, 1.0, 0.0).astype(BF16)

        st = [state[hh] for hh in range(hb)]

        def task(c, hh):
            rows, lanes = slice(c * CHUNK, (c + 1) * CHUNK), slice(hh * k_, (hh + 1) * k_)
            qp, fp, gp = q_ref[rows, lanes].astype(F32), f_ref[rows, lanes].astype(F32), g_ref[rows, lanes].astype(F32)
            vb = i_ref[rows, lanes].astype(BF16)
            _, _, k, _, qf, logf = _chunk_gates(qp, fp, lb_ref[:, lanes])
            yield
            b = _tri_dot(tri, logf)
            yield
            e_b, e_qm, e_km, e_kl, e_l = _chunk_decays(b)
            qm, km, qe, kl = (qf * e_qm).astype(BF16), (k * e_km).astype(BF16), (qf * e_b).astype(BF16), (k * e_kl).astype(BF16)
            yield
            a = jnp.where(causal, _dot(qm, km, "nt"), 0.0).astype(BF16)
            update = _dot(vb, kl, "tn")
            yield
            st_ref[c, hh] = st[hh]
            o = _dot(qe, st[hh], "nt") + _dot(a, vb, "nn")
            st[hh] = st[hh] * e_l + update
            yield
            o_ref[rows, lanes] = o
            on = o * lax.rsqrt(jnp.mean(o * o, axis=-1, keepdims=True) + EPS) * nwv
            mix_ref[rows, lanes] = (on * (gp * _sigmoid(gp))).astype(BF16)

        _emit_pipelined((task(c, hh) for c in range(nch) for hh in range(hb)), 6)
        for hh in range(hb):
            state[hh] = st[hh]

    def col(g):
        return pl.BlockSpec((None, t, hb * k_), lambda h, i: (g, i, h))

    grid = (h_ // hb, nt)
    return pl.pallas_call(
        _hosted(compute, 6, 3, 1, ride, grid),
        name="hgrn_fwd",
        grid=grid,
        in_specs=[
            col(0),
            col(1),
            col(2),
            col(3),
            pl.BlockSpec((1, hb * k_), lambda h, i: (0, h)),
            pl.BlockSpec((1, k_), lambda h, i: (0, 0)),
        ]
        + [HBM_SPEC] * ride.n,
        out_specs=[
            pl.BlockSpec((t, hb * k_), lambda h, i: (i, h)),
            pl.BlockSpec((t, hb * k_), lambda h, i: (i, h)),
            pl.BlockSpec((nch, hb, k_, k_), lambda h, i: (i, h, 0, 0)),
        ]
        + [HBM_SPEC] * ride.n,
        out_shape=[
            jax.ShapeDtypeStruct((s, 2 * h_ * k_), BF16),
            jax.ShapeDtypeStruct((s, h_ * k_), F32),
            jax.ShapeDtypeStruct((s // CHUNK, h_, k_, k_), F32),
        ]
        + ride.out_shape,
        scratch_shapes=[pltpu.VMEM((hb, k_, k_), F32)] + ride.scratch,
        compiler_params=_params(2),
    )(proj, proj, proj, proj, lb, nw, *ride.inputs)


def _hgrn_bwd(proj, lb, nw, o_pre, states, dmix, ride):
    s = proj.shape[1]
    t = min(TS, s)
    nt, nch = s // t, t // CHUNK
    h_, k_, hb = HGRN_HEADS, HEAD_DIM, HEADS_PER_STEP

    def compute(q_ref, f_ref, i_ref, g_ref, lb_ref, nw_ref, o_ref, st_ref, dm_ref, dp_ref, dlb_ref, dnw_ref, dstate, acc_lb, acc_nw):
        h, i = pl.program_id(0), pl.program_id(1)
        dq_ref, df_ref, di_ref, dg_ref = dp_ref.at[0], dp_ref.at[1], dp_ref.at[2], dp_ref.at[3]

        @pl.when(i == 0)
        def _():
            dstate[...] = jnp.zeros_like(dstate)
            acc_lb[...] = jnp.zeros_like(acc_lb)

        @pl.when((i == 0) & (h == 0))
        def _():
            acc_nw[...] = jnp.zeros_like(acc_nw)

        nwv = nw_ref[...]
        rr = lax.broadcasted_iota(jnp.int32, (CHUNK, CHUNK), 0)
        cc = lax.broadcasted_iota(jnp.int32, (CHUNK, CHUNK), 1)
        causal = rr >= cc
        tri = jnp.where(causal, 1.0, 0.0).astype(BF16)
        tri_t =jnp.where(cc >= rr, 1.0, 0.0).astype(BF16)
        last_row = lax.broadcasted_iota(jnp.int32, (CHUNK, k_), 0) == CHUNK - 1

        dst = [dstate[hh] for hh in range(hb)]
        sum_nw = [jnp.zeros((8, k_), F32)]
        sum_lb = [jnp.zeros((8, k_), F32) for _ in range(hb)]

        def task(c, hh):
            rows, lanes = slice(c * CHUNK, (c + 1) * CHUNK), slice(hh * k_, (hh + 1) * k_)
            lbv = lb_ref[:, lanes]
            qp, fp, gp = q_ref[rows, lanes].astype(F32), f_ref[rows, lanes].astype(F32), g_ref[rows, lanes].astype(F32)
            vb = i_ref[rows, lanes].astype(BF16)
            sf, f, k, sq, qf, logf = _chunk_gates(qp, fp, lbv)
            yield
            b = _tri_dot(tri, logf)
            dm = dm_ref[rows, lanes]
            o = o_ref[rows, lanes]
            rinv = lax.rsqrt(jnp.mean(o * o, axis=-1, keepdims=True) + EPS)
            ohat = o * rinv
            sg = _sigmoid(gp)
            dg_ref[rows, lanes] = (dm * (ohat * nwv) * (sg * (1.0 + gp * (1.0 - sg)))).astype(BF16)
            don = dm * (gp * sg)
            sum_nw[0] = sum_nw[0] + _rows_to_8(don * ohat)
            dohat = don * nwv
            do = rinv * (dohat - ohat * jnp.mean(dohat * ohat, axis=-1, keepdims=True))
            yield
            e_b, e_qm, e_km, e_kl, e_l = _chunk_decays(b)
            st0 = st_ref[c, hh]
            do2 = _pair(do)
            qm2, km2 = _pair(qf * e_qm), _pair(k * e_km)
            st02, qe2 = _pair(st0), _pair(qf * e_b)
            yield
            a = jnp.where(causal, _dot(qm2[0], km2[0], "nt"), 0.0)
            da2 = _pair(jnp.where(causal, _dot_pairs(do2, (vb, None), "nt"), 0.0))
            dq_state = _dot_pairs(do2, st02, "nn")
            update = _dot_pairs(do2, qe2, "tn")
            yield
            dst2 = _pair(dst[hh])
            dq = e_b * dq_state + e_qm * _dot_pairs(da2, km2, "nn")
            dk_state = e_kl * _dot_pairs((vb, None), dst2, "nn")
            dk = e_km * _dot_pairs(da2, qm2, "tn") + dk_state
            dv = _dot(a, do2[0], "tn") + _dot(k * e_kl, dst2[0], "nt")
            d_bl = jnp.sum(k * dk_state, axis=0, keepdims=True) + e_l * jnp.sum(st0 * dst[hh], axis=0, keepdims=True)
            dst[hh] = dst[hh] * e_l + update
            yield
            dlogf = _tri_dot(tri_t, qf * dq - k * dk + jnp.where(last_row, d_bl, 0.0))
            yield
            dfv = dlogf / f - dk
            df_ref[rows, lanes] = (dfv * (1.0 - lbv) * (sf * (1.0 - sf))).astype(BF16)
            sum_lb[hh] = sum_lb[hh] + _rows_to_8(dfv * (1.0 - sf))
            dq_ref[rows, lanes] = (dq * (sq * (1.0 + qp * (1.0 - sq)))).astype(BF16)
            di_ref[rows, lanes] = dv.astype(BF16)

        _emit_pipelined((task(c, hh) for c in reversed(range(nch)) for hh in range(hb)), 7)
        acc_nw[...] += sum_nw[0]
        for hh in range(hb):
            dstate[hh] = dst[hh]
            acc_lb[:, hh * k_ : (hh + 1) * k_] += sum_lb[hh]

        @pl.when(i == nt - 1)
        def _():
            dlb_ref[...] = jnp.sum(acc_lb[...], axis=0, keepdims=True)

        @pl.when((i == nt - 1) & (h == h_ // hb - 1))
        def _():
            dnw_ref[...] = jnp.sum(acc_nw[...], axis=0, keepdims=True)

    def col(g):
        return pl.BlockSpec((None, t, hb * k_), lambda h, i: (g, nt - 1 - i, h))

    row = pl.BlockSpec((t, hb * k_), lambda h, i: (nt - 1 - i, h))
    grid = (h_ // hb, nt)
    return pl.pallas_call(
        _hosted(compute, 9, 3, 3, ride, grid),
        name="hgrn_bwd",
        grid=grid,
        in_specs=[
            col(0),
            col(1),
            col(2),
            col(3),
            pl.BlockSpec((1, hb * k_), lambda h, i: (0, h)),
            pl.BlockSpec((1, k_), lambda h, i: (0, 0)),
            row,
            pl.BlockSpec((nch, hb, k_, k_), lambda h, i: (nt - 1 - i, h, 0, 0)),
            row,
        ]
        + [HBM_SPEC] * ride.n,
        out_specs=[
            pl.BlockSpec((4, t, hb * k_), lambda h, i: (0, nt - 1 - i, h)),
            pl.BlockSpec((1, hb * k_), lambda h, i: (0, h)),
            pl.BlockSpec((1, k_), lambda h, i: (0, 0)),
        ]
        + [HBM_SPEC] * ride.n,
        out_shape=[
            jax.ShapeDtypeStruct((8, s, h_ * k_), BF16),
            jax.ShapeDtypeStruct((1, h_ * k_), F32),
            jax.ShapeDtypeStruct((1, k_), F32),
        ]
        + ride.out_shape,
        scratch_shapes=[pltpu.VMEM((hb, k_, k_), F32), pltpu.VMEM((8, hb * k_), F32), pltpu.VMEM((8, k_), F32)] + ride.scratch,
        compiler_params=_params(2),
    )(proj, proj, proj, proj, lb, nw, o_pre, states, dmix, *ride.inputs)


def _shift_rows(ext, k):
    n = ext.shape[0]
    return pltpu.roll(ext, k % n, axis=0)


def _row_select(parts):
    w = parts[0].shape[1]
    row = lax.broadcasted_iota(jnp.int32, (8, w), 0)
    out = jnp.zeros((8, w), F32)
    for r, p in enumerate(parts):
        out = out + jnp.where(row == r, p, 0.0)
    return out


def _sconv_fwd(proj, sw, mix):
    s = proj.shape[1]
    t = min(TS, s)
    nt, w = s // t, 512
    ncol, hb = GROUP_W // w, t // HALO

    def body(cb_ref, cc_ref, ch_ref, ccp_ref, chp_ref, sw_ref, mix_in, mix_ref):
        del mix_in
        i = pl.program_id(0)
        u = cc_ref[...].astype(F32) * ch_ref[...].astype(F32)
        up = jnp.where(i > 0, ccp_ref[...].astype(F32) * chp_ref[...].astype(F32), 0.0)
        ext = jnp.concatenate([up, u], axis=0)
        z = sw_ref[0:1, :] * _shift_rows(ext, 2)[HALO:] + sw_ref[1:2, :] * _shift_rows(ext, 1)[HALO:] + sw_ref[2:3, :] * u
        mix_ref[...] = (cb_ref[...].astype(F32) * z).astype(BF16)

    def cur(g):
        return pl.BlockSpec((None, t, w), lambda i, c: (g, i, c))

    def prev(g):
        return pl.BlockSpec((None, HALO, w), lambda i, c: (g, jnp.maximum(i * hb - 1, 0), c))

    return pl.pallas_call(
        body,
        name="sconv_fwd",
        grid=(nt, ncol),
        in_specs=[cur(4), cur(5), cur(6), prev(5), prev(6), pl.BlockSpec((3, w), lambda i, c: (0, c)), pl.BlockSpec(memory_space=pl.ANY)],
        out_specs=pl.BlockSpec((t, w), lambda i, c: (i, ncol + c)),
        out_shape=jax.ShapeDtypeStruct(mix.shape, mix.dtype),
        input_output_aliases={6: 0},
        compiler_params=_params(2),
    )(proj, proj, proj, proj, proj, sw, mix)


def _sconv_bwd(proj, sw, dmix, dproj):
    s = proj.shape[1]
    t = min(TS, s)
    nt, w = s // t, 512
    ncol, hb = GROUP_W // w, t // HALO

    def body(cb_ref, cbn_ref, cc_ref, ccp_ref, ch_ref, chp_ref, dy_ref, dyn_ref, sw_ref, dp_in, dp_ref, dsw_ref, acc):
        del dp_in
        i = pl.program_id(1)
        w0, w1, w2 = sw_ref[0:1, :], sw_ref[1:2, :], sw_ref[2:3, :]
        ccv, chv, cbv, dyv = cc_ref[...].astype(F32), ch_ref[...].astype(F32), cb_ref[...].astype(F32), dy_ref[...]
        u = ccv * chv
        up = jnp.where(i > 0, ccp_ref[...].astype(F32) * chp_ref[...].astype(F32), 0.0)
        ext = jnp.concatenate([up, u], axis=0)
        u1, u2 = _shift_rows(ext, 1)[HALO:], _shift_rows(ext, 2)[HALO:]
        z = w0 * u2 + w1 * u1 + w2 * u
        dz = dyv * cbv
        dzn = jnp.where(i < nt - 1, dyn_ref[...] * cbn_ref[...].astype(F32), 0.0)
        dze = jnp.concatenate([dz, dzn], axis=0)
        du = w2 * dz + w1 * _shift_rows(dze, -1)[:t] + w0 * _shift_rows(dze, -2)[:t]
        dp_ref[0] = (dyv * z).astype(BF16)
        dp_ref[1] = (du * chv).astype(BF16)
        dp_ref[2] = (du * ccv).astype(BF16)
        dp_ref[3] = jnp.zeros((t, w), BF16)

        @pl.when(i == 0)
        def _():
            acc[...] = jnp.zeros_like(acc)

        acc[...] += _row_select([jnp.sum(dz * uu, axis=0, keepdims=True) for uu in (u2, u1, u)])

        @pl.when(i == nt - 1)
        def _():
            dsw_ref[...] = acc[...]

    def cur(g):
        return pl.BlockSpec((None, t, w), lambda c, i: (g, i, c))

    def prev(g):
        return pl.BlockSpec((None, HALO, w), lambda c, i: (g, jnp.maximum(i * hb - 1, 0), c))

    def nxt(g):
        return pl.BlockSpec((None, HALO, w), lambda c, i: (g, jnp.minimum((i + 1) * hb, s // HALO - 1), c))

    return pl.pallas_call(
        body,
        name="sconv_bwd",
        grid=(ncol, nt),
        in_specs=[
            cur(4),
            nxt(4),
            cur(5),
            prev(5),
            cur(6),
            prev(6),
            pl.BlockSpec((t, w), lambda c, i: (i, ncol + c)),
            pl.BlockSpec((HALO, w), lambda c, i: (jnp.minimum((i + 1) * hb, s // HALO - 1), ncol + c)),
            pl.BlockSpec((3, w), lambda c, i: (0, c)),
            pl.BlockSpec(memory_space=pl.ANY),
        ],
        out_specs=[pl.BlockSpec((4, t, w), lambda c, i: (1, i, c)), pl.BlockSpec((8, w), lambda c, i: (0, c))],
        out_shape=[jax.ShapeDtypeStruct(dproj.shape, dproj.dtype), jax.ShapeDtypeStruct((8, GROUP_W), F32)],
        scratch_shapes=[pltpu.VMEM((8, w), F32)],
        input_output_aliases={9: 0},
        compiler_params=_params(2),
    )(proj, proj, proj, proj, proj, proj, dmix, dmix, sw, dproj)


def _ffn_down(a_pre, up, cw, cb, wd_g, x2):
    nd, s, f = a_pre.shape
    d = wd_g.shape[2]
    t = min(TS, s)
    nt, hb = s // t, t // HALO

    def body(a_ref, ap_ref, up_ref, cw_ref, cb_ref, w_ref, res_ref, o_ref, act_ref):
        i, j = pl.program_id(0), pl.program_id(1)
        av = a_ref[...].astype(F32)
        ext = jnp.concatenate([jnp.where(i > 0, ap_ref[...].astype(F32), 0.0), av], axis=0)
        a = cw_ref[0:1, :] * _shift_rows(ext, 2)[HALO:] + cw_ref[1:2, :] * _shift_rows(ext, 1)[HALO:] + cw_ref[2:3, :] * av
        a = a + cb_ref[...]
        act = (a * _sigmoid(a) * up_ref[...].astype(F32)).astype(BF16)
        act_ref[...] = act

        @pl.when(j == 0)
        def _():
            o_ref[...] = res_ref[...]

        o_ref[...] += _dot(act, w_ref[...], "nn")

    cur = pl.BlockSpec((None, t, f), lambda i, j: (j, i, 0))
    prev = pl.BlockSpec((None, HALO, f), lambda i, j: (j, jnp.maximum(i * hb - 1, 0), 0))
    row = pl.BlockSpec((t, d), lambda i, j: (i, 0))
    return pl.pallas_call(
        body,
        name="ffn_down",
        grid=(nt, nd),
        in_specs=[
            cur,
            prev,
            cur,
            pl.BlockSpec((None, 3, f), lambda i, j: (j, 0, 0)),
            pl.BlockSpec((None, 1, f), lambda i, j: (j, 0, 0)),
            pl.BlockSpec((None, f, d), lambda i, j: (j, 0, 0)),
            row,
        ],
        out_specs=[row, cur],
        out_shape=[jax.ShapeDtypeStruct((s, d), F32), jax.ShapeDtypeStruct(a_pre.shape, BF16)],
        compiler_params=_params(2),
    )(a_pre, a_pre, up, cw, cb, wd_g, x2)


def _ffn_act_bwd(a_pre, up, dact, cw, cb, ride):
    nd, s, f = a_pre.shape
    t = min(TS, s)
    nt, hb = s // t, t // HALO

    def compute(a_ref, ap_ref, an_ref, up_ref, upn_ref, da_ref, dan_ref, cw_ref, cb_ref, dap_ref, dup_ref, dsm_ref):
        i, j = pl.program_id(0), pl.program_id(1)
        w0, w1, w2 = cw_ref[0:1, :], cw_ref[1:2, :], cw_ref[2:3, :]
        av = a_ref[...].astype(F32)
        ext = jnp.concatenate([jnp.where(i > 0, ap_ref[...].astype(F32), 0.0), av, an_ref[...].astype(F32)], axis=0)
        e1, e2 = _shift_rows(ext, 1), _shift_rows(ext, 2)
        a = (w0 * e2 + w1 * e1 + w2 * ext)[HALO:] + cb_ref[...]
        sg = _sigmoid(a)
        dact_e = jnp.concatenate([da_ref[...], dan_ref[...]], axis=0).astype(F32)
        up_e = jnp.concatenate([up_ref[...], upn_ref[...]], axis=0).astype(F32)
        row = lax.broadcasted_iota(jnp.int32, a.shape, 0)
        live = (row < t) | (i < nt - 1)
        d_a = jnp.where(live, dact_e * up_e * (sg * (1.0 + a * (1.0 - sg))), 0.0)
        dup_ref[...] = (dact_e * (a * sg))[:t].astype(BF16)
        dap_ref[...] = (w2 * d_a + w1 * _shift_rows(d_a, -1) + w0 * _shift_rows(d_a, -2))[:t].astype(BF16)

        d_a_cur = d_a[:t]
        sums = [jnp.sum(d_a_cur * e[HALO : HALO + t], axis=0, keepdims=True) for e in (e2, e1, ext)]
        part = _row_select(sums + [jnp.sum(d_a_cur, axis=0, keepdims=True)])

        @pl.when(i == 0)
        def _():
            dsm_ref[j] = part

        @pl.when(i > 0)
        def _():
            dsm_ref[j] += part

    cur = pl.BlockSpec((None, t, f), lambda i, j: (j, i, 0))
    prev = pl.BlockSpec((None, HALO, f), lambda i, j: (j, jnp.maximum(i * hb - 1, 0), 0))
    nxt = pl.BlockSpec((None, HALO, f), lambda i, j: (j, jnp.minimum((i + 1) * hb, s // HALO - 1), 0))
    grid = (nt, nd)
    return pl.pallas_call(
        _hosted(compute, 9, 3, 0, ride, grid),
        name="ffn_act_bwd",
        grid=grid,
        in_specs=[cur, prev, nxt, cur, nxt, cur, nxt, pl.BlockSpec((None, 3, f), lambda i, j: (j, 0, 0)), pl.BlockSpec((None, 1, f), lambda i, j: (j, 0, 0))]
        + [HBM_SPEC] * ride.n,
        out_specs=[cur, cur, pl.BlockSpec((nd, 8, f), lambda i, j: (0, 0, 0))] + [HBM_SPEC] * ride.n,
        out_shape=[jax.ShapeDtypeStruct(a_pre.shape, BF16), jax.ShapeDtypeStruct(a_pre.shape, BF16), jax.ShapeDtypeStruct((nd, 8, f), F32)]
        + ride.out_shape,
        scratch_shapes=ride.scratch,
        compiler_params=_params(2),
    )(a_pre, a_pre, a_pre, up, up, dact, dact, cw, cb, *ride.inputs)


def _softmax_rows(sc):
    m = jnp.max(sc, axis=-1, keepdims=True)
    e = jnp.exp(sc - m)
    return e / jnp.sum(e, axis=-1, keepdims=True)


def _attn_fwd(q, km, vm):
    s, d = q.shape
    m = km.shape[0]
    t = min(TS, s)
    hd = d // MEM_HEADS
    scale = hd**-0.5

    def body(q_ref, k_ref, v_ref, o_ref):
        for h in range(MEM_HEADS):
            cols = slice(h * hd, (h + 1) * hd)
            p = _softmax_rows(_dot(q_ref[:, cols], k_ref[:, cols], "nt") * scale)
            o_ref[:, cols] = _dot(p, v_ref[:, cols], "nn").astype(BF16)

    row = pl.BlockSpec((t, d), lambda i: (i, 0))
    full = pl.BlockSpec((m, d), lambda i: (0, 0))
    return pl.pallas_call(
        body,
        name="attn_fwd",
        grid=(s // t,),
        in_specs=[row, full, full],
        out_specs=row,
        out_shape=jax.ShapeDtypeStruct((s, d), BF16),
        compiler_params=_params(1),
    )(q, km, vm)


def _attn_bwd(q, km, vm, do):
    s, d = q.shape
    m = km.shape[0]
    t = min(TS, s)
    steps = s // t
    hd = d // MEM_HEADS
    scale = hd**-0.5

    def body(q_ref, k_ref, v_ref, do_ref, dq_ref, dk_ref, dv_ref):
        @pl.when(pl.program_id(0) == 0)
        def _():
            dk_ref[...] = jnp.zeros_like(dk_ref)
            dv_ref[...] = jnp.zeros_like(dv_ref)

        for h in range(MEM_HEADS):
            cols = slice(h * hd, (h + 1) * hd)
            qh, kh, vh, doh = q_ref[:, cols], k_ref[:, cols], v_ref[:, cols], do_ref[:, cols]
            p = _softmax_rows(_dot(qh, kh, "nt") * scale)
            dp = _dot(doh, vh, "nt")
            ds = p * (dp - jnp.sum(dp * p, axis=-1, keepdims=True)) * scale
            dq_ref[:, cols] = _dot(ds, kh, "nn").astype(BF16)
            dk_ref[:, cols] += _dot(ds, qh, "tn")
            dv_ref[:, cols] += _dot(p, doh, "tn")

    row = pl.BlockSpec((t, d), lambda i: (i, 0))
    full = pl.BlockSpec((m, d), lambda i: (0, 0))
    return pl.pallas_call(
        body,
        name="attn_bwd",
        grid=(steps,),
        in_specs=[row, full, full, row],
        out_specs=[row, full, full],
        out_shape=[jax.ShapeDtypeStruct((s, d), BF16), jax.ShapeDtypeStruct((m, d), F32), jax.ShapeDtypeStruct((m, d), F32)],
        compiler_params=_params(1),
    )(q, km, vm, do)


def _row_tile(rows, cols):
    best = 8
    for tr in range(8, rows + 1, 8):
        if rows % tr == 0 and tr * cols * 4 <= (1 << 20):
            best = tr
    return best


def _adamw(w, g, m, v):
    m = ADAM_B1 * m + (1.0 - ADAM_B1) * g
    v = ADAM_B2 * v + (1.0 - ADAM_B2) * (g * g)
    m_hat = m / (1.0 - ADAM_B1**ADAM_STEP)
    v_hat = v / (1.0 - ADAM_B2**ADAM_STEP)
    delta = -ADAM_LR * (m_hat / (jnp.sqrt(v_hat) + ADAM_EPS) + ADAM_WD * w)
    return delta, m, v


def _adam_shard(name, parts, w, m, v):
    rows, cols = w.shape
    tr = _row_tile(rows, cols)

    def body(*refs):
        part_refs = refs[:N_DEV]
        w_ref, m_ref, v_ref, go_ref, d_ref, mo_ref, vo_ref = refs[N_DEV:]
        g = part_refs[0][...].astype(F32)
        for dev in range(1, N_DEV):
            g = g + part_refs[dev][...].astype(F32)
        go_ref[...] = g
        d_ref[...], mo_ref[...], vo_ref[...] = _adamw(w_ref[...], g, m_ref[...], v_ref[...])

    def part(dev):
        return pl.BlockSpec((None, tr, cols), lambda i: (dev, i, 0))

    flat = pl.BlockSpec((tr, cols), lambda i: (i, 0))
    return pl.pallas_call(
        body,
        name=name,
        grid=(rows // tr,),
        in_specs=[part(dev) for dev in range(N_DEV)] + [flat, flat, flat],
        out_specs=[flat, flat, flat, flat],
        out_shape=[jax.ShapeDtypeStruct((rows, cols), F32)] * 4,
        compiler_params=_params(1),
    )(*([parts] * N_DEV), w, m, v)


def _sum_devices(gathered):
    def body(g_ref, o_ref):
        tot = g_ref[0]
        for dev in range(1, N_DEV):
            tot = tot + g_ref[dev]
        o_ref[...] = tot

    return pl.pallas_call(body, name="sum_devices", out_shape=jax.ShapeDtypeStruct(gathered.shape[1:], F32))(gathered)


def _adam_packed(w, g, m, v):
    def body(w_ref, g_ref, m_ref, v_ref, d_ref, mo_ref, vo_ref):
        d_ref[...], mo_ref[...], vo_ref[...] = _adamw(w_ref[...], g_ref[...], m_ref[...], v_ref[...])

    return pl.pallas_call(body, name="adam_packed", out_shape=[jax.ShapeDtypeStruct(w.shape, F32)] * 3)(w, g, m, v)


def _padded(size):
    return -(-size // 128) * 128


def _pack(parts):
    flat = []
    for p in parts:
        v = p.reshape(-1)
        flat.append(jnp.pad(v, (0, _padded(v.shape[0]) - v.shape[0])))
    return jnp.concatenate(flat).reshape(-1, 128)


def _unpack(packed, shapes):
    flat = packed.reshape(-1)
    out, pos = [], 0
    for shp in shapes:
        size = math.prod(shp)
        out.append(flat[pos : pos + size].reshape(shp))
        pos += _padded(size)
    return out


WEIGHTS = (
    "hgrn_lb norm1_w w_in hgrn_norm_w sconv_w w_out norm2_w mem_norm_w wq wk wv wo norm3_w w_gate w_up ffn_conv_w "
    "ffn_conv_b w_down final_norm_w"
).split()
BIG = ["w_in", "w_out", "wq", "wk", "wv", "wo", "w_gate", "w_up", "w_down"]
SMALL_SHARDED = ["sconv_w", "ffn_conv_w"]
SMALL = [n for n in WEIGHTS if n not in BIG]


def kernel(x, mem, hgrn_lb, norm1_w, w_in, hgrn_norm_w, sconv_w, w_out, norm2_w, mem_norm_w, wq, wk, wv, wo, norm3_w, w_gate, w_up, ffn_conv_w, ffn_conv_b, w_down, final_norm_w, loss_target, m_hgrn_lb, m_norm1_w, m_w_in, m_hgrn_norm_w, m_sconv_w, m_w_out, m_norm2_w, m_mem_norm_w, m_wq, m_wk, m_wv, m_wo, m_norm3_w, m_w_gate, m_w_up, m_ffn_conv_w, m_ffn_conv_b, m_w_down, m_final_norm_w, v_hgrn_lb, v_norm1_w, v_w_in, v_hgrn_norm_w, v_sconv_w, v_w_out, v_norm2_w, v_mem_norm_w, v_wq, v_wk, v_wv, v_wo, v_norm3_w, v_w_gate, v_w_up, v_ffn_conv_w, v_ffn_conv_b, v_w_down, v_final_norm_w):
    given = dict(locals())
    w_of = {n: given[n] for n in WEIGHTS}
    m_of = {n: given["m_" + n] for n in WEIGHTS}
    v_of = {n: given["v_" + n] for n in WEIGHTS}
    x2, mem2, target = x[0], mem[0], loss_target[0]
    d = x2.shape[1]
    xi, yi, ci = lax.axis_index("x"), lax.axis_index("y"), lax.axis_index("c")
    me = 4 * xi + 2 * yi + ci
    final_w = final_norm_w.reshape(1, d)

    shard = {n: w_of[n][0].astype(BF16) for n in BIG}
    small_shard = jnp.concatenate([sconv_w[0], ffn_conv_w[0]], axis=1)
    h1, w_in_g, small_g = _rmsnorm("norm1", x2, norm1_w, _GatherRide([shard["w_in"], small_shard]))
    w_in_f = jnp.transpose(w_in_g, (1, 0, 2)).reshape(d, -1)
    conv_ch = sconv_w.shape[2]
    sw_f = jnp.transpose(small_g[:, :, :conv_ch], (1, 0, 2)).reshape(3, -1)
    cw_g = small_g[:, :, conv_ch:]
    cb_g = ffn_conv_b.reshape(N_DEV, 1, -1)

    lb = _lb_fwd(hgrn_lb)
    proj, *attn_w = _mm_proj(h1, w_in_f, _GatherRide([shard[n] for n in ("w_out", "wq", "wk", "wv", "wo")]))
    w_out_f, wq_f, wk_f, wv_f, wo_f = (a.reshape(d, d) for a in attn_w)
    mix, o_pre, states, wg_g, wu_g = _hgrn_fwd(proj, lb, hgrn_norm_w, _GatherRide([shard["w_gate"], shard["w_up"]]))
    mix = _sconv_fwd(proj, sw_f, mix)
    x1 = _mm_nn("mm_out", mix, w_out_f, F32, residual=x2)
    h2 = _rmsnorm("norm2", x1, norm2_w)
    mem_n = _rmsnorm("norm_mem", mem2, mem_norm_w)
    q = _mm_nn("mm_q", h2, wq_f, BF16)
    km = _mm_nn("mm_k", mem_n, wk_f, BF16)
    vm = _mm_nn("mm_v", mem_n, wv_f, BF16)
    ao = _attn_fwd(q, km, vm)
    x2r = _mm_nn("mm_o", ao, wo_f, F32, residual=x1)
    h3 = _rmsnorm("norm3", x2r, norm3_w)
    a_pre, up, wd_g = _mm_gate_up(h3, wg_g, wu_g, _GatherRide([shard["w_down"]]))
    x3, act = _ffn_down(a_pre, up, cw_g, cb_g, wd_g, x2r)

    dx3, dx3b, loss_blk, d_final = _loss_head(x3, target, final_w)
    parts = {}
    dact = _mm_dact(dx3b, wd_g)
    dwd = _mm_dwdown(act, dx3b)
    dapre, dup, d_ffn_small, parts["w_down"] = _ffn_act_bwd(a_pre, up, dact, cw_g, cb_g, _ScatterRide([dwd]))
    dwg, dwu = _mm_dw_cols("mm_dw_gate_up", h3, [dapre, dup])
    dh3, parts["w_gate"], parts["w_up"] = _mm_dh3(dapre, dup, wg_g, wu_g, _ScatterRide([dwg, dwu]))
    dx2, dx2b, d_n3 = _rmsnorm_bwd("norm3_bwd", dh3, x2r, norm3_w, dx3)
    dao = _mm_nt("mm_dao", [(dx2b, wo_f)], BF16)
    dwo = _mm_tn("mm_dwo", ao, dx2b)
    dq, dkm, dvm = _attn_bwd(q, km, vm, dao)
    dwq = _mm_tn("mm_dwq", h2, dq)
    dh2 = _mm_nt("mm_dh2", [(dq, wq_f)], F32)
    dwk = _mm_tn("mm_dwk", mem_n, dkm)
    dwv = _mm_tn("mm_dwv", mem_n, dvm)
    dmem_n = _mm_nt("mm_dmem", [(dkm, wk_f), (dvm, wv_f)], F32)
    _, _, d_nm = _rmsnorm_bwd("norm_mem_bwd", dmem_n, mem2, mem_norm_w)
    dx1, dx1b, d_n2 = _rmsnorm_bwd("norm2_bwd", dh2, x1, norm2_w, dx2)
    dmix = _mm_nt("mm_dmix", [(dx1b, w_out_f)], F32)
    dwout = _mm_tn("mm_dwout", mix, dx1b)
    dwout_g, dwq_g, dwk_g, dwv_g, dwo_g = (a.reshape(N_DEV, d // N_DEV, d) for a in (dwout, dwq, dwk, dwv, dwo))
    dproj, d_lb, d_nw, parts["w_out"], parts["wq"], parts["wk"] = _hgrn_bwd(
        proj, lb, hgrn_norm_w, o_pre, states, dmix, _ScatterRide([dwout_g, dwq_g, dwk_g])
    )
    dproj, d_sw = _sconv_bwd(proj, sw_f, dmix, dproj)
    dwin, parts["wv"], parts["wo"] = _mm_dwin(h1, dproj, N_GROUPS, _ScatterRide([dwv_g, dwo_g]))
    dwin_g = jnp.transpose(dwin.reshape(d, N_DEV, -1), (1, 0, 2))
    dh1, parts["w_in"] = _mm_dh1(dproj, w_in_f, N_GROUPS, _ScatterRide([dwin_g]))
    grad_x, _, d_n1 = _rmsnorm_bwd("norm1_bwd", dh1, x2, norm1_w, dx1)
    d_hlb = _lb_bwd(hgrn_lb, d_lb)

    small_parts = [d_hlb, d_n1, d_nw, d_sw[0:3], d_n2, d_nm, d_n3, d_ffn_small[:, 0:3, :], d_ffn_small[:, 3, :], d_final, loss_blk]
    small_shapes = [p.shape for p in small_parts]
    total = _sum_devices(_allgather("gather_small", [_pack(small_parts)])[0])
    g_hlb, g_n1, g_nw, g_sw, g_n2, g_nm, g_n3, g_cw, g_cb, g_final, loss_row = _unpack(total, small_shapes)
    grads = {
        "hgrn_lb": g_hlb,
        "norm1_w": g_n1,
        "hgrn_norm_w": g_nw,
        "sconv_w": lax.dynamic_slice_in_dim(g_sw, me * conv_ch, conv_ch, axis=1).reshape(sconv_w.shape),
        "norm2_w": g_n2,
        "mem_norm_w": g_nm,
        "norm3_w": g_n3,
        "ffn_conv_w": lax.dynamic_index_in_dim(g_cw, me, axis=0, keepdims=True),
        "ffn_conv_b": g_cb.reshape(ffn_conv_b.shape),
        "final_norm_w": g_final.reshape(final_norm_w.shape),
    }
    shapes = [w_of[n].shape for n in SMALL]
    deltas_p, new_m_p, new_v_p = _adam_packed(
        _pack([w_of[n] for n in SMALL]), _pack([grads[n] for n in SMALL]), _pack([m_of[n] for n in SMALL]), _pack([v_of[n] for n in SMALL])
    )
    deltas = dict(zip(SMALL, _unpack(deltas_p, shapes)))
    new_m = dict(zip(SMALL, _unpack(new_m_p, shapes)))
    new_v = dict(zip(SMALL, _unpack(new_v_p, shapes)))

    for n in BIG:
        shp = w_of[n].shape
        gw, dw, mw, vw = _adam_shard("adam_" + n, parts[n], w_of[n][0], m_of[n][0], v_of[n][0])
        grads[n], deltas[n], new_m[n], new_v[n] = (a.reshape(shp) for a in (gw, dw, mw, vw))

    loss = loss_row[0, 0]
    return (
        loss,
        grad_x.reshape(x.shape),
        *[grads[n] for n in WEIGHTS],
        *[deltas[n] for n in WEIGHTS],
        *[new_m[n] for n in WEIGHTS],
        *[new_v[n] for n in WEIGHTS],
    )
```

```python
import functools
import math

import jax
import jax.numpy as jnp
from jax import lax
from jax.experimental import pallas as pl
from jax.experimental.pallas import tpu as pltpu

F32 = jnp.float32
BF16 = jnp.bfloat16
MESH = pl.DeviceIdType.MESH
N_DEV = 8
EPS = 1e-6
CHUNK = 64
HGRN_HEADS = 8
HEAD_DIM = 128
HEADS_PER_STEP = 2
MEM_HEADS = 4
GROUP_W = HGRN_HEADS * HEAD_DIM
N_GROUPS = 7
HALO = 16
TS = 512
TR = 256
TM = 1024
TN = 1024
TK = 1024
VMEM_LIMIT = 48 * 1024 * 1024

ADAM_LR = 0.001
ADAM_B1 = 0.9
ADAM_B2 = 0.999
ADAM_EPS = 1e-08
ADAM_WD = 0.01
ADAM_STEP = 10

_DIMS = {
    "nn": (((1,), (0,)), ((), ())),
    "nt": (((1,), (1,)), ((), ())),
    "tn": (((0,), (0,)), ((), ())),
}


def _params(n_axes):
    return pltpu.CompilerParams(dimension_semantics=("arbitrary",) * n_axes, vmem_limit_bytes=VMEM_LIMIT)


def _dot(a, b, mode):
    return lax.dot_general(a.astype(BF16), b.astype(BF16), _DIMS[mode], preferred_element_type=F32)


def _sigmoid(v):
    return 0.5 * jnp.tanh(0.5 * v) + 0.5


def _block_dims(spec):
    return tuple(d for d in spec.block_shape if d is not None)


HBM_SPEC = pl.BlockSpec(memory_space=pltpu.HBM)


def _place():
    x, y, c = lax.axis_index("x"), lax.axis_index("y"), lax.axis_index("c")
    return x, y, c, [(1 - x, y), (x, 1 - y), (1 - x, 1 - y)]


def _slab(p):
    return 4 * p[0] + 2 * p[1] + p[2]


def _remote(src, dst, send_sem, recv_sem, to):
    return pltpu.make_async_remote_copy(src_ref=src, dst_ref=dst, send_sem=send_sem, recv_sem=recv_sem, device_id=to, device_id_type=MESH)


class _GatherRide:
    def __init__(self, shards):
        self.inputs = list(shards)
        self.n = len(shards)
        self.out_shape = [jax.ShapeDtypeStruct((N_DEV, *a.shape), a.dtype) for a in shards]
        self.scratch = [pltpu.SemaphoreType.DMA((self.n, 7)), pltpu.SemaphoreType.DMA((self.n, 7)), pltpu.SemaphoreType.DMA((self.n,))]

    @staticmethod
    def _copy(outs, sems, a, k, block, to, src=None):
        dst = outs[a].at[_slab(block)]
        return _remote(dst if src is None else src, dst, sems[0].at[a, k], sems[1].at[a, k], to)

    def _mine(self, ins, outs, sems, a):
        x, y, c, chips = _place()
        local = pltpu.make_async_copy(ins[a], outs[a].at[_slab((x, y, c))], sems[2].at[a])
        first = [self._copy(outs, sems, a, 0, (x, y, c), (x, y, 1 - c), src=ins[a])]
        first += [self._copy(outs, sems, a, 1 + j, (x, y, c), (*chip, c), src=ins[a]) for j, chip in enumerate(chips)]
        return local, first

    def start(self, ins, outs, sems):
        for a in range(self.n):
            local, first = self._mine(ins, outs, sems, a)
            local.start()
            for cp in first:
                cp.start()

    def forward(self, ins, outs, sems):
        x, y, c, chips = _place()
        for j, chip in enumerate(chips):
            for a in range(self.n):
                self._copy(outs, sems, a, 1 + j, (*chip, c), (x, y, c)).wait_recv()
                self._copy(outs, sems, a, 4 + j, (*chip, c), (x, y, 1 - c)).start()

    def finish(self, ins, outs, sems):
        x, y, c, chips = _place()
        for a in range(self.n):
            self._copy(outs, sems, a, 0, (x, y, 1 - c), (x, y, c)).wait_recv()
            for j, chip in enumerate(chips):
                self._copy(outs, sems, a, 4 + j, (*chip, 1 - c), (x, y, c)).wait_recv()
        for a in range(self.n):
            local, first = self._mine(ins, outs, sems, a)
            for cp in first:
                cp.wait_send()
            for j, chip in enumerate(chips):
                self._copy(outs, sems, a, 4 + j, (*chip, c), (x, y, 1 - c)).wait_send()
            local.wait()


class _ScatterRide:
    def __init__(self, grads):
        self.inputs = list(grads)
        self.n = len(grads)
        self.out_shape = [jax.ShapeDtypeStruct(g.shape, g.dtype) for g in grads]
        self.scratch = [pltpu.SemaphoreType.DMA((self.n, 7)), pltpu.SemaphoreType.DMA((self.n, 7)), pltpu.SemaphoreType.DMA((self.n,))]

    @staticmethod
    def _peers():
        x, y, c, chips = _place()
        return _slab((x, y, c)), [(x, y, 1 - c)] + [(*chip, c) for chip in chips] + [(*chip, 1 - c) for chip in chips]

    def _sends(self, ins, outs, sems, a):
        me, peers = self._peers()
        local = pltpu.make_async_copy(ins[a].at[me], outs[a].at[me], sems[2].at[a])
        return [local] + [_remote(ins[a].at[_slab(p)], outs[a].at[me], sems[0].at[a, k], sems[1].at[a, k], p) for k, p in enumerate(peers)]

    def start(self, ins, outs, sems):
        for a in range(self.n):
            for cp in self._sends(ins, outs, sems, a):
                cp.start()

    def forward(self, ins, outs, sems):
        pass

    def finish(self, ins, outs, sems):
        me, peers = self._peers()
        for a in range(self.n):
            for k, p in enumerate(peers):
                _remote(ins[a].at[me], outs[a].at[_slab(p)], sems[0].at[a, k], sems[1].at[a, k], p).wait_recv()
            local, *sends = self._sends(ins, outs, sems, a)
            for cp in sends:
                cp.wait_send()
            local.wait()


def _grid_step(grid):
    step = 0
    for axis, size in enumerate(grid):
        step = step * size + pl.program_id(axis)
    return step


def _ride_begin(ride, grid, ins, outs, sems):
    step, n_steps = _grid_step(grid), math.prod(grid)

    @pl.when(step == 0)
    def _():
        ride.start(ins, outs, sems)

    @pl.when(step == (3 * n_steps) // 4)
    def _():
        ride.forward(ins, outs, sems)


def _ride_end(ride, grid, ins, outs, sems):
    @pl.when(_grid_step(grid) == math.prod(grid) - 1)
    def _():
        ride.finish(ins, outs, sems)


def _allgather(name, shards):
    ride = _GatherRide(shards)
    n = ride.n

    def body(*refs):
        ins, outs, sems = refs[:n], refs[n : 2 * n], refs[2 * n :]
        ride.start(ins, outs, sems)
        ride.forward(ins, outs, sems)
        ride.finish(ins, outs, sems)

    return pl.pallas_call(
        body,
        name=name,
        in_specs=[HBM_SPEC] * n,
        out_specs=[HBM_SPEC] * n,
        out_shape=ride.out_shape,
        scratch_shapes=ride.scratch,
    )(*shards)


def _matmul(name, grid, k_axis, pairs, outs, mode, residual=None, ride=None):
    n_pairs, n_out, has_res = len(pairs), len(outs), residual is not None
    k_steps = grid[k_axis] if k_axis is not None else 1
    n_acc = n_out if k_steps > 1 else 0
    n_ride = ride.n if ride is not None else 0

    def body(*refs):
        ins = refs[: 2 * n_pairs]
        pos = 2 * n_pairs
        res_ref = refs[pos] if has_res else None
        pos += int(has_res)
        ride_ins = refs[pos : pos + n_ride]
        pos += n_ride
        out_refs = refs[pos : pos + n_out]
        pos += n_out
        ride_outs = refs[pos : pos + n_ride]
        pos += n_ride
        acc_refs = refs[pos : pos + n_acc]
        ride_sems = refs[pos + n_acc :]
        if ride is not None:
            _ride_begin(ride, grid, ride_ins, ride_outs, ride_sems)

        def partial(o):
            tot = None
            for p in outs[o][3]:
                d = _dot(ins[2 * p][...], ins[2 * p + 1][...], mode)
                tot = d if tot is None else tot + d
            return tot

        if k_steps == 1:
            for o in range(n_out):
                val = partial(o)
                if has_res and o == 0:
                    val = val + res_ref[...]
                out_refs[o][...] = val.astype(out_refs[o].dtype)
        else:
            k = pl.program_id(k_axis)

            @pl.when(k == 0)
            def _():
                for o in range(n_out):
                    if has_res and o == 0:
                        acc_refs[o][...] = res_ref[...]
                    else:
                        acc_refs[o][...] = jnp.zeros_like(acc_refs[o])

            for o in range(n_out):
                acc_refs[o][...] += partial(o)

            @pl.when(k == k_steps - 1)
            def _():
                for o in range(n_out):
                    out_refs[o][...] = acc_refs[o][...].astype(out_refs[o].dtype)

        if ride is not None:
            _ride_end(ride, grid, ride_ins, ride_outs, ride_sems)

    args, in_specs = [], []
    for a, a_spec, b, b_spec in pairs:
        args += [a, b]
        in_specs += [a_spec, b_spec]
    if has_res:
        args.append(residual[0])
        in_specs.append(residual[1])
    out_specs = [o[2] for o in outs]
    out_shape = [jax.ShapeDtypeStruct(o[0], o[1]) for o in outs]
    scratch = [pltpu.VMEM(_block_dims(o[2]), F32) for o in outs] if k_steps > 1 else []
    if ride is not None:
        args += ride.inputs
        in_specs += [HBM_SPEC] * n_ride
        out_specs += [HBM_SPEC] * n_ride
        out_shape += ride.out_shape
        scratch += ride.scratch
    return pl.pallas_call(
        body,
        name=name,
        grid=grid,
        in_specs=in_specs,
        out_specs=out_specs,
        out_shape=out_shape,
        scratch_shapes=scratch,
        compiler_params=_params(len(grid)),
    )(*args)


def _out_tiles(m, n, out_dtype):
    if out_dtype == F32 and m >= TM:
        return TM // 2, n
    return min(TM, m), min(TN, n)


def _mm_nn(name, a, w, out_dtype, residual=None):
    m, k = a.shape
    n = w.shape[1]
    tm, tn = _out_tiles(m, n, out_dtype)
    pairs = [(a, pl.BlockSpec((tm, k), lambda i, j: (i, 0)), w, pl.BlockSpec((k, tn), lambda i, j: (0, j)))]
    o_spec = pl.BlockSpec((tm, tn), lambda i, j: (i, j))
    res = (residual, o_spec) if residual is not None else None
    return _matmul(name, (m // tm, n // tn), None, pairs, [((m, n), out_dtype, o_spec, [0])], "nn", res)[0]


def _mm_nt(name, pairs_in, out_dtype):
    m, k = pairs_in[0][0].shape
    n = pairs_in[0][1].shape[0]
    tm, tn = _out_tiles(m, n, out_dtype)
    pairs = [
        (a, pl.BlockSpec((tm, k), lambda i, j: (i, 0)), w, pl.BlockSpec((tn, k), lambda i, j: (j, 0)))
        for a, w in pairs_in
    ]
    o_spec = pl.BlockSpec((tm, tn), lambda i, j: (i, j))
    outs = [((m, n), out_dtype, o_spec, list(range(len(pairs))))]
    return _matmul(name, (m // tm, n // tn), None, pairs, outs, "nt")[0]


def _mm_tn(name, a, b):
    s, i_dim = a.shape
    n = b.shape[1]
    ts, ti, tn = min(TK, s), min(TN, i_dim), min(TN, n)
    pairs = [(a, pl.BlockSpec((ts, ti), lambda i, j, k: (k, i)), b, pl.BlockSpec((ts, tn), lambda i, j, k: (k, j)))]
    outs = [((i_dim, n), BF16, pl.BlockSpec((ti, tn), lambda i, j, k: (i, j)), [0])]
    return _matmul(name, (i_dim // ti, n // tn, s // ts), 2, pairs, outs, "tn")[0]


def _mm_proj(h1, w_in, ride):
    s, d = h1.shape
    g = w_in.shape[1] // GROUP_W
    tm = min(TM, s)
    pairs = [(h1, pl.BlockSpec((tm, d), lambda i, j: (i, 0)), w_in, pl.BlockSpec((d, GROUP_W), lambda i, j: (0, j)))]
    outs = [((g, s, GROUP_W), BF16, pl.BlockSpec((None, tm, GROUP_W), lambda i, j: (j, i, 0)), [0])]
    return _matmul("mm_proj", (s // tm, g), None, pairs, outs, "nn", ride=ride)


def _mm_gate_up(h3, wg_g, wu_g, ride):
    s, d = h3.shape
    f = wg_g.shape[2]
    tm = min(TM, s)
    a_spec = pl.BlockSpec((tm, d), lambda i, j: (i, 0))
    w_spec = pl.BlockSpec((None, d, f), lambda i, j: (j, 0, 0))
    o_spec = pl.BlockSpec((None, tm, f), lambda i, j: (j, i, 0))
    pairs = [(h3, a_spec, wg_g, w_spec), (h3, a_spec, wu_g, w_spec)]
    outs = [((N_DEV, s, f), BF16, o_spec, [0]), ((N_DEV, s, f), BF16, o_spec, [1])]
    return _matmul("mm_gate_up", (s // tm, N_DEV), None, pairs, outs, "nn", ride=ride)


def _mm_dact(dx3b, wd_g):
    s, d = dx3b.shape
    f = wd_g.shape[1]
    tm = min(TM, s)
    pairs = [(dx3b, pl.BlockSpec((tm, d), lambda i, j: (i, 0)), wd_g, pl.BlockSpec((None, f, d), lambda i, j: (j, 0, 0)))]
    outs = [((N_DEV, s, f), BF16, pl.BlockSpec((None, tm, f), lambda i, j: (j, i, 0)), [0])]
    return _matmul("mm_dact", (s // tm, N_DEV), None, pairs, outs, "nt")[0]


def _mm_dwdown(act, dx3b):
    _, s, f = act.shape
    d = dx3b.shape[1]
    ts, tn = min(TK, s), min(TN, d)
    pairs = [
        (
            act,
            pl.BlockSpec((None, ts, f), lambda j, n, k: (j, k, 0)),
            dx3b,
            pl.BlockSpec((ts, tn), lambda j, n, k: (k, n)),
        )
    ]
    outs = [((N_DEV, f, d), BF16, pl.BlockSpec((None, f, tn), lambda j, n, k: (j, 0, n)), [0])]
    return _matmul("mm_dwdown", (N_DEV, d // tn, s // ts), 2, pairs, outs, "tn")[0]


def _mm_dh3(dapre, dup, wg_g, wu_g, ride):
    _, s, f = dapre.shape
    d = wg_g.shape[1]
    tm, tn = min(TM, s), min(TN, d)
    a_spec = pl.BlockSpec((None, tm, f), lambda i, j, k: (k, i, 0))
    w_spec = pl.BlockSpec((None, tn, f), lambda i, j, k: (k, j, 0))
    pairs = [(dapre, a_spec, wg_g, w_spec), (dup, a_spec, wu_g, w_spec)]
    outs = [((s, d), F32, pl.BlockSpec((tm, tn), lambda i, j, k: (i, j)), [0, 1])]
    return _matmul("mm_dh3", (s // tm, d // tn, N_DEV), 2, pairs, outs, "nt", ride=ride)


def _mm_dw_cols(name, h, dcols_list):
    s, d = h.shape
    f = dcols_list[0].shape[2]
    ts, ti = min(TK, s), min(TN, d)
    a_spec = pl.BlockSpec((ts, ti), lambda j, i, k: (k, i))
    b_spec = pl.BlockSpec((None, ts, f), lambda j, i, k: (j, k, 0))
    o_spec = pl.BlockSpec((None, ti, f), lambda j, i, k: (j, i, 0))
    pairs = [(h, a_spec, dc, b_spec) for dc in dcols_list]
    outs = [((N_DEV, d, f), BF16, o_spec, [p]) for p in range(len(pairs))]
    return _matmul(name, (N_DEV, d // ti, s // ts), 2, pairs, outs, "tn")


def _mm_dwin(h1, dproj, groups, ride):
    s, d = h1.shape
    ts, ti = min(TK, s), min(TN, d)
    pairs = [
        (
            h1,
            pl.BlockSpec((ts, ti), lambda j, i, k: (k, i)),
            dproj,
            pl.BlockSpec((None, ts, GROUP_W), lambda j, i, k: (j, k, 0)),
        )
    ]
    outs = [((d, groups * GROUP_W), BF16, pl.BlockSpec((ti, GROUP_W), lambda j, i, k: (i, j)), [0])]
    return _matmul("mm_dwin", (groups, d // ti, s // ts), 2, pairs, outs, "tn", ride=ride)


def _mm_dh1(dproj, w_in, groups, ride):
    s = dproj.shape[1]
    d = w_in.shape[0]
    tm, tn = min(TM, s), min(TN, d)
    pairs = [
        (
            dproj,
            pl.BlockSpec((None, tm, GROUP_W), lambda i, j, k: (k, i, 0)),
            w_in,
            pl.BlockSpec((tn, GROUP_W), lambda i, j, k: (j, k)),
        )
    ]
    outs = [((s, d), F32, pl.BlockSpec((tm, tn), lambda i, j, k: (i, j)), [0])]
    return _matmul("mm_dh1", (s // tm, d // tn, groups), 2, pairs, outs, "nt", ride=ride)


def _rows_to_8(v):
    t, d = v.shape
    return jnp.sum(v.reshape(t // 8, 8, d), axis=0)


def _rmsnorm(name, x, w, ride=None):
    s, d = x.shape
    t = min(TS, s)
    grid = (s // t,)

    def compute(x_ref, w_ref, o_ref):
        xv = x_ref[...]
        r = lax.rsqrt(jnp.mean(xv * xv, axis=-1, keepdims=True) + EPS)
        o_ref[...] = (xv * r * w_ref[...]).astype(o_ref.dtype)

    in_specs = [pl.BlockSpec((t, d), lambda i: (i, 0)), pl.BlockSpec((1, d), lambda i: (0, 0))]
    out_spec = pl.BlockSpec((t, d), lambda i: (i, 0))
    out_shape = jax.ShapeDtypeStruct((s, d), BF16)
    if ride is None:
        return pl.pallas_call(compute, name=name, grid=grid, in_specs=in_specs, out_specs=out_spec, out_shape=out_shape, compiler_params=_params(1))(x, w)
    return pl.pallas_call(
        _hosted(compute, 2, 1, 0, ride, grid),
        name=name,
        grid=grid,
        in_specs=in_specs + [HBM_SPEC] * ride.n,
        out_specs=[out_spec] + [HBM_SPEC] * ride.n,
        out_shape=[out_shape] + ride.out_shape,
        scratch_shapes=ride.scratch,
        compiler_params=_params(1),
    )(x, w, *ride.inputs)


def _rmsnorm_bwd(name, dh, x, w, res=None):
    s, d = x.shape
    t = min(TR, s)
    steps = s // t
    has_res = res is not None

    def body(*refs):
        if has_res:
            dh_ref, x_ref, w_ref, res_ref, dx_ref, dxb_ref, dw_ref, acc = refs
        else:
            dh_ref, x_ref, w_ref, dx_ref, dxb_ref, dw_ref, acc = refs
        i = pl.program_id(0)
        xv = x_ref[...]
        r = lax.rsqrt(jnp.mean(xv * xv, axis=-1, keepdims=True) + EPS)
        xn = xv * r
        dhv = dh_ref[...].astype(F32)
        dxn = dhv * w_ref[...]
        dx = r * (dxn - xn * jnp.mean(dxn * xn, axis=-1, keepdims=True))
        if has_res:
            dx = dx + res_ref[...]
        dx_ref[...] = dx
        dxb_ref[...] = dx.astype(BF16)

        @pl.when(i == 0)
        def _():
            acc[...] = jnp.zeros_like(acc)

        acc[...] += _rows_to_8(dhv * xn)

        @pl.when(i == steps - 1)
        def _():
            dw_ref[...] = jnp.sum(acc[...], axis=0, keepdims=True)

    row = pl.BlockSpec((t, d), lambda i: (i, 0))
    vec = pl.BlockSpec((1, d), lambda i: (0, 0))
    args = [dh, x, w] + ([res] if has_res else [])
    return pl.pallas_call(
        body,
        name=name,
        grid=(steps,),
        in_specs=[row, row, vec] + ([row] if has_res else []),
        out_specs=[row, row, vec],
        out_shape=[
            jax.ShapeDtypeStruct((s, d), F32),
            jax.ShapeDtypeStruct((s, d), BF16),
            jax.ShapeDtypeStruct((1, d), F32),
        ],
        scratch_shapes=[pltpu.VMEM((8, d), F32)],
        compiler_params=_params(1),
    )(*args)


def _loss_head(x3, target, w):
    s, d = x3.shape
    t = min(TR, s)
    steps = s // t

    def body(x_ref, t_ref, w_ref, dx_ref, dxb_ref, loss_ref, dw_ref, acc_l, acc_w):
        i = pl.program_id(0)
        xv = x_ref[...]
        wv = w_ref[...]
        r = lax.rsqrt(jnp.mean(xv * xv, axis=-1, keepdims=True) + EPS)
        xn = xv * r
        err = xn * wv - t_ref[...]
        dy = err * (1.0 / d)
        dxn = dy * wv
        dx = r * (dxn - xn * jnp.mean(dxn * xn, axis=-1, keepdims=True))
        dx_ref[...] = dx
        dxb_ref[...] = dx.astype(BF16)

        @pl.when(i == 0)
        def _():
            acc_l[...] = jnp.zeros_like(acc_l)
            acc_w[...] = jnp.zeros_like(acc_w)

        acc_l[...] += _rows_to_8(err * err)
        acc_w[...] += _rows_to_8(dy * xn)

        @pl.when(i == steps - 1)
        def _():
            tot = jnp.sum(jnp.sum(acc_l[...], axis=0, keepdims=True), axis=1, keepdims=True) * (0.5 / d)
            loss_ref[...] = jnp.broadcast_to(tot, loss_ref.shape)
            dw_ref[...] = jnp.sum(acc_w[...], axis=0, keepdims=True)

    row = pl.BlockSpec((t, d), lambda i: (i, 0))
    vec = pl.BlockSpec((1, d), lambda i: (0, 0))
    return pl.pallas_call(
        body,
        name="loss_head",
        grid=(steps,),
        in_specs=[row, row, vec],
        out_specs=[row, row, pl.BlockSpec((1, 128), lambda i: (0, 0)), vec],
        out_shape=[
            jax.ShapeDtypeStruct((s, d), F32),
            jax.ShapeDtypeStruct((s, d), BF16),
            jax.ShapeDtypeStruct((1, 128), F32),
            jax.ShapeDtypeStruct((1, d), F32),
        ],
        scratch_shapes=[pltpu.VMEM((8, d), F32), pltpu.VMEM((8, d), F32)],
        compiler_params=_params(1),
    )(x3, target, w)


def _lb_fwd(hgrn_lb):
    def body(p_ref, lb_ref):
        p = p_ref[...]
        m = jnp.max(p, axis=0, keepdims=True)
        e = jnp.exp(p - m)
        lb_ref[...] = e[0:1, :] / jnp.sum(e, axis=0, keepdims=True)

    return pl.pallas_call(body, name="lb_fwd", out_shape=jax.ShapeDtypeStruct((1, hgrn_lb.shape[1]), F32))(hgrn_lb)


def _lb_bwd(hgrn_lb, dlb):
    def body(p_ref, d_ref, o_ref):
        p = p_ref[...]
        m = jnp.max(p, axis=0, keepdims=True)
        e = jnp.exp(p - m)
        sm = e / jnp.sum(e, axis=0, keepdims=True)
        p0 = sm[0:1, :]
        row = lax.broadcasted_iota(jnp.int32, sm.shape, 0)
        o_ref[...] = d_ref[...] * p0 * (jnp.where(row == 0, 1.0, 0.0) - sm)

    return pl.pallas_call(body, name="lb_bwd", out_shape=jax.ShapeDtypeStruct(hgrn_lb.shape, F32))(hgrn_lb, dlb)


def _split3(v):
    hi = v.astype(BF16)
    r1 = v - hi.astype(F32)
    mid = r1.astype(BF16)
    lo = (r1 - mid.astype(F32)).astype(BF16)
    return hi, mid, lo


def _pair(v):
    hi = v.astype(BF16)
    return hi, (v - hi.astype(F32)).astype(BF16)


def _dot_pairs(a2, b2, mode):
    dims = _DIMS[mode]
    out = lax.dot_general(a2[0], b2[0], dims, preferred_element_type=F32)
    if a2[1] is not None:
        out = lax.dot_general(a2[1], b2[0], dims, preferred_element_type=F32) + out
    if b2[1] is not None:
        out = lax.dot_general(a2[0], b2[1], dims, preferred_element_type=F32) + out
    return out


def _tri_dot(tri, v):
    hi, mid, lo = _split3(v)
    dims = _DIMS["nn"]
    out = lax.dot_general(tri, lo, dims, preferred_element_type=F32)
    out = out + lax.dot_general(tri, mid, dims, preferred_element_type=F32)
    return out + lax.dot_general(tri, hi, dims, preferred_element_type=F32)


def _chunk_gates(qp, fp, lbv):
    sf = _sigmoid(fp)
    f = lbv + (1.0 - lbv) * sf
    k = 1.0 - f
    sq = _sigmoid(qp)
    qf = qp * sq
    return sf, f, k, sq, qf, jnp.log(f)


def _chunk_decays(b):
    row = lax.broadcasted_iota(jnp.int32, b.shape, 0)
    bm = jnp.sum(jnp.where(row == CHUNK // 2 - 1, b, 0.0), axis=0, keepdims=True)
    bl = jnp.sum(jnp.where(row == CHUNK - 1, b, 0.0), axis=0, keepdims=True)
    return jnp.exp(b), jnp.exp(b - bm), jnp.exp(bm - b), jnp.exp(bl - b), jnp.exp(bl)


def _emit_pipelined(tasks, n_stages):
    tasks = list(tasks)
    for step in range(len(tasks) + n_stages - 1):
        for t in range(max(step - n_stages + 1, 0), min(step, len(tasks) - 1) + 1):
            next(tasks[t], None)


def _hosted(compute, n_in, n_out, n_scratch, ride, grid):
    n_ride = ride.n

    def body(*refs):
        ins, ride_ins = refs[:n_in], refs[n_in : n_in + n_ride]
        pos = n_in + n_ride
        outs, ride_outs = refs[pos : pos + n_out], refs[pos + n_out : pos + n_out + n_ride]
        pos += n_out + n_ride
        scratch, sems = refs[pos : pos + n_scratch], refs[pos + n_scratch :]
        _ride_begin(ride, grid, ride_ins, ride_outs, sems)
        compute(*ins, *outs, *scratch)
        _ride_end(ride, grid, ride_ins, ride_outs, sems)

    return body


def _hgrn_fwd(proj, lb, nw, ride):
    s = proj.shape[1]
    t = min(TS, s)
    nt, nch = s // t, t // CHUNK
    h_, k_, hb = HGRN_HEADS, HEAD_DIM, HEADS_PER_STEP

    def compute(q_ref, f_ref, i_ref, g_ref, lb_ref, nw_ref, mix_ref, o_ref, st_ref, state):
        @pl.when(pl.program_id(1) == 0)
        def _():
            state[...] = jnp.zeros_like(state)

        nwv = nw_ref[...]
        rr = lax.broadcasted_iota(jnp.int32, (CHUNK, CHUNK), 0)
        cc = lax.broadcasted_iota(jnp.int32, (CHUNK, CHUNK), 1)
        causal = rr >= cc
        tri = jnp.where(causal, 1.0, 0.0).astype(BF16)

        st = [state[hh] for hh in range(hb)]

        def task(c, hh):
            rows, lanes = slice(c * CHUNK, (c + 1) * CHUNK), slice(hh * k_, (hh + 1) * k_)
            qp, fp, gp = q_ref[rows, lanes].astype(F32), f_ref[rows, lanes].astype(F32), g_ref[rows, lanes].astype(F32)
            vb = i_ref[rows, lanes].astype(BF16)
            _, _, k, _, qf, logf = _chunk_gates(qp, fp, lb_ref[:, lanes])
            yield
            b = _tri_dot(tri, logf)
            yield
            e_b, e_qm, e_km, e_kl, e_l = _chunk_decays(b)
            qm, km, qe, kl = (qf * e_qm).astype(BF16), (k * e_km).astype(BF16), (qf * e_b).astype(BF16), (k * e_kl).astype(BF16)
            yield
            a = jnp.where(causal, _dot(qm, km, "nt"), 0.0).astype(BF16)
            update = _dot(vb, kl, "tn")
            yield
            st_ref[c, hh] = st[hh]
            o = _dot(qe, st[hh], "nt") + _dot(a, vb, "nn")
            st[hh] = st[hh] * e_l + update
            yield
            o_ref[rows, lanes] = o
            on = o * lax.rsqrt(jnp.mean(o * o, axis=-1, keepdims=True) + EPS) * nwv
            mix_ref[rows, lanes] = (on * (gp * _sigmoid(gp))).astype(BF16)

        _emit_pipelined((task(c, hh) for c in range(nch) for hh in range(hb)), 6)
        for hh in range(hb):
            state[hh] = st[hh]

    def col(g):
        return pl.BlockSpec((None, t, hb * k_), lambda h, i: (g, i, h))

    grid = (h_ // hb, nt)
    return pl.pallas_call(
        _hosted(compute, 6, 3, 1, ride, grid),
        name="hgrn_fwd",
        grid=grid,
        in_specs=[
            col(0),
            col(1),
            col(2),
            col(3),
            pl.BlockSpec((1, hb * k_), lambda h, i: (0, h)),
            pl.BlockSpec((1, k_), lambda h, i: (0, 0)),
        ]
        + [HBM_SPEC] * ride.n,
        out_specs=[
            pl.BlockSpec((t, hb * k_), lambda h, i: (i, h)),
            pl.BlockSpec((t, hb * k_), lambda h, i: (i, h)),
            pl.BlockSpec((nch, hb, k_, k_), lambda h, i: (i, h, 0, 0)),
        ]
        + [HBM_SPEC] * ride.n,
        out_shape=[
            jax.ShapeDtypeStruct((s, 2 * h_ * k_), BF16),
            jax.ShapeDtypeStruct((s, h_ * k_), F32),
            jax.ShapeDtypeStruct((s // CHUNK, h_, k_, k_), F32),
        ]
        + ride.out_shape,
        scratch_shapes=[pltpu.VMEM((hb, k_, k_), F32)] + ride.scratch,
        compiler_params=_params(2),
    )(proj, proj, proj, proj, lb, nw, *ride.inputs)


def _hgrn_bwd(proj, lb, nw, o_pre, states, dmix, ride):
    s = proj.shape[1]
    t = min(TS, s)
    nt, nch = s // t, t // CHUNK
    h_, k_, hb = HGRN_HEADS, HEAD_DIM, HEADS_PER_STEP

    def compute(q_ref, f_ref, i_ref, g_ref, lb_ref, nw_ref, o_ref, st_ref, dm_ref, dp_ref, dlb_ref, dnw_ref, dstate, acc_lb, acc_nw):
        h, i = pl.program_id(0), pl.program_id(1)
        dq_ref, df_ref, di_ref, dg_ref = dp_ref.at[0], dp_ref.at[1], dp_ref.at[2], dp_ref.at[3]

        @pl.when(i == 0)
        def _():
            dstate[...] = jnp.zeros_like(dstate)
            acc_lb[...] = jnp.zeros_like(acc_lb)

        @pl.when((i == 0) & (h == 0))
        def _():
            acc_nw[...] = jnp.zeros_like(acc_nw)

        nwv = nw_ref[...]
        rr = lax.broadcasted_iota(jnp.int32, (CHUNK, CHUNK), 0)
        cc = lax.broadcasted_iota(jnp.int32, (CHUNK, CHUNK), 1)
        causal = rr >= cc
        tri = jnp.where(causal, 1.0, 0.0).astype(BF16)
        tri_t =jnp.where(cc >= rr, 1.0, 0.0).astype(BF16)
        last_row = lax.broadcasted_iota(jnp.int32, (CHUNK, k_), 0) == CHUNK - 1

        dst = [dstate[hh] for hh in range(hb)]
        sum_nw = [jnp.zeros((8, k_), F32)]
        sum_lb = [jnp.zeros((8, k_), F32) for _ in range(hb)]

        def task(c, hh):
            rows, lanes = slice(c * CHUNK, (c + 1) * CHUNK), slice(hh * k_, (hh + 1) * k_)
            lbv = lb_ref[:, lanes]
            qp, fp, gp = q_ref[rows, lanes].astype(F32), f_ref[rows, lanes].astype(F32), g_ref[rows, lanes].astype(F32)
            vb = i_ref[rows, lanes].astype(BF16)
            sf, f, k, sq, qf, logf = _chunk_gates(qp, fp, lbv)
            yield
            b = _tri_dot(tri, logf)
            dm = dm_ref[rows, lanes]
            o = o_ref[rows, lanes]
            rinv = lax.rsqrt(jnp.mean(o * o, axis=-1, keepdims=True) + EPS)
            ohat = o * rinv
            sg = _sigmoid(gp)
            dg_ref[rows, lanes] = (dm * (ohat * nwv) * (sg * (1.0 + gp * (1.0 - sg)))).astype(BF16)
            don = dm * (gp * sg)
            sum_nw[0] = sum_nw[0] + _rows_to_8(don * ohat)
            dohat = don * nwv
            do = rinv * (dohat - ohat * jnp.mean(dohat * ohat, axis=-1, keepdims=True))
            yield
            e_b, e_qm, e_km, e_kl, e_l = _chunk_decays(b)
            st0 = st_ref[c, hh]
            do2 = _pair(do)
            qm2, km2 = _pair(qf * e_qm), _pair(k * e_km)
            st02, qe2 = _pair(st0), _pair(qf * e_b)
            yield
            a = jnp.where(causal, _dot(qm2[0], km2[0], "nt"), 0.0)
            da2 = _pair(jnp.where(causal, _dot_pairs(do2, (vb, None), "nt"), 0.0))
            dq_state = _dot_pairs(do2, st02, "nn")
            update = _dot_pairs(do2, qe2, "tn")
            yield
            dst2 = _pair(dst[hh])
            dq = e_b * dq_state + e_qm * _dot_pairs(da2, km2, "nn")
            dk_state = e_kl * _dot_pairs((vb, None), dst2, "nn")
            dk = e_km * _dot_pairs(da2, qm2, "tn") + dk_state
            dv = _dot(a, do2[0], "tn") + _dot(k * e_kl, dst2[0], "nt")
            d_bl = jnp.sum(k * dk_state, axis=0, keepdims=True) + e_l * jnp.sum(st0 * dst[hh], axis=0, keepdims=True)
            dst[hh] = dst[hh] * e_l + update
            yield
            dlogf = _tri_dot(tri_t, qf * dq - k * dk + jnp.where(last_row, d_bl, 0.0))
            yield
            dfv = dlogf / f - dk
            df_ref[rows, lanes] = (dfv * (1.0 - lbv) * (sf * (1.0 - sf))).astype(BF16)
            sum_lb[hh] = sum_lb[hh] + _rows_to_8(dfv * (1.0 - sf))
            dq_ref[rows, lanes] = (dq * (sq * (1.0 + qp * (1.0 - sq)))).astype(BF16)
            di_ref[rows, lanes] = dv.astype(BF16)

        _emit_pipelined((task(c, hh) for c in reversed(range(nch)) for hh in range(hb)), 7)
        acc_nw[...] += sum_nw[0]
        for hh in range(hb):
            dstate[hh] = dst[hh]
            acc_lb[:, hh * k_ : (hh + 1) * k_] += sum_lb[hh]

        @pl.when(i == nt - 1)
        def _():
            dlb_ref[...] = jnp.sum(acc_lb[...], axis=0, keepdims=True)

        @pl.when((i == nt - 1) & (h == h_ // hb - 1))
        def _():
            dnw_ref[...] = jnp.sum(acc_nw[...], axis=0, keepdims=True)

    def col(g):
        return pl.BlockSpec((None, t, hb * k_), lambda h, i: (g, nt - 1 - i, h))

    row = pl.BlockSpec((t, hb * k_), lambda h, i: (nt - 1 - i, h))
    grid = (h_ // hb, nt)
    return pl.pallas_call(
        _hosted(compute, 9, 3, 3, ride, grid),
        name="hgrn_bwd",
        grid=grid,
        in_specs=[
            col(0),
            col(1),
            col(2),
            col(3),
            pl.BlockSpec((1, hb * k_), lambda h, i: (0, h)),
            pl.BlockSpec((1, k_), lambda h, i: (0, 0)),
            row,
            pl.BlockSpec((nch, hb, k_, k_), lambda h, i: (nt - 1 - i, h, 0, 0)),
            row,
        ]
        + [HBM_SPEC] * ride.n,
        out_specs=[
            pl.BlockSpec((4, t, hb * k_), lambda h, i: (0, nt - 1 - i, h)),
            pl.BlockSpec((1, hb * k_), lambda h, i: (0, h)),
            pl.BlockSpec((1, k_), lambda h, i: (0, 0)),
        ]
        + [HBM_SPEC] * ride.n,
        out_shape=[
            jax.ShapeDtypeStruct((8, s, h_ * k_), BF16),
            jax.ShapeDtypeStruct((1, h_ * k_), F32),
            jax.ShapeDtypeStruct((1, k_), F32),
        ]
        + ride.out_shape,
        scratch_shapes=[pltpu.VMEM((hb, k_, k_), F32), pltpu.VMEM((8, hb * k_), F32), pltpu.VMEM((8, k_), F32)] + ride.scratch,
        compiler_params=_params(2),
    )(proj, proj, proj, proj, lb, nw, o_pre, states, dmix, *ride.inputs)


def _shift_rows(ext, k):
    n = ext.shape[0]
    return pltpu.roll(ext, k % n, axis=0)


def _row_select(parts):
    w = parts[0].shape[1]
    row = lax.broadcasted_iota(jnp.int32, (8, w), 0)
    out = jnp.zeros((8, w), F32)
    for r, p in enumerate(parts):
        out = out + jnp.where(row == r, p, 0.0)
    return out


def _sconv_fwd(proj, sw, mix):
    s = proj.shape[1]
    t = min(TS, s)
    nt, w = s // t, 512
    ncol, hb = GROUP_W // w, t // HALO

    def body(cb_ref, cc_ref, ch_ref, ccp_ref, chp_ref, sw_ref, mix_in, mix_ref):
        del mix_in
        i = pl.program_id(0)
        u = cc_ref[...].astype(F32) * ch_ref[...].astype(F32)
        up = jnp.where(i > 0, ccp_ref[...].astype(F32) * chp_ref[...].astype(F32), 0.0)
        ext = jnp.concatenate([up, u], axis=0)
        z = sw_ref[0:1, :] * _shift_rows(ext, 2)[HALO:] + sw_ref[1:2, :] * _shift_rows(ext, 1)[HALO:] + sw_ref[2:3, :] * u
        mix_ref[...] = (cb_ref[...].astype(F32) * z).astype(BF16)

    def cur(g):
        return pl.BlockSpec((None, t, w), lambda i, c: (g, i, c))

    def prev(g):
        return pl.BlockSpec((None, HALO, w), lambda i, c: (g, jnp.maximum(i * hb - 1, 0), c))

    return pl.pallas_call(
        body,
        name="sconv_fwd",
        grid=(nt, ncol),
        in_specs=[cur(4), cur(5), cur(6), prev(5), prev(6), pl.BlockSpec((3, w), lambda i, c: (0, c)), pl.BlockSpec(memory_space=pl.ANY)],
        out_specs=pl.BlockSpec((t, w), lambda i, c: (i, ncol + c)),
        out_shape=jax.ShapeDtypeStruct(mix.shape, mix.dtype),
        input_output_aliases={6: 0},
        compiler_params=_params(2),
    )(proj, proj, proj, proj, proj, sw, mix)


def _sconv_bwd(proj, sw, dmix, dproj):
    s = proj.shape[1]
    t = min(TS, s)
    nt, w = s // t, 512
    ncol, hb = GROUP_W // w, t // HALO

    def body(cb_ref, cbn_ref, cc_ref, ccp_ref, ch_ref, chp_ref, dy_ref, dyn_ref, sw_ref, dp_in, dp_ref, dsw_ref, acc):
        del dp_in
        i = pl.program_id(1)
        w0, w1, w2 = sw_ref[0:1, :], sw_ref[1:2, :], sw_ref[2:3, :]
        ccv, chv, cbv, dyv = cc_ref[...].astype(F32), ch_ref[...].astype(F32), cb_ref[...].astype(F32), dy_ref[...]
        u = ccv * chv
        up = jnp.where(i > 0, ccp_ref[...].astype(F32) * chp_ref[...].astype(F32), 0.0)
        ext = jnp.concatenate([up, u], axis=0)
        u1, u2 = _shift_rows(ext, 1)[HALO:], _shift_rows(ext, 2)[HALO:]
        z = w0 * u2 + w1 * u1 + w2 * u
        dz = dyv * cbv
        dzn = jnp.where(i < nt - 1, dyn_ref[...] * cbn_ref[...].astype(F32), 0.0)
        dze = jnp.concatenate([dz, dzn], axis=0)
        du = w2 * dz + w1 * _shift_rows(dze, -1)[:t] + w0 * _shift_rows(dze, -2)[:t]
        dp_ref[0] = (dyv * z).astype(BF16)
        dp_ref[1] = (du * chv).astype(BF16)
        dp_ref[2] = (du * ccv).astype(BF16)
        dp_ref[3] = jnp.zeros((t, w), BF16)

        @pl.when(i == 0)
        def _():
            acc[...] = jnp.zeros_like(acc)

        acc[...] += _row_select([jnp.sum(dz * uu, axis=0, keepdims=True) for uu in (u2, u1, u)])

        @pl.when(i == nt - 1)
        def _():
            dsw_ref[...] = acc[...]

    def cur(g):
        return pl.BlockSpec((None, t, w), lambda c, i: (g, i, c))

    def prev(g):
        return pl.BlockSpec((None, HALO, w), lambda c, i: (g, jnp.maximum(i * hb - 1, 0), c))

    def nxt(g):
        return pl.BlockSpec((None, HALO, w), lambda c, i: (g, jnp.minimum((i + 1) * hb, s // HALO - 1), c))

    return pl.pallas_call(
        body,
        name="sconv_bwd",
        grid=(ncol, nt),
        in_specs=[
            cur(4),
            nxt(4),
            cur(5),
            prev(5),
            cur(6),
            prev(6),
            pl.BlockSpec((t, w), lambda c, i: (i, ncol + c)),
            pl.BlockSpec((HALO, w), lambda c, i: (jnp.minimum((i + 1) * hb, s // HALO - 1), ncol + c)),
            pl.BlockSpec((3, w), lambda c, i: (0, c)),
            pl.BlockSpec(memory_space=pl.ANY),
        ],
        out_specs=[pl.BlockSpec((4, t, w), lambda c, i: (1, i, c)), pl.BlockSpec((8, w), lambda c, i: (0, c))],
        out_shape=[jax.ShapeDtypeStruct(dproj.shape, dproj.dtype), jax.ShapeDtypeStruct((8, GROUP_W), F32)],
        scratch_shapes=[pltpu.VMEM((8, w), F32)],
        input_output_aliases={9: 0},
        compiler_params=_params(2),
    )(proj, proj, proj, proj, proj, proj, dmix, dmix, sw, dproj)


def _ffn_down(a_pre, up, cw, cb, wd_g, x2):
    nd, s, f = a_pre.shape
    d = wd_g.shape[2]
    t = min(TS, s)
    nt, hb = s // t, t // HALO

    def body(a_ref, ap_ref, up_ref, cw_ref, cb_ref, w_ref, res_ref, o_ref, act_ref):
        i, j = pl.program_id(0), pl.program_id(1)
        av = a_ref[...].astype(F32)
        ext = jnp.concatenate([jnp.where(i > 0, ap_ref[...].astype(F32), 0.0), av], axis=0)
        a = cw_ref[0:1, :] * _shift_rows(ext, 2)[HALO:] + cw_ref[1:2, :] * _shift_rows(ext, 1)[HALO:] + cw_ref[2:3, :] * av
        a = a + cb_ref[...]
        act = (a * _sigmoid(a) * up_ref[...].astype(F32)).astype(BF16)
        act_ref[...] = act

        @pl.when(j == 0)
        def _():
            o_ref[...] = res_ref[...]

        o_ref[...] += _dot(act, w_ref[...], "nn")

    cur = pl.BlockSpec((None, t, f), lambda i, j: (j, i, 0))
    prev = pl.BlockSpec((None, HALO, f), lambda i, j: (j, jnp.maximum(i * hb - 1, 0), 0))
    row = pl.BlockSpec((t, d), lambda i, j: (i, 0))
    return pl.pallas_call(
        body,
        name="ffn_down",
        grid=(nt, nd),
        in_specs=[
            cur,
            prev,
            cur,
            pl.BlockSpec((None, 3, f), lambda i, j: (j, 0, 0)),
            pl.BlockSpec((None, 1, f), lambda i, j: (j, 0, 0)),
            pl.BlockSpec((None, f, d), lambda i, j: (j, 0, 0)),
            row,
        ],
        out_specs=[row, cur],
        out_shape=[jax.ShapeDtypeStruct((s, d), F32), jax.ShapeDtypeStruct(a_pre.shape, BF16)],
        compiler_params=_params(2),
    )(a_pre, a_pre, up, cw, cb, wd_g, x2)


def _ffn_act_bwd(a_pre, up, dact, cw, cb, ride):
    nd, s, f = a_pre.shape
    t = min(TS, s)
    nt, hb = s // t, t // HALO

    def compute(a_ref, ap_ref, an_ref, up_ref, upn_ref, da_ref, dan_ref, cw_ref, cb_ref, dap_ref, dup_ref, dsm_ref):
        i, j = pl.program_id(0), pl.program_id(1)
        w0, w1, w2 = cw_ref[0:1, :], cw_ref[1:2, :], cw_ref[2:3, :]
        av = a_ref[...].astype(F32)
        ext = jnp.concatenate([jnp.where(i > 0, ap_ref[...].astype(F32), 0.0), av, an_ref[...].astype(F32)], axis=0)
        e1, e2 = _shift_rows(ext, 1), _shift_rows(ext, 2)
        a = (w0 * e2 + w1 * e1 + w2 * ext)[HALO:] + cb_ref[...]
        sg = _sigmoid(a)
        dact_e = jnp.concatenate([da_ref[...], dan_ref[...]], axis=0).astype(F32)
        up_e = jnp.concatenate([up_ref[...], upn_ref[...]], axis=0).astype(F32)
        row = lax.broadcasted_iota(jnp.int32, a.shape, 0)
        live = (row < t) | (i < nt - 1)
        d_a = jnp.where(live, dact_e * up_e * (sg * (1.0 + a * (1.0 - sg))), 0.0)
        dup_ref[...] = (dact_e * (a * sg))[:t].astype(BF16)
        dap_ref[...] = (w2 * d_a + w1 * _shift_rows(d_a, -1) + w0 * _shift_rows(d_a, -2))[:t].astype(BF16)

        d_a_cur = d_a[:t]
        sums = [jnp.sum(d_a_cur * e[HALO : HALO + t], axis=0, keepdims=True) for e in (e2, e1, ext)]
        part = _row_select(sums + [jnp.sum(d_a_cur, axis=0, keepdims=True)])

        @pl.when(i == 0)
        def _():
            dsm_ref[j] = part

        @pl.when(i > 0)
        def _():
            dsm_ref[j] += part

    cur = pl.BlockSpec((None, t, f), lambda i, j: (j, i, 0))
    prev = pl.BlockSpec((None, HALO, f), lambda i, j: (j, jnp.maximum(i * hb - 1, 0), 0))
    nxt = pl.BlockSpec((None, HALO, f), lambda i, j: (j, jnp.minimum((i + 1) * hb, s // HALO - 1), 0))
    grid = (nt, nd)
    return pl.pallas_call(
        _hosted(compute, 9, 3, 0, ride, grid),
        name="ffn_act_bwd",
        grid=grid,
        in_specs=[cur, prev, nxt, cur, nxt, cur, nxt, pl.BlockSpec((None, 3, f), lambda i, j: (j, 0, 0)), pl.BlockSpec((None, 1, f), lambda i, j: (j, 0, 0))]
        + [HBM_SPEC] * ride.n,
        out_specs=[cur, cur, pl.BlockSpec((nd, 8, f), lambda i, j: (0, 0, 0))] + [HBM_SPEC] * ride.n,
        out_shape=[jax.ShapeDtypeStruct(a_pre.shape, BF16), jax.ShapeDtypeStruct(a_pre.shape, BF16), jax.ShapeDtypeStruct((nd, 8, f), F32)]
        + ride.out_shape,
        scratch_shapes=ride.scratch,
        compiler_params=_params(2),
    )(a_pre, a_pre, a_pre, up, up, dact, dact, cw, cb, *ride.inputs)


def _softmax_rows(sc):
    m = jnp.max(sc, axis=-1, keepdims=True)
    e = jnp.exp(sc - m)
    return e / jnp.sum(e, axis=-1, keepdims=True)


def _attn_fwd(q, km, vm):
    s, d = q.shape
    m = km.shape[0]
    t = min(TS, s)
    hd = d // MEM_HEADS
    scale = hd**-0.5

    def body(q_ref, k_ref, v_ref, o_ref):
        for h in range(MEM_HEADS):
            cols = slice(h * hd, (h + 1) * hd)
            p = _softmax_rows(_dot(q_ref[:, cols], k_ref[:, cols], "nt") * scale)
            o_ref[:, cols] = _dot(p, v_ref[:, cols], "nn").astype(BF16)

    row = pl.BlockSpec((t, d), lambda i: (i, 0))
    full = pl.BlockSpec((m, d), lambda i: (0, 0))
    return pl.pallas_call(
        body,
        name="attn_fwd",
        grid=(s // t,),
        in_specs=[row, full, full],
        out_specs=row,
        out_shape=jax.ShapeDtypeStruct((s, d), BF16),
        compiler_params=_params(1),
    )(q, km, vm)


def _attn_bwd(q, km, vm, do):
    s, d = q.shape
    m = km.shape[0]
    t = min(TS, s)
    steps = s // t
    hd = d // MEM_HEADS
    scale = hd**-0.5

    def body(q_ref, k_ref, v_ref, do_ref, dq_ref, dk_ref, dv_ref):
        @pl.when(pl.program_id(0) == 0)
        def _():
            dk_ref[...] = jnp.zeros_like(dk_ref)
            dv_ref[...] = jnp.zeros_like(dv_ref)

        for h in range(MEM_HEADS):
            cols = slice(h * hd, (h + 1) * hd)
            qh, kh, vh, doh = q_ref[:, cols], k_ref[:, cols], v_ref[:, cols], do_ref[:, cols]
            p = _softmax_rows(_dot(qh, kh, "nt") * scale)
            dp = _dot(doh, vh, "nt")
            ds = p * (dp - jnp.sum(dp * p, axis=-1, keepdims=True)) * scale
            dq_ref[:, cols] = _dot(ds, kh, "nn").astype(BF16)
            dk_ref[:, cols] += _dot(ds, qh, "tn")
            dv_ref[:, cols] += _dot(p, doh, "tn")

    row = pl.BlockSpec((t, d), lambda i: (i, 0))
    full = pl.BlockSpec((m, d), lambda i: (0, 0))
    return pl.pallas_call(
        body,
        name="attn_bwd",
        grid=(steps,),
        in_specs=[row, full, full, row],
        out_specs=[row, full, full],
        out_shape=[jax.ShapeDtypeStruct((s, d), BF16), jax.ShapeDtypeStruct((m, d), F32), jax.ShapeDtypeStruct((m, d), F32)],
        compiler_params=_params(1),
    )(q, km, vm, do)


def _row_tile(rows, cols):
    best = 8
    for tr in range(8, rows + 1, 8):
        if rows % tr == 0 and tr * cols * 4 <= (1 << 20):
            best = tr
    return best


def _adamw(w, g, m, v):
    m = ADAM_B1 * m + (1.0 - ADAM_B1) * g
    v = ADAM_B2 * v + (1.0 - ADAM_B2) * (g * g)
    m_hat = m / (1.0 - ADAM_B1**ADAM_STEP)
    v_hat = v / (1.0 - ADAM_B2**ADAM_STEP)
    delta = -ADAM_LR * (m_hat / (jnp.sqrt(v_hat) + ADAM_EPS) + ADAM_WD * w)
    return delta, m, v


def _adam_shard(name, parts, w, m, v):
    rows, cols = w.shape
    tr = _row_tile(rows, cols)

    def body(*refs):
        part_refs = refs[:N_DEV]
        w_ref, m_ref, v_ref, go_ref, d_ref, mo_ref, vo_ref = refs[N_DEV:]
        g = part_refs[0][...].astype(F32)
        for dev in range(1, N_DEV):
            g = g + part_refs[dev][...].astype(F32)
        go_ref[...] = g
        d_ref[...], mo_ref[...], vo_ref[...] = _adamw(w_ref[...], g, m_ref[...], v_ref[...])

    def part(dev):
        return pl.BlockSpec((None, tr, cols), lambda i: (dev, i, 0))

    flat = pl.BlockSpec((tr, cols), lambda i: (i, 0))
    return pl.pallas_call(
        body,
        name=name,
        grid=(rows // tr,),
        in_specs=[part(dev) for dev in range(N_DEV)] + [flat, flat, flat],
        out_specs=[flat, flat, flat, flat],
        out_shape=[jax.ShapeDtypeStruct((rows, cols), F32)] * 4,
        compiler_params=_params(1),
    )(*([parts] * N_DEV), w, m, v)


def _sum_devices(gathered):
    def body(g_ref, o_ref):
        tot = g_ref[0]
        for dev in range(1, N_DEV):
            tot = tot + g_ref[dev]
        o_ref[...] = tot

    return pl.pallas_call(body, name="sum_devices", out_shape=jax.ShapeDtypeStruct(gathered.shape[1:], F32))(gathered)


def _adam_small(ws, gs, ms, vs):
    n = len(ws)

    def body(*refs):
        w_refs, g_refs, m_refs, v_refs = (refs[q * n : (q + 1) * n] for q in range(4))
        outs = refs[4 * n :]
        for p in range(n):
            outs[p][...], outs[n + p][...], outs[2 * n + p][...] = _adamw(w_refs[p][...], g_refs[p][...], m_refs[p][...], v_refs[p][...])

    shapes = [jax.ShapeDtypeStruct(w.shape, F32) for w in ws]
    res = pl.pallas_call(body, name="adam_small", out_shape=shapes * 3)(*ws, *gs, *ms, *vs)
    return res[:n], res[n : 2 * n], res[2 * n :]


def _padded(size):
    return -(-size // 128) * 128


def _pack(parts):
    flat = []
    for p in parts:
        v = p.reshape(-1)
        flat.append(jnp.pad(v, (0, _padded(v.shape[0]) - v.shape[0])))
    return jnp.concatenate(flat).reshape(-1, 128)


def _unpack(packed, shapes):
    flat = packed.reshape(-1)
    out, pos = [], 0
    for shp in shapes:
        size = math.prod(shp)
        out.append(flat[pos : pos + size].reshape(shp))
        pos += _padded(size)
    return out


WEIGHTS = (
    "hgrn_lb norm1_w w_in hgrn_norm_w sconv_w w_out norm2_w mem_norm_w wq wk wv wo norm3_w w_gate w_up ffn_conv_w "
    "ffn_conv_b w_down final_norm_w"
).split()
BIG = ["w_in", "w_out", "wq", "wk", "wv", "wo", "w_gate", "w_up", "w_down"]
SMALL_SHARDED = ["sconv_w", "ffn_conv_w"]
SMALL = [n for n in WEIGHTS if n not in BIG]


def kernel(x, mem, hgrn_lb, norm1_w, w_in, hgrn_norm_w, sconv_w, w_out, norm2_w, mem_norm_w, wq, wk, wv, wo, norm3_w, w_gate, w_up, ffn_conv_w, ffn_conv_b, w_down, final_norm_w, loss_target, m_hgrn_lb, m_norm1_w, m_w_in, m_hgrn_norm_w, m_sconv_w, m_w_out, m_norm2_w, m_mem_norm_w, m_wq, m_wk, m_wv, m_wo, m_norm3_w, m_w_gate, m_w_up, m_ffn_conv_w, m_ffn_conv_b, m_w_down, m_final_norm_w, v_hgrn_lb, v_norm1_w, v_w_in, v_hgrn_norm_w, v_sconv_w, v_w_out, v_norm2_w, v_mem_norm_w, v_wq, v_wk, v_wv, v_wo, v_norm3_w, v_w_gate, v_w_up, v_ffn_conv_w, v_ffn_conv_b, v_w_down, v_final_norm_w):
    given = dict(locals())
    w_of = {n: given[n] for n in WEIGHTS}
    m_of = {n: given["m_" + n] for n in WEIGHTS}
    v_of = {n: given["v_" + n] for n in WEIGHTS}
    x2, mem2, target = x[0], mem[0], loss_target[0]
    d = x2.shape[1]
    xi, yi, ci = lax.axis_index("x"), lax.axis_index("y"), lax.axis_index("c")
    me = 4 * xi + 2 * yi + ci
    final_w = final_norm_w.reshape(1, d)

    shard = {n: w_of[n][0].astype(BF16) for n in BIG}
    small_shard = jnp.concatenate([sconv_w[0], ffn_conv_w[0]], axis=1)
    h1, w_in_g, small_g = _rmsnorm("norm1", x2, norm1_w, _GatherRide([shard["w_in"], small_shard]))
    w_in_f = jnp.transpose(w_in_g, (1, 0, 2)).reshape(d, -1)
    conv_ch = sconv_w.shape[2]
    sw_f = jnp.transpose(small_g[:, :, :conv_ch], (1, 0, 2)).reshape(3, -1)
    cw_g = small_g[:, :, conv_ch:]
    cb_g = ffn_conv_b.reshape(N_DEV, 1, -1)

    lb = _lb_fwd(hgrn_lb)
    proj, *attn_w = _mm_proj(h1, w_in_f, _GatherRide([shard[n] for n in ("w_out", "wq", "wk", "wv", "wo")]))
    w_out_f, wq_f, wk_f, wv_f, wo_f = (a.reshape(d, d) for a in attn_w)
    mix, o_pre, states, wg_g, wu_g = _hgrn_fwd(proj, lb, hgrn_norm_w, _GatherRide([shard["w_gate"], shard["w_up"]]))
    mix = _sconv_fwd(proj, sw_f, mix)
    x1 = _mm_nn("mm_out", mix, w_out_f, F32, residual=x2)
    h2 = _rmsnorm("norm2", x1, norm2_w)
    mem_n = _rmsnorm("norm_mem", mem2, mem_norm_w)
    q = _mm_nn("mm_q", h2, wq_f, BF16)
    km = _mm_nn("mm_k", mem_n, wk_f, BF16)
    vm = _mm_nn("mm_v", mem_n, wv_f, BF16)
    ao = _attn_fwd(q, km, vm)
    x2r = _mm_nn("mm_o", ao, wo_f, F32, residual=x1)
    h3 = _rmsnorm("norm3", x2r, norm3_w)
    a_pre, up, wd_g = _mm_gate_up(h3, wg_g, wu_g, _GatherRide([shard["w_down"]]))
    x3, act = _ffn_down(a_pre, up, cw_g, cb_g, wd_g, x2r)

    dx3, dx3b, loss_blk, d_final = _loss_head(x3, target, final_w)
    parts = {}
    dact = _mm_dact(dx3b, wd_g)
    dwd = _mm_dwdown(act, dx3b)
    dapre, dup, d_ffn_small, parts["w_down"] = _ffn_act_bwd(a_pre, up, dact, cw_g, cb_g, _ScatterRide([dwd]))
    dwg, dwu = _mm_dw_cols("mm_dw_gate_up", h3, [dapre, dup])
    dh3, parts["w_gate"], parts["w_up"] = _mm_dh3(dapre, dup, wg_g, wu_g, _ScatterRide([dwg, dwu]))
    dx2, dx2b, d_n3 = _rmsnorm_bwd("norm3_bwd", dh3, x2r, norm3_w, dx3)
    dao = _mm_nt("mm_dao", [(dx2b, wo_f)], BF16)
    dwo = _mm_tn("mm_dwo", ao, dx2b)
    dq, dkm, dvm = _attn_bwd(q, km, vm, dao)
    dwq = _mm_tn("mm_dwq", h2, dq)
    dh2 = _mm_nt("mm_dh2", [(dq, wq_f)], F32)
    dwk = _mm_tn("mm_dwk", mem_n, dkm)
    dwv = _mm_tn("mm_dwv", mem_n, dvm)
    dmem_n = _mm_nt("mm_dmem", [(dkm, wk_f), (dvm, wv_f)], F32)
    _, _, d_nm = _rmsnorm_bwd("norm_mem_bwd", dmem_n, mem2, mem_norm_w)
    dx1, dx1b, d_n2 = _rmsnorm_bwd("norm2_bwd", dh2, x1, norm2_w, dx2)
    dmix = _mm_nt("mm_dmix", [(dx1b, w_out_f)], F32)
    dwout = _mm_tn("mm_dwout", mix, dx1b)
    dwout_g, dwq_g, dwk_g, dwv_g, dwo_g = (a.reshape(N_DEV, d // N_DEV, d) for a in (dwout, dwq, dwk, dwv, dwo))
    dproj, d_lb, d_nw, parts["w_out"], parts["wq"] = _hgrn_bwd(proj, lb, hgrn_norm_w, o_pre, states, dmix, _ScatterRide([dwout_g, dwq_g]))
    dproj, d_sw = _sconv_bwd(proj, sw_f, dmix, dproj)
    dwin, parts["wk"], parts["wv"], parts["wo"] = _mm_dwin(h1, dproj, N_GROUPS, _ScatterRide([dwk_g, dwv_g, dwo_g]))
    dwin_g = jnp.transpose(dwin.reshape(d, N_DEV, -1), (1, 0, 2))
    dh1, parts["w_in"] = _mm_dh1(dproj, w_in_f, N_GROUPS, _ScatterRide([dwin_g]))
    grad_x, _, d_n1 = _rmsnorm_bwd("norm1_bwd", dh1, x2, norm1_w, dx1)
    d_hlb = _lb_bwd(hgrn_lb, d_lb)

    small_parts = [d_hlb, d_n1, d_nw, d_sw[0:3], d_n2, d_nm, d_n3, d_ffn_small[:, 0:3, :], d_ffn_small[:, 3, :], d_final, loss_blk]
    small_shapes = [p.shape for p in small_parts]
    total = _sum_devices(_allgather("gather_small", [_pack(small_parts)])[0])
    g_hlb, g_n1, g_nw, g_sw, g_n2, g_nm, g_n3, g_cw, g_cb, g_final, loss_row = _unpack(total, small_shapes)
    grads = {
        "hgrn_lb": g_hlb,
        "norm1_w": g_n1,
        "hgrn_norm_w": g_nw,
        "sconv_w": lax.dynamic_slice_in_dim(g_sw, me * conv_ch, conv_ch, axis=1).reshape(sconv_w.shape),
        "norm2_w": g_n2,
        "mem_norm_w": g_nm,
        "norm3_w": g_n3,
        "ffn_conv_w": lax.dynamic_index_in_dim(g_cw, me, axis=0, keepdims=True),
        "ffn_conv_b": g_cb.reshape(ffn_conv_b.shape),
        "final_norm_w": g_final.reshape(final_norm_w.shape),
    }
    def rows_of(a):
        return a.reshape(-1, a.shape[-1])

    small_out = _adam_small(*([rows_of(src[n]) for n in SMALL] for src in (w_of, grads, m_of, v_of)))
    deltas, new_m, new_v = ({n: a.reshape(w_of[n].shape) for n, a in zip(SMALL, out)} for out in small_out)

    for n in BIG:
        shp = w_of[n].shape
        gw, dw, mw, vw = _adam_shard("adam_" + n, parts[n], w_of[n][0], m_of[n][0], v_of[n][0])
        grads[n], deltas[n], new_m[n], new_v[n] = (a.reshape(shp) for a in (gw, dw, mw, vw))

    loss = loss_row[0, 0]
    return (
        loss,
        grad_x.reshape(x.shape),
        *[grads[n] for n in WEIGHTS],
        *[deltas[n] for n in WEIGHTS],
        *[new_m[n] for n in WEIGHTS],
        *[new_v[n] for n in WEIGHTS],
    )
```

```python
import functools
import math

import jax
import jax.numpy as jnp
from jax import lax
from jax.experimental import pallas as pl
from jax.experimental.pallas import tpu as pltpu

F32 = jnp.float32
BF16 = jnp.bfloat16
MESH = pl.DeviceIdType.MESH
N_DEV = 8
EPS = 1e-6
CHUNK = 64
HGRN_HEADS = 8
HEAD_DIM = 128
HEADS_PER_STEP = 4
MEM_HEADS = 4
GROUP_W = HGRN_HEADS * HEAD_DIM
N_GROUPS = 7
HALO = 16
TS = 512
TR = 256
TM = 1024
TN = 1024
TK = 1024
VMEM_LIMIT = 48 * 1024 * 1024

ADAM_LR = 0.001
ADAM_B1 = 0.9
ADAM_B2 = 0.999
ADAM_EPS = 1e-08
ADAM_WD = 0.01
ADAM_STEP = 10

_DIMS = {
    "nn": (((1,), (0,)), ((), ())),
    "nt": (((1,), (1,)), ((), ())),
    "tn": (((0,), (0,)), ((), ())),
}


def _params(n_axes):
    return pltpu.CompilerParams(dimension_semantics=("arbitrary",) * n_axes, vmem_limit_bytes=VMEM_LIMIT)


def _dot(a, b, mode):
    return lax.dot_general(a.astype(BF16), b.astype(BF16), _DIMS[mode], preferred_element_type=F32)


def _sigmoid(v):
    return 0.5 * jnp.tanh(0.5 * v) + 0.5


def _block_dims(spec):
    return tuple(d for d in spec.block_shape if d is not None)


HBM_SPEC = pl.BlockSpec(memory_space=pltpu.HBM)


def _place():
    x, y, c = lax.axis_index("x"), lax.axis_index("y"), lax.axis_index("c")
    return x, y, c, [(1 - x, y), (x, 1 - y), (1 - x, 1 - y)]


def _slab(p):
    return 4 * p[0] + 2 * p[1] + p[2]


def _remote(src, dst, send_sem, recv_sem, to):
    return pltpu.make_async_remote_copy(src_ref=src, dst_ref=dst, send_sem=send_sem, recv_sem=recv_sem, device_id=to, device_id_type=MESH)


class _GatherRide:
    def __init__(self, shards):
        self.inputs = list(shards)
        self.n = len(shards)
        self.out_shape = [jax.ShapeDtypeStruct((N_DEV, *a.shape), a.dtype) for a in shards]
        self.scratch = [pltpu.SemaphoreType.DMA((self.n, 7)), pltpu.SemaphoreType.DMA((self.n, 7)), pltpu.SemaphoreType.DMA((self.n,))]

    @staticmethod
    def _copy(outs, sems, a, k, block, to, src=None):
        dst = outs[a].at[_slab(block)]
        return _remote(dst if src is None else src, dst, sems[0].at[a, k], sems[1].at[a, k], to)

    def _mine(self, ins, outs, sems, a):
        x, y, c, chips = _place()
        local = pltpu.make_async_copy(ins[a], outs[a].at[_slab((x, y, c))], sems[2].at[a])
        first = [self._copy(outs, sems, a, 0, (x, y, c), (x, y, 1 - c), src=ins[a])]
        first += [self._copy(outs, sems, a, 1 + j, (x, y, c), (*chip, c), src=ins[a]) for j, chip in enumerate(chips)]
        return local, first

    def start(self, ins, outs, sems):
        for a in range(self.n):
            local, first = self._mine(ins, outs, sems, a)
            local.start()
            for cp in first:
                cp.start()

    def forward(self, ins, outs, sems):
        x, y, c, chips = _place()
        for j, chip in enumerate(chips):
            for a in range(self.n):
                self._copy(outs, sems, a, 1 + j, (*chip, c), (x, y, c)).wait_recv()
                self._copy(outs, sems, a, 4 + j, (*chip, c), (x, y, 1 - c)).start()

    def finish(self, ins, outs, sems):
        x, y, c, chips = _place()
        for a in range(self.n):
            self._copy(outs, sems, a, 0, (x, y, 1 - c), (x, y, c)).wait_recv()
            for j, chip in enumerate(chips):
                self._copy(outs, sems, a, 4 + j, (*chip, 1 - c), (x, y, c)).wait_recv()
        for a in range(self.n):
            local, first = self._mine(ins, outs, sems, a)
            for cp in first:
                cp.wait_send()
            for j, chip in enumerate(chips):
                self._copy(outs, sems, a, 4 + j, (*chip, c), (x, y, 1 - c)).wait_send()
            local.wait()


class _ScatterRide:
    def __init__(self, grads):
        self.inputs = list(grads)
        self.n = len(grads)
        self.out_shape = [jax.ShapeDtypeStruct(g.shape, g.dtype) for g in grads]
        self.scratch = [pltpu.SemaphoreType.DMA((self.n, 7)), pltpu.SemaphoreType.DMA((self.n, 7)), pltpu.SemaphoreType.DMA((self.n,))]

    @staticmethod
    def _peers():
        x, y, c, chips = _place()
        return _slab((x, y, c)), [(x, y, 1 - c)] + [(*chip, c) for chip in chips] + [(*chip, 1 - c) for chip in chips]

    def _sends(self, ins, outs, sems, a):
        me, peers = self._peers()
        local = pltpu.make_async_copy(ins[a].at[me], outs[a].at[me], sems[2].at[a])
        return [local] + [_remote(ins[a].at[_slab(p)], outs[a].at[me], sems[0].at[a, k], sems[1].at[a, k], p) for k, p in enumerate(peers)]

    def start(self, ins, outs, sems):
        for a in range(self.n):
            for cp in self._sends(ins, outs, sems, a):
                cp.start()

    def forward(self, ins, outs, sems):
        pass

    def finish(self, ins, outs, sems):
        me, peers = self._peers()
        for a in range(self.n):
            for k, p in enumerate(peers):
                _remote(ins[a].at[me], outs[a].at[_slab(p)], sems[0].at[a, k], sems[1].at[a, k], p).wait_recv()
            local, *sends = self._sends(ins, outs, sems, a)
            for cp in sends:
                cp.wait_send()
            local.wait()


def _grid_step(grid):
    step = 0
    for axis, size in enumerate(grid):
        step = step * size + pl.program_id(axis)
    return step


def _ride_begin(ride, grid, ins, outs, sems):
    step, n_steps = _grid_step(grid), math.prod(grid)

    @pl.when(step == 0)
    def _():
        ride.start(ins, outs, sems)

    @pl.when(step == (3 * n_steps) // 4)
    def _():
        ride.forward(ins, outs, sems)


def _ride_end(ride, grid, ins, outs, sems):
    @pl.when(_grid_step(grid) == math.prod(grid) - 1)
    def _():
        ride.finish(ins, outs, sems)


def _allgather(name, shards):
    ride = _GatherRide(shards)
    n = ride.n

    def body(*refs):
        ins, outs, sems = refs[:n], refs[n : 2 * n], refs[2 * n :]
        ride.start(ins, outs, sems)
        ride.forward(ins, outs, sems)
        ride.finish(ins, outs, sems)

    return pl.pallas_call(
        body,
        name=name,
        in_specs=[HBM_SPEC] * n,
        out_specs=[HBM_SPEC] * n,
        out_shape=ride.out_shape,
        scratch_shapes=ride.scratch,
    )(*shards)


def _matmul(name, grid, k_axis, pairs, outs, mode, residual=None, ride=None):
    n_pairs, n_out, has_res = len(pairs), len(outs), residual is not None
    k_steps = grid[k_axis] if k_axis is not None else 1
    n_acc = n_out if k_steps > 1 else 0
    n_ride = ride.n if ride is not None else 0

    def body(*refs):
        ins = refs[: 2 * n_pairs]
        pos = 2 * n_pairs
        res_ref = refs[pos] if has_res else None
        pos += int(has_res)
        ride_ins = refs[pos : pos + n_ride]
        pos += n_ride
        out_refs = refs[pos : pos + n_out]
        pos += n_out
        ride_outs = refs[pos : pos + n_ride]
        pos += n_ride
        acc_refs = refs[pos : pos + n_acc]
        ride_sems = refs[pos + n_acc :]
        if ride is not None:
            _ride_begin(ride, grid, ride_ins, ride_outs, ride_sems)

        def partial(o):
            tot = None
            for p in outs[o][3]:
                d = _dot(ins[2 * p][...], ins[2 * p + 1][...], mode)
                tot = d if tot is None else tot + d
            return tot

        if k_steps == 1:
            for o in range(n_out):
                val = partial(o)
                if has_res and o == 0:
                    val = val + res_ref[...]
                out_refs[o][...] = val.astype(out_refs[o].dtype)
        else:
            k = pl.program_id(k_axis)

            @pl.when(k == 0)
            def _():
                for o in range(n_out):
                    if has_res and o == 0:
                        acc_refs[o][...] = res_ref[...]
                    else:
                        acc_refs[o][...] = jnp.zeros_like(acc_refs[o])

            for o in range(n_out):
                acc_refs[o][...] += partial(o)

            @pl.when(k == k_steps - 1)
            def _():
                for o in range(n_out):
                    out_refs[o][...] = acc_refs[o][...].astype(out_refs[o].dtype)

        if ride is not None:
            _ride_end(ride, grid, ride_ins, ride_outs, ride_sems)

    args, in_specs = [], []
    for a, a_spec, b, b_spec in pairs:
        args += [a, b]
        in_specs += [a_spec, b_spec]
    if has_res:
        args.append(residual[0])
        in_specs.append(residual[1])
    out_specs = [o[2] for o in outs]
    out_shape = [jax.ShapeDtypeStruct(o[0], o[1]) for o in outs]
    scratch = [pltpu.VMEM(_block_dims(o[2]), F32) for o in outs] if k_steps > 1 else []
    if ride is not None:
        args += ride.inputs
        in_specs += [HBM_SPEC] * n_ride
        out_specs += [HBM_SPEC] * n_ride
        out_shape += ride.out_shape
        scratch += ride.scratch
    return pl.pallas_call(
        body,
        name=name,
        grid=grid,
        in_specs=in_specs,
        out_specs=out_specs,
        out_shape=out_shape,
        scratch_shapes=scratch,
        compiler_params=_params(len(grid)),
    )(*args)


def _out_tiles(m, n, out_dtype):
    if out_dtype == F32 and m >= TM:
        return TM // 2, n
    return min(TM, m), min(TN, n)


def _mm_nn(name, a, w, out_dtype, residual=None):
    m, k = a.shape
    n = w.shape[1]
    tm, tn = _out_tiles(m, n, out_dtype)
    pairs = [(a, pl.BlockSpec((tm, k), lambda i, j: (i, 0)), w, pl.BlockSpec((k, tn), lambda i, j: (0, j)))]
    o_spec = pl.BlockSpec((tm, tn), lambda i, j: (i, j))
    res = (residual, o_spec) if residual is not None else None
    return _matmul(name, (m // tm, n // tn), None, pairs, [((m, n), out_dtype, o_spec, [0])], "nn", res)[0]


def _mm_nt(name, pairs_in, out_dtype):
    m, k = pairs_in[0][0].shape
    n = pairs_in[0][1].shape[0]
    tm, tn = _out_tiles(m, n, out_dtype)
    pairs = [
        (a, pl.BlockSpec((tm, k), lambda i, j: (i, 0)), w, pl.BlockSpec((tn, k), lambda i, j: (j, 0)))
        for a, w in pairs_in
    ]
    o_spec = pl.BlockSpec((tm, tn), lambda i, j: (i, j))
    outs = [((m, n), out_dtype, o_spec, list(range(len(pairs))))]
    return _matmul(name, (m // tm, n // tn), None, pairs, outs, "nt")[0]


def _mm_tn(name, a, b):
    s, i_dim = a.shape
    n = b.shape[1]
    ts, ti, tn = min(TK, s), min(TN, i_dim), min(TN, n)
    pairs = [(a, pl.BlockSpec((ts, ti), lambda i, j, k: (k, i)), b, pl.BlockSpec((ts, tn), lambda i, j, k: (k, j)))]
    outs = [((i_dim, n), BF16, pl.BlockSpec((ti, tn), lambda i, j, k: (i, j)), [0])]
    return _matmul(name, (i_dim // ti, n // tn, s // ts), 2, pairs, outs, "tn")[0]


def _mm_proj(h1, w_in, ride):
    s, d = h1.shape
    g = w_in.shape[1] // GROUP_W
    tm = min(TM, s)
    pairs = [(h1, pl.BlockSpec((tm, d), lambda i, j: (i, 0)), w_in, pl.BlockSpec((d, GROUP_W), lambda i, j: (0, j)))]
    outs = [((g, s, GROUP_W), BF16, pl.BlockSpec((None, tm, GROUP_W), lambda i, j: (j, i, 0)), [0])]
    return _matmul("mm_proj", (s // tm, g), None, pairs, outs, "nn", ride=ride)


def _mm_gate_up(h3, wg_g, wu_g, ride):
    s, d = h3.shape
    f = wg_g.shape[2]
    tm = min(TM, s)
    a_spec = pl.BlockSpec((tm, d), lambda i, j: (i, 0))
    w_spec = pl.BlockSpec((None, d, f), lambda i, j: (j, 0, 0))
    o_spec = pl.BlockSpec((None, tm, f), lambda i, j: (j, i, 0))
    pairs = [(h3, a_spec, wg_g, w_spec), (h3, a_spec, wu_g, w_spec)]
    outs = [((N_DEV, s, f), BF16, o_spec, [0]), ((N_DEV, s, f), BF16, o_spec, [1])]
    return _matmul("mm_gate_up", (s // tm, N_DEV), None, pairs, outs, "nn", ride=ride)


def _mm_dact(dx3b, wd_g):
    s, d = dx3b.shape
    f = wd_g.shape[1]
    tm = min(TM, s)
    pairs = [(dx3b, pl.BlockSpec((tm, d), lambda i, j: (i, 0)), wd_g, pl.BlockSpec((None, f, d), lambda i, j: (j, 0, 0)))]
    outs = [((N_DEV, s, f), BF16, pl.BlockSpec((None, tm, f), lambda i, j: (j, i, 0)), [0])]
    return _matmul("mm_dact", (s // tm, N_DEV), None, pairs, outs, "nt")[0]


def _mm_dwdown(act, dx3b):
    _, s, f = act.shape
    d = dx3b.shape[1]
    ts, tn = min(TK, s), min(TN, d)
    pairs = [
        (
            act,
            pl.BlockSpec((None, ts, f), lambda j, n, k: (j, k, 0)),
            dx3b,
            pl.BlockSpec((ts, tn), lambda j, n, k: (k, n)),
        )
    ]
    outs = [((N_DEV, f, d), BF16, pl.BlockSpec((None, f, tn), lambda j, n, k: (j, 0, n)), [0])]
    return _matmul("mm_dwdown", (N_DEV, d // tn, s // ts), 2, pairs, outs, "tn")[0]


def _mm_dh3(dapre, dup, wg_g, wu_g, ride):
    _, s, f = dapre.shape
    d = wg_g.shape[1]
    tm, tn = min(TM, s), min(TN, d)
    a_spec = pl.BlockSpec((None, tm, f), lambda i, j, k: (k, i, 0))
    w_spec = pl.BlockSpec((None, tn, f), lambda i, j, k: (k, j, 0))
    pairs = [(dapre, a_spec, wg_g, w_spec), (dup, a_spec, wu_g, w_spec)]
    outs = [((s, d), F32, pl.BlockSpec((tm, tn), lambda i, j, k: (i, j)), [0, 1])]
    return _matmul("mm_dh3", (s // tm, d // tn, N_DEV), 2, pairs, outs, "nt", ride=ride)


def _mm_dw_cols(name, h, dcols_list):
    s, d = h.shape
    f = dcols_list[0].shape[2]
    ts, ti = min(TK, s), min(TN, d)
    a_spec = pl.BlockSpec((ts, ti), lambda j, i, k: (k, i))
    b_spec = pl.BlockSpec((None, ts, f), lambda j, i, k: (j, k, 0))
    o_spec = pl.BlockSpec((None, ti, f), lambda j, i, k: (j, i, 0))
    pairs = [(h, a_spec, dc, b_spec) for dc in dcols_list]
    outs = [((N_DEV, d, f), BF16, o_spec, [p]) for p in range(len(pairs))]
    return _matmul(name, (N_DEV, d // ti, s // ts), 2, pairs, outs, "tn")


def _mm_dwin(h1, dproj, groups, ride):
    s, d = h1.shape
    ts, ti = min(TK, s), min(TN, d)
    pairs = [
        (
            h1,
            pl.BlockSpec((ts, ti), lambda j, i, k: (k, i)),
            dproj,
            pl.BlockSpec((None, ts, GROUP_W), lambda j, i, k: (j, k, 0)),
        )
    ]
    outs = [((d, groups * GROUP_W), BF16, pl.BlockSpec((ti, GROUP_W), lambda j, i, k: (i, j)), [0])]
    return _matmul("mm_dwin", (groups, d // ti, s // ts), 2, pairs, outs, "tn", ride=ride)


def _mm_dh1(dproj, w_in, groups, ride):
    s = dproj.shape[1]
    d = w_in.shape[0]
    tm, tn = min(TM, s), min(TN, d)
    pairs = [
        (
            dproj,
            pl.BlockSpec((None, tm, GROUP_W), lambda i, j, k: (k, i, 0)),
            w_in,
            pl.BlockSpec((tn, GROUP_W), lambda i, j, k: (j, k)),
        )
    ]
    outs = [((s, d), F32, pl.BlockSpec((tm, tn), lambda i, j, k: (i, j)), [0])]
    return _matmul("mm_dh1", (s // tm, d // tn, groups), 2, pairs, outs, "nt", ride=ride)


def _rows_to_8(v):
    t, d = v.shape
    return jnp.sum(v.reshape(t // 8, 8, d), axis=0)


def _rmsnorm(name, x, w, ride=None):
    s, d = x.shape
    t = min(TS, s)
    grid = (s // t,)

    def compute(x_ref, w_ref, o_ref):
        xv = x_ref[...]
        r = lax.rsqrt(jnp.mean(xv * xv, axis=-1, keepdims=True) + EPS)
        o_ref[...] = (xv * r * w_ref[...]).astype(o_ref.dtype)

    in_specs = [pl.BlockSpec((t, d), lambda i: (i, 0)), pl.BlockSpec((1, d), lambda i: (0, 0))]
    out_spec = pl.BlockSpec((t, d), lambda i: (i, 0))
    out_shape = jax.ShapeDtypeStruct((s, d), BF16)
    if ride is None:
        return pl.pallas_call(compute, name=name, grid=grid, in_specs=in_specs, out_specs=out_spec, out_shape=out_shape, compiler_params=_params(1))(x, w)
    return pl.pallas_call(
        _hosted(compute, 2, 1, 0, ride, grid),
        name=name,
        grid=grid,
        in_specs=in_specs + [HBM_SPEC] * ride.n,
        out_specs=[out_spec] + [HBM_SPEC] * ride.n,
        out_shape=[out_shape] + ride.out_shape,
        scratch_shapes=ride.scratch,
        compiler_params=_params(1),
    )(x, w, *ride.inputs)


def _rmsnorm_bwd(name, dh, x, w, res=None):
    s, d = x.shape
    t = min(TR, s)
    steps = s // t
    has_res = res is not None

    def body(*refs):
        if has_res:
            dh_ref, x_ref, w_ref, res_ref, dx_ref, dxb_ref, dw_ref, acc = refs
        else:
            dh_ref, x_ref, w_ref, dx_ref, dxb_ref, dw_ref, acc = refs
        i = pl.program_id(0)
        xv = x_ref[...]
        r = lax.rsqrt(jnp.mean(xv * xv, axis=-1, keepdims=True) + EPS)
        xn = xv * r
        dhv = dh_ref[...].astype(F32)
        dxn = dhv * w_ref[...]
        dx = r * (dxn - xn * jnp.mean(dxn * xn, axis=-1, keepdims=True))
        if has_res:
            dx = dx + res_ref[...]
        dx_ref[...] = dx
        dxb_ref[...] = dx.astype(BF16)

        @pl.when(i == 0)
        def _():
            acc[...] = jnp.zeros_like(acc)

        acc[...] += _rows_to_8(dhv * xn)

        @pl.when(i == steps - 1)
        def _():
            dw_ref[...] = jnp.sum(acc[...], axis=0, keepdims=True)

    row = pl.BlockSpec((t, d), lambda i: (i, 0))
    vec = pl.BlockSpec((1, d), lambda i: (0, 0))
    args = [dh, x, w] + ([res] if has_res else [])
    return pl.pallas_call(
        body,
        name=name,
        grid=(steps,),
        in_specs=[row, row, vec] + ([row] if has_res else []),
        out_specs=[row, row, vec],
        out_shape=[
            jax.ShapeDtypeStruct((s, d), F32),
            jax.ShapeDtypeStruct((s, d), BF16),
            jax.ShapeDtypeStruct((1, d), F32),
        ],
        scratch_shapes=[pltpu.VMEM((8, d), F32)],
        compiler_params=_params(1),
    )(*args)


def _loss_head(x3, target, w):
    s, d = x3.shape
    t = min(TR, s)
    steps = s // t

    def body(x_ref, t_ref, w_ref, dx_ref, dxb_ref, loss_ref, dw_ref, acc_l, acc_w):
        i = pl.program_id(0)
        xv = x_ref[...]
        wv = w_ref[...]
        r = lax.rsqrt(jnp.mean(xv * xv, axis=-1, keepdims=True) + EPS)
        xn = xv * r
        err = xn * wv - t_ref[...]
        dy = err * (1.0 / d)
        dxn = dy * wv
        dx = r * (dxn - xn * jnp.mean(dxn * xn, axis=-1, keepdims=True))
        dx_ref[...] = dx
        dxb_ref[...] = dx.astype(BF16)

        @pl.when(i == 0)
        def _():
            acc_l[...] = jnp.zeros_like(acc_l)
            acc_w[...] = jnp.zeros_like(acc_w)

        acc_l[...] += _rows_to_8(err * err)
        acc_w[...] += _rows_to_8(dy * xn)

        @pl.when(i == steps - 1)
        def _():
            tot = jnp.sum(jnp.sum(acc_l[...], axis=0, keepdims=True), axis=1, keepdims=True) * (0.5 / d)
            loss_ref[...] = jnp.broadcast_to(tot, loss_ref.shape)
            dw_ref[...] = jnp.sum(acc_w[...], axis=0, keepdims=True)

    row = pl.BlockSpec((t, d), lambda i: (i, 0))
    vec = pl.BlockSpec((1, d), lambda i: (0, 0))
    return pl.pallas_call(
        body,
        name="loss_head",
        grid=(steps,),
        in_specs=[row, row, vec],
        out_specs=[row, row, pl.BlockSpec((1, 128), lambda i: (0, 0)), vec],
        out_shape=[
            jax.ShapeDtypeStruct((s, d), F32),
            jax.ShapeDtypeStruct((s, d), BF16),
            jax.ShapeDtypeStruct((1, 128), F32),
            jax.ShapeDtypeStruct((1, d), F32),
        ],
        scratch_shapes=[pltpu.VMEM((8, d), F32), pltpu.VMEM((8, d), F32)],
        compiler_params=_params(1),
    )(x3, target, w)


def _lb_fwd(hgrn_lb):
    def body(p_ref, lb_ref):
        p = p_ref[...]
        m = jnp.max(p, axis=0, keepdims=True)
        e = jnp.exp(p - m)
        lb_ref[...] = e[0:1, :] / jnp.sum(e, axis=0, keepdims=True)

    return pl.pallas_call(body, name="lb_fwd", out_shape=jax.ShapeDtypeStruct((1, hgrn_lb.shape[1]), F32))(hgrn_lb)


def _lb_bwd(hgrn_lb, dlb):
    def body(p_ref, d_ref, o_ref):
        p = p_ref[...]
        m = jnp.max(p, axis=0, keepdims=True)
        e = jnp.exp(p - m)
        sm = e / jnp.sum(e, axis=0, keepdims=True)
        p0 = sm[0:1, :]
        row = lax.broadcasted_iota(jnp.int32, sm.shape, 0)
        o_ref[...] = d_ref[...] * p0 * (jnp.where(row == 0, 1.0, 0.0) - sm)

    return pl.pallas_call(body, name="lb_bwd", out_shape=jax.ShapeDtypeStruct(hgrn_lb.shape, F32))(hgrn_lb, dlb)


def _split3(v):
    hi = v.astype(BF16)
    r1 = v - hi.astype(F32)
    mid = r1.astype(BF16)
    lo = (r1 - mid.astype(F32)).astype(BF16)
    return hi, mid, lo


def _pair(v):
    hi = v.astype(BF16)
    return hi, (v - hi.astype(F32)).astype(BF16)


def _dot_pairs(a2, b2, mode):
    dims = _DIMS[mode]
    out = lax.dot_general(a2[0], b2[0], dims, preferred_element_type=F32)
    if a2[1] is not None:
        out = lax.dot_general(a2[1], b2[0], dims, preferred_element_type=F32) + out
    if b2[1] is not None:
        out = lax.dot_general(a2[0], b2[1], dims, preferred_element_type=F32) + out
    return out


def _tri_dot(tri, v):
    hi, mid, lo = _split3(v)
    dims = _DIMS["nn"]
    out = lax.dot_general(tri, lo, dims, preferred_element_type=F32)
    out = out + lax.dot_general(tri, mid, dims, preferred_element_type=F32)
    return out + lax.dot_general(tri, hi, dims, preferred_element_type=F32)


def _chunk_gates(qp, fp, lbv):
    sf = _sigmoid(fp)
    f = lbv + (1.0 - lbv) * sf
    k = 1.0 - f
    sq = _sigmoid(qp)
    qf = qp * sq
    return sf, f, k, sq, qf, jnp.log(f)


def _chunk_decays(b):
    row = lax.broadcasted_iota(jnp.int32, b.shape, 0)
    bm = jnp.sum(jnp.where(row == CHUNK // 2 - 1, b, 0.0), axis=0, keepdims=True)
    bl = jnp.sum(jnp.where(row == CHUNK - 1, b, 0.0), axis=0, keepdims=True)
    return jnp.exp(b), jnp.exp(b - bm), jnp.exp(bm - b), jnp.exp(bl - b), jnp.exp(bl)


def _emit_pipelined(tasks, n_stages):
    tasks = list(tasks)
    for step in range(len(tasks) + n_stages - 1):
        for t in range(max(step - n_stages + 1, 0), min(step, len(tasks) - 1) + 1):
            next(tasks[t], None)


def _hosted(compute, n_in, n_out, n_scratch, ride, grid):
    n_ride = ride.n

    def body(*refs):
        ins, ride_ins = refs[:n_in], refs[n_in : n_in + n_ride]
        pos = n_in + n_ride
        outs, ride_outs = refs[pos : pos + n_out], refs[pos + n_out : pos + n_out + n_ride]
        pos += n_out + n_ride
        scratch, sems = refs[pos : pos + n_scratch], refs[pos + n_scratch :]
        _ride_begin(ride, grid, ride_ins, ride_outs, sems)
        compute(*ins, *outs, *scratch)
        _ride_end(ride, grid, ride_ins, ride_outs, sems)

    return body


def _hgrn_fwd(proj, lb, nw, ride):
    s = proj.shape[1]
    t = min(TS, s)
    nt, nch = s // t, t // CHUNK
    h_, k_, hb = HGRN_HEADS, HEAD_DIM, HEADS_PER_STEP

    def compute(q_ref, f_ref, i_ref, g_ref, lb_ref, nw_ref, mix_ref, o_ref, st_ref, state):
        @pl.when(pl.program_id(1) == 0)
        def _():
            state[...] = jnp.zeros_like(state)

        nwv = nw_ref[...]
        rr = lax.broadcasted_iota(jnp.int32, (CHUNK, CHUNK), 0)
        cc = lax.broadcasted_iota(jnp.int32, (CHUNK, CHUNK), 1)
        causal = rr >= cc
        tri = jnp.where(causal, 1.0, 0.0).astype(BF16)

        st = [state[hh] for hh in range(hb)]

        def task(c, hh):
            rows, lanes = slice(c * CHUNK, (c + 1) * CHUNK), slice(hh * k_, (hh + 1) * k_)
            qp, fp, gp = q_ref[rows, lanes].astype(F32), f_ref[rows, lanes].astype(F32), g_ref[rows, lanes].astype(F32)
            vb = i_ref[rows, lanes].astype(BF16)
            _, _, k, _, qf, logf = _chunk_gates(qp, fp, lb_ref[:, lanes])
            yield
            b = _tri_dot(tri, logf)
            yield
            e_b, e_qm, e_km, e_kl, e_l = _chunk_decays(b)
            qm, km, qe, kl = (qf * e_qm).astype(BF16), (k * e_km).astype(BF16), (qf * e_b).astype(BF16), (k * e_kl).astype(BF16)
            yield
            a = jnp.where(causal, _dot(qm, km, "nt"), 0.0).astype(BF16)
            update = _dot(vb, kl, "tn")
            yield
            st_ref[c, hh] = st[hh]
            o = _dot(qe, st[hh], "nt") + _dot(a, vb, "nn")
            st[hh] = st[hh] * e_l + update
            yield
            o_ref[rows, lanes] = o
            on = o * lax.rsqrt(jnp.mean(o * o, axis=-1, keepdims=True) + EPS) * nwv
            mix_ref[rows, lanes] = (on * (gp * _sigmoid(gp))).astype(BF16)

        _emit_pipelined((task(c, hh) for c in range(nch) for hh in range(hb)), 6)
        for hh in range(hb):
            state[hh] = st[hh]

    def col(g):
        return pl.BlockSpec((None, t, hb * k_), lambda h, i: (g, i, h))

    grid = (h_ // hb, nt)
    return pl.pallas_call(
        _hosted(compute, 6, 3, 1, ride, grid),
        name="hgrn_fwd",
        grid=grid,
        in_specs=[
            col(0),
            col(1),
            col(2),
            col(3),
            pl.BlockSpec((1, hb * k_), lambda h, i: (0, h)),
            pl.BlockSpec((1, k_), lambda h, i: (0, 0)),
        ]
        + [HBM_SPEC] * ride.n,
        out_specs=[
            pl.BlockSpec((t, hb * k_), lambda h, i: (i, h)),
            pl.BlockSpec((t, hb * k_), lambda h, i: (i, h)),
            pl.BlockSpec((nch, hb, k_, k_), lambda h, i: (i, h, 0, 0)),
        ]
        + [HBM_SPEC] * ride.n,
        out_shape=[
            jax.ShapeDtypeStruct((s, 2 * h_ * k_), BF16),
            jax.ShapeDtypeStruct((s, h_ * k_), F32),
            jax.ShapeDtypeStruct((s // CHUNK, h_, k_, k_), F32),
        ]
        + ride.out_shape,
        scratch_shapes=[pltpu.VMEM((hb, k_, k_), F32)] + ride.scratch,
        compiler_params=_params(2),
    )(proj, proj, proj, proj, lb, nw, *ride.inputs)


def _hgrn_bwd(proj, lb, nw, o_pre, states, dmix, ride):
    s = proj.shape[1]
    t = min(TS, s)
    nt, nch = s // t, t // CHUNK
    h_, k_, hb = HGRN_HEADS, HEAD_DIM, HEADS_PER_STEP

    def compute(q_ref, f_ref, i_ref, g_ref, lb_ref, nw_ref, o_ref, st_ref, dm_ref, dp_ref, dlb_ref, dnw_ref, dstate, acc_lb, acc_nw):
        h, i = pl.program_id(0), pl.program_id(1)
        dq_ref, df_ref, di_ref, dg_ref = dp_ref.at[0], dp_ref.at[1], dp_ref.at[2], dp_ref.at[3]

        @pl.when(i == 0)
        def _():
            dstate[...] = jnp.zeros_like(dstate)
            acc_lb[...] = jnp.zeros_like(acc_lb)

        @pl.when((i == 0) & (h == 0))
        def _():
            acc_nw[...] = jnp.zeros_like(acc_nw)

        nwv = nw_ref[...]
        rr = lax.broadcasted_iota(jnp.int32, (CHUNK, CHUNK), 0)
        cc = lax.broadcasted_iota(jnp.int32, (CHUNK, CHUNK), 1)
        causal = rr >= cc
        tri = jnp.where(causal, 1.0, 0.0).astype(BF16)
        tri_t =jnp.where(cc >= rr, 1.0, 0.0).astype(BF16)
        last_row = lax.broadcasted_iota(jnp.int32, (CHUNK, k_), 0) == CHUNK - 1

        dst = [dstate[hh] for hh in range(hb)]
        sum_nw = [jnp.zeros((8, k_), F32)]
        sum_lb = [jnp.zeros((8, k_), F32) for _ in range(hb)]

        def task(c, hh):
            rows, lanes = slice(c * CHUNK, (c + 1) * CHUNK), slice(hh * k_, (hh + 1) * k_)
            lbv = lb_ref[:, lanes]
            qp, fp, gp = q_ref[rows, lanes].astype(F32), f_ref[rows, lanes].astype(F32), g_ref[rows, lanes].astype(F32)
            vb = i_ref[rows, lanes].astype(BF16)
            sf, f, k, sq, qf, logf = _chunk_gates(qp, fp, lbv)
            yield
            b = _tri_dot(tri, logf)
            dm = dm_ref[rows, lanes]
            o = o_ref[rows, lanes]
            rinv = lax.rsqrt(jnp.mean(o * o, axis=-1, keepdims=True) + EPS)
            ohat = o * rinv
            sg = _sigmoid(gp)
            dg_ref[rows, lanes] = (dm * (ohat * nwv) * (sg * (1.0 + gp * (1.0 - sg)))).astype(BF16)
            don = dm * (gp * sg)
            sum_nw[0] = sum_nw[0] + _rows_to_8(don * ohat)
            dohat = don * nwv
            do = rinv * (dohat - ohat * jnp.mean(dohat * ohat, axis=-1, keepdims=True))
            yield
            e_b, e_qm, e_km, e_kl, e_l = _chunk_decays(b)
            st0 = st_ref[c, hh]
            do2 = _pair(do)
            qm2, km2 = _pair(qf * e_qm), _pair(k * e_km)
            st02, qe2 = _pair(st0), _pair(qf * e_b)
            yield
            a = jnp.where(causal, _dot(qm2[0], km2[0], "nt"), 0.0)
            da2 = _pair(jnp.where(causal, _dot_pairs(do2, (vb, None), "nt"), 0.0))
            dq_state = _dot_pairs(do2, st02, "nn")
            update = _dot_pairs(do2, qe2, "tn")
            yield
            dst2 = _pair(dst[hh])
            dq = e_b * dq_state + e_qm * _dot_pairs(da2, km2, "nn")
            dk_state = e_kl * _dot_pairs((vb, None), dst2, "nn")
            dk = e_km * _dot_pairs(da2, qm2, "tn") + dk_state
            dv = _dot(a, do2[0], "tn") + _dot(k * e_kl, dst2[0], "nt")
            d_bl = jnp.sum(k * dk_state, axis=0, keepdims=True) + e_l * jnp.sum(st0 * dst[hh], axis=0, keepdims=True)
            dst[hh] = dst[hh] * e_l + update
            yield
            dlogf = _tri_dot(tri_t, qf * dq - k * dk + jnp.where(last_row, d_bl, 0.0))
            yield
            dfv = dlogf / f - dk
            df_ref[rows, lanes] = (dfv * (1.0 - lbv) * (sf * (1.0 - sf))).astype(BF16)
            sum_lb[hh] = sum_lb[hh] + _rows_to_8(dfv * (1.0 - sf))
            dq_ref[rows, lanes] = (dq * (sq * (1.0 + qp * (1.0 - sq)))).astype(BF16)
            di_ref[rows, lanes] = dv.astype(BF16)

        _emit_pipelined((task(c, hh) for c in reversed(range(nch)) for hh in range(hb)), 7)
        acc_nw[...] += sum_nw[0]
        for hh in range(hb):
            dstate[hh] = dst[hh]
            acc_lb[:, hh * k_ : (hh + 1) * k_] += sum_lb[hh]

        @pl.when(i == nt - 1)
        def _():
            dlb_ref[...] = jnp.sum(acc_lb[...], axis=0, keepdims=True)

        @pl.when((i == nt - 1) & (h == h_ // hb - 1))
        def _():
            dnw_ref[...] = jnp.sum(acc_nw[...], axis=0, keepdims=True)

    def col(g):
        return pl.BlockSpec((None, t, hb * k_), lambda h, i: (g, nt - 1 - i, h))

    row = pl.BlockSpec((t, hb * k_), lambda h, i: (nt - 1 - i, h))
    grid = (h_ // hb, nt)
    return pl.pallas_call(
        _hosted(compute, 9, 3, 3, ride, grid),
        name="hgrn_bwd",
        grid=grid,
        in_specs=[
            col(0),
            col(1),
            col(2),
            col(3),
            pl.BlockSpec((1, hb * k_), lambda h, i: (0, h)),
            pl.BlockSpec((1, k_), lambda h, i: (0, 0)),
            row,
            pl.BlockSpec((nch, hb, k_, k_), lambda h, i: (nt - 1 - i, h, 0, 0)),
            row,
        ]
        + [HBM_SPEC] * ride.n,
        out_specs=[
            pl.BlockSpec((4, t, hb * k_), lambda h, i: (0, nt - 1 - i, h)),
            pl.BlockSpec((1, hb * k_), lambda h, i: (0, h)),
            pl.BlockSpec((1, k_), lambda h, i: (0, 0)),
        ]
        + [HBM_SPEC] * ride.n,
        out_shape=[
            jax.ShapeDtypeStruct((8, s, h_ * k_), BF16),
            jax.ShapeDtypeStruct((1, h_ * k_), F32),
            jax.ShapeDtypeStruct((1, k_), F32),
        ]
        + ride.out_shape,
        scratch_shapes=[pltpu.VMEM((hb, k_, k_), F32), pltpu.VMEM((8, hb * k_), F32), pltpu.VMEM((8, k_), F32)] + ride.scratch,
        compiler_params=_params(2),
    )(proj, proj, proj, proj, lb, nw, o_pre, states, dmix, *ride.inputs)


def _shift_rows(ext, k):
    n = ext.shape[0]
    return pltpu.roll(ext, k % n, axis=0)


def _row_select(parts):
    w = parts[0].shape[1]
    row = lax.broadcasted_iota(jnp.int32, (8, w), 0)
    out = jnp.zeros((8, w), F32)
    for r, p in enumerate(parts):
        out = out + jnp.where(row == r, p, 0.0)
    return out


def _sconv_fwd(proj, sw, mix):
    s = proj.shape[1]
    t = min(TS, s)
    nt, w = s // t, 512
    ncol, hb = GROUP_W // w, t // HALO

    def body(cb_ref, cc_ref, ch_ref, ccp_ref, chp_ref, sw_ref, mix_in, mix_ref):
        del mix_in
        i = pl.program_id(0)
        u = cc_ref[...].astype(F32) * ch_ref[...].astype(F32)
        up = jnp.where(i > 0, ccp_ref[...].astype(F32) * chp_ref[...].astype(F32), 0.0)
        ext = jnp.concatenate([up, u], axis=0)
        z = sw_ref[0:1, :] * _shift_rows(ext, 2)[HALO:] + sw_ref[1:2, :] * _shift_rows(ext, 1)[HALO:] + sw_ref[2:3, :] * u
        mix_ref[...] = (cb_ref[...].astype(F32) * z).astype(BF16)

    def cur(g):
        return pl.BlockSpec((None, t, w), lambda i, c: (g, i, c))

    def prev(g):
        return pl.BlockSpec((None, HALO, w), lambda i, c: (g, jnp.maximum(i * hb - 1, 0), c))

    return pl.pallas_call(
        body,
        name="sconv_fwd",
        grid=(nt, ncol),
        in_specs=[cur(4), cur(5), cur(6), prev(5), prev(6), pl.BlockSpec((3, w), lambda i, c: (0, c)), pl.BlockSpec(memory_space=pl.ANY)],
        out_specs=pl.BlockSpec((t, w), lambda i, c: (i, ncol + c)),
        out_shape=jax.ShapeDtypeStruct(mix.shape, mix.dtype),
        input_output_aliases={6: 0},
        compiler_params=_params(2),
    )(proj, proj, proj, proj, proj, sw, mix)


def _sconv_bwd(proj, sw, dmix, dproj):
    s = proj.shape[1]
    t = min(TS, s)
    nt, w = s // t, 512
    ncol, hb = GROUP_W // w, t // HALO

    def body(cb_ref, cbn_ref, cc_ref, ccp_ref, ch_ref, chp_ref, dy_ref, dyn_ref, sw_ref, dp_in, dp_ref, dsw_ref, acc):
        del dp_in
        i = pl.program_id(1)
        w0, w1, w2 = sw_ref[0:1, :], sw_ref[1:2, :], sw_ref[2:3, :]
        ccv, chv, cbv, dyv = cc_ref[...].astype(F32), ch_ref[...].astype(F32), cb_ref[...].astype(F32), dy_ref[...]
        u = ccv * chv
        up = jnp.where(i > 0, ccp_ref[...].astype(F32) * chp_ref[...].astype(F32), 0.0)
        ext = jnp.concatenate([up, u], axis=0)
        u1, u2 = _shift_rows(ext, 1)[HALO:], _shift_rows(ext, 2)[HALO:]
        z = w0 * u2 + w1 * u1 + w2 * u
        dz = dyv * cbv
        dzn = jnp.where(i < nt - 1, dyn_ref[...] * cbn_ref[...].astype(F32), 0.0)
        dze = jnp.concatenate([dz, dzn], axis=0)
        du = w2 * dz + w1 * _shift_rows(dze, -1)[:t] + w0 * _shift_rows(dze, -2)[:t]
        dp_ref[0] = (dyv * z).astype(BF16)
        dp_ref[1] = (du * chv).astype(BF16)
        dp_ref[2] = (du * ccv).astype(BF16)
        dp_ref[3] = jnp.zeros((t, w), BF16)

        @pl.when(i == 0)
        def _():
            acc[...] = jnp.zeros_like(acc)

        acc[...] += _row_select([jnp.sum(dz * uu, axis=0, keepdims=True) for uu in (u2, u1, u)])

        @pl.when(i == nt - 1)
        def _():
            dsw_ref[...] = acc[...]

    def cur(g):
        return pl.BlockSpec((None, t, w), lambda c, i: (g, i, c))

    def prev(g):
        return pl.BlockSpec((None, HALO, w), lambda c, i: (g, jnp.maximum(i * hb - 1, 0), c))

    def nxt(g):
        return pl.BlockSpec((None, HALO, w), lambda c, i: (g, jnp.minimum((i + 1) * hb, s // HALO - 1), c))

    return pl.pallas_call(
        body,
        name="sconv_bwd",
        grid=(ncol, nt),
        in_specs=[
            cur(4),
            nxt(4),
            cur(5),
            prev(5),
            cur(6),
            prev(6),
            pl.BlockSpec((t, w), lambda c, i: (i, ncol + c)),
            pl.BlockSpec((HALO, w), lambda c, i: (jnp.minimum((i + 1) * hb, s // HALO - 1), ncol + c)),
            pl.BlockSpec((3, w), lambda c, i: (0, c)),
            pl.BlockSpec(memory_space=pl.ANY),
        ],
        out_specs=[pl.BlockSpec((4, t, w), lambda c, i: (1, i, c)), pl.BlockSpec((8, w), lambda c, i: (0, c))],
        out_shape=[jax.ShapeDtypeStruct(dproj.shape, dproj.dtype), jax.ShapeDtypeStruct((8, GROUP_W), F32)],
        scratch_shapes=[pltpu.VMEM((8, w), F32)],
        input_output_aliases={9: 0},
        compiler_params=_params(2),
    )(proj, proj, proj, proj, proj, proj, dmix, dmix, sw, dproj)


def _ffn_down(a_pre, up, cw, cb, wd_g, x2):
    nd, s, f = a_pre.shape
    d = wd_g.shape[2]
    t = min(TS, s)
    nt, hb = s // t, t // HALO

    def body(a_ref, ap_ref, up_ref, cw_ref, cb_ref, w_ref, res_ref, o_ref, act_ref):
        i, j = pl.program_id(0), pl.program_id(1)
        av = a_ref[...].astype(F32)
        ext = jnp.concatenate([jnp.where(i > 0, ap_ref[...].astype(F32), 0.0), av], axis=0)
        a = cw_ref[0:1, :] * _shift_rows(ext, 2)[HALO:] + cw_ref[1:2, :] * _shift_rows(ext, 1)[HALO:] + cw_ref[2:3, :] * av
        a = a + cb_ref[...]
        act = (a * _sigmoid(a) * up_ref[...].astype(F32)).astype(BF16)
        act_ref[...] = act

        @pl.when(j == 0)
        def _():
            o_ref[...] = res_ref[...]

        o_ref[...] += _dot(act, w_ref[...], "nn")

    cur = pl.BlockSpec((None, t, f), lambda i, j: (j, i, 0))
    prev = pl.BlockSpec((None, HALO, f), lambda i, j: (j, jnp.maximum(i * hb - 1, 0), 0))
    row = pl.BlockSpec((t, d), lambda i, j: (i, 0))
    return pl.pallas_call(
        body,
        name="ffn_down",
        grid=(nt, nd),
        in_specs=[
            cur,
            prev,
            cur,
            pl.BlockSpec((None, 3, f), lambda i, j: (j, 0, 0)),
            pl.BlockSpec((None, 1, f), lambda i, j: (j, 0, 0)),
            pl.BlockSpec((None, f, d), lambda i, j: (j, 0, 0)),
            row,
        ],
        out_specs=[row, cur],
        out_shape=[jax.ShapeDtypeStruct((s, d), F32), jax.ShapeDtypeStruct(a_pre.shape, BF16)],
        compiler_params=_params(2),
    )(a_pre, a_pre, up, cw, cb, wd_g, x2)


def _ffn_act_bwd(a_pre, up, dact, cw, cb, ride):
    nd, s, f = a_pre.shape
    t = min(TS, s)
    nt, hb = s // t, t // HALO

    def compute(a_ref, ap_ref, an_ref, up_ref, upn_ref, da_ref, dan_ref, cw_ref, cb_ref, dap_ref, dup_ref, dsm_ref):
        i, j = pl.program_id(0), pl.program_id(1)
        w0, w1, w2 = cw_ref[0:1, :], cw_ref[1:2, :], cw_ref[2:3, :]
        av = a_ref[...].astype(F32)
        ext = jnp.concatenate([jnp.where(i > 0, ap_ref[...].astype(F32), 0.0), av, an_ref[...].astype(F32)], axis=0)
        e1, e2 = _shift_rows(ext, 1), _shift_rows(ext, 2)
        a = (w0 * e2 + w1 * e1 + w2 * ext)[HALO:] + cb_ref[...]
        sg = _sigmoid(a)
        dact_next = jnp.where(i < nt - 1, dan_ref[...], jnp.zeros_like(dan_ref))
        dact_e = jnp.concatenate([da_ref[...], dact_next], axis=0).astype(F32)
        up_e = jnp.concatenate([up_ref[...], upn_ref[...]], axis=0).astype(F32)
        d_a = dact_e * up_e * (sg * (1.0 + a * (1.0 - sg)))
        dup_ref[...] = (dact_e * (a * sg))[:t].astype(BF16)
        dap_ref[...] = (w2 * d_a + w1 * _shift_rows(d_a, -1) + w0 * _shift_rows(d_a, -2))[:t].astype(BF16)

        d_a_cur = d_a[:t]
        sums = [jnp.sum(d_a_cur * e[HALO : HALO + t], axis=0, keepdims=True) for e in (e2, e1, ext)]
        part = _row_select(sums + [jnp.sum(d_a_cur, axis=0, keepdims=True)])

        @pl.when(i == 0)
        def _():
            dsm_ref[j] = part

        @pl.when(i > 0)
        def _():
            dsm_ref[j] += part

    cur = pl.BlockSpec((None, t, f), lambda i, j: (j, i, 0))
    prev = pl.BlockSpec((None, HALO, f), lambda i, j: (j, jnp.maximum(i * hb - 1, 0), 0))
    nxt = pl.BlockSpec((None, HALO, f), lambda i, j: (j, jnp.minimum((i + 1) * hb, s // HALO - 1), 0))
    grid = (nt, nd)
    return pl.pallas_call(
        _hosted(compute, 9, 3, 0, ride, grid),
        name="ffn_act_bwd",
        grid=grid,
        in_specs=[cur, prev, nxt, cur, nxt, cur, nxt, pl.BlockSpec((None, 3, f), lambda i, j: (j, 0, 0)), pl.BlockSpec((None, 1, f), lambda i, j: (j, 0, 0))]
        + [HBM_SPEC] * ride.n,
        out_specs=[cur, cur, pl.BlockSpec((nd, 8, f), lambda i, j: (0, 0, 0))] + [HBM_SPEC] * ride.n,
        out_shape=[jax.ShapeDtypeStruct(a_pre.shape, BF16), jax.ShapeDtypeStruct(a_pre.shape, BF16), jax.ShapeDtypeStruct((nd, 8, f), F32)]
        + ride.out_shape,
        scratch_shapes=ride.scratch,
        compiler_params=_params(2),
    )(a_pre, a_pre, a_pre, up, up, dact, dact, cw, cb, *ride.inputs)


def _softmax_rows(sc):
    m = jnp.max(sc, axis=-1, keepdims=True)
    e = jnp.exp(sc - m)
    return e / jnp.sum(e, axis=-1, keepdims=True)


def _attn_fwd(q, km, vm):
    s, d = q.shape
    m = km.shape[0]
    t = min(TS, s)
    hd = d // MEM_HEADS
    scale = hd**-0.5

    def body(q_ref, k_ref, v_ref, o_ref):
        for h in range(MEM_HEADS):
            cols = slice(h * hd, (h + 1) * hd)
            p = _softmax_rows(_dot(q_ref[:, cols], k_ref[:, cols], "nt") * scale)
            o_ref[:, cols] = _dot(p, v_ref[:, cols], "nn").astype(BF16)

    row = pl.BlockSpec((t, d), lambda i: (i, 0))
    full = pl.BlockSpec((m, d), lambda i: (0, 0))
    return pl.pallas_call(
        body,
        name="attn_fwd",
        grid=(s // t,),
        in_specs=[row, full, full],
        out_specs=row,
        out_shape=jax.ShapeDtypeStruct((s, d), BF16),
        compiler_params=_params(1),
    )(q, km, vm)


def _attn_bwd(q, km, vm, do):
    s, d = q.shape
    m = km.shape[0]
    t = min(TS, s)
    steps = s // t
    hd = d // MEM_HEADS
    scale = hd**-0.5

    def body(q_ref, k_ref, v_ref, do_ref, dq_ref, dk_ref, dv_ref):
        @pl.when(pl.program_id(0) == 0)
        def _():
            dk_ref[...] = jnp.zeros_like(dk_ref)
            dv_ref[...] = jnp.zeros_like(dv_ref)

        for h in range(MEM_HEADS):
            cols = slice(h * hd, (h + 1) * hd)
            qh, kh, vh, doh = q_ref[:, cols], k_ref[:, cols], v_ref[:, cols], do_ref[:, cols]
            p = _softmax_rows(_dot(qh, kh, "nt") * scale)
            dp = _dot(doh, vh, "nt")
            ds = p * (dp - jnp.sum(dp * p, axis=-1, keepdims=True)) * scale
            dq_ref[:, cols] = _dot(ds, kh, "nn").astype(BF16)
            dk_ref[:, cols] += _dot(ds, qh, "tn")
            dv_ref[:, cols] += _dot(p, doh, "tn")

    row = pl.BlockSpec((t, d), lambda i: (i, 0))
    full = pl.BlockSpec((m, d), lambda i: (0, 0))
    return pl.pallas_call(
        body,
        name="attn_bwd",
        grid=(steps,),
        in_specs=[row, full, full, row],
        out_specs=[row, full, full],
        out_shape=[jax.ShapeDtypeStruct((s, d), BF16), jax.ShapeDtypeStruct((m, d), F32), jax.ShapeDtypeStruct((m, d), F32)],
        compiler_params=_params(1),
    )(q, km, vm, do)


def _row_tile(rows, cols):
    best = 8
    for tr in range(8, rows + 1, 8):
        if rows % tr == 0 and tr * cols * 4 <= (1 << 20):
            best = tr
    return best


def _adamw(w, g, m, v):
    m = ADAM_B1 * m + (1.0 - ADAM_B1) * g
    v = ADAM_B2 * v + (1.0 - ADAM_B2) * (g * g)
    m_hat = m / (1.0 - ADAM_B1**ADAM_STEP)
    v_hat = v / (1.0 - ADAM_B2**ADAM_STEP)
    delta = -ADAM_LR * (m_hat / (jnp.sqrt(v_hat) + ADAM_EPS) + ADAM_WD * w)
    return delta, m, v


def _adam_shard(name, parts, w, m, v):
    rows, cols = w.shape
    tr = _row_tile(rows, cols)

    def body(*refs):
        part_refs = refs[:N_DEV]
        w_ref, m_ref, v_ref, go_ref, d_ref, mo_ref, vo_ref = refs[N_DEV:]
        g = part_refs[0][...].astype(F32)
        for dev in range(1, N_DEV):
            g = g + part_refs[dev][...].astype(F32)
        go_ref[...] = g
        d_ref[...], mo_ref[...], vo_ref[...] = _adamw(w_ref[...], g, m_ref[...], v_ref[...])

    def part(dev):
        return pl.BlockSpec((None, tr, cols), lambda i: (dev, i, 0))

    flat = pl.BlockSpec((tr, cols), lambda i: (i, 0))
    return pl.pallas_call(
        body,
        name=name,
        grid=(rows // tr,),
        in_specs=[part(dev) for dev in range(N_DEV)] + [flat, flat, flat],
        out_specs=[flat, flat, flat, flat],
        out_shape=[jax.ShapeDtypeStruct((rows, cols), F32)] * 4,
        compiler_params=_params(1),
    )(*([parts] * N_DEV), w, m, v)


def _sum_devices(gathered):
    def body(g_ref, o_ref):
        tot = g_ref[0]
        for dev in range(1, N_DEV):
            tot = tot + g_ref[dev]
        o_ref[...] = tot

    return pl.pallas_call(body, name="sum_devices", out_shape=jax.ShapeDtypeStruct(gathered.shape[1:], F32))(gathered)


def _adam_small(ws, gs, ms, vs):
    n = len(ws)

    def body(*refs):
        w_refs, g_refs, m_refs, v_refs = (refs[q * n : (q + 1) * n] for q in range(4))
        outs = refs[4 * n :]
        for p in range(n):
            outs[p][...], outs[n + p][...], outs[2 * n + p][...] = _adamw(w_refs[p][...], g_refs[p][...], m_refs[p][...], v_refs[p][...])

    shapes = [jax.ShapeDtypeStruct(w.shape, F32) for w in ws]
    res = pl.pallas_call(body, name="adam_small", out_shape=shapes * 3)(*ws, *gs, *ms, *vs)
    return res[:n], res[n : 2 * n], res[2 * n :]


def _padded(size):
    return -(-size // 128) * 128


def _pack(parts):
    flat = []
    for p in parts:
        v = p.reshape(-1)
        flat.append(jnp.pad(v, (0, _padded(v.shape[0]) - v.shape[0])))
    return jnp.concatenate(flat).reshape(-1, 128)


def _unpack(packed, shapes):
    flat = packed.reshape(-1)
    out, pos = [], 0
    for shp in shapes:
        size = math.prod(shp)
        out.append(flat[pos : pos + size].reshape(shp))
        pos += _padded(size)
    return out


WEIGHTS = (
    "hgrn_lb norm1_w w_in hgrn_norm_w sconv_w w_out norm2_w mem_norm_w wq wk wv wo norm3_w w_gate w_up ffn_conv_w "
    "ffn_conv_b w_down final_norm_w"
).split()
BIG = ["w_in", "w_out", "wq", "wk", "wv", "wo", "w_gate", "w_up", "w_down"]
SMALL_SHARDED = ["sconv_w", "ffn_conv_w"]
SMALL = [n for n in WEIGHTS if n not in BIG]


def kernel(x, mem, hgrn_lb, norm1_w, w_in, hgrn_norm_w, sconv_w, w_out, norm2_w, mem_norm_w, wq, wk, wv, wo, norm3_w, w_gate, w_up, ffn_conv_w, ffn_conv_b, w_down, final_norm_w, loss_target, m_hgrn_lb, m_norm1_w, m_w_in, m_hgrn_norm_w, m_sconv_w, m_w_out, m_norm2_w, m_mem_norm_w, m_wq, m_wk, m_wv, m_wo, m_norm3_w, m_w_gate, m_w_up, m_ffn_conv_w, m_ffn_conv_b, m_w_down, m_final_norm_w, v_hgrn_lb, v_norm1_w, v_w_in, v_hgrn_norm_w, v_sconv_w, v_w_out, v_norm2_w, v_mem_norm_w, v_wq, v_wk, v_wv, v_wo, v_norm3_w, v_w_gate, v_w_up, v_ffn_conv_w, v_ffn_conv_b, v_w_down, v_final_norm_w):
    given = dict(locals())
    w_of = {n: given[n] for n in WEIGHTS}
    m_of = {n: given["m_" + n] for n in WEIGHTS}
    v_of = {n: given["v_" + n] for n in WEIGHTS}
    x2, mem2, target = x[0], mem[0], loss_target[0]
    d = x2.shape[1]
    xi, yi, ci = lax.axis_index("x"), lax.axis_index("y"), lax.axis_index("c")
    me = 4 * xi + 2 * yi + ci
    final_w = final_norm_w.reshape(1, d)

    shard = {n: w_of[n][0].astype(BF16) for n in BIG}
    small_shard = jnp.concatenate([sconv_w[0], ffn_conv_w[0]], axis=1)
    h1, w_in_g, small_g = _rmsnorm("norm1", x2, norm1_w, _GatherRide([shard["w_in"], small_shard]))
    w_in_f = jnp.transpose(w_in_g, (1, 0, 2)).reshape(d, -1)
    conv_ch = sconv_w.shape[2]
    sw_f = jnp.transpose(small_g[:, :, :conv_ch], (1, 0, 2)).reshape(3, -1)
    cw_g = small_g[:, :, conv_ch:]
    cb_g = ffn_conv_b.reshape(N_DEV, 1, -1)

    lb = _lb_fwd(hgrn_lb)
    proj, *attn_w = _mm_proj(h1, w_in_f, _GatherRide([shard[n] for n in ("w_out", "wq", "wk", "wv", "wo")]))
    w_out_f, wq_f, wk_f, wv_f, wo_f = (a.reshape(d, d) for a in attn_w)
    mix, o_pre, states, wg_g, wu_g = _hgrn_fwd(proj, lb, hgrn_norm_w, _GatherRide([shard["w_gate"], shard["w_up"]]))
    mix = _sconv_fwd(proj, sw_f, mix)
    x1 = _mm_nn("mm_out", mix, w_out_f, F32, residual=x2)
    h2 = _rmsnorm("norm2", x1, norm2_w)
    mem_n = _rmsnorm("norm_mem", mem2, mem_norm_w)
    q = _mm_nn("mm_q", h2, wq_f, BF16)
    km = _mm_nn("mm_k", mem_n, wk_f, BF16)
    vm = _mm_nn("mm_v", mem_n, wv_f, BF16)
    ao = _attn_fwd(q, km, vm)
    x2r = _mm_nn("mm_o", ao, wo_f, F32, residual=x1)
    h3 = _rmsnorm("norm3", x2r, norm3_w)
    a_pre, up, wd_g = _mm_gate_up(h3, wg_g, wu_g, _GatherRide([shard["w_down"]]))
    x3, act = _ffn_down(a_pre, up, cw_g, cb_g, wd_g, x2r)

    dx3, dx3b, loss_blk, d_final = _loss_head(x3, target, final_w)
    parts = {}
    dact = _mm_dact(dx3b, wd_g)
    dwd = _mm_dwdown(act, dx3b)
    dapre, dup, d_ffn_small, parts["w_down"] = _ffn_act_bwd(a_pre, up, dact, cw_g, cb_g, _ScatterRide([dwd]))
    dwg, dwu = _mm_dw_cols("mm_dw_gate_up", h3, [dapre, dup])
    dh3, parts["w_gate"], parts["w_up"] = _mm_dh3(dapre, dup, wg_g, wu_g, _ScatterRide([dwg, dwu]))
    dx2, dx2b, d_n3 = _rmsnorm_bwd("norm3_bwd", dh3, x2r, norm3_w, dx3)
    dao = _mm_nt("mm_dao", [(dx2b, wo_f)], BF16)
    dwo = _mm_tn("mm_dwo", ao, dx2b)
    dq, dkm, dvm = _attn_bwd(q, km, vm, dao)
    dwq = _mm_tn("mm_dwq", h2, dq)
    dh2 = _mm_nt("mm_dh2", [(dq, wq_f)], F32)
    dwk = _mm_tn("mm_dwk", mem_n, dkm)
    dwv = _mm_tn("mm_dwv", mem_n, dvm)
    dmem_n = _mm_nt("mm_dmem", [(dkm, wk_f), (dvm, wv_f)], F32)
    _, _, d_nm = _rmsnorm_bwd("norm_mem_bwd", dmem_n, mem2, mem_norm_w)
    dx1, dx1b, d_n2 = _rmsnorm_bwd("norm2_bwd", dh2, x1, norm2_w, dx2)
    dmix = _mm_nt("mm_dmix", [(dx1b, w_out_f)], F32)
    dwout = _mm_tn("mm_dwout", mix, dx1b)
    dwout_g, dwq_g, dwk_g, dwv_g, dwo_g = (a.reshape(N_DEV, d // N_DEV, d) for a in (dwout, dwq, dwk, dwv, dwo))
    dproj, d_lb, d_nw, parts["w_out"], parts["wq"] = _hgrn_bwd(proj, lb, hgrn_norm_w, o_pre, states, dmix, _ScatterRide([dwout_g, dwq_g]))
    dproj, d_sw = _sconv_bwd(proj, sw_f, dmix, dproj)
    dwin, parts["wk"], parts["wv"], parts["wo"] = _mm_dwin(h1, dproj, N_GROUPS, _ScatterRide([dwk_g, dwv_g, dwo_g]))
    dwin_g = jnp.transpose(dwin.reshape(d, N_DEV, -1), (1, 0, 2))
    dh1, parts["w_in"] = _mm_dh1(dproj, w_in_f, N_GROUPS, _ScatterRide([dwin_g]))
    grad_x, _, d_n1 = _rmsnorm_bwd("norm1_bwd", dh1, x2, norm1_w, dx1)
    d_hlb = _lb_bwd(hgrn_lb, d_lb)

    small_parts = [d_hlb, d_n1, d_nw, d_sw[0:3], d_n2, d_nm, d_n3, d_ffn_small[:, 0:3, :], d_ffn_small[:, 3, :], d_final, loss_blk]
    small_shapes = [p.shape for p in small_parts]
    total = _sum_devices(_allgather("gather_small", [_pack(small_parts)])[0])
    g_hlb, g_n1, g_nw, g_sw, g_n2, g_nm, g_n3, g_cw, g_cb, g_final, loss_row = _unpack(total, small_shapes)
    grads = {
        "hgrn_lb": g_hlb,
        "norm1_w": g_n1,
        "hgrn_norm_w": g_nw,
        "sconv_w": lax.dynamic_slice_in_dim(g_sw, me * conv_ch, conv_ch, axis=1).reshape(sconv_w.shape),
        "norm2_w": g_n2,
        "mem_norm_w": g_nm,
        "norm3_w": g_n3,
        "ffn_conv_w": lax.dynamic_index_in_dim(g_cw, me, axis=0, keepdims=True),
        "ffn_conv_b": g_cb.reshape(ffn_conv_b.shape),
        "final_norm_w": g_final.reshape(final_norm_w.shape),
    }
    def rows_of(a):
        return a.reshape(-1, a.shape[-1])

    small_out = _adam_small(*([rows_of(src[n]) for n in SMALL] for src in (w_of, grads, m_of, v_of)))
    deltas, new_m, new_v = ({n: a.reshape(w_of[n].shape) for n, a in zip(SMALL, out)} for out in small_out)

    for n in BIG:
        shp = w_of[n].shape
        gw, dw, mw, vw = _adam_shard("adam_" + n, parts[n], w_of[n][0], m_of[n][0], v_of[n][0])
        grads[n], deltas[n], new_m[n], new_v[n] = (a.reshape(shp) for a in (gw, dw, mw, vw))

    loss = loss_row[0, 0]
    return (
        loss,
        grad_x.reshape(x.shape),
        *[grads[n] for n in WEIGHTS],
        *[deltas[n] for n in WEIGHTS],
        *[new_m[n] for n in WEIGHTS],
        *[new_v[n] for n in WEIGHTS],
    )
```

```python
import functools
import math

import jax
import jax.numpy as jnp
from jax import lax
from jax.experimental import pallas as pl
from jax.experimental.pallas import tpu as pltpu

F32 = jnp.float32
BF16 = jnp.bfloat16
MESH = pl.DeviceIdType.MESH
N_DEV = 8
EPS = 1e-6
CHUNK = 64
HGRN_HEADS = 8
HEAD_DIM = 128
HEADS_PER_STEP = 4
MEM_HEADS = 4
GROUP_W = HGRN_HEADS * HEAD_DIM
N_GROUPS = 7
HALO = 16
TS = 512
TR = 256
TM = 1024
TN = 1024
TK = 1024
VMEM_LIMIT = 48 * 1024 * 1024

ADAM_LR = 0.001
ADAM_B1 = 0.9
ADAM_B2 = 0.999
ADAM_EPS = 1e-08
ADAM_WD = 0.01
ADAM_STEP = 10

_DIMS = {
    "nn": (((1,), (0,)), ((), ())),
    "nt": (((1,), (1,)), ((), ())),
    "tn": (((0,), (0,)), ((), ())),
}


def _params(n_axes):
    return pltpu.CompilerParams(dimension_semantics=("arbitrary",) * n_axes, vmem_limit_bytes=VMEM_LIMIT)


def _dot(a, b, mode):
    return lax.dot_general(a.astype(BF16), b.astype(BF16), _DIMS[mode], preferred_element_type=F32)


def _sigmoid(v):
    return 0.5 * jnp.tanh(0.5 * v) + 0.5


def _block_dims(spec):
    return tuple(d for d in spec.block_shape if d is not None)


HBM_SPEC = pl.BlockSpec(memory_space=pltpu.HBM)


def _place():
    x, y, c = lax.axis_index("x"), lax.axis_index("y"), lax.axis_index("c")
    return x, y, c, [(1 - x, y), (x, 1 - y), (1 - x, 1 - y)]


def _slab(p):
    return 4 * p[0] + 2 * p[1] + p[2]


def _remote(src, dst, send_sem, recv_sem, to):
    return pltpu.make_async_remote_copy(src_ref=src, dst_ref=dst, send_sem=send_sem, recv_sem=recv_sem, device_id=to, device_id_type=MESH)


class _GatherRide:
    def __init__(self, shards):
        self.inputs = list(shards)
        self.n = len(shards)
        self.out_shape = [jax.ShapeDtypeStruct((N_DEV, *a.shape), a.dtype) for a in shards]
        self.scratch = [pltpu.SemaphoreType.DMA((self.n, 7)), pltpu.SemaphoreType.DMA((self.n, 7)), pltpu.SemaphoreType.DMA((self.n,))]

    @staticmethod
    def _copy(outs, sems, a, k, block, to, src=None):
        dst = outs[a].at[_slab(block)]
        return _remote(dst if src is None else src, dst, sems[0].at[a, k], sems[1].at[a, k], to)

    def _mine(self, ins, outs, sems, a):
        x, y, c, chips = _place()
        local = pltpu.make_async_copy(ins[a], outs[a].at[_slab((x, y, c))], sems[2].at[a])
        first = [self._copy(outs, sems, a, 0, (x, y, c), (x, y, 1 - c), src=ins[a])]
        first += [self._copy(outs, sems, a, 1 + j, (x, y, c), (*chip, c), src=ins[a]) for j, chip in enumerate(chips)]
        return local, first

    def start(self, ins, outs, sems):
        for a in range(self.n):
            local, first = self._mine(ins, outs, sems, a)
            local.start()
            for cp in first:
                cp.start()

    def forward(self, ins, outs, sems):
        x, y, c, chips = _place()
        for j, chip in enumerate(chips):
            for a in range(self.n):
                self._copy(outs, sems, a, 1 + j, (*chip, c), (x, y, c)).wait_recv()
                self._copy(outs, sems, a, 4 + j, (*chip, c), (x, y, 1 - c)).start()

    def finish(self, ins, outs, sems):
        x, y, c, chips = _place()
        for a in range(self.n):
            self._copy(outs, sems, a, 0, (x, y, 1 - c), (x, y, c)).wait_recv()
            for j, chip in enumerate(chips):
                self._copy(outs, sems, a, 4 + j, (*chip, 1 - c), (x, y, c)).wait_recv()
        for a in range(self.n):
            local, first = self._mine(ins, outs, sems, a)
            for cp in first:
                cp.wait_send()
            for j, chip in enumerate(chips):
                self._copy(outs, sems, a, 4 + j, (*chip, c), (x, y, 1 - c)).wait_send()
            local.wait()


class _ScatterRide:
    def __init__(self, grads, column_slabs=False):
        self.inputs = list(grads)
        self.n = len(grads)
        self.column_slabs = column_slabs
        if column_slabs:
            self.out_shape = [jax.ShapeDtypeStruct((N_DEV, g.shape[0], g.shape[1] // N_DEV), g.dtype) for g in grads]
        else:
            self.out_shape = [jax.ShapeDtypeStruct(g.shape, g.dtype) for g in grads]
        self.scratch = [pltpu.SemaphoreType.DMA((self.n, 7)), pltpu.SemaphoreType.DMA((self.n, 7)), pltpu.SemaphoreType.DMA((self.n,))]

    @staticmethod
    def _peers():
        x, y, c, chips = _place()
        return _slab((x, y, c)), [(x, y, 1 - c)] + [(*chip, c) for chip in chips] + [(*chip, 1 - c) for chip in chips]

    def _slab_of(self, ref, j):
        if not self.column_slabs:
            return ref.at[j]
        cols = ref.shape[1] // N_DEV
        return ref.at[:, pl.ds(pl.multiple_of(j * cols, 128), cols)]

    def _sends(self, ins, outs, sems, a):
        me, peers = self._peers()
        local = pltpu.make_async_copy(self._slab_of(ins[a], me), outs[a].at[me], sems[2].at[a])
        return [local] + [_remote(self._slab_of(ins[a], _slab(p)), outs[a].at[me], sems[0].at[a, k], sems[1].at[a, k], p) for k, p in enumerate(peers)]

    def start(self, ins, outs, sems):
        for a in range(self.n):
            for cp in self._sends(ins, outs, sems, a):
                cp.start()

    def forward(self, ins, outs, sems):
        pass

    def finish(self, ins, outs, sems):
        me, peers = self._peers()
        for a in range(self.n):
            for k, p in enumerate(peers):
                _remote(outs[a].at[me], outs[a].at[_slab(p)], sems[0].at[a, k], sems[1].at[a, k], p).wait_recv()
            local, *sends = self._sends(ins, outs, sems, a)
            for cp in sends:
                cp.wait_send()
            local.wait()


def _grid_step(grid):
    step = 0
    for axis, size in enumerate(grid):
        step = step * size + pl.program_id(axis)
    return step


def _ride_begin(ride, grid, ins, outs, sems):
    step, n_steps = _grid_step(grid), math.prod(grid)

    @pl.when(step == 0)
    def _():
        ride.start(ins, outs, sems)

    @pl.when(step == (3 * n_steps) // 4)
    def _():
        ride.forward(ins, outs, sems)


def _ride_end(ride, grid, ins, outs, sems):
    @pl.when(_grid_step(grid) == math.prod(grid) - 1)
    def _():
        ride.finish(ins, outs, sems)


def _allgather(name, shards):
    ride = _GatherRide(shards)
    n = ride.n

    def body(*refs):
        ins, outs, sems = refs[:n], refs[n : 2 * n], refs[2 * n :]
        ride.start(ins, outs, sems)
        ride.forward(ins, outs, sems)
        ride.finish(ins, outs, sems)

    return pl.pallas_call(
        body,
        name=name,
        in_specs=[HBM_SPEC] * n,
        out_specs=[HBM_SPEC] * n,
        out_shape=ride.out_shape,
        scratch_shapes=ride.scratch,
    )(*shards)


def _matmul(name, grid, k_axis, pairs, outs, mode, residual=None, ride=None):
    n_pairs, n_out, has_res = len(pairs), len(outs), residual is not None
    k_steps = grid[k_axis] if k_axis is not None else 1
    n_acc = n_out if k_steps > 1 else 0
    n_ride = ride.n if ride is not None else 0

    def body(*refs):
        ins = refs[: 2 * n_pairs]
        pos = 2 * n_pairs
        res_ref = refs[pos] if has_res else None
        pos += int(has_res)
        ride_ins = refs[pos : pos + n_ride]
        pos += n_ride
        out_refs = refs[pos : pos + n_out]
        pos += n_out
        ride_outs = refs[pos : pos + n_ride]
        pos += n_ride
        acc_refs = refs[pos : pos + n_acc]
        ride_sems = refs[pos + n_acc :]
        if ride is not None:
            _ride_begin(ride, grid, ride_ins, ride_outs, ride_sems)

        def partial(o):
            tot = None
            for p in outs[o][3]:
                d = _dot(ins[2 * p][...], ins[2 * p + 1][...], mode)
                tot = d if tot is None else tot + d
            return tot

        if k_steps == 1:
            for o in range(n_out):
                val = partial(o)
                if has_res and o == 0:
                    val = val + res_ref[...]
                out_refs[o][...] = val.astype(out_refs[o].dtype)
        else:
            k = pl.program_id(k_axis)

            @pl.when(k == 0)
            def _():
                for o in range(n_out):
                    if has_res and o == 0:
                        acc_refs[o][...] = res_ref[...]
                    else:
                        acc_refs[o][...] = jnp.zeros_like(acc_refs[o])

            for o in range(n_out):
                acc_refs[o][...] += partial(o)

            @pl.when(k == k_steps - 1)
            def _():
                for o in range(n_out):
                    out_refs[o][...] = acc_refs[o][...].astype(out_refs[o].dtype)

        if ride is not None:
            _ride_end(ride, grid, ride_ins, ride_outs, ride_sems)

    args, in_specs = [], []
    for a, a_spec, b, b_spec in pairs:
        args += [a, b]
        in_specs += [a_spec, b_spec]
    if has_res:
        args.append(residual[0])
        in_specs.append(residual[1])
    out_specs = [o[2] for o in outs]
    out_shape = [jax.ShapeDtypeStruct(o[0], o[1]) for o in outs]
    scratch = [pltpu.VMEM(_block_dims(o[2]), F32) for o in outs] if k_steps > 1 else []
    if ride is not None:
        args += ride.inputs
        in_specs += [HBM_SPEC] * n_ride
        out_specs += [HBM_SPEC] * n_ride
        out_shape += ride.out_shape
        scratch += ride.scratch
    return pl.pallas_call(
        body,
        name=name,
        grid=grid,
        in_specs=in_specs,
        out_specs=out_specs,
        out_shape=out_shape,
        scratch_shapes=scratch,
        compiler_params=_params(len(grid)),
    )(*args)


def _out_tiles(m, n, out_dtype):
    if out_dtype == F32 and m >= TM:
        return TM // 2, n
    return min(TM, m), min(TN, n)


def _mm_nn(name, a, w, out_dtype, residual=None):
    m, k = a.shape
    n = w.shape[1]
    tm, tn = _out_tiles(m, n, out_dtype)
    pairs = [(a, pl.BlockSpec((tm, k), lambda i, j: (i, 0)), w, pl.BlockSpec((k, tn), lambda i, j: (0, j)))]
    o_spec = pl.BlockSpec((tm, tn), lambda i, j: (i, j))
    res = (residual, o_spec) if residual is not None else None
    return _matmul(name, (m // tm, n // tn), None, pairs, [((m, n), out_dtype, o_spec, [0])], "nn", res)[0]


def _mm_nt(name, pairs_in, out_dtype):
    m, k = pairs_in[0][0].shape
    n = pairs_in[0][1].shape[0]
    tm, tn = _out_tiles(m, n, out_dtype)
    pairs = [
        (a, pl.BlockSpec((tm, k), lambda i, j: (i, 0)), w, pl.BlockSpec((tn, k), lambda i, j: (j, 0)))
        for a, w in pairs_in
    ]
    o_spec = pl.BlockSpec((tm, tn), lambda i, j: (i, j))
    outs = [((m, n), out_dtype, o_spec, list(range(len(pairs))))]
    return _matmul(name, (m // tm, n // tn), None, pairs, outs, "nt")[0]


def _mm_tn(name, a, b):
    s, i_dim = a.shape
    n = b.shape[1]
    ts, ti, tn = min(TK, s), min(TN, i_dim), min(TN, n)
    pairs = [(a, pl.BlockSpec((ts, ti), lambda i, j, k: (k, i)), b, pl.BlockSpec((ts, tn), lambda i, j, k: (k, j)))]
    outs = [((i_dim, n), BF16, pl.BlockSpec((ti, tn), lambda i, j, k: (i, j)), [0])]
    return _matmul(name, (i_dim // ti, n // tn, s // ts), 2, pairs, outs, "tn")[0]


def _mm_proj(h1, w_in, ride):
    s, d = h1.shape
    g = w_in.shape[1] // GROUP_W
    tm = min(TM, s)
    pairs = [(h1, pl.BlockSpec((tm, d), lambda i, j: (i, 0)), w_in, pl.BlockSpec((d, GROUP_W), lambda i, j: (0, j)))]
    outs = [((g, s, GROUP_W), BF16, pl.BlockSpec((None, tm, GROUP_W), lambda i, j: (j, i, 0)), [0])]
    return _matmul("mm_proj", (s // tm, g), None, pairs, outs, "nn", ride=ride)


def _mm_gate_up(h3, wg_g, wu_g, ride):
    s, d = h3.shape
    f = wg_g.shape[2]
    tm = min(TM, s)
    a_spec = pl.BlockSpec((tm, d), lambda i, j: (i, 0))
    w_spec = pl.BlockSpec((None, d, f), lambda i, j: (j, 0, 0))
    o_spec = pl.BlockSpec((None, tm, f), lambda i, j: (j, i, 0))
    pairs = [(h3, a_spec, wg_g, w_spec), (h3, a_spec, wu_g, w_spec)]
    outs = [((N_DEV, s, f), BF16, o_spec, [0]), ((N_DEV, s, f), BF16, o_spec, [1])]
    return _matmul("mm_gate_up", (s // tm, N_DEV), None, pairs, outs, "nn", ride=ride)


def _mm_dact(dx3b, wd_g):
    s, d = dx3b.shape
    f = wd_g.shape[1]
    tm = min(TM, s)
    pairs = [(dx3b, pl.BlockSpec((tm, d), lambda i, j: (i, 0)), wd_g, pl.BlockSpec((None, f, d), lambda i, j: (j, 0, 0)))]
    outs = [((N_DEV, s, f), BF16, pl.BlockSpec((None, tm, f), lambda i, j: (j, i, 0)), [0])]
    return _matmul("mm_dact", (s // tm, N_DEV), None, pairs, outs, "nt")[0]


def _mm_dwdown(act, dx3b):
    _, s, f = act.shape
    d = dx3b.shape[1]
    ts, tn = min(TK, s), min(TN, d)
    pairs = [
        (
            act,
            pl.BlockSpec((None, ts, f), lambda j, n, k: (j, k, 0)),
            dx3b,
            pl.BlockSpec((ts, tn), lambda j, n, k: (k, n)),
        )
    ]
    outs = [((N_DEV, f, d), BF16, pl.BlockSpec((None, f, tn), lambda j, n, k: (j, 0, n)), [0])]
    return _matmul("mm_dwdown", (N_DEV, d // tn, s // ts), 2, pairs, outs, "tn")[0]


def _mm_dh3(dapre, dup, wg_g, wu_g, ride):
    _, s, f = dapre.shape
    d = wg_g.shape[1]
    tm, tn = min(TM, s), min(TN, d)
    a_spec = pl.BlockSpec((None, tm, f), lambda i, j, k: (k, i, 0))
    w_spec = pl.BlockSpec((None, tn, f), lambda i, j, k: (k, j, 0))
    pairs = [(dapre, a_spec, wg_g, w_spec), (dup, a_spec, wu_g, w_spec)]
    outs = [((s, d), F32, pl.BlockSpec((tm, tn), lambda i, j, k: (i, j)), [0, 1])]
    return _matmul("mm_dh3", (s // tm, d // tn, N_DEV), 2, pairs, outs, "nt", ride=ride)


def _mm_dw_cols(name, h, dcols_list):
    s, d = h.shape
    f = dcols_list[0].shape[2]
    ts, ti = min(TK, s), min(TN, d)
    a_spec = pl.BlockSpec((ts, ti), lambda j, i, k: (k, i))
    b_spec = pl.BlockSpec((None, ts, f), lambda j, i, k: (j, k, 0))
    o_spec = pl.BlockSpec((None, ti, f), lambda j, i, k: (j, i, 0))
    pairs = [(h, a_spec, dc, b_spec) for dc in dcols_list]
    outs = [((N_DEV, d, f), BF16, o_spec, [p]) for p in range(len(pairs))]
    return _matmul(name, (N_DEV, d // ti, s // ts), 2, pairs, outs, "tn")


def _mm_dwin(h1, dproj, groups, ride):
    s, d = h1.shape
    ts, ti = min(TK, s), min(TN, d)
    pairs = [
        (
            h1,
            pl.BlockSpec((ts, ti), lambda j, i, k: (k, i)),
            dproj,
            pl.BlockSpec((None, ts, GROUP_W), lambda j, i, k: (j, k, 0)),
        )
    ]
    outs = [((d, groups * GROUP_W), BF16, pl.BlockSpec((ti, GROUP_W), lambda j, i, k: (i, j)), [0])]
    return _matmul("mm_dwin", (groups, d // ti, s // ts), 2, pairs, outs, "tn", ride=ride)


def _mm_dh1(dproj, w_in, groups, ride):
    s = dproj.shape[1]
    d = w_in.shape[0]
    tm, tn = min(TM, s), min(TN, d)
    pairs = [
        (
            dproj,
            pl.BlockSpec((None, tm, GROUP_W), lambda i, j, k: (k, i, 0)),
            w_in,
            pl.BlockSpec((tn, GROUP_W), lambda i, j, k: (j, k)),
        )
    ]
    outs = [((s, d), F32, pl.BlockSpec((tm, tn), lambda i, j, k: (i, j)), [0])]
    return _matmul("mm_dh1", (s // tm, d // tn, groups), 2, pairs, outs, "nt", ride=ride)


def _rows_to_8(v):
    t, d = v.shape
    return jnp.sum(v.reshape(t // 8, 8, d), axis=0)


def _rmsnorm(name, x, w, ride=None):
    s, d = x.shape
    t = min(TS, s)
    grid = (s // t,)

    def compute(x_ref, w_ref, o_ref):
        xv = x_ref[...]
        r = lax.rsqrt(jnp.mean(xv * xv, axis=-1, keepdims=True) + EPS)
        o_ref[...] = (xv * r * w_ref[...]).astype(o_ref.dtype)

    in_specs = [pl.BlockSpec((t, d), lambda i: (i, 0)), pl.BlockSpec((1, d), lambda i: (0, 0))]
    out_spec = pl.BlockSpec((t, d), lambda i: (i, 0))
    out_shape = jax.ShapeDtypeStruct((s, d), BF16)
    if ride is None:
        return pl.pallas_call(compute, name=name, grid=grid, in_specs=in_specs, out_specs=out_spec, out_shape=out_shape, compiler_params=_params(1))(x, w)
    return pl.pallas_call(
        _hosted(compute, 2, 1, 0, ride, grid),
        name=name,
        grid=grid,
        in_specs=in_specs + [HBM_SPEC] * ride.n,
        out_specs=[out_spec] + [HBM_SPEC] * ride.n,
        out_shape=[out_shape] + ride.out_shape,
        scratch_shapes=ride.scratch,
        compiler_params=_params(1),
    )(x, w, *ride.inputs)


def _rmsnorm_bwd(name, dh, x, w, res=None):
    s, d = x.shape
    t = min(TR, s)
    steps = s // t
    has_res = res is not None

    def body(*refs):
        if has_res:
            dh_ref, x_ref, w_ref, res_ref, dx_ref, dxb_ref, dw_ref, acc = refs
        else:
            dh_ref, x_ref, w_ref, dx_ref, dxb_ref, dw_ref, acc = refs
        i = pl.program_id(0)
        xv = x_ref[...]
        r = lax.rsqrt(jnp.mean(xv * xv, axis=-1, keepdims=True) + EPS)
        xn = xv * r
        dhv = dh_ref[...].astype(F32)
        dxn = dhv * w_ref[...]
        dx = r * (dxn - xn * jnp.mean(dxn * xn, axis=-1, keepdims=True))
        if has_res:
            dx = dx + res_ref[...]
        dx_ref[...] = dx
        dxb_ref[...] = dx.astype(BF16)

        @pl.when(i == 0)
        def _():
            acc[...] = jnp.zeros_like(acc)

        acc[...] += _rows_to_8(dhv * xn)

        @pl.when(i == steps - 1)
        def _():
            dw_ref[...] = jnp.sum(acc[...], axis=0, keepdims=True)

    row = pl.BlockSpec((t, d), lambda i: (i, 0))
    vec = pl.BlockSpec((1, d), lambda i: (0, 0))
    args = [dh, x, w] + ([res] if has_res else [])
    return pl.pallas_call(
        body,
        name=name,
        grid=(steps,),
        in_specs=[row, row, vec] + ([row] if has_res else []),
        out_specs=[row, row, vec],
        out_shape=[
            jax.ShapeDtypeStruct((s, d), F32),
            jax.ShapeDtypeStruct((s, d), BF16),
            jax.ShapeDtypeStruct((1, d), F32),
        ],
        scratch_shapes=[pltpu.VMEM((8, d), F32)],
        compiler_params=_params(1),
    )(*args)


def _loss_head(x3, target, w):
    s, d = x3.shape
    t = min(TR, s)
    steps = s // t

    def body(x_ref, t_ref, w_ref, dx_ref, dxb_ref, loss_ref, dw_ref, acc_l, acc_w):
        i = pl.program_id(0)
        xv = x_ref[...]
        wv = w_ref[...]
        r = lax.rsqrt(jnp.mean(xv * xv, axis=-1, keepdims=True) + EPS)
        xn = xv * r
        err = xn * wv - t_ref[...]
        dy = err * (1.0 / d)
        dxn = dy * wv
        dx = r * (dxn - xn * jnp.mean(dxn * xn, axis=-1, keepdims=True))
        dx_ref[...] = dx
        dxb_ref[...] = dx.astype(BF16)

        @pl.when(i == 0)
        def _():
            acc_l[...] = jnp.zeros_like(acc_l)
            acc_w[...] = jnp.zeros_like(acc_w)

        acc_l[...] += _rows_to_8(err * err)
        acc_w[...] += _rows_to_8(dy * xn)

        @pl.when(i == steps - 1)
        def _():
            tot = jnp.sum(jnp.sum(acc_l[...], axis=0, keepdims=True), axis=1, keepdims=True) * (0.5 / d)
            loss_ref[...] = jnp.broadcast_to(tot, loss_ref.shape)
            dw_ref[...] = jnp.sum(acc_w[...], axis=0, keepdims=True)

    row = pl.BlockSpec((t, d), lambda i: (i, 0))
    vec = pl.BlockSpec((1, d), lambda i: (0, 0))
    return pl.pallas_call(
        body,
        name="loss_head",
        grid=(steps,),
        in_specs=[row, row, vec],
        out_specs=[row, row, pl.BlockSpec((1, 128), lambda i: (0, 0)), vec],
        out_shape=[
            jax.ShapeDtypeStruct((s, d), F32),
            jax.ShapeDtypeStruct((s, d), BF16),
            jax.ShapeDtypeStruct((1, 128), F32),
            jax.ShapeDtypeStruct((1, d), F32),
        ],
        scratch_shapes=[pltpu.VMEM((8, d), F32), pltpu.VMEM((8, d), F32)],
        compiler_params=_params(1),
    )(x3, target, w)


def _lb_fwd(hgrn_lb):
    def body(p_ref, lb_ref):
        p = p_ref[...]
        m = jnp.max(p, axis=0, keepdims=True)
        e = jnp.exp(p - m)
        lb_ref[...] = e[0:1, :] / jnp.sum(e, axis=0, keepdims=True)

    return pl.pallas_call(body, name="lb_fwd", out_shape=jax.ShapeDtypeStruct((1, hgrn_lb.shape[1]), F32))(hgrn_lb)


def _lb_bwd(hgrn_lb, dlb):
    def body(p_ref, d_ref, o_ref):
        p = p_ref[...]
        m = jnp.max(p, axis=0, keepdims=True)
        e = jnp.exp(p - m)
        sm = e / jnp.sum(e, axis=0, keepdims=True)
        p0 = sm[0:1, :]
        row = lax.broadcasted_iota(jnp.int32, sm.shape, 0)
        o_ref[...] = d_ref[...] * p0 * (jnp.where(row == 0, 1.0, 0.0) - sm)

    return pl.pallas_call(body, name="lb_bwd", out_shape=jax.ShapeDtypeStruct(hgrn_lb.shape, F32))(hgrn_lb, dlb)


def _split3(v):
    hi = v.astype(BF16)
    r1 = v - hi.astype(F32)
    mid = r1.astype(BF16)
    lo = (r1 - mid.astype(F32)).astype(BF16)
    return hi, mid, lo


def _pair(v):
    hi = v.astype(BF16)
    return hi, (v - hi.astype(F32)).astype(BF16)


def _dot_pairs(a2, b2, mode):
    dims = _DIMS[mode]
    out = lax.dot_general(a2[0], b2[0], dims, preferred_element_type=F32)
    if a2[1] is not None:
        out = lax.dot_general(a2[1], b2[0], dims, preferred_element_type=F32) + out
    if b2[1] is not None:
        out = lax.dot_general(a2[0], b2[1], dims, preferred_element_type=F32) + out
    return out


def _tri_dot(tri, v):
    hi, mid, lo = _split3(v)
    dims = _DIMS["nn"]
    out = lax.dot_general(tri, lo, dims, preferred_element_type=F32)
    out = out + lax.dot_general(tri, mid, dims, preferred_element_type=F32)
    return out + lax.dot_general(tri, hi, dims, preferred_element_type=F32)


def _chunk_gates(qp, fp, lbv):
    sf = _sigmoid(fp)
    f = lbv + (1.0 - lbv) * sf
    k = 1.0 - f
    sq = _sigmoid(qp)
    qf = qp * sq
    return sf, f, k, sq, qf, jnp.log(f)


def _chunk_decays(b):
    row = lax.broadcasted_iota(jnp.int32, b.shape, 0)
    bm = jnp.sum(jnp.where(row == CHUNK // 2 - 1, b, 0.0), axis=0, keepdims=True)
    bl = jnp.sum(jnp.where(row == CHUNK - 1, b, 0.0), axis=0, keepdims=True)
    return jnp.exp(b), jnp.exp(b - bm), jnp.exp(bm - b), jnp.exp(bl - b), jnp.exp(bl)


def _emit_pipelined(tasks, n_stages):
    tasks = list(tasks)
    for step in range(len(tasks) + n_stages - 1):
        for t in range(max(step - n_stages + 1, 0), min(step, len(tasks) - 1) + 1):
            next(tasks[t], None)


def _hosted(compute, n_in, n_out, n_scratch, ride, grid):
    n_ride = ride.n

    def body(*refs):
        ins, ride_ins = refs[:n_in], refs[n_in : n_in + n_ride]
        pos = n_in + n_ride
        outs, ride_outs = refs[pos : pos + n_out], refs[pos + n_out : pos + n_out + n_ride]
        pos += n_out + n_ride
        scratch, sems = refs[pos : pos + n_scratch], refs[pos + n_scratch :]
        _ride_begin(ride, grid, ride_ins, ride_outs, sems)
        compute(*ins, *outs, *scratch)
        _ride_end(ride, grid, ride_ins, ride_outs, sems)

    return body


def _hgrn_fwd(proj, lb, nw, ride):
    s = proj.shape[1]
    t = min(TS, s)
    nt, nch = s // t, t // CHUNK
    h_, k_, hb = HGRN_HEADS, HEAD_DIM, HEADS_PER_STEP

    def compute(q_ref, f_ref, i_ref, g_ref, lb_ref, nw_ref, mix_ref, o_ref, st_ref, state):
        @pl.when(pl.program_id(1) == 0)
        def _():
            state[...] = jnp.zeros_like(state)

        nwv = nw_ref[...]
        rr = lax.broadcasted_iota(jnp.int32, (CHUNK, CHUNK), 0)
        cc = lax.broadcasted_iota(jnp.int32, (CHUNK, CHUNK), 1)
        causal = rr >= cc
        tri = jnp.where(causal, 1.0, 0.0).astype(BF16)

        st = [state[hh] for hh in range(hb)]

        def task(c, hh):
            rows, lanes = slice(c * CHUNK, (c + 1) * CHUNK), slice(hh * k_, (hh + 1) * k_)
            qp, fp, gp = q_ref[rows, lanes].astype(F32), f_ref[rows, lanes].astype(F32), g_ref[rows, lanes].astype(F32)
            vb = i_ref[rows, lanes].astype(BF16)
            _, _, k, _, qf, logf = _chunk_gates(qp, fp, lb_ref[:, lanes])
            yield
            b = _tri_dot(tri, logf)
            yield
            e_b, e_qm, e_km, e_kl, e_l = _chunk_decays(b)
            qm, km, qe, kl = (qf * e_qm).astype(BF16), (k * e_km).astype(BF16), (qf * e_b).astype(BF16), (k * e_kl).astype(BF16)
            yield
            a = jnp.where(causal, _dot(qm, km, "nt"), 0.0).astype(BF16)
            update = _dot(vb, kl, "tn")
            yield
            st_ref[c, hh] = st[hh]
            o = _dot(qe, st[hh], "nt") + _dot(a, vb, "nn")
            st[hh] = st[hh] * e_l + update
            yield
            o_ref[rows, lanes] = o
            on = o * lax.rsqrt(jnp.mean(o * o, axis=-1, keepdims=True) + EPS) * nwv
            mix_ref[rows, lanes] = (on * (gp * _sigmoid(gp))).astype(BF16)

        _emit_pipelined((task(c, hh) for c in range(nch) for hh in range(hb)), 6)
        for hh in range(hb):
            state[hh] = st[hh]

    def col(g):
        return pl.BlockSpec((None, t, hb * k_), lambda h, i: (g, i, h))

    grid = (h_ // hb, nt)
    return pl.pallas_call(
        _hosted(compute, 6, 3, 1, ride, grid),
        name="hgrn_fwd",
        grid=grid,
        in_specs=[
            col(0),
            col(1),
            col(2),
            col(3),
            pl.BlockSpec((1, hb * k_), lambda h, i: (0, h)),
            pl.BlockSpec((1, k_), lambda h, i: (0, 0)),
        ]
        + [HBM_SPEC] * ride.n,
        out_specs=[
            pl.BlockSpec((t, hb * k_), lambda h, i: (i, h)),
            pl.BlockSpec((t, hb * k_), lambda h, i: (i, h)),
            pl.BlockSpec((nch, hb, k_, k_), lambda h, i: (i, h, 0, 0)),
        ]
        + [HBM_SPEC] * ride.n,
        out_shape=[
            jax.ShapeDtypeStruct((s, 2 * h_ * k_), BF16),
            jax.ShapeDtypeStruct((s, h_ * k_), F32),
            jax.ShapeDtypeStruct((s // CHUNK, h_, k_, k_), F32),
        ]
        + ride.out_shape,
        scratch_shapes=[pltpu.VMEM((hb, k_, k_), F32)] + ride.scratch,
        compiler_params=_params(2),
    )(proj, proj, proj, proj, lb, nw, *ride.inputs)


def _hgrn_bwd(proj, lb, nw, o_pre, states, dmix, ride):
    s = proj.shape[1]
    t = min(TS, s)
    nt, nch = s // t, t // CHUNK
    h_, k_, hb = HGRN_HEADS, HEAD_DIM, HEADS_PER_STEP

    def compute(q_ref, f_ref, i_ref, g_ref, lb_ref, nw_ref, o_ref, st_ref, dm_ref, dp_ref, dlb_ref, dnw_ref, dstate, acc_lb, acc_nw):
        h, i = pl.program_id(0), pl.program_id(1)
        dq_ref, df_ref, di_ref, dg_ref = dp_ref.at[0], dp_ref.at[1], dp_ref.at[2], dp_ref.at[3]

        @pl.when(i == 0)
        def _():
            dstate[...] = jnp.zeros_like(dstate)
            acc_lb[...] = jnp.zeros_like(acc_lb)

        @pl.when((i == 0) & (h == 0))
        def _():
            acc_nw[...] = jnp.zeros_like(acc_nw)

        nwv = nw_ref[...]
        rr = lax.broadcasted_iota(jnp.int32, (CHUNK, CHUNK), 0)
        cc = lax.broadcasted_iota(jnp.int32, (CHUNK, CHUNK), 1)
        causal = rr >= cc
        tri = jnp.where(causal, 1.0, 0.0).astype(BF16)
        tri_t =jnp.where(cc >= rr, 1.0, 0.0).astype(BF16)
        last_row = lax.broadcasted_iota(jnp.int32, (CHUNK, k_), 0) == CHUNK - 1

        dst = [dstate[hh] for hh in range(hb)]
        sum_nw = [jnp.zeros((8, k_), F32)]
        sum_lb = [jnp.zeros((8, k_), F32) for _ in range(hb)]

        def task(c, hh):
            rows, lanes = slice(c * CHUNK, (c + 1) * CHUNK), slice(hh * k_, (hh + 1) * k_)
            lbv = lb_ref[:, lanes]
            qp, fp, gp = q_ref[rows, lanes].astype(F32), f_ref[rows, lanes].astype(F32), g_ref[rows, lanes].astype(F32)
            vb = i_ref[rows, lanes].astype(BF16)
            sf, f, k, sq, qf, logf = _chunk_gates(qp, fp, lbv)
            yield
            b = _tri_dot(tri, logf)
            dm = dm_ref[rows, lanes]
            o = o_ref[rows, lanes]
            rinv = lax.rsqrt(jnp.mean(o * o, axis=-1, keepdims=True) + EPS)
            ohat = o * rinv
            sg = _sigmoid(gp)
            dg_ref[rows, lanes] = (dm * (ohat * nwv) * (sg * (1.0 + gp * (1.0 - sg)))).astype(BF16)
            don = dm * (gp * sg)
            sum_nw[0] = sum_nw[0] + _rows_to_8(don * ohat)
            dohat = don * nwv
            do = rinv * (dohat - ohat * jnp.mean(dohat * ohat, axis=-1, keepdims=True))
            yield
            e_b, e_qm, e_km, e_kl, e_l = _chunk_decays(b)
            st0 = st_ref[c, hh]
            do2 = _pair(do)
            qm2, km2 = _pair(qf * e_qm), _pair(k * e_km)
            st02, qe2 = _pair(st0), _pair(qf * e_b)
            yield
            a = jnp.where(causal, _dot(qm2[0], km2[0], "nt"), 0.0)
            da2 = _pair(jnp.where(causal, _dot_pairs(do2, (vb, None), "nt"), 0.0))
            dq_state = _dot_pairs(do2, st02, "nn")
            update = _dot_pairs(do2, qe2, "tn")
            yield
            dst2 = _pair(dst[hh])
            dq = e_b * dq_state + e_qm * _dot_pairs(da2, km2, "nn")
            dk_state = e_kl * _dot_pairs((vb, None), dst2, "nn")
            dk = e_km * _dot_pairs(da2, qm2, "tn") + dk_state
            dv = _dot(a, do2[0], "tn") + _dot(k * e_kl, dst2[0], "nt")
            d_bl = jnp.sum(k * dk_state, axis=0, keepdims=True) + e_l * jnp.sum(st0 * dst[hh], axis=0, keepdims=True)
            dst[hh] = dst[hh] * e_l + update
            yield
            dlogf = _tri_dot(tri_t, qf * dq - k * dk + jnp.where(last_row, d_bl, 0.0))
            yield
            dfv = dlogf / f - dk
            df_ref[rows, lanes] = (dfv * (1.0 - lbv) * (sf * (1.0 - sf))).astype(BF16)
            sum_lb[hh] = sum_lb[hh] + _rows_to_8(dfv * (1.0 - sf))
            dq_ref[rows, lanes] = (dq * (sq * (1.0 + qp * (1.0 - sq)))).astype(BF16)
            di_ref[rows, lanes] = dv.astype(BF16)

        _emit_pipelined((task(c, hh) for c in reversed(range(nch)) for hh in range(hb)), 7)
        acc_nw[...] += sum_nw[0]
        for hh in range(hb):
            dstate[hh] = dst[hh]
            acc_lb[:, hh * k_ : (hh + 1) * k_] += sum_lb[hh]

        @pl.when(i == nt - 1)
        def _():
            dlb_ref[...] = jnp.sum(acc_lb[...], axis=0, keepdims=True)

        @pl.when((i == nt - 1) & (h == h_ // hb - 1))
        def _():
            dnw_ref[...] = jnp.sum(acc_nw[...], axis=0, keepdims=True)

    def col(g):
        return pl.BlockSpec((None, t, hb * k_), lambda h, i: (g, nt - 1 - i, h))

    row = pl.BlockSpec((t, hb * k_), lambda h, i: (nt - 1 - i, h))
    grid = (h_ // hb, nt)
    return pl.pallas_call(
        _hosted(compute, 9, 3, 3, ride, grid),
        name="hgrn_bwd",
        grid=grid,
        in_specs=[
            col(0),
            col(1),
            col(2),
            col(3),
            pl.BlockSpec((1, hb * k_), lambda h, i: (0, h)),
            pl.BlockSpec((1, k_), lambda h, i: (0, 0)),
            row,
            pl.BlockSpec((nch, hb, k_, k_), lambda h, i: (nt - 1 - i, h, 0, 0)),
            row,
        ]
        + [HBM_SPEC] * ride.n,
        out_specs=[
            pl.BlockSpec((4, t, hb * k_), lambda h, i: (0, nt - 1 - i, h)),
            pl.BlockSpec((1, hb * k_), lambda h, i: (0, h)),
            pl.BlockSpec((1, k_), lambda h, i: (0, 0)),
        ]
        + [HBM_SPEC] * ride.n,
        out_shape=[
            jax.ShapeDtypeStruct((8, s, h_ * k_), BF16),
            jax.ShapeDtypeStruct((1, h_ * k_), F32),
            jax.ShapeDtypeStruct((1, k_), F32),
        ]
        + ride.out_shape,
        scratch_shapes=[pltpu.VMEM((hb, k_, k_), F32), pltpu.VMEM((8, hb * k_), F32), pltpu.VMEM((8, k_), F32)] + ride.scratch,
        compiler_params=_params(2),
    )(proj, proj, proj, proj, lb, nw, o_pre, states, dmix, *ride.inputs)


def _shift_rows(ext, k):
    n = ext.shape[0]
    return pltpu.roll(ext, k % n, axis=0)


def _row_select(parts):
    w = parts[0].shape[1]
    row = lax.broadcasted_iota(jnp.int32, (8, w), 0)
    out = jnp.zeros((8, w), F32)
    for r, p in enumerate(parts):
        out = out + jnp.where(row == r, p, 0.0)
    return out


def _sconv_fwd(proj, sw, mix):
    s = proj.shape[1]
    t = min(TS, s)
    nt, w = s // t, 512
    ncol, hb = GROUP_W // w, t // HALO

    def body(cb_ref, cc_ref, ch_ref, ccp_ref, chp_ref, sw_ref, mix_in, mix_ref):
        del mix_in
        i = pl.program_id(0)
        u = cc_ref[...].astype(F32) * ch_ref[...].astype(F32)
        up = jnp.where(i > 0, ccp_ref[...].astype(F32) * chp_ref[...].astype(F32), 0.0)
        ext = jnp.concatenate([up, u], axis=0)
        z = sw_ref[0:1, :] * _shift_rows(ext, 2)[HALO:] + sw_ref[1:2, :] * _shift_rows(ext, 1)[HALO:] + sw_ref[2:3, :] * u
        mix_ref[...] = (cb_ref[...].astype(F32) * z).astype(BF16)

    def cur(g):
        return pl.BlockSpec((None, t, w), lambda i, c: (g, i, c))

    def prev(g):
        return pl.BlockSpec((None, HALO, w), lambda i, c: (g, jnp.maximum(i * hb - 1, 0), c))

    return pl.pallas_call(
        body,
        name="sconv_fwd",
        grid=(nt, ncol),
        in_specs=[cur(4), cur(5), cur(6), prev(5), prev(6), pl.BlockSpec((3, w), lambda i, c: (0, c)), pl.BlockSpec(memory_space=pl.ANY)],
        out_specs=pl.BlockSpec((t, w), lambda i, c: (i, ncol + c)),
        out_shape=jax.ShapeDtypeStruct(mix.shape, mix.dtype),
        input_output_aliases={6: 0},
        compiler_params=_params(2),
    )(proj, proj, proj, proj, proj, sw, mix)


def _sconv_bwd(proj, sw, dmix, dproj):
    s = proj.shape[1]
    t = min(TS, s)
    nt, w = s // t, 512
    ncol, hb = GROUP_W // w, t // HALO

    def body(cb_ref, cbn_ref, cc_ref, ccp_ref, ch_ref, chp_ref, dy_ref, dyn_ref, sw_ref, dp_in, dp_ref, dsw_ref, acc):
        del dp_in
        i = pl.program_id(1)
        w0, w1, w2 = sw_ref[0:1, :], sw_ref[1:2, :], sw_ref[2:3, :]
        ccv, chv, cbv, dyv = cc_ref[...].astype(F32), ch_ref[...].astype(F32), cb_ref[...].astype(F32), dy_ref[...]
        u = ccv * chv
        up = jnp.where(i > 0, ccp_ref[...].astype(F32) * chp_ref[...].astype(F32), 0.0)
        ext = jnp.concatenate([up, u], axis=0)
        u1, u2 = _shift_rows(ext, 1)[HALO:], _shift_rows(ext, 2)[HALO:]
        z = w0 * u2 + w1 * u1 + w2 * u
        dz = dyv * cbv
        dzn = jnp.where(i < nt - 1, dyn_ref[...] * cbn_ref[...].astype(F32), 0.0)
        dze = jnp.concatenate([dz, dzn], axis=0)
        du = w2 * dz + w1 * _shift_rows(dze, -1)[:t] + w0 * _shift_rows(dze, -2)[:t]
        dp_ref[0] = (dyv * z).astype(BF16)
        dp_ref[1] = (du * chv).astype(BF16)
        dp_ref[2] = (du * ccv).astype(BF16)
        dp_ref[3] = jnp.zeros((t, w), BF16)

        @pl.when(i == 0)
        def _():
            acc[...] = jnp.zeros_like(acc)

        acc[...] += _row_select([jnp.sum(dz * uu, axis=0, keepdims=True) for uu in (u2, u1, u)])

        @pl.when(i == nt - 1)
        def _():
            dsw_ref[...] = acc[...]

    def cur(g):
        return pl.BlockSpec((None, t, w), lambda c, i: (g, i, c))

    def prev(g):
        return pl.BlockSpec((None, HALO, w), lambda c, i: (g, jnp.maximum(i * hb - 1, 0), c))

    def nxt(g):
        return pl.BlockSpec((None, HALO, w), lambda c, i: (g, jnp.minimum((i + 1) * hb, s // HALO - 1), c))

    return pl.pallas_call(
        body,
        name="sconv_bwd",
        grid=(ncol, nt),
        in_specs=[
            cur(4),
            nxt(4),
            cur(5),
            prev(5),
            cur(6),
            prev(6),
            pl.BlockSpec((t, w), lambda c, i: (i, ncol + c)),
            pl.BlockSpec((HALO, w), lambda c, i: (jnp.minimum((i + 1) * hb, s // HALO - 1), ncol + c)),
            pl.BlockSpec((3, w), lambda c, i: (0, c)),
            pl.BlockSpec(memory_space=pl.ANY),
        ],
        out_specs=[pl.BlockSpec((4, t, w), lambda c, i: (1, i, c)), pl.BlockSpec((8, w), lambda c, i: (0, c))],
        out_shape=[jax.ShapeDtypeStruct(dproj.shape, dproj.dtype), jax.ShapeDtypeStruct((8, GROUP_W), F32)],
        scratch_shapes=[pltpu.VMEM((8, w), F32)],
        input_output_aliases={9: 0},
        compiler_params=_params(2),
    )(proj, proj, proj, proj, proj, proj, dmix, dmix, sw, dproj)


def _ffn_down(a_pre, up, cw, cb, wd_g, x2):
    nd, s, f = a_pre.shape
    d = wd_g.shape[2]
    t = min(TS, s)
    nt, hb = s // t, t // HALO

    def body(a_ref, ap_ref, up_ref, cw_ref, cb_ref, w_ref, res_ref, o_ref, act_ref):
        i, j = pl.program_id(0), pl.program_id(1)
        av = a_ref[...].astype(F32)
        ext = jnp.concatenate([jnp.where(i > 0, ap_ref[...].astype(F32), 0.0), av], axis=0)
        a = cw_ref[0:1, :] * _shift_rows(ext, 2)[HALO:] + cw_ref[1:2, :] * _shift_rows(ext, 1)[HALO:] + cw_ref[2:3, :] * av
        a = a + cb_ref[...]
        act = (a * _sigmoid(a) * up_ref[...].astype(F32)).astype(BF16)
        act_ref[...] = act

        @pl.when(j == 0)
        def _():
            o_ref[...] = res_ref[...]

        o_ref[...] += _dot(act, w_ref[...], "nn")

    cur = pl.BlockSpec((None, t, f), lambda i, j: (j, i, 0))
    prev = pl.BlockSpec((None, HALO, f), lambda i, j: (j, jnp.maximum(i * hb - 1, 0), 0))
    row = pl.BlockSpec((t, d), lambda i, j: (i, 0))
    return pl.pallas_call(
        body,
        name="ffn_down",
        grid=(nt, nd),
        in_specs=[
            cur,
            prev,
            cur,
            pl.BlockSpec((None, 3, f), lambda i, j: (j, 0, 0)),
            pl.BlockSpec((None, 1, f), lambda i, j: (j, 0, 0)),
            pl.BlockSpec((None, f, d), lambda i, j: (j, 0, 0)),
            row,
        ],
        out_specs=[row, cur],
        out_shape=[jax.ShapeDtypeStruct((s, d), F32), jax.ShapeDtypeStruct(a_pre.shape, BF16)],
        compiler_params=_params(2),
    )(a_pre, a_pre, up, cw, cb, wd_g, x2)


def _ffn_act_bwd(a_pre, up, dact, cw, cb, ride):
    nd, s, f = a_pre.shape
    t = min(TS, s)
    nt, hb = s // t, t // HALO

    def compute(a_ref, ap_ref, an_ref, up_ref, upn_ref, da_ref, dan_ref, cw_ref, cb_ref, dap_ref, dup_ref, dsm_ref):
        i, j = pl.program_id(0), pl.program_id(1)
        w0, w1, w2 = cw_ref[0:1, :], cw_ref[1:2, :], cw_ref[2:3, :]
        av = a_ref[...].astype(F32)
        ext = jnp.concatenate([jnp.where(i > 0, ap_ref[...].astype(F32), 0.0), av, an_ref[...].astype(F32)], axis=0)
        e1, e2 = _shift_rows(ext, 1), _shift_rows(ext, 2)
        a = (w0 * e2 + w1 * e1 + w2 * ext)[HALO:] + cb_ref[...]
        sg = _sigmoid(a)
        dact_next = jnp.where(i < nt - 1, dan_ref[...], jnp.zeros_like(dan_ref))
        dact_e = jnp.concatenate([da_ref[...], dact_next], axis=0).astype(F32)
        up_e = jnp.concatenate([up_ref[...], upn_ref[...]], axis=0).astype(F32)
        d_a = dact_e * up_e * (sg * (1.0 + a * (1.0 - sg)))
        dup_ref[...] = (dact_e * (a * sg))[:t].astype(BF16)
        dap_ref[...] = (w2 * d_a + w1 * _shift_rows(d_a, -1) + w0 * _shift_rows(d_a, -2))[:t].astype(BF16)

        d_a_cur = d_a[:t]
        sums = [jnp.sum(d_a_cur * e[HALO : HALO + t], axis=0, keepdims=True) for e in (e2, e1, ext)]
        part = _row_select(sums + [jnp.sum(d_a_cur, axis=0, keepdims=True)])

        @pl.when(i == 0)
        def _():
            dsm_ref[j] = part

        @pl.when(i > 0)
        def _():
            dsm_ref[j] += part

    cur = pl.BlockSpec((None, t, f), lambda i, j: (j, i, 0))
    prev = pl.BlockSpec((None, HALO, f), lambda i, j: (j, jnp.maximum(i * hb - 1, 0), 0))
    nxt = pl.BlockSpec((None, HALO, f), lambda i, j: (j, jnp.minimum((i + 1) * hb, s // HALO - 1), 0))
    grid = (nt, nd)
    return pl.pallas_call(
        _hosted(compute, 9, 3, 0, ride, grid),
        name="ffn_act_bwd",
        grid=grid,
        in_specs=[cur, prev, nxt, cur, nxt, cur, nxt, pl.BlockSpec((None, 3, f), lambda i, j: (j, 0, 0)), pl.BlockSpec((None, 1, f), lambda i, j: (j, 0, 0))]
        + [HBM_SPEC] * ride.n,
        out_specs=[cur, cur, pl.BlockSpec((nd, 8, f), lambda i, j: (0, 0, 0))] + [HBM_SPEC] * ride.n,
        out_shape=[jax.ShapeDtypeStruct(a_pre.shape, BF16), jax.ShapeDtypeStruct(a_pre.shape, BF16), jax.ShapeDtypeStruct((nd, 8, f), F32)]
        + ride.out_shape,
        scratch_shapes=ride.scratch,
        compiler_params=_params(2),
    )(a_pre, a_pre, a_pre, up, up, dact, dact, cw, cb, *ride.inputs)


def _softmax_rows(sc):
    m = jnp.max(sc, axis=-1, keepdims=True)
    e = jnp.exp(sc - m)
    return e / jnp.sum(e, axis=-1, keepdims=True)


def _attn_fwd(q, km, vm):
    s, d = q.shape
    m = km.shape[0]
    t = min(TS, s)
    hd = d // MEM_HEADS
    scale = hd**-0.5

    def body(q_ref, k_ref, v_ref, o_ref):
        for h in range(MEM_HEADS):
            cols = slice(h * hd, (h + 1) * hd)
            p = _softmax_rows(_dot(q_ref[:, cols], k_ref[:, cols], "nt") * scale)
            o_ref[:, cols] = _dot(p, v_ref[:, cols], "nn").astype(BF16)

    row = pl.BlockSpec((t, d), lambda i: (i, 0))
    full = pl.BlockSpec((m, d), lambda i: (0, 0))
    return pl.pallas_call(
        body,
        name="attn_fwd",
        grid=(s // t,),
        in_specs=[row, full, full],
        out_specs=row,
        out_shape=jax.ShapeDtypeStruct((s, d), BF16),
        compiler_params=_params(1),
    )(q, km, vm)


def _attn_bwd(q, km, vm, do):
    s, d = q.shape
    m = km.shape[0]
    t = min(TS, s)
    steps = s // t
    hd = d // MEM_HEADS
    scale = hd**-0.5

    def body(q_ref, k_ref, v_ref, do_ref, dq_ref, dk_ref, dv_ref):
        @pl.when(pl.program_id(0) == 0)
        def _():
            dk_ref[...] = jnp.zeros_like(dk_ref)
            dv_ref[...] = jnp.zeros_like(dv_ref)

        for h in range(MEM_HEADS):
            cols = slice(h * hd, (h + 1) * hd)
            qh, kh, vh, doh = q_ref[:, cols], k_ref[:, cols], v_ref[:, cols], do_ref[:, cols]
            p = _softmax_rows(_dot(qh, kh, "nt") * scale)
            dp = _dot(doh, vh, "nt")
            ds = p * (dp - jnp.sum(dp * p, axis=-1, keepdims=True)) * scale
            dq_ref[:, cols] = _dot(ds, kh, "nn").astype(BF16)
            dk_ref[:, cols] += _dot(ds, qh, "tn")
            dv_ref[:, cols] += _dot(p, doh, "tn")

    row = pl.BlockSpec((t, d), lambda i: (i, 0))
    full = pl.BlockSpec((m, d), lambda i: (0, 0))
    return pl.pallas_call(
        body,
        name="attn_bwd",
        grid=(steps,),
        in_specs=[row, full, full, row],
        out_specs=[row, full, full],
        out_shape=[jax.ShapeDtypeStruct((s, d), BF16), jax.ShapeDtypeStruct((m, d), F32), jax.ShapeDtypeStruct((m, d), F32)],
        compiler_params=_params(1),
    )(q, km, vm, do)


def _row_tile(rows, cols):
    best = 8
    for tr in range(8, rows + 1, 8):
        if rows % tr == 0 and tr * cols * 4 <= (1 << 20):
            best = tr
    return best


def _adamw(w, g, m, v):
    m = ADAM_B1 * m + (1.0 - ADAM_B1) * g
    v = ADAM_B2 * v + (1.0 - ADAM_B2) * (g * g)
    m_hat = m / (1.0 - ADAM_B1**ADAM_STEP)
    v_hat = v / (1.0 - ADAM_B2**ADAM_STEP)
    delta = -ADAM_LR * (m_hat / (jnp.sqrt(v_hat) + ADAM_EPS) + ADAM_WD * w)
    return delta, m, v


def _adam_shard(name, parts, w, m, v):
    rows, cols = w.shape
    tr = _row_tile(rows, cols)

    def body(*refs):
        part_refs = refs[:N_DEV]
        w_ref, m_ref, v_ref, go_ref, d_ref, mo_ref, vo_ref = refs[N_DEV:]
        g = part_refs[0][...].astype(F32)
        for dev in range(1, N_DEV):
            g = g + part_refs[dev][...].astype(F32)
        go_ref[...] = g
        d_ref[...], mo_ref[...], vo_ref[...] = _adamw(w_ref[...], g, m_ref[...], v_ref[...])

    def part(dev):
        return pl.BlockSpec((None, tr, cols), lambda i: (dev, i, 0))

    flat = pl.BlockSpec((tr, cols), lambda i: (i, 0))
    return pl.pallas_call(
        body,
        name=name,
        grid=(rows // tr,),
        in_specs=[part(dev) for dev in range(N_DEV)] + [flat, flat, flat],
        out_specs=[flat, flat, flat, flat],
        out_shape=[jax.ShapeDtypeStruct((rows, cols), F32)] * 4,
        compiler_params=_params(1),
    )(*([parts] * N_DEV), w, m, v)


def _sum_devices(gathered):
    def body(g_ref, o_ref):
        tot = g_ref[0]
        for dev in range(1, N_DEV):
            tot = tot + g_ref[dev]
        o_ref[...] = tot

    return pl.pallas_call(body, name="sum_devices", out_shape=jax.ShapeDtypeStruct(gathered.shape[1:], F32))(gathered)


def _adam_small(ws, gs, ms, vs):
    n = len(ws)

    def body(*refs):
        w_refs, g_refs, m_refs, v_refs = (refs[q * n : (q + 1) * n] for q in range(4))
        outs = refs[4 * n :]
        for p in range(n):
            outs[p][...], outs[n + p][...], outs[2 * n + p][...] = _adamw(w_refs[p][...], g_refs[p][...], m_refs[p][...], v_refs[p][...])

    shapes = [jax.ShapeDtypeStruct(w.shape, F32) for w in ws]
    res = pl.pallas_call(body, name="adam_small", out_shape=shapes * 3)(*ws, *gs, *ms, *vs)
    return res[:n], res[n : 2 * n], res[2 * n :]


def _padded(size):
    return -(-size // 128) * 128


def _pack(parts):
    flat = []
    for p in parts:
        v = p.reshape(-1)
        flat.append(jnp.pad(v, (0, _padded(v.shape[0]) - v.shape[0])))
    return jnp.concatenate(flat).reshape(-1, 128)


def _unpack(packed, shapes):
    flat = packed.reshape(-1)
    out, pos = [], 0
    for shp in shapes:
        size = math.prod(shp)
        out.append(flat[pos : pos + size].reshape(shp))
        pos += _padded(size)
    return out


WEIGHTS = (
    "hgrn_lb norm1_w w_in hgrn_norm_w sconv_w w_out norm2_w mem_norm_w wq wk wv wo norm3_w w_gate w_up ffn_conv_w "
    "ffn_conv_b w_down final_norm_w"
).split()
BIG = ["w_in", "w_out", "wq", "wk", "wv", "wo", "w_gate", "w_up", "w_down"]
SMALL_SHARDED = ["sconv_w", "ffn_conv_w"]
SMALL = [n for n in WEIGHTS if n not in BIG]


def kernel(x, mem, hgrn_lb, norm1_w, w_in, hgrn_norm_w, sconv_w, w_out, norm2_w, mem_norm_w, wq, wk, wv, wo, norm3_w, w_gate, w_up, ffn_conv_w, ffn_conv_b, w_down, final_norm_w, loss_target, m_hgrn_lb, m_norm1_w, m_w_in, m_hgrn_norm_w, m_sconv_w, m_w_out, m_norm2_w, m_mem_norm_w, m_wq, m_wk, m_wv, m_wo, m_norm3_w, m_w_gate, m_w_up, m_ffn_conv_w, m_ffn_conv_b, m_w_down, m_final_norm_w, v_hgrn_lb, v_norm1_w, v_w_in, v_hgrn_norm_w, v_sconv_w, v_w_out, v_norm2_w, v_mem_norm_w, v_wq, v_wk, v_wv, v_wo, v_norm3_w, v_w_gate, v_w_up, v_ffn_conv_w, v_ffn_conv_b, v_w_down, v_final_norm_w):
    given = dict(locals())
    w_of = {n: given[n] for n in WEIGHTS}
    m_of = {n: given["m_" + n] for n in WEIGHTS}
    v_of = {n: given["v_" + n] for n in WEIGHTS}
    x2, mem2, target = x[0], mem[0], loss_target[0]
    d = x2.shape[1]
    xi, yi, ci = lax.axis_index("x"), lax.axis_index("y"), lax.axis_index("c")
    me = 4 * xi + 2 * yi + ci
    final_w = final_norm_w.reshape(1, d)

    shard = {n: w_of[n][0].astype(BF16) for n in BIG}
    small_shard = jnp.concatenate([sconv_w[0], ffn_conv_w[0]], axis=1)
    h1, w_in_g, small_g = _rmsnorm("norm1", x2, norm1_w, _GatherRide([shard["w_in"], small_shard]))
    w_in_f = jnp.transpose(w_in_g, (1, 0, 2)).reshape(d, -1)
    conv_ch = sconv_w.shape[2]
    sw_f = jnp.transpose(small_g[:, :, :conv_ch], (1, 0, 2)).reshape(3, -1)
    cw_g = small_g[:, :, conv_ch:]
    cb_g = ffn_conv_b.reshape(N_DEV, 1, -1)

    lb = _lb_fwd(hgrn_lb)
    proj, *attn_w = _mm_proj(h1, w_in_f, _GatherRide([shard[n] for n in ("w_out", "wq", "wk", "wv", "wo")]))
    w_out_f, wq_f, wk_f, wv_f, wo_f = (a.reshape(d, d) for a in attn_w)
    mix, o_pre, states, wg_g, wu_g = _hgrn_fwd(proj, lb, hgrn_norm_w, _GatherRide([shard["w_gate"], shard["w_up"]]))
    mix = _sconv_fwd(proj, sw_f, mix)
    x1 = _mm_nn("mm_out", mix, w_out_f, F32, residual=x2)
    h2 = _rmsnorm("norm2", x1, norm2_w)
    mem_n = _rmsnorm("norm_mem", mem2, mem_norm_w)
    q = _mm_nn("mm_q", h2, wq_f, BF16)
    km = _mm_nn("mm_k", mem_n, wk_f, BF16)
    vm = _mm_nn("mm_v", mem_n, wv_f, BF16)
    ao = _attn_fwd(q, km, vm)
    x2r = _mm_nn("mm_o", ao, wo_f, F32, residual=x1)
    h3 = _rmsnorm("norm3", x2r, norm3_w)
    a_pre, up, wd_g = _mm_gate_up(h3, wg_g, wu_g, _GatherRide([shard["w_down"]]))
    x3, act = _ffn_down(a_pre, up, cw_g, cb_g, wd_g, x2r)

    dx3, dx3b, loss_blk, d_final = _loss_head(x3, target, final_w)
    parts = {}
    dact = _mm_dact(dx3b, wd_g)
    dwd = _mm_dwdown(act, dx3b)
    dapre, dup, d_ffn_small, parts["w_down"] = _ffn_act_bwd(a_pre, up, dact, cw_g, cb_g, _ScatterRide([dwd]))
    dwg, dwu = _mm_dw_cols("mm_dw_gate_up", h3, [dapre, dup])
    dh3, parts["w_gate"], parts["w_up"] = _mm_dh3(dapre, dup, wg_g, wu_g, _ScatterRide([dwg, dwu]))
    dx2, dx2b, d_n3 = _rmsnorm_bwd("norm3_bwd", dh3, x2r, norm3_w, dx3)
    dao = _mm_nt("mm_dao", [(dx2b, wo_f)], BF16)
    dwo = _mm_tn("mm_dwo", ao, dx2b)
    dq, dkm, dvm = _attn_bwd(q, km, vm, dao)
    dwq = _mm_tn("mm_dwq", h2, dq)
    dh2 = _mm_nt("mm_dh2", [(dq, wq_f)], F32)
    dwk = _mm_tn("mm_dwk", mem_n, dkm)
    dwv = _mm_tn("mm_dwv", mem_n, dvm)
    dmem_n = _mm_nt("mm_dmem", [(dkm, wk_f), (dvm, wv_f)], F32)
    _, _, d_nm = _rmsnorm_bwd("norm_mem_bwd", dmem_n, mem2, mem_norm_w)
    dx1, dx1b, d_n2 = _rmsnorm_bwd("norm2_bwd", dh2, x1, norm2_w, dx2)
    dmix = _mm_nt("mm_dmix", [(dx1b, w_out_f)], F32)
    dwout = _mm_tn("mm_dwout", mix, dx1b)
    dwout_g, dwq_g, dwk_g, dwv_g, dwo_g = (a.reshape(N_DEV, d // N_DEV, d) for a in (dwout, dwq, dwk, dwv, dwo))
    dproj, d_lb, d_nw, parts["w_out"], parts["wq"] = _hgrn_bwd(proj, lb, hgrn_norm_w, o_pre, states, dmix, _ScatterRide([dwout_g, dwq_g]))
    dproj, d_sw = _sconv_bwd(proj, sw_f, dmix, dproj)
    dwin, parts["wk"], parts["wv"], parts["wo"] = _mm_dwin(h1, dproj, N_GROUPS, _ScatterRide([dwk_g, dwv_g, dwo_g]))
    dh1, parts["w_in"] = _mm_dh1(dproj, w_in_f, N_GROUPS, _ScatterRide([dwin], column_slabs=True))
    grad_x, _, d_n1 = _rmsnorm_bwd("norm1_bwd", dh1, x2, norm1_w, dx1)
    d_hlb = _lb_bwd(hgrn_lb, d_lb)

    small_parts = [d_hlb, d_n1, d_nw, d_sw[0:3], d_n2, d_nm, d_n3, d_ffn_small[:, 0:3, :], d_ffn_small[:, 3, :], d_final, loss_blk]
    small_shapes = [p.shape for p in small_parts]
    total = _sum_devices(_allgather("gather_small", [_pack(small_parts)])[0])
    g_hlb, g_n1, g_nw, g_sw, g_n2, g_nm, g_n3, g_cw, g_cb, g_final, loss_row = _unpack(total, small_shapes)
    grads = {
        "hgrn_lb": g_hlb,
        "norm1_w": g_n1,
        "hgrn_norm_w": g_nw,
        "sconv_w": lax.dynamic_slice_in_dim(g_sw, me * conv_ch, conv_ch, axis=1).reshape(sconv_w.shape),
        "norm2_w": g_n2,
        "mem_norm_w": g_nm,
        "norm3_w": g_n3,
        "ffn_conv_w": lax.dynamic_index_in_dim(g_cw, me, axis=0, keepdims=True),
        "ffn_conv_b": g_cb.reshape(ffn_conv_b.shape),
        "final_norm_w": g_final.reshape(final_norm_w.shape),
    }
    def rows_of(a):
        return a.reshape(-1, a.shape[-1])

    small_out = _adam_small(*([rows_of(src[n]) for n in SMALL] for src in (w_of, grads, m_of, v_of)))
    deltas, new_m, new_v = ({n: a.reshape(w_of[n].shape) for n, a in zip(SMALL, out)} for out in small_out)

    for n in BIG:
        shp = w_of[n].shape
        gw, dw, mw, vw = _adam_shard("adam_" + n, parts[n], w_of[n][0], m_of[n][0], v_of[n][0])
        grads[n], deltas[n], new_m[n], new_v[n] = (a.reshape(shp) for a in (gw, dw, mw, vw))

    loss = loss_row[0, 0]
    return (
        loss,
        grad_x.reshape(x.shape),
        *[grads[n] for n in WEIGHTS],
        *[deltas[n] for n in WEIGHTS],
        *[new_m[n] for n in WEIGHTS],
        *[new_v[n] for n in WEIGHTS],
    )
```
